```python
import jax, jax.numpy as jnp
from jax import lax
import numpy as np

D_MODEL = 1024
BATCH = 8
SEQ = 4096
DEPTH = 4

LRU_WIDTH = D_MODEL
LRU_HEADS = 8
LRU_HEAD_DIM = LRU_WIDTH // LRU_HEADS
CONV_WIDTH = 4
LRU_C = 8.0
POOL_WINDOWS = (2, 4, 8, 16)
POOL_GROUPS = len(POOL_WINDOWS)
POOL_WIDTH = D_MODEL // 2
POOL_GROUP_DIM = POOL_WIDTH // POOL_GROUPS
N_BRANCHES = 2
IN_PROJ_WIDTH = 2 * LRU_WIDTH + POOL_WIDTH + N_BRANCHES * D_MODEL
N_EXPERTS = 32
TOP_K = 4
D_FF_EXPERT = D_MODEL
SWIGLU_LIMIT = 7.0
SWIGLU_ALPHA = 1.702
EXPERT_BLOCK = 128
NORM_EPS = 1e-6

kernel_name = 'hybrid_rglru_pool_moe_adaln'


def _rmsnorm(x, g):
    xf = x.astype(jnp.float32)
    y = xf * lax.rsqrt(jnp.mean(xf * xf, axis=-1, keepdims=True) + NORM_EPS)
    return (y * g.astype(jnp.float32)).astype(x.dtype)


def _modulate(h, shift, scale):
    return h * (1 + scale[:, None, :]) + shift[:, None, :]


def _block_diag(x, w):
    bsz, s, _ = x.shape
    n, di, do = w.shape
    y = jnp.einsum('bsgi,gio->bsgo', x.reshape(bsz, s, n, di), w)
    return y.reshape(bsz, s, n * do)


def _causal_depthwise_conv(x, w, b):
    y = lax.conv_general_dilated(x, w[:, None, :], window_strides=(1,), padding=[(CONV_WIDTH - 1, 0)], dimension_numbers=('NWC', 'WIO', 'NWC'), feature_group_count=x.shape[-1])
    return y + b


def _rg_lru(x, w_a, b_a, w_x, b_x, lam):
    r = jax.nn.sigmoid((_block_diag(x, w_a) + b_a).astype(jnp.float32))
    i = jax.nn.sigmoid((_block_diag(x, w_x) + b_x).astype(jnp.float32))
    log_a = -LRU_C * r * jax.nn.softplus(-lam.astype(jnp.float32))
    a = jnp.exp(log_a)
    mult = jnp.sqrt(-jnp.expm1(2.0 * log_a))
    u = mult * (i * x.astype(jnp.float32))

    def combine(left, right):
        a_l, b_l = left
        a_r, b_r = right
        return a_l * a_r, a_r * b_l + b_r

    _, h = lax.associative_scan(combine, (a, u), axis=1)
    return h.astype(x.dtype)


def _multiscale_pool(x):
    bsz, s, _ = x.shape
    xf = x.astype(jnp.float32)
    cs = lax.cumsum(xf, axis=1)
    w_max = max(POOL_WINDOWS)
    cs_pad = jnp.pad(cs, ((0, 0), (w_max, 0), (0, 0)))
    t = jnp.arange(s, dtype=jnp.float32)
    outs = []
    for g, w in enumerate(POOL_WINDOWS):
        lo, hi = g * POOL_GROUP_DIM, (g + 1) * POOL_GROUP_DIM
        win_sum = cs[:, :, lo:hi] - cs_pad[:, w_max - w:w_max - w + s, lo:hi]
        count = jnp.minimum(t + 1.0, float(w))[None, :, None]
        outs.append(win_sum / count - xf[:, :, lo:hi])
    return jnp.concatenate(outs, axis=-1).astype(x.dtype)


def _hybrid_mixer(h, w_in, conv_w, conv_b, w_rg_a, b_rg_a, w_rg_x, b_rg_x, lam, w_pool, pool_scale, w_up_a, w_up_b, w_out):
    bsz, s, _ = h.shape
    z = h @ w_in
    o1 = LRU_WIDTH
    o2 = o1 + LRU_WIDTH
    o3 = o2 + POOL_WIDTH
    x_lru, g_lru, x_pool, gate_logits = z[..., :o1], z[..., o1:o2], z[..., o2:o3], z[..., o3:]
    y_lru = _rg_lru(_causal_depthwise_conv(x_lru, conv_w, conv_b), w_rg_a, b_rg_a, w_rg_x, b_rg_x, lam)
    y_lru = y_lru * jax.nn.gelu(g_lru)
    y_pool = _block_diag(_multiscale_pool(x_pool), w_pool) * pool_scale
    p_a = y_lru @ w_up_a
    p_b = y_pool @ w_up_b
    gates = jax.nn.sigmoid(gate_logits).reshape(bsz, s, N_BRANCHES, D_MODEL)
    merged = gates[:, :, 0, :] * p_a + gates[:, :, 1, :] * p_b
    return merged @ w_out


def _moe(h, w_router, b_router, w_e_in, b_e_in, w_e_out, b_e_out):
    bsz, s, d = h.shape
    n_tok = bsz * s
    ht = h.reshape(n_tok, d)
    logits = (ht @ w_router + b_router).astype(jnp.float32)
    top_vals, top_idx = lax.top_k(logits, TOP_K)
    weights = jax.nn.softmax(top_vals, axis=-1)
    n_assign = n_tok * TOP_K
    flat_e = top_idx.reshape(n_assign).astype(jnp.int32)
    flat_tok = jnp.repeat(jnp.arange(n_tok, dtype=jnp.int32), TOP_K)
    flat_w = weights.reshape(n_assign)
    order = jnp.argsort(flat_e)
    sorted_e = flat_e[order]
    sorted_tok = flat_tok[order]
    sorted_w = flat_w[order]
    counts = jnp.bincount(flat_e, length=N_EXPERTS)
    padded = ((counts + EXPERT_BLOCK - 1) // EXPERT_BLOCK) * EXPERT_BLOCK
    start_sorted = jnp.cumsum(counts) - counts
    pad_end = jnp.cumsum(padded)
    start_pad = pad_end - padded
    dest = start_pad[sorted_e] + jnp.arange(n_assign, dtype=jnp.int32) - start_sorted[sorted_e]
    n_blocks = -(-n_assign // EXPERT_BLOCK) + N_EXPERTS
    n_slots = n_blocks * EXPERT_BLOCK
    buf_tok = jnp.zeros((n_slots,), jnp.int32).at[dest].set(sorted_tok)
    buf_w = jnp.zeros((n_slots,), jnp.float32).at[dest].set(sorted_w)
    block_start = jnp.arange(n_blocks, dtype=jnp.int32) * EXPERT_BLOCK
    block_e = jnp.minimum(jnp.searchsorted(pad_end, block_start, side='right'), N_EXPERTS - 1).astype(jnp.int32)

    def expert_block(args):
        tok, e = args
        xb = ht[tok]
        gu = xb @ w_e_in[e] + b_e_in[e]
        gate, up = gu[:, :D_FF_EXPERT], gu[:, D_FF_EXPERT:]
        gate = jnp.minimum(gate, SWIGLU_LIMIT)
        up = jnp.clip(up, -SWIGLU_LIMIT, SWIGLU_LIMIT)
        act = gate * jax.nn.sigmoid(SWIGLU_ALPHA * gate) * (up + 1)
        return act @ w_e_out[e] + b_e_out[e]

    outs = lax.map(expert_block, (buf_tok.reshape(n_blocks, EXPERT_BLOCK), block_e))
    y = jnp.zeros((n_tok, d), jnp.float32).at[buf_tok].add(outs.reshape(n_slots, d).astype(jnp.float32) * buf_w[:, None])
    return y.reshape(bsz, s, d).astype(h.dtype)


def setup_inputs(seed: int = 0) -> dict:
    key = jax.random.key(seed)
    ks = jax.random.split(key, 26)
    f32 = jnp.float32
    L, D = DEPTH, D_MODEL

    def nrm(k, shape, scale):
        return jax.random.normal(k, shape, f32) * scale

    u = jax.random.uniform(ks[14], (L, LRU_WIDTH), f32, 0.9, 0.999)
    s = u ** (1.0 / LRU_C)
    lru_lambda = jnp.log(s) - jnp.log1p(-s)
    return {
        'x': nrm(ks[0], (BATCH, SEQ, D), 1.0),
        'c': nrm(ks[1], (BATCH, D), 1.0),
        'norm1_g': 1.0 + nrm(ks[2], (L, D), 0.05),
        'norm2_g': 1.0 + nrm(ks[3], (L, D), 0.05),
        'w_ada': nrm(ks[4], (L, D, 6 * D), 0.5 * D ** -0.5),
        'b_ada': nrm(ks[5], (L, 6 * D), 0.02),
        'w_in': nrm(ks[6], (L, D, IN_PROJ_WIDTH), D ** -0.5),
        'conv_w': nrm(ks[7], (L, CONV_WIDTH, LRU_WIDTH), CONV_WIDTH ** -0.5),
        'conv_b': nrm(ks[8], (L, LRU_WIDTH), 0.02),
        'w_rg_a': nrm(ks[9], (L, LRU_HEADS, LRU_HEAD_DIM, LRU_HEAD_DIM), LRU_HEAD_DIM ** -0.5),
        'b_rg_a': nrm(ks[10], (L, LRU_WIDTH), 0.1),
        'w_rg_x': nrm(ks[11], (L, LRU_HEADS, LRU_HEAD_DIM, LRU_HEAD_DIM), LRU_HEAD_DIM ** -0.5),
        'b_rg_x': nrm(ks[12], (L, LRU_WIDTH), 0.1),
        'lru_lambda': lru_lambda,
        'w_pool': nrm(ks[13], (L, POOL_GROUPS, POOL_GROUP_DIM, POOL_GROUP_DIM), POOL_GROUP_DIM ** -0.5),
        'pool_scale': 1.0 + nrm(ks[15], (L, POOL_WIDTH), 0.1),
        'w_up_a': nrm(ks[16], (L, LRU_WIDTH, D), LRU_WIDTH ** -0.5),
        'w_up_b': nrm(ks[17], (L, POOL_WIDTH, D), POOL_WIDTH ** -0.5),
        'w_out': nrm(ks[18], (L, D, D), D ** -0.5),
        'w_router': nrm(ks[19], (L, D, N_EXPERTS), D ** -0.5),
        'b_router': nrm(ks[20], (L, N_EXPERTS), 0.01),
        'w_e_in': nrm(ks[21], (L, N_EXPERTS, D, 2 * D_FF_EXPERT), D ** -0.5),
        'b_e_in': nrm(ks[22], (L, N_EXPERTS, 2 * D_FF_EXPERT), 0.02),
        'w_e_out': nrm(ks[23], (L, N_EXPERTS, D_FF_EXPERT, D), D_FF_EXPERT ** -0.5),
        'b_e_out': nrm(ks[24], (L, N_EXPERTS, D), 0.02),
        'final_g': 1.0 + nrm(ks[25], (D,), 0.05),
    }


def reference(x, c, norm1_g, norm2_g, w_ada, b_ada, w_in, conv_w, conv_b, w_rg_a, b_rg_a, w_rg_x, b_rg_x, lru_lambda, w_pool, pool_scale, w_up_a, w_up_b, w_out, w_router, b_router, w_e_in, b_e_in, w_e_out, b_e_out, final_g):
    c_act = jax.nn.silu(c)
    for l in range(DEPTH):
        mod = c_act @ w_ada[l] + b_ada[l]
        shift1, scale1, gate1, shift2, scale2, gate2 = jnp.split(mod, 6, axis=-1)
        h = _modulate(_rmsnorm(x, norm1_g[l]), shift1, scale1)
        mix = _hybrid_mixer(h, w_in[l], conv_w[l], conv_b[l], w_rg_a[l], b_rg_a[l], w_rg_x[l], b_rg_x[l], lru_lambda[l], w_pool[l], pool_scale[l], w_up_a[l], w_up_b[l], w_out[l])
        x = x + gate1[:, None, :] * mix
        h = _modulate(_rmsnorm(x, norm2_g[l]), shift2, scale2)
        ffn = _moe(h, w_router[l], b_router[l], w_e_in[l], b_e_in[l], w_e_out[l], b_e_out[l])
        x = x + gate2[:, None, :] * ffn
    return _rmsnorm(x, final_g)
```

```python
import functools

import jax
import jax.numpy as jnp
from jax import lax
from jax.experimental import pallas as pl
from jax.experimental.pallas import tpu as pltpu

D_MODEL = 1024
N_HEADS = 8
HEAD_DIM = 128
CONV_WIDTH = 4
LRU_C = 8.0
POOL_WINDOWS = (2, 4, 8, 16)
POOL_WIDTH = 512
POOL_GROUP_DIM = 128
N_EXPERTS = 32
TOP_K = 4
D_FF = 1024
SWIGLU_LIMIT = 7.0
SWIGLU_ALPHA = 1.702
NORM_EPS = 1e-6

SUBLANES = 8
VMEM_LIMIT = 56 * 1024 * 1024

MIX_ROWS = 256
ROUTE_ROWS = 512
EXPERT_ROWS = 256

F32 = jnp.float32
BF16 = jnp.bfloat16


def _const_spec(shape):
    nd = len(shape)
    return pl.BlockSpec(shape, lambda *_: (0,) * nd, pipeline_mode=pl.Buffered(1))


def _ada_kernel(c_ref, w_ref, b_ref, o_ref):
    c = c_ref[...]
    c_act = c * jax.nn.sigmoid(c)
    o_ref[0] = jnp.dot(c_act, w_ref[0], preferred_element_type=F32,
                       precision=lax.Precision.HIGHEST) + b_ref[0]


def _ada_mod(c, w_ada, b_ada):
    depth, d, n = w_ada.shape
    bsz = c.shape[0]
    tn = 1536
    return pl.pallas_call(
        _ada_kernel,
        out_shape=jax.ShapeDtypeStruct((depth, bsz, n), F32),
        grid=(depth, n // tn),
        in_specs=[
            pl.BlockSpec((bsz, d), lambda l, j: (0, 0)),
            pl.BlockSpec((1, d, tn), lambda l, j: (l, 0, j)),
            pl.BlockSpec((1, 1, tn), lambda l, j: (l, 0, j)),
        ],
        out_specs=pl.BlockSpec((1, bsz, tn), lambda l, j: (l, 0, j)),
        compiler_params=pltpu.CompilerParams(
            dimension_semantics=("arbitrary", "arbitrary"),
            vmem_limit_bytes=VMEM_LIMIT),
        name="ada_mod",
    )(c, w_ada, b_ada.reshape(depth, 1, n))


def _shift_rows(a, j, carry_ref):
    rolled = pltpu.roll(a, j, axis=0)
    head = rolled[0:SUBLANES]
    row = lax.broadcasted_iota(jnp.int32, head.shape, 0)
    top = jnp.where(row < j, carry_ref[...], head)
    carry_ref[...] = head
    return jnp.concatenate([top, rolled[SUBLANES:]], axis=0)


def _shift8_rows(a, carry_ref):
    top = carry_ref[...]
    carry_ref[...] = a[a.shape[0] - SUBLANES:]
    return jnp.concatenate([top, a[:a.shape[0] - SUBLANES]], axis=0)


def _linear_scan(a, u, h_ref):
    ts, d = a.shape
    ng = ts // SUBLANES
    a3 = a.reshape(ng, SUBLANES, d)
    u3 = u.reshape(ng, SUBLANES, d)
    sub = lax.broadcasted_iota(jnp.int32, (ng, SUBLANES, d), 1)
    for step in (1, 2, 4):
        keep = sub >= step
        a_prev = jnp.where(keep, pltpu.roll(a3, step, axis=1), 1.0)
        u_prev = jnp.where(keep, pltpu.roll(u3, step, axis=1), 0.0)
        u3 = a3 * u_prev + u3
        a3 = a3 * a_prev
    h = h_ref[...]
    outs = []
    for g in range(ng):
        hg = u3[g] + a3[g] * h
        outs.append(hg)
        h = hg[SUBLANES - 1:SUBLANES]
    h_ref[...] = h
    return jnp.concatenate(outs, axis=0)


def _mixer_kernel(x_ref, mod_ref, g1_ref, win_ref, convw_ref, convb_ref, wrg_ref,
                  brga_ref, brgx_ref, lam_ref, wpool_ref, pscale_ref, wupa_ref,
                  wupb_ref, wout_ref, o_ref,
                  cc1, cc2, cc3, pc1, pc2, pc4, pc8, h_c):
    s = pl.program_id(1)
    ts = x_ref.shape[1]
    d = D_MODEL

    @pl.when(s == 0)
    def _():
        for ref in (cc1, cc2, cc3, pc1, pc2, pc4, pc8, h_c):
            ref[...] = jnp.zeros(ref.shape, ref.dtype)

    x = x_ref[0]
    mod = mod_ref[0]
    shift1, scale1, gate1 = mod[0:1], mod[1:2], mod[2:3]
    ms = jnp.mean(x * x, axis=-1, keepdims=True)
    hn = x * lax.rsqrt(ms + NORM_EPS) * g1_ref[...]
    hb = (hn * (1.0 + scale1) + shift1).astype(BF16)

    def in_proj(lo, hi):
        return jnp.dot(hb, win_ref[:, lo:hi], preferred_element_type=F32)

    x_lru = in_proj(0, d)
    cw = convw_ref[...]
    xc = (convb_ref[...] + cw[3:4] * x_lru
          + cw[2:3] * _shift_rows(x_lru, 1, cc1)
          + cw[1:2] * _shift_rows(x_lru, 2, cc2)
          + cw[0:1] * _shift_rows(x_lru, 3, cc3))
    xcb = xc.astype(BF16)
    pre = [jnp.dot(xcb[:, k * HEAD_DIM:(k + 1) * HEAD_DIM], wrg_ref[k],
                   preferred_element_type=F32) for k in range(N_HEADS)]
    pre_a = jnp.concatenate([p[:, :HEAD_DIM] for p in pre], axis=-1)
    pre_x = jnp.concatenate([p[:, HEAD_DIM:] for p in pre], axis=-1)
    r = jax.nn.sigmoid(pre_a + brga_ref[...])
    gi = jax.nn.sigmoid(pre_x + brgx_ref[...])
    log_a = (-LRU_C) * r * jax.nn.softplus(-lam_ref[...])
    a = jnp.exp(log_a)
    mult = jnp.sqrt(jnp.tanh(-log_a) * (1.0 + a * a))
    u = mult * (gi * xc)
    h = _linear_scan(a, u, h_c)
    y_lru = h * jax.nn.gelu(in_proj(d, 2 * d))
    p_a = jnp.dot(y_lru.astype(BF16), wupa_ref[...], preferred_element_type=F32)
    gates_a = jax.nn.sigmoid(in_proj(2 * d + POOL_WIDTH, 3 * d + POOL_WIDTH))
    merged = gates_a * p_a

    x_pool = in_proj(2 * d, 2 * d + POOL_WIDTH)
    g = POOL_GROUP_DIM
    s2 = x_pool + _shift_rows(x_pool, 1, pc1)
    s4 = s2[:, g:] + _shift_rows(s2[:, g:], 2, pc2)
    s8 = s4[:, g:] + _shift_rows(s4[:, g:], 4, pc4)
    s16 = s8[:, g:] + _shift8_rows(s8[:, g:], pc8)
    wins = (s2[:, :g], s4[:, :g], s8[:, :g], s16)
    t1 = (s * ts + lax.broadcasted_iota(jnp.int32, (ts, 1), 0) + 1).astype(F32)
    pooled = []
    for k, w in enumerate(POOL_WINDOWS):
        cnt = jnp.minimum(t1, float(w))
        pk = wins[k] / cnt - x_pool[:, k * g:(k + 1) * g]
        pooled.append(jnp.dot(pk.astype(BF16), wpool_ref[k], preferred_element_type=F32))
    y_pool = jnp.concatenate(pooled, axis=-1) * pscale_ref[...]
    p_b = jnp.dot(y_pool.astype(BF16), wupb_ref[...], preferred_element_type=F32)
    gates_b = jax.nn.sigmoid(in_proj(3 * d + POOL_WIDTH, 4 * d + POOL_WIDTH))
    merged = merged + gates_b * p_b

    mix = jnp.dot(merged.astype(BF16), wout_ref[...], preferred_element_type=F32)
    o_ref[0] = x + gate1 * mix


def _mixer_layer(x, mod_l, g1, w_in, conv_w, conv_b, w_rg, b_rg_a, b_rg_x, lam,
                 w_pool, pool_scale, w_up_a, w_up_b, w_out):
    bsz, seq, d = x.shape
    ts = MIX_ROWS
    row = lambda v: v.reshape(1, -1)
    args = (x, mod_l, row(g1), w_in, conv_w, row(conv_b), w_rg, row(b_rg_a), row(b_rg_x),
            row(lam), w_pool, row(pool_scale), w_up_a, w_up_b, w_out)
    in_specs = [
        pl.BlockSpec((1, ts, d), lambda b, s: (b, s, 0)),
        pl.BlockSpec((1, 6, d), lambda b, s: (b, 0, 0)),
    ] + [_const_spec(a.shape) for a in args[2:]]
    g = POOL_GROUP_DIM
    scratch = [
        pltpu.VMEM((SUBLANES, d), F32), pltpu.VMEM((SUBLANES, d), F32),
        pltpu.VMEM((SUBLANES, d), F32),
        pltpu.VMEM((SUBLANES, 4 * g), F32), pltpu.VMEM((SUBLANES, 3 * g), F32),
        pltpu.VMEM((SUBLANES, 2 * g), F32), pltpu.VMEM((SUBLANES, g), F32),
        pltpu.VMEM((1, d), F32),
    ]
    return pl.pallas_call(
        _mixer_kernel,
        out_shape=jax.ShapeDtypeStruct(x.shape, F32),
        grid=(bsz, seq // ts),
        in_specs=in_specs,
        out_specs=pl.BlockSpec((1, ts, d), lambda b, s: (b, s, 0)),
        scratch_shapes=scratch,
        compiler_params=pltpu.CompilerParams(
            dimension_semantics=("arbitrary", "arbitrary"),
            vmem_limit_bytes=VMEM_LIMIT),
        name="mixer",
    )(*args)


def _router_kernel(x_ref, mod_ref, g2_ref, wr_ref, br_ref,
                   h_ref, idx_ref, pos_ref, wt_ref, cnt_ref, run_c):
    i = pl.program_id(0)
    tr = x_ref.shape[0]

    @pl.when(i == 0)
    def _():
        run_c[...] = jnp.zeros(run_c.shape, run_c.dtype)

    x = x_ref[...]
    mod = mod_ref[0]
    shift2, scale2 = mod[3:4], mod[4:5]
    ms = jnp.mean(x * x, axis=-1, keepdims=True)
    hn = x * lax.rsqrt(ms + NORM_EPS) * g2_ref[...]
    h = hn * (1.0 + scale2) + shift2
    h_ref[...] = h.astype(BF16)
    logits = jnp.dot(h, wr_ref[...], preferred_element_type=F32,
                     precision=lax.Precision.HIGHEST) + br_ref[...]

    lane = lax.broadcasted_iota(jnp.int32, (tr, N_EXPERTS), 1)
    vals, idxs, hots = [], [], []
    cur = logits
    for _ in range(TOP_K):
        m = jnp.max(cur, axis=-1, keepdims=True)
        am = jnp.min(jnp.where(cur == m, lane, N_EXPERTS), axis=-1, keepdims=True)
        hot = lane == am
        vals.append(m)
        idxs.append(am)
        hots.append(hot)
        cur = jnp.where(hot, -jnp.inf, cur)
    exps = [jnp.exp(v - vals[0]) for v in vals]
    denom = exps[0] + exps[1] + exps[2] + exps[3]

    sel = (hots[0] | hots[1] | hots[2] | hots[3]).astype(F32)
    ri = lax.broadcasted_iota(jnp.int32, (tr, tr), 0)
    ci = lax.broadcasted_iota(jnp.int32, (tr, tr), 1)
    tri = (ci < ri).astype(BF16)
    before = jnp.dot(tri, sel.astype(BF16), preferred_element_type=F32) + run_c[...]
    run_c[...] = run_c[...] + jnp.sum(sel, axis=0, keepdims=True)
    cnt_ref[...] = run_c[...].astype(jnp.int32)

    lane4 = lax.broadcasted_iota(jnp.int32, (tr, TOP_K), 1)
    idx_o = jnp.zeros((tr, TOP_K), jnp.int32)
    pos_o = jnp.zeros((tr, TOP_K), jnp.int32)
    wt_o = jnp.zeros((tr, TOP_K), F32)
    for k in range(TOP_K):
        pk = jnp.sum(jnp.where(hots[k], before, 0.0), axis=-1, keepdims=True)
        idx_o = jnp.where(lane4 == k, idxs[k], idx_o)
        pos_o = jnp.where(lane4 == k, pk.astype(jnp.int32), pos_o)
        wt_o = jnp.where(lane4 == k, exps[k] / denom, wt_o)
    idx_ref[...] = idx_o
    pos_ref[...] = pos_o
    wt_ref[...] = wt_o


def _router(x2d, mod_l, g2, w_router, b_router, seq):
    n_tok, d = x2d.shape
    tr = ROUTE_ROWS
    per_b = seq // tr
    return pl.pallas_call(
        _router_kernel,
        out_shape=(
            jax.ShapeDtypeStruct((n_tok, d), BF16),
            jax.ShapeDtypeStruct((n_tok, TOP_K), jnp.int32),
            jax.ShapeDtypeStruct((n_tok, TOP_K), jnp.int32),
            jax.ShapeDtypeStruct((n_tok, TOP_K), F32),
            jax.ShapeDtypeStruct((1, N_EXPERTS), jnp.int32),
        ),
        grid=(n_tok // tr,),
        in_specs=[
            pl.BlockSpec((tr, d), lambda i: (i, 0)),
            pl.BlockSpec((1, 6, d), lambda i: (i // per_b, 0, 0)),
            _const_spec((1, d)),
            _const_spec((d, N_EXPERTS)),
            _const_spec((1, N_EXPERTS)),
        ],
        out_specs=(
            pl.BlockSpec((tr, d), lambda i: (i, 0)),
            pl.BlockSpec((tr, TOP_K), lambda i: (i, 0)),
            pl.BlockSpec((tr, TOP_K), lambda i: (i, 0)),
            pl.BlockSpec((tr, TOP_K), lambda i: (i, 0)),
            pl.BlockSpec((1, N_EXPERTS), lambda i: (0, 0)),
        ),
        scratch_shapes=[pltpu.VMEM((1, N_EXPERTS), F32)],
        compiler_params=pltpu.CompilerParams(
            dimension_semantics=("arbitrary",), vmem_limit_bytes=VMEM_LIMIT),
        name="router",
    )(x2d, mod_l, g2.reshape(1, d), w_router, b_router.reshape(1, N_EXPERTS))


def _expert_kernel(be_ref, nb_ref, xs_ref, win_ref, bin_ref, wout_ref, bout_ref, ys_ref):
    i = pl.program_id(0)

    @pl.when(i < nb_ref[0])
    def _():
        gu = jnp.dot(xs_ref[...], win_ref[0], preferred_element_type=F32) + bin_ref[0]
        gate = jnp.minimum(gu[:, :D_FF], SWIGLU_LIMIT)
        up = jnp.clip(gu[:, D_FF:], -SWIGLU_LIMIT, SWIGLU_LIMIT)
        act = gate * jax.nn.sigmoid(SWIGLU_ALPHA * gate) * (up + 1.0)
        out = jnp.dot(act.astype(BF16), wout_ref[0], preferred_element_type=F32) + bout_ref[0]
        ys_ref[...] = out.astype(ys_ref.dtype)

    @pl.when(i >= nb_ref[0])
    def _():
        ys_ref[...] = jnp.zeros(ys_ref.shape, ys_ref.dtype)


def _experts(block_e, n_used, xs, w_e_in, b_e_in, w_e_out, b_e_out):
    n_slots, d = xs.shape
    tm = EXPERT_ROWS
    grid_spec = pltpu.PrefetchScalarGridSpec(
        num_scalar_prefetch=2,
        grid=(n_slots // tm,),
        in_specs=[
            pl.BlockSpec((tm, d), lambda i, be, nb: (i, 0)),
            pl.BlockSpec((1, d, 2 * D_FF), lambda i, be, nb: (be[i], 0, 0)),
            pl.BlockSpec((1, 1, 2 * D_FF), lambda i, be, nb: (be[i], 0, 0)),
            pl.BlockSpec((1, D_FF, d), lambda i, be, nb: (be[i], 0, 0)),
            pl.BlockSpec((1, 1, d), lambda i, be, nb: (be[i], 0, 0)),
        ],
        out_specs=pl.BlockSpec((tm, d), lambda i, be, nb: (i, 0)),
    )
    return pl.pallas_call(
        _expert_kernel,
        out_shape=jax.ShapeDtypeStruct((n_slots, d), BF16),
        grid_spec=grid_spec,
        compiler_params=pltpu.CompilerParams(
            dimension_semantics=("arbitrary",), vmem_limit_bytes=VMEM_LIMIT),
        name="experts",
    )(block_e, n_used, xs, w_e_in, b_e_in.reshape(N_EXPERTS, 1, 2 * D_FF),
      w_e_out, b_e_out.reshape(N_EXPERTS, 1, d))


def _final_norm_kernel(x_ref, g_ref, o_ref):
    x = x_ref[...]
    ms = jnp.mean(x * x, axis=-1, keepdims=True)
    o_ref[...] = x * lax.rsqrt(ms + NORM_EPS) * g_ref[...]


def _final_norm(x2d, g):
    n_tok, d = x2d.shape
    tr = 1024
    return pl.pallas_call(
        _final_norm_kernel,
        out_shape=jax.ShapeDtypeStruct((n_tok, d), F32),
        grid=(n_tok // tr,),
        in_specs=[pl.BlockSpec((tr, d), lambda i: (i, 0)), _const_spec((1, d))],
        out_specs=pl.BlockSpec((tr, d), lambda i: (i, 0)),
        compiler_params=pltpu.CompilerParams(
            dimension_semantics=("arbitrary",), vmem_limit_bytes=VMEM_LIMIT),
        name="final_norm",
    )(x2d, g.reshape(1, d))


def _moe_layer(x, mod_l, g2, w_router, b_router, w_e_in, b_e_in, w_e_out, b_e_out):
    bsz, seq, d = x.shape
    n_tok = bsz * seq
    tm = EXPERT_ROWS
    x2d = x.reshape(n_tok, d)
    h, idx, pos, wts, counts = _router(x2d, mod_l, g2, w_router, b_router, seq)
    counts = counts[0]
    padded = ((counts + tm - 1) // tm) * tm
    pad_end = jnp.cumsum(padded)
    start_pad = pad_end - padded
    dest = start_pad[idx] + pos
    n_blocks = (n_tok * TOP_K) // tm + N_EXPERTS
    n_slots = n_blocks * tm
    tok = jnp.broadcast_to(jnp.arange(n_tok, dtype=jnp.int32)[:, None], dest.shape)
    buf_tok = jnp.zeros((n_slots,), jnp.int32).at[dest.reshape(-1)].set(tok.reshape(-1))
    block_start = jnp.arange(n_blocks, dtype=jnp.int32) * tm
    block_e = jnp.minimum(jnp.searchsorted(pad_end, block_start, side='right'),
                          N_EXPERTS - 1).astype(jnp.int32)
    n_used = (pad_end[-1] // tm).astype(jnp.int32).reshape(1)
    xs = h[buf_tok]
    ys = _experts(block_e, n_used, xs, w_e_in, b_e_in, w_e_out, b_e_out)
    y4 = ys[dest.reshape(-1)].reshape(n_tok, TOP_K, d).astype(F32)
    ffn = jnp.sum(y4 * wts[:, :, None], axis=1)
    gate2 = mod_l[:, 5, :]
    return x + gate2[:, None, :] * ffn.reshape(bsz, seq, d)


def kernel(x, c, norm1_g, norm2_g, w_ada, b_ada, w_in, conv_w, conv_b, w_rg_a, b_rg_a,
           w_rg_x, b_rg_x, lru_lambda, w_pool, pool_scale, w_up_a, w_up_b, w_out,
           w_router, b_router, w_e_in, b_e_in, w_e_out, b_e_out, final_g):
    depth = w_in.shape[0]
    bsz, seq, d = x.shape
    mod = _ada_mod(c, w_ada, b_ada).reshape(depth, bsz, 6, d)
    w_rg = jnp.concatenate([w_rg_a, w_rg_x], axis=-1).astype(BF16)
    for l in range(depth):
        x = _mixer_layer(x, mod[l], norm1_g[l], w_in[l].astype(BF16), conv_w[l], conv_b[l],
                         w_rg[l], b_rg_a[l], b_rg_x[l], lru_lambda[l],
                         w_pool[l].astype(BF16), pool_scale[l], w_up_a[l].astype(BF16),
                         w_up_b[l].astype(BF16), w_out[l].astype(BF16))
        x = _moe_layer(x, mod[l], norm2_g[l], w_router[l], b_router[l],
                       w_e_in[l].astype(BF16), b_e_in[l], w_e_out[l].astype(BF16), b_e_out[l])
    return _final_norm(x.reshape(bsz * seq, d), final_g).reshape(bsz, seq, d)
```

```python
import functools

import jax
import jax.numpy as jnp
from jax import lax
from jax.experimental import pallas as pl
from jax.experimental.pallas import tpu as pltpu
from jax.experimental.pallas import tpu_sc as plsc

D_MODEL = 1024
N_HEADS = 8
HEAD_DIM = 128
LRU_C = 8.0
POOL_WINDOWS = (2, 4, 8, 16)
POOL_WIDTH = 512
POOL_GROUP_DIM = 128
N_EXPERTS = 32
TOP_K = 4
D_FF = 1024
SWIGLU_LIMIT = 7.0
SWIGLU_ALPHA = 1.702
NORM_EPS = 1e-6

SUBLANES = 8
VMEM_LIMIT = 56 * 1024 * 1024
SC_CORES = 2
SC_SUBCORES = 16
SC_WORKERS = SC_CORES * SC_SUBCORES
SC_CHUNK = 64

MIX_ROWS = 256
ROUTE_ROWS = 512
EXPERT_ROWS = 256
COMBINE_ROWS = 512
PACKED = D_MODEL // 2

F32 = jnp.float32
BF16 = jnp.bfloat16
I32 = jnp.int32
HIGH_MASK = -65536


def _const_spec(shape):
    nd = len(shape)
    return pl.BlockSpec(shape, lambda *_: (0,) * nd, pipeline_mode=pl.Buffered(1))


def _pack_rows(v):
    bits = lax.bitcast_convert_type(v.astype(BF16).astype(F32), I32)
    lo, hi = bits[:, :PACKED], bits[:, PACKED:]
    return (hi & HIGH_MASK) | lax.shift_right_logical(lo, 16)


def _unpack_rows(p):
    lo = lax.bitcast_convert_type(lax.shift_left(p, 16), F32)
    hi = lax.bitcast_convert_type(p & HIGH_MASK, F32)
    return lo, hi


def _ada_kernel(c_ref, w_ref, b_ref, o_ref):
    c = c_ref[...]
    c_act = c * jax.nn.sigmoid(c)
    o_ref[0] = jnp.dot(c_act, w_ref[0], preferred_element_type=F32,
                       precision=lax.Precision.HIGHEST) + b_ref[0]


def _ada_mod(c, w_ada, b_ada):
    depth, d, n = w_ada.shape
    bsz = c.shape[0]
    tn = 1536
    return pl.pallas_call(
        _ada_kernel,
        out_shape=jax.ShapeDtypeStruct((depth, bsz, n), F32),
        grid=(depth, n // tn),
        in_specs=[
            pl.BlockSpec((bsz, d), lambda l, j: (0, 0)),
            pl.BlockSpec((1, d, tn), lambda l, j: (l, 0, j)),
            pl.BlockSpec((1, 1, tn), lambda l, j: (l, 0, j)),
        ],
        out_specs=pl.BlockSpec((1, bsz, tn), lambda l, j: (l, 0, j)),
        compiler_params=pltpu.CompilerParams(
            dimension_semantics=("arbitrary", "arbitrary"),
            vmem_limit_bytes=VMEM_LIMIT),
        name="ada_mod",
    )(c, w_ada, b_ada.reshape(depth, 1, n))


def _shift_rows(a, j, carry_ref):
    rolled = pltpu.roll(a, j, axis=0)
    head = rolled[0:SUBLANES]
    row = lax.broadcasted_iota(I32, head.shape, 0)
    top = jnp.where(row < j, carry_ref[...], head)
    carry_ref[...] = head
    return jnp.concatenate([top, rolled[SUBLANES:]], axis=0)


def _shift8_rows(a, carry_ref):
    top = carry_ref[...]
    carry_ref[...] = a[a.shape[0] - SUBLANES:]
    return jnp.concatenate([top, a[:a.shape[0] - SUBLANES]], axis=0)


def _linear_scan(a, u, h_ref):
    ts, d = a.shape
    ng = ts // SUBLANES
    a3 = a.reshape(ng, SUBLANES, d)
    u3 = u.reshape(ng, SUBLANES, d)
    sub = lax.broadcasted_iota(I32, (ng, SUBLANES, d), 1)
    for step in (1, 2, 4):
        keep = sub >= step
        a_prev = jnp.where(keep, pltpu.roll(a3, step, axis=1), 1.0)
        u_prev = jnp.where(keep, pltpu.roll(u3, step, axis=1), 0.0)
        u3 = a3 * u_prev + u3
        a3 = a3 * a_prev
    h = h_ref[...]
    outs = []
    for g in range(ng):
        hg = u3[g] + a3[g] * h
        outs.append(hg)
        h = hg[SUBLANES - 1:SUBLANES]
    h_ref[...] = h
    return jnp.concatenate(outs, axis=0)


def _mixer_kernel(x_ref, mod_ref, g1_ref, win_ref, convw_ref, convb_ref, wrg_ref,
                  brga_ref, brgx_ref, lam_ref, wpool_ref, pscale_ref, wupa_ref,
                  wupb_ref, wout_ref, o_ref,
                  cc1, cc2, cc3, pc1, pc2, pc4, pc8, h_c):
    s = pl.program_id(1)
    ts = x_ref.shape[1]
    d = D_MODEL

    @pl.when(s == 0)
    def _():
        for ref in (cc1, cc2, cc3, pc1, pc2, pc4, pc8, h_c):
            ref[...] = jnp.zeros(ref.shape, ref.dtype)

    x = x_ref[0]
    mod = mod_ref[0]
    shift1, scale1, gate1 = mod[0:1], mod[1:2], mod[2:3]
    ms = jnp.mean(x * x, axis=-1, keepdims=True)
    hn = x * lax.rsqrt(ms + NORM_EPS) * g1_ref[...]
    hb = (hn * (1.0 + scale1) + shift1).astype(BF16)

    def in_proj(lo, hi):
        return jnp.dot(hb, win_ref[:, lo:hi], preferred_element_type=F32)

    x_lru = in_proj(0, d)
    cw = convw_ref[...]
    xc = (convb_ref[...] + cw[3:4] * x_lru
          + cw[2:3] * _shift_rows(x_lru, 1, cc1)
          + cw[1:2] * _shift_rows(x_lru, 2, cc2)
          + cw[0:1] * _shift_rows(x_lru, 3, cc3))
    xcb = xc.astype(BF16)
    pre = [jnp.dot(xcb[:, k * HEAD_DIM:(k + 1) * HEAD_DIM], wrg_ref[k],
                   preferred_element_type=F32) for k in range(N_HEADS)]
    pre_a = jnp.concatenate([p[:, :HEAD_DIM] for p in pre], axis=-1)
    pre_x = jnp.concatenate([p[:, HEAD_DIM:] for p in pre], axis=-1)
    r = jax.nn.sigmoid(pre_a + brga_ref[...])
    gi = jax.nn.sigmoid(pre_x + brgx_ref[...])
    log_a = (-LRU_C) * r * jax.nn.softplus(-lam_ref[...])
    a = jnp.exp(log_a)
    mult = jnp.sqrt(jnp.tanh(-log_a) * (1.0 + a * a))
    u = mult * (gi * xc)
    h = _linear_scan(a, u, h_c)
    y_lru = h * jax.nn.gelu(in_proj(d, 2 * d))
    p_a = jnp.dot(y_lru.astype(BF16), wupa_ref[...], preferred_element_type=F32)
    gates_a = jax.nn.sigmoid(in_proj(2 * d + POOL_WIDTH, 3 * d + POOL_WIDTH))
    merged = gates_a * p_a

    x_pool = in_proj(2 * d, 2 * d + POOL_WIDTH)
    g = POOL_GROUP_DIM
    s2 = x_pool + _shift_rows(x_pool, 1, pc1)
    s4 = s2[:, g:] + _shift_rows(s2[:, g:], 2, pc2)
    s8 = s4[:, g:] + _shift_rows(s4[:, g:], 4, pc4)
    s16 = s8[:, g:] + _shift8_rows(s8[:, g:], pc8)
    wins = (s2[:, :g], s4[:, :g], s8[:, :g], s16)
    t1 = (s * ts + lax.broadcasted_iota(I32, (ts, 1), 0) + 1).astype(F32)
    pooled = []
    for k, w in enumerate(POOL_WINDOWS):
        cnt = jnp.minimum(t1, float(w))
        pk = wins[k] / cnt - x_pool[:, k * g:(k + 1) * g]
        pooled.append(jnp.dot(pk.astype(BF16), wpool_ref[k], preferred_element_type=F32))
    y_pool = jnp.concatenate(pooled, axis=-1) * pscale_ref[...]
    p_b = jnp.dot(y_pool.astype(BF16), wupb_ref[...], preferred_element_type=F32)
    gates_b = jax.nn.sigmoid(in_proj(3 * d + POOL_WIDTH, 4 * d + POOL_WIDTH))
    merged = merged + gates_b * p_b

    mix = jnp.dot(merged.astype(BF16), wout_ref[...], preferred_element_type=F32)
    o_ref[0] = x + gate1 * mix


def _mixer_layer(x, mod_l, g1, w_in, conv_w, conv_b, w_rg, b_rg_a, b_rg_x, lam,
                 w_pool, pool_scale, w_up_a, w_up_b, w_out):
    bsz, seq, d = x.shape
    ts = MIX_ROWS
    row = lambda v: v.reshape(1, -1)
    args = (x, mod_l, row(g1), w_in, conv_w, row(conv_b), w_rg, row(b_rg_a), row(b_rg_x),
            row(lam), w_pool, row(pool_scale), w_up_a, w_up_b, w_out)
    in_specs = [
        pl.BlockSpec((1, ts, d), lambda b, s: (b, s, 0)),
        pl.BlockSpec((1, 6, d), lambda b, s: (b, 0, 0)),
    ] + [_const_spec(a.shape) for a in args[2:]]
    g = POOL_GROUP_DIM
    scratch = [
        pltpu.VMEM((SUBLANES, d), F32), pltpu.VMEM((SUBLANES, d), F32),
        pltpu.VMEM((SUBLANES, d), F32),
        pltpu.VMEM((SUBLANES, 4 * g), F32), pltpu.VMEM((SUBLANES, 3 * g), F32),
        pltpu.VMEM((SUBLANES, 2 * g), F32), pltpu.VMEM((SUBLANES, g), F32),
        pltpu.VMEM((1, d), F32),
    ]
    return pl.pallas_call(
        _mixer_kernel,
        out_shape=jax.ShapeDtypeStruct(x.shape, F32),
        grid=(bsz, seq // ts),
        in_specs=in_specs,
        out_specs=pl.BlockSpec((1, ts, d), lambda b, s: (b, s, 0)),
        scratch_shapes=scratch,
        compiler_params=pltpu.CompilerParams(
            dimension_semantics=("arbitrary", "arbitrary"),
            vmem_limit_bytes=VMEM_LIMIT),
        name="mixer",
    )(*args)


def _router_kernel(x_ref, mod_ref, g2_ref, wr_ref, br_ref,
                   h_ref, idx_ref, pos_ref, wt_ref, cnt_ref, run_c):
    i = pl.program_id(0)
    tr = x_ref.shape[0]

    @pl.when(i == 0)
    def _():
        run_c[...] = jnp.zeros(run_c.shape, run_c.dtype)

    x = x_ref[...]
    mod = mod_ref[0]
    shift2, scale2 = mod[3:4], mod[4:5]
    ms = jnp.mean(x * x, axis=-1, keepdims=True)
    hn = x * lax.rsqrt(ms + NORM_EPS) * g2_ref[...]
    h = hn * (1.0 + scale2) + shift2
    h_ref[...] = _pack_rows(h)

    h_hi = h.astype(BF16)
    h_lo = (h - h_hi.astype(F32)).astype(BF16)
    w = wr_ref[...]
    w_hi = w.astype(BF16)
    w_lo = (w - w_hi.astype(F32)).astype(BF16)
    first = jnp.dot(h_hi, jnp.concatenate([w_hi, w_lo], axis=-1), preferred_element_type=F32)
    logits = (first[:, :N_EXPERTS] + first[:, N_EXPERTS:]
              + jnp.dot(h_lo, w_hi, preferred_element_type=F32) + br_ref[...])

    lane = lax.broadcasted_iota(I32, (tr, N_EXPERTS), 1)
    vals, idxs, hots = [], [], []
    cur = logits
    for _ in range(TOP_K):
        m = jnp.max(cur, axis=-1, keepdims=True)
        am = jnp.min(jnp.where(cur == m, lane, N_EXPERTS), axis=-1, keepdims=True)
        hot = lane == am
        vals.append(m)
        idxs.append(am)
        hots.append(hot)
        cur = jnp.where(hot, -jnp.inf, cur)
    exps = [jnp.exp(v - vals[0]) for v in vals]
    denom = exps[0] + exps[1] + exps[2] + exps[3]

    sel = (hots[0] | hots[1] | hots[2] | hots[3]).astype(F32)
    ri = lax.broadcasted_iota(I32, (tr, tr), 0)
    ci = lax.broadcasted_iota(I32, (tr, tr), 1)
    tri = (ci < ri).astype(BF16)
    before = jnp.dot(tri, sel.astype(BF16), preferred_element_type=F32) + run_c[...]
    run_c[...] = run_c[...] + jnp.sum(sel, axis=0, keepdims=True)
    cnt_ref[...] = run_c[...].astype(I32)

    lane4 = lax.broadcasted_iota(I32, (tr, TOP_K), 1)
    idx_o = jnp.zeros((tr, TOP_K), I32)
    pos_o = jnp.zeros((tr, TOP_K), I32)
    wt_o = jnp.zeros((tr, TOP_K), F32)
    for k in range(TOP_K):
        pk = jnp.sum(jnp.where(hots[k], before, 0.0), axis=-1, keepdims=True)
        idx_o = jnp.where(lane4 == k, idxs[k], idx_o)
        pos_o = jnp.where(lane4 == k, pk.astype(I32), pos_o)
        wt_o = jnp.where(lane4 == k, exps[k] / denom, wt_o)
    idx_ref[...] = idx_o
    pos_ref[...] = pos_o
    wt_ref[...] = wt_o


def _router(x2d, mod_l, g2, w_router, b_router, seq):
    n_tok, d = x2d.shape
    tr = ROUTE_ROWS
    per_b = seq // tr
    return pl.pallas_call(
        _router_kernel,
        out_shape=(
            jax.ShapeDtypeStruct((n_tok, PACKED), I32),
            jax.ShapeDtypeStruct((n_tok, TOP_K), I32),
            jax.ShapeDtypeStruct((n_tok, TOP_K), I32),
            jax.ShapeDtypeStruct((n_tok, TOP_K), F32),
            jax.ShapeDtypeStruct((1, N_EXPERTS), I32),
        ),
        grid=(n_tok // tr,),
        in_specs=[
            pl.BlockSpec((tr, d), lambda i: (i, 0)),
            pl.BlockSpec((1, 6, d), lambda i: (i // per_b, 0, 0)),
            _const_spec((1, d)),
            _const_spec((d, N_EXPERTS)),
            _const_spec((1, N_EXPERTS)),
        ],
        out_specs=(
            pl.BlockSpec((tr, PACKED), lambda i: (i, 0)),
            pl.BlockSpec((tr, TOP_K), lambda i: (i, 0)),
            pl.BlockSpec((tr, TOP_K), lambda i: (i, 0)),
            pl.BlockSpec((tr, TOP_K), lambda i: (i, 0)),
            pl.BlockSpec((1, N_EXPERTS), lambda i: (0, 0)),
        ),
        scratch_shapes=[pltpu.VMEM((1, N_EXPERTS), F32)],
        compiler_params=pltpu.CompilerParams(
            dimension_semantics=("arbitrary",), vmem_limit_bytes=VMEM_LIMIT),
        name="router",
    )(x2d, mod_l, g2.reshape(1, d), w_router, b_router.reshape(1, N_EXPERTS))


def _sc_mesh():
    return plsc.VectorSubcoreMesh(core_axis_name="c", subcore_axis_name="s")


def _sc_worker_id():
    return lax.axis_index("s") * SC_CORES + lax.axis_index("c")


def _sc_dispatch(h_packed, dest_sc, n_slots):
    n_tok, width = h_packed.shape
    per_w = n_tok // SC_WORKERS
    n_chunks = per_w // SC_CHUNK

    def body(h_hbm, dest_hbm, xs_hbm, idx_v, rows_v, sem_in, sem_out):
        base = _sc_worker_id() * per_w
        pltpu.sync_copy(dest_hbm.at[_sc_worker_id()], idx_v)

        def load(j, buf):
            return pltpu.make_async_copy(
                h_hbm.at[pl.ds(base + j * SC_CHUNK, SC_CHUNK)], rows_v.at[buf], sem_in)

        load(0, 0).start()

        @pl.loop(0, n_chunks, step=2)
        def _(j0):
            for buf in range(2):
                j = j0 + buf
                load(j, buf).wait()

                @pl.when(j + 1 < n_chunks)
                def _():
                    load(j + 1, 1 - buf).start()

                copies = [
                    pltpu.make_async_copy(
                        rows_v.at[buf], xs_hbm.at[idx_v.at[j * TOP_K + k]], sem_out)
                    for k in range(TOP_K)]
                for cp in copies:
                    cp.start()
                for cp in copies:
                    cp.wait()

    return pl.kernel(
        body,
        out_type=jax.ShapeDtypeStruct((n_slots, width), I32),
        mesh=_sc_mesh(),
        scratch_types=[
            pltpu.VMEM((n_chunks * TOP_K, SC_CHUNK), I32),
            pltpu.VMEM((2, SC_CHUNK, width), I32),
            pltpu.SemaphoreType.DMA,
            pltpu.SemaphoreType.DMA,
        ],
        name="sc_dispatch",
    )(h_packed, dest_sc)


def _sc_gather(ys_packed, dest_sc):
    width = ys_packed.shape[1]
    n_chunks = dest_sc.shape[1]
    per_w = n_chunks * SC_CHUNK
    n_rows = SC_WORKERS * per_w

    def body(ys_hbm, dest_hbm, out_hbm, idx_v, rows_v, sem_g, sem_w):
        base = _sc_worker_id() * per_w
        pltpu.sync_copy(dest_hbm.at[_sc_worker_id()], idx_v)

        def gather(j, buf):
            return pltpu.make_async_copy(ys_hbm.at[idx_v.at[j]], rows_v.at[buf], sem_g)

        def writeout(j, buf):
            return pltpu.make_async_copy(
                rows_v.at[buf], out_hbm.at[pl.ds(base + j * SC_CHUNK, SC_CHUNK)], sem_w)

        gather(0, 0).start()

        @pl.loop(0, n_chunks, step=2)
        def _(j0):
            for buf in range(2):
                j = j0 + buf
                gather(j, buf).wait()

                @pl.when(j >= 1)
                def _():
                    writeout(j - 1, 1 - buf).wait()

                @pl.when(j + 1 < n_chunks)
                def _():
                    gather(j + 1, 1 - buf).start()

                writeout(j, buf).start()

        writeout(n_chunks - 1, 1).wait()

    return pl.kernel(
        body,
        out_type=jax.ShapeDtypeStruct((n_rows, width), I32),
        mesh=_sc_mesh(),
        scratch_types=[
            pltpu.VMEM((n_chunks, SC_CHUNK), I32),
            pltpu.VMEM((2, SC_CHUNK, width), I32),
            pltpu.SemaphoreType.DMA,
            pltpu.SemaphoreType.DMA,
        ],
        name="sc_gather",
    )(ys_packed, dest_sc)


def _expert_kernel(be_ref, nb_ref, xs_ref, win_ref, bin_ref, wout_ref, bout_ref, ys_ref,
                   win_bf, wout_bf):
    i = pl.program_id(0)
    prev = be_ref[jnp.maximum(i - 1, 0)]

    @pl.when((i == 0) | (be_ref[i] != prev))
    def _():
        chunk = 128

        def cast_in(r, c):
            rows = pl.ds(pl.multiple_of(r * chunk, chunk), chunk)
            win_bf[rows, :] = win_ref[0, rows, :].astype(BF16)
            return c

        def cast_out(r, c):
            rows = pl.ds(pl.multiple_of(r * chunk, chunk), chunk)
            wout_bf[rows, :] = wout_ref[0, rows, :].astype(BF16)
            return c

        lax.fori_loop(0, D_MODEL // chunk, cast_in, 0)
        lax.fori_loop(0, D_FF // chunk, cast_out, 0)

    @pl.when(i < nb_ref[0])
    def _():
        lo, hi = _unpack_rows(xs_ref[...])
        gu = (jnp.dot(lo.astype(BF16), win_bf[:PACKED, :], preferred_element_type=F32)
              + jnp.dot(hi.astype(BF16), win_bf[PACKED:, :], preferred_element_type=F32)
              + bin_ref[0])
        gate = jnp.minimum(gu[:, :D_FF], SWIGLU_LIMIT)
        up = jnp.clip(gu[:, D_FF:], -SWIGLU_LIMIT, SWIGLU_LIMIT)
        act = gate * jax.nn.sigmoid(SWIGLU_ALPHA * gate) * (up + 1.0)
        out = jnp.dot(act.astype(BF16), wout_bf[...], preferred_element_type=F32) + bout_ref[0]
        ys_ref[...] = _pack_rows(out)

    @pl.when(i >= nb_ref[0])
    def _():
        ys_ref[...] = jnp.zeros(ys_ref.shape, ys_ref.dtype)


def _experts(layer, block_e, n_used, xs, w_e_in, b_e_in, w_e_out, b_e_out):
    n_slots, width = xs.shape
    d = D_MODEL
    tm = EXPERT_ROWS
    grid_spec = pltpu.PrefetchScalarGridSpec(
        num_scalar_prefetch=2,
        grid=(n_slots // tm,),
        in_specs=[
            pl.BlockSpec((tm, width), lambda i, be, nb: (i, 0)),
            pl.BlockSpec((None, 1, d, 2 * D_FF), lambda i, be, nb: (layer, be[i], 0, 0)),
            pl.BlockSpec((None, 1, 1, 2 * D_FF), lambda i, be, nb: (layer, be[i], 0, 0)),
            pl.BlockSpec((None, 1, D_FF, d), lambda i, be, nb: (layer, be[i], 0, 0)),
            pl.BlockSpec((None, 1, 1, d), lambda i, be, nb: (layer, be[i], 0, 0)),
        ],
        out_specs=pl.BlockSpec((tm, width), lambda i, be, nb: (i, 0)),
        scratch_shapes=[pltpu.VMEM((d, 2 * D_FF), BF16), pltpu.VMEM((D_FF, d), BF16)],
    )
    return pl.pallas_call(
        _expert_kernel,
        out_shape=jax.ShapeDtypeStruct((n_slots, width), I32),
        grid_spec=grid_spec,
        compiler_params=pltpu.CompilerParams(
            dimension_semantics=("arbitrary",), vmem_limit_bytes=VMEM_LIMIT),
        name="experts",
    )(block_e, n_used, xs, w_e_in, b_e_in.reshape(-1, N_EXPERTS, 1, 2 * D_FF),
      w_e_out, b_e_out.reshape(-1, N_EXPERTS, 1, d))


def _combine_kernel(x_ref, y4_ref, wt_ref, mod_ref, o_ref):
    wt = wt_ref[...]
    acc_lo = jnp.zeros((x_ref.shape[0], PACKED), F32)
    acc_hi = jnp.zeros((x_ref.shape[0], PACKED), F32)
    for k in range(TOP_K):
        lo, hi = _unpack_rows(y4_ref[:, k * PACKED:(k + 1) * PACKED])
        wk = wt[:, k:k + 1]
        acc_lo = acc_lo + wk * lo
        acc_hi = acc_hi + wk * hi
    ffn = jnp.concatenate([acc_lo, acc_hi], axis=-1)
    gate2 = mod_ref[0][5:6]
    o_ref[...] = x_ref[...] + gate2 * ffn


def _combine(x2d, y4, wts, mod_l, seq):
    n_tok, d = x2d.shape
    tc = COMBINE_ROWS
    per_b = seq // tc
    return pl.pallas_call(
        _combine_kernel,
        out_shape=jax.ShapeDtypeStruct((n_tok, d), F32),
        grid=(n_tok // tc,),
        in_specs=[
            pl.BlockSpec((tc, d), lambda i: (i, 0)),
            pl.BlockSpec((tc, TOP_K * PACKED), lambda i: (i, 0)),
            pl.BlockSpec((tc, TOP_K), lambda i: (i, 0)),
            pl.BlockSpec((1, 6, d), lambda i: (i // per_b, 0, 0)),
        ],
        out_specs=pl.BlockSpec((tc, d), lambda i: (i, 0)),
        compiler_params=pltpu.CompilerParams(
            dimension_semantics=("arbitrary",), vmem_limit_bytes=VMEM_LIMIT),
        name="combine",
    )(x2d, y4, wts, mod_l)


def _final_norm_kernel(x_ref, g_ref, o_ref):
    x = x_ref[...]
    ms = jnp.mean(x * x, axis=-1, keepdims=True)
    o_ref[...] = x * lax.rsqrt(ms + NORM_EPS) * g_ref[...]


def _final_norm(x2d, g):
    n_tok, d = x2d.shape
    tr = 1024
    return pl.pallas_call(
        _final_norm_kernel,
        out_shape=jax.ShapeDtypeStruct((n_tok, d), F32),
        grid=(n_tok // tr,),
        in_specs=[pl.BlockSpec((tr, d), lambda i: (i, 0)), _const_spec((1, d))],
        out_specs=pl.BlockSpec((tr, d), lambda i: (i, 0)),
        compiler_params=pltpu.CompilerParams(
            dimension_semantics=("arbitrary",), vmem_limit_bytes=VMEM_LIMIT),
        name="final_norm",
    )(x2d, g.reshape(1, d))


def _moe_layer(layer, x, mod_l, g2, w_router, b_router, w_e_in, b_e_in, w_e_out, b_e_out):
    bsz, seq, d = x.shape
    n_tok = bsz * seq
    tm = EXPERT_ROWS
    x2d = x.reshape(n_tok, d)
    h_packed, idx, pos, wts, counts = _router(x2d, mod_l, g2, w_router, b_router, seq)

    counts = counts[0]
    padded = ((counts + tm - 1) // tm) * tm
    pad_end = jnp.cumsum(padded)
    start_pad = pad_end - padded
    experts = jnp.arange(N_EXPERTS, dtype=I32)
    dest = jnp.sum(jnp.where(idx[:, :, None] == experts, start_pad, 0), axis=-1) + pos
    n_blocks = (n_tok * TOP_K) // tm + N_EXPERTS
    n_slots = n_blocks * tm
    block_start = jnp.arange(n_blocks, dtype=I32) * tm
    block_e = jnp.minimum(
        jnp.sum((block_start[:, None] >= pad_end[None, :]).astype(I32), axis=-1),
        N_EXPERTS - 1)
    n_used = (pad_end[-1] // tm).astype(I32).reshape(1)

    per_w = n_tok // SC_WORKERS
    dest_scatter = (dest.reshape(SC_WORKERS, per_w // SC_CHUNK, SC_CHUNK, TOP_K)
                    .transpose(0, 1, 3, 2)
                    .reshape(SC_WORKERS, (per_w // SC_CHUNK) * TOP_K, SC_CHUNK))
    xs = _sc_dispatch(h_packed, dest_scatter, n_slots)
    ys = _experts(layer, block_e, n_used, xs, w_e_in, b_e_in, w_e_out, b_e_out)
    dest_gather = dest.reshape(SC_WORKERS, (per_w * TOP_K) // SC_CHUNK, SC_CHUNK)
    y4 = _sc_gather(ys, dest_gather).reshape(n_tok, TOP_K * PACKED)
    return _combine(x2d, y4, wts, mod_l, seq).reshape(bsz, seq, d)


def kernel(x, c, norm1_g, norm2_g, w_ada, b_ada, w_in, conv_w, conv_b, w_rg_a, b_rg_a,
           w_rg_x, b_rg_x, lru_lambda, w_pool, pool_scale, w_up_a, w_up_b, w_out,
           w_router, b_router, w_e_in, b_e_in, w_e_out, b_e_out, final_g):
    depth = w_in.shape[0]
    bsz, seq, d = x.shape
    mod = _ada_mod(c, w_ada, b_ada).reshape(depth, bsz, 6, d)
    w_rg = jnp.concatenate([w_rg_a, w_rg_x], axis=-1).astype(BF16)
    for l in range(depth):
        x = _mixer_layer(x, mod[l], norm1_g[l], w_in[l].astype(BF16), conv_w[l], conv_b[l],
                         w_rg[l], b_rg_a[l], b_rg_x[l], lru_lambda[l],
                         w_pool[l].astype(BF16), pool_scale[l], w_up_a[l].astype(BF16),
                         w_up_b[l].astype(BF16), w_out[l].astype(BF16))
        x = _moe_layer(l, x, mod[l], norm2_g[l], w_router[l], b_router[l],
                       w_e_in, b_e_in, w_e_out, b_e_out)
    return _final_norm(x.reshape(bsz * seq, d), final_g).reshape(bsz, seq, d)
```

```python
import math

import jax
import jax.numpy as jnp
from jax import lax
from jax.experimental import pallas as pl
from jax.experimental.pallas import tpu as pltpu
from jax.experimental.pallas import tpu_sc as plsc

D_MODEL = 1024
BATCH = 8
N_HEADS = 8
HEAD_DIM = 128
CONV_WIDTH = 4
LRU_C = 8.0
POOL_WINDOWS = (2, 4, 8, 16)
POOL_WIDTH = 512
POOL_GROUP_DIM = 128
N_EXPERTS = 32
TOP_K = 4
D_FF = 1024
SWIGLU_LIMIT = 7.0
SWIGLU_ALPHA = 1.702
NORM_EPS = 1e-6

SUBLANES = 8
VMEM_LIMIT = 56 * 1024 * 1024
SC_CORES = 2
SC_SUBCORES = 16
SC_WORKERS = SC_CORES * SC_SUBCORES
SC_CHUNK = 64

MIX_STEPS = 32
ROUTE_ROWS = 512
EXPERT_ROWS = 256
COMBINE_ROWS = 512
PACKED = D_MODEL // 2

F32 = jnp.float32
BF16 = jnp.bfloat16
I32 = jnp.int32
HIGH_MASK = -65536
LOG2_E = math.log2(math.e)


def _const_spec(shape):
    nd = len(shape)
    return pl.BlockSpec(shape, lambda *_: (0,) * nd, pipeline_mode=pl.Buffered(1))


def _pack_rows(v):
    bits = lax.bitcast_convert_type(v.astype(BF16).astype(F32), I32)
    lo, hi = bits[:, :PACKED], bits[:, PACKED:]
    return (hi & HIGH_MASK) | lax.shift_right_logical(lo, 16)


def _unpack_rows(p):
    lo = lax.bitcast_convert_type(lax.shift_left(p, 16), F32)
    hi = lax.bitcast_convert_type(p & HIGH_MASK, F32)
    return lo, hi


def _sigmoid(v):
    return 0.5 * jnp.tanh(0.5 * v) + 0.5


def _per_batch(v, slab, op):
    rows, d = v.shape
    return op(v.reshape(rows // BATCH, BATCH, d), slab[None]).reshape(rows, d)


def _modulated_norm(x, g, shift, scale):
    ms = jnp.mean(x * x, axis=-1, keepdims=True)
    hn = x * lax.rsqrt(ms + NORM_EPS) * g
    return _per_batch(_per_batch(hn, 1.0 + scale, jnp.multiply), shift, jnp.add)


def _ada_kernel(c_ref, w_ref, b_ref, o_ref):
    c = c_ref[...]
    c_act = c * jax.nn.sigmoid(c)
    o_ref[0] = jnp.dot(c_act, w_ref[0], preferred_element_type=F32,
                       precision=lax.Precision.HIGHEST) + b_ref[0]


def _ada_mod(c, w_ada, b_ada):
    depth, d, n = w_ada.shape
    bsz = c.shape[0]
    tn = 1536
    return pl.pallas_call(
        _ada_kernel,
        out_shape=jax.ShapeDtypeStruct((depth, bsz, n), F32),
        grid=(depth, n // tn),
        in_specs=[
            pl.BlockSpec((bsz, d), lambda l, j: (0, 0)),
            pl.BlockSpec((1, d, tn), lambda l, j: (l, 0, j)),
            pl.BlockSpec((1, 1, tn), lambda l, j: (l, 0, j)),
        ],
        out_specs=pl.BlockSpec((1, bsz, tn), lambda l, j: (l, 0, j)),
        compiler_params=pltpu.CompilerParams(
            dimension_semantics=("arbitrary", "arbitrary"),
            vmem_limit_bytes=VMEM_LIMIT),
        name="ada_mod",
    )(c, w_ada, b_ada.reshape(depth, 1, n))


def _delay(a, steps, carry_ref):
    n = steps * BATCH
    keep = carry_ref.shape[0]
    prev = carry_ref[keep - n:, :]
    return jnp.concatenate([prev, a[:a.shape[0] - n]], axis=0)


def _save_tail(a, carry_ref):
    carry_ref[...] = a[a.shape[0] - carry_ref.shape[0]:]


def _mixer_kernel(x_ref, mod_ref, g1_ref, win_ref, convw_ref, convb_ref, wrg_ref,
                  brga_ref, brgx_ref, lam_ref, wpool_ref, pscale_ref, wupa_ref,
                  wupb_ref, wout_ref, o_ref,
                  conv_c, p1_c, p2_c, p4_c, p8_c, h_c):
    i = pl.program_id(0)
    rows = x_ref.shape[0]
    steps = rows // BATCH
    d = D_MODEL

    @pl.when(i == 0)
    def _():
        for ref in (conv_c, p1_c, p2_c, p4_c, p8_c, h_c):
            ref[...] = jnp.zeros(ref.shape, ref.dtype)

    x = x_ref[...]
    shift1, scale1, gate1 = mod_ref[0], mod_ref[1], mod_ref[2]
    hb = _modulated_norm(x, g1_ref[...], shift1, scale1).astype(BF16)

    def in_proj(lo, hi):
        return jnp.dot(hb, win_ref[:, lo:hi], preferred_element_type=F32)

    x_lru = in_proj(0, d)
    cw = convw_ref[...]
    xc = (convb_ref[...] + cw[3:4] * x_lru
          + cw[2:3] * _delay(x_lru, 1, conv_c)
          + cw[1:2] * _delay(x_lru, 2, conv_c)
          + cw[0:1] * _delay(x_lru, 3, conv_c))
    _save_tail(x_lru, conv_c)
    xcb = xc.astype(BF16)
    pre = [jnp.dot(xcb[:, k * HEAD_DIM:(k + 1) * HEAD_DIM], wrg_ref[k],
                   preferred_element_type=F32) for k in range(N_HEADS)]
    pre_a = jnp.concatenate([p[:, :HEAD_DIM] for p in pre], axis=-1)
    pre_x = jnp.concatenate([p[:, HEAD_DIM:] for p in pre], axis=-1)
    r = _sigmoid(pre_a + brga_ref[...])
    gi = _sigmoid(pre_x + brgx_ref[...])
    rate = (-LRU_C * LOG2_E) * jax.nn.softplus(-lam_ref[...])
    a = jnp.exp2(r * rate)
    y = 1.0 - a * a
    mult = jnp.where(y > 0.0, y * lax.rsqrt(y), 0.0)
    u = mult * (gi * xc)
    h = h_c[...]
    hs = []
    for t in range(steps):
        sl = slice(t * BATCH, (t + 1) * BATCH)
        h = a[sl] * h + u[sl]
        hs.append(h)
    h_c[...] = h
    y_lru = jnp.concatenate(hs, axis=0) * jax.nn.gelu(in_proj(d, 2 * d))
    p_a = jnp.dot(y_lru.astype(BF16), wupa_ref[...], preferred_element_type=F32)
    merged = _sigmoid(in_proj(2 * d + POOL_WIDTH, 3 * d + POOL_WIDTH)) * p_a

    x_pool = in_proj(2 * d, 2 * d + POOL_WIDTH)
    g = POOL_GROUP_DIM
    s2 = x_pool + _delay(x_pool, 1, p1_c)
    s4 = s2[:, g:] + _delay(s2[:, g:], 2, p2_c)
    s8 = s4[:, g:] + _delay(s4[:, g:], 4, p4_c)
    s16 = s8[:, g:] + _delay(s8[:, g:], 8, p8_c)
    _save_tail(x_pool, p1_c)
    _save_tail(s2[:, g:], p2_c)
    _save_tail(s4[:, g:], p4_c)
    _save_tail(s8[:, g:], p8_c)
    wins = (s2[:, :g], s4[:, :g], s8[:, :g], s16)
    row = lax.broadcasted_iota(I32, (rows, 1), 0)
    t1 = (i * steps + row // BATCH + 1).astype(F32)
    pooled = []
    for k, w in enumerate(POOL_WINDOWS):
        cnt = jnp.minimum(t1, float(w))
        pk = wins[k] / cnt - x_pool[:, k * g:(k + 1) * g]
        pooled.append(jnp.dot(pk.astype(BF16), wpool_ref[k], preferred_element_type=F32))
    y_pool = jnp.concatenate(pooled, axis=-1) * pscale_ref[...]
    p_b = jnp.dot(y_pool.astype(BF16), wupb_ref[...], preferred_element_type=F32)
    merged = merged + _sigmoid(in_proj(3 * d + POOL_WIDTH, 4 * d + POOL_WIDTH)) * p_b

    mix = jnp.dot(merged.astype(BF16), wout_ref[...], preferred_element_type=F32)
    o_ref[...] = x + _per_batch(mix, gate1, jnp.multiply)


def _mixer_layer(x2d, mod_l, g1, w_in, conv_w, conv_b, w_rg, b_rg_a, b_rg_x, lam,
                 w_pool, pool_scale, w_up_a, w_up_b, w_out):
    n_tok, d = x2d.shape
    rows = MIX_STEPS * BATCH
    row = lambda v: v.reshape(1, -1)
    args = (x2d, mod_l, row(g1), w_in, conv_w, row(conv_b), w_rg, row(b_rg_a), row(b_rg_x),
            row(lam), w_pool, row(pool_scale), w_up_a, w_up_b, w_out)
    in_specs = [pl.BlockSpec((rows, d), lambda i: (i, 0))]
    in_specs += [_const_spec(a.shape) for a in args[1:]]
    g = POOL_GROUP_DIM
    scratch = [
        pltpu.VMEM(((CONV_WIDTH - 1) * BATCH, d), F32),
        pltpu.VMEM((1 * BATCH, 4 * g), F32), pltpu.VMEM((2 * BATCH, 3 * g), F32),
        pltpu.VMEM((4 * BATCH, 2 * g), F32), pltpu.VMEM((8 * BATCH, g), F32),
        pltpu.VMEM((BATCH, d), F32),
    ]
    return pl.pallas_call(
        _mixer_kernel,
        out_shape=jax.ShapeDtypeStruct(x2d.shape, F32),
        grid=(n_tok // rows,),
        in_specs=in_specs,
        out_specs=pl.BlockSpec((rows, d), lambda i: (i, 0)),
        scratch_shapes=scratch,
        compiler_params=pltpu.CompilerParams(
            dimension_semantics=("arbitrary",), vmem_limit_bytes=VMEM_LIMIT),
        name="mixer",
    )(*args)


def _router_kernel(x_ref, mod_ref, g2_ref, wr_ref, br_ref,
                   h_ref, idx_ref, pos_ref, wt_ref, cnt_ref, run_c):
    i = pl.program_id(0)
    tr = x_ref.shape[0]

    @pl.when(i == 0)
    def _():
        run_c[...] = jnp.zeros(run_c.shape, run_c.dtype)

    h = _modulated_norm(x_ref[...], g2_ref[...], mod_ref[3], mod_ref[4])
    h_ref[...] = _pack_rows(h)

    h_hi = h.astype(BF16)
    h_lo = (h - h_hi.astype(F32)).astype(BF16)
    w = wr_ref[...]
    w_hi = w.astype(BF16)
    w_lo = (w - w_hi.astype(F32)).astype(BF16)
    first = jnp.dot(h_hi, jnp.concatenate([w_hi, w_lo], axis=-1), preferred_element_type=F32)
    logits = (first[:, :N_EXPERTS] + first[:, N_EXPERTS:]
              + jnp.dot(h_lo, w_hi, preferred_element_type=F32) + br_ref[...])

    lane = lax.broadcasted_iota(I32, (tr, N_EXPERTS), 1)
    vals, idxs, hots = [], [], []
    cur = logits
    for _ in range(TOP_K):
        m = jnp.max(cur, axis=-1, keepdims=True)
        am = jnp.min(jnp.where(cur == m, lane, N_EXPERTS), axis=-1, keepdims=True)
        hot = lane == am
        vals.append(m)
        idxs.append(am)
        hots.append(hot)
        cur = jnp.where(hot, -jnp.inf, cur)
    exps = [jnp.exp(v - vals[0]) for v in vals]
    denom = exps[0] + exps[1] + exps[2] + exps[3]

    sel = (hots[0] | hots[1] | hots[2] | hots[3]).astype(F32)
    ri = lax.broadcasted_iota(I32, (tr, tr), 0)
    ci = lax.broadcasted_iota(I32, (tr, tr), 1)
    tri = (ci < ri).astype(BF16)
    before = jnp.dot(tri, sel.astype(BF16), preferred_element_type=F32) + run_c[...]
    run_c[...] = run_c[...] + jnp.sum(sel, axis=0, keepdims=True)
    cnt_ref[...] = run_c[...].astype(I32)

    lane4 = lax.broadcasted_iota(I32, (tr, TOP_K), 1)
    idx_o = jnp.zeros((tr, TOP_K), I32)
    pos_o = jnp.zeros((tr, TOP_K), I32)
    wt_o = jnp.zeros((tr, TOP_K), F32)
    for k in range(TOP_K):
        pk = jnp.sum(jnp.where(hots[k], before, 0.0), axis=-1, keepdims=True)
        idx_o = jnp.where(lane4 == k, idxs[k], idx_o)
        pos_o = jnp.where(lane4 == k, pk.astype(I32), pos_o)
        wt_o = jnp.where(lane4 == k, exps[k] / denom, wt_o)
    idx_ref[...] = idx_o
    pos_ref[...] = pos_o
    wt_ref[...] = wt_o


def _router(x2d, mod_l, g2, w_router, b_router):
    n_tok, d = x2d.shape
    tr = ROUTE_ROWS
    return pl.pallas_call(
        _router_kernel,
        out_shape=(
            jax.ShapeDtypeStruct((n_tok, PACKED), I32),
            jax.ShapeDtypeStruct((n_tok, TOP_K), I32),
            jax.ShapeDtypeStruct((n_tok, TOP_K), I32),
            jax.ShapeDtypeStruct((n_tok, TOP_K), F32),
            jax.ShapeDtypeStruct((1, N_EXPERTS), I32),
        ),
        grid=(n_tok // tr,),
        in_specs=[
            pl.BlockSpec((tr, d), lambda i: (i, 0)),
            _const_spec(mod_l.shape),
            _const_spec((1, d)),
            _const_spec((d, N_EXPERTS)),
            _const_spec((1, N_EXPERTS)),
        ],
        out_specs=(
            pl.BlockSpec((tr, PACKED), lambda i: (i, 0)),
            pl.BlockSpec((tr, TOP_K), lambda i: (i, 0)),
            pl.BlockSpec((tr, TOP_K), lambda i: (i, 0)),
            pl.BlockSpec((tr, TOP_K), lambda i: (i, 0)),
            pl.BlockSpec((1, N_EXPERTS), lambda i: (0, 0)),
        ),
        scratch_shapes=[pltpu.VMEM((1, N_EXPERTS), F32)],
        compiler_params=pltpu.CompilerParams(
            dimension_semantics=("arbitrary",), vmem_limit_bytes=VMEM_LIMIT),
        name="router",
    )(x2d, mod_l, g2.reshape(1, d), w_router, b_router.reshape(1, N_EXPERTS))


def _sc_mesh():
    return plsc.VectorSubcoreMesh(core_axis_name="c", subcore_axis_name="s")


def _sc_worker_id():
    return lax.axis_index("s") * SC_CORES + lax.axis_index("c")


def _sc_dispatch(h_packed, dest_sc, n_slots):
    n_tok, width = h_packed.shape
    per_w = n_tok // SC_WORKERS
    n_chunks = per_w // SC_CHUNK

    def body(h_hbm, dest_hbm, xs_hbm, idx_v, rows_v, sem_in, sem_out):
        base = _sc_worker_id() * per_w
        pltpu.sync_copy(dest_hbm.at[_sc_worker_id()], idx_v)

        def load(j, buf):
            return pltpu.make_async_copy(
                h_hbm.at[pl.ds(base + j * SC_CHUNK, SC_CHUNK)], rows_v.at[buf], sem_in)

        load(0, 0).start()

        @pl.loop(0, n_chunks, step=2)
        def _(j0):
            for buf in range(2):
                j = j0 + buf
                load(j, buf).wait()

                @pl.when(j + 1 < n_chunks)
                def _():
                    load(j + 1, 1 - buf).start()

                copies = [
                    pltpu.make_async_copy(
                        rows_v.at[buf], xs_hbm.at[idx_v.at[j * TOP_K + k]], sem_out)
                    for k in range(TOP_K)]
                for cp in copies:
                    cp.start()
                for cp in copies:
                    cp.wait()

    return pl.kernel(
        body,
        out_type=jax.ShapeDtypeStruct((n_slots, width), I32),
        mesh=_sc_mesh(),
        scratch_types=[
            pltpu.VMEM((n_chunks * TOP_K, SC_CHUNK), I32),
            pltpu.VMEM((2, SC_CHUNK, width), I32),
            pltpu.SemaphoreType.DMA,
            pltpu.SemaphoreType.DMA,
        ],
        name="sc_dispatch",
    )(h_packed, dest_sc)


def _sc_gather(ys_packed, dest_sc):
    width = ys_packed.shape[1]
    n_chunks = dest_sc.shape[1]
    per_w = n_chunks * SC_CHUNK
    n_rows = SC_WORKERS * per_w

    def body(ys_hbm, dest_hbm, out_hbm, idx_v, rows_v, sem_g, sem_w):
        base = _sc_worker_id() * per_w
        pltpu.sync_copy(dest_hbm.at[_sc_worker_id()], idx_v)

        def gather(j, buf):
            return pltpu.make_async_copy(ys_hbm.at[idx_v.at[j]], rows_v.at[buf], sem_g)

        def writeout(j, buf):
            return pltpu.make_async_copy(
                rows_v.at[buf], out_hbm.at[pl.ds(base + j * SC_CHUNK, SC_CHUNK)], sem_w)

        gather(0, 0).start()

        @pl.loop(0, n_chunks, step=2)
        def _(j0):
            for buf in range(2):
                j = j0 + buf
                gather(j, buf).wait()

                @pl.when(j >= 1)
                def _():
                    writeout(j - 1, 1 - buf).wait()

                @pl.when(j + 1 < n_chunks)
                def _():
                    gather(j + 1, 1 - buf).start()

                writeout(j, buf).start()

        writeout(n_chunks - 1, 1).wait()

    return pl.kernel(
        body,
        out_type=jax.ShapeDtypeStruct((n_rows, width), I32),
        mesh=_sc_mesh(),
        scratch_types=[
            pltpu.VMEM((n_chunks, SC_CHUNK), I32),
            pltpu.VMEM((2, SC_CHUNK, width), I32),
            pltpu.SemaphoreType.DMA,
            pltpu.SemaphoreType.DMA,
        ],
        name="sc_gather",
    )(ys_packed, dest_sc)


def _expert_kernel(be_ref, nb_ref, xs_ref, win_ref, bin_ref, wout_ref, bout_ref, ys_ref,
                   win_bf, wout_bf):
    i = pl.program_id(0)
    prev = be_ref[jnp.maximum(i - 1, 0)]

    @pl.when((i == 0) | (be_ref[i] != prev))
    def _():
        chunk = 128

        def cast_in(r, c):
            rows = pl.ds(pl.multiple_of(r * chunk, chunk), chunk)
            win_bf[rows, :] = win_ref[0, rows, :].astype(BF16)
            return c

        def cast_out(r, c):
            rows = pl.ds(pl.multiple_of(r * chunk, chunk), chunk)
            wout_bf[rows, :] = wout_ref[0, rows, :].astype(BF16)
            return c

        lax.fori_loop(0, D_MODEL // chunk, cast_in, 0)
        lax.fori_loop(0, D_FF // chunk, cast_out, 0)

    @pl.when(i < nb_ref[0])
    def _():
        lo, hi = _unpack_rows(xs_ref[...])
        gu = (jnp.dot(lo.astype(BF16), win_bf[:PACKED, :], preferred_element_type=F32)
              + jnp.dot(hi.astype(BF16), win_bf[PACKED:, :], preferred_element_type=F32)
              + bin_ref[0])
        gate = jnp.minimum(gu[:, :D_FF], SWIGLU_LIMIT)
        up = jnp.clip(gu[:, D_FF:], -SWIGLU_LIMIT, SWIGLU_LIMIT)
        act = gate * _sigmoid(SWIGLU_ALPHA * gate) * (up + 1.0)
        out = jnp.dot(act.astype(BF16), wout_bf[...], preferred_element_type=F32) + bout_ref[0]
        ys_ref[...] = _pack_rows(out)

    @pl.when(i >= nb_ref[0])
    def _():
        ys_ref[...] = jnp.zeros(ys_ref.shape, ys_ref.dtype)


def _experts(layer, block_e, n_used, xs, w_e_in, b_e_in, w_e_out, b_e_out):
    n_slots, width = xs.shape
    d = D_MODEL
    tm = EXPERT_ROWS
    grid_spec = pltpu.PrefetchScalarGridSpec(
        num_scalar_prefetch=2,
        grid=(n_slots // tm,),
        in_specs=[
            pl.BlockSpec((tm, width), lambda i, be, nb: (i, 0)),
            pl.BlockSpec((None, 1, d, 2 * D_FF), lambda i, be, nb: (layer, be[i], 0, 0)),
            pl.BlockSpec((None, 1, 1, 2 * D_FF), lambda i, be, nb: (layer, be[i], 0, 0)),
            pl.BlockSpec((None, 1, D_FF, d), lambda i, be, nb: (layer, be[i], 0, 0)),
            pl.BlockSpec((None, 1, 1, d), lambda i, be, nb: (layer, be[i], 0, 0)),
        ],
        out_specs=pl.BlockSpec((tm, width), lambda i, be, nb: (i, 0)),
        scratch_shapes=[pltpu.VMEM((d, 2 * D_FF), BF16), pltpu.VMEM((D_FF, d), BF16)],
    )
    return pl.pallas_call(
        _expert_kernel,
        out_shape=jax.ShapeDtypeStruct((n_slots, width), I32),
        grid_spec=grid_spec,
        compiler_params=pltpu.CompilerParams(
            dimension_semantics=("arbitrary",), vmem_limit_bytes=VMEM_LIMIT),
        name="experts",
    )(block_e, n_used, xs, w_e_in, b_e_in.reshape(-1, N_EXPERTS, 1, 2 * D_FF),
      w_e_out, b_e_out.reshape(-1, N_EXPERTS, 1, d))


def _combine_kernel(x_ref, y0_ref, y1_ref, y2_ref, y3_ref, wt_ref, mod_ref, o_ref):
    wt = wt_ref[...]
    acc_lo = jnp.zeros((x_ref.shape[0], PACKED), F32)
    acc_hi = jnp.zeros((x_ref.shape[0], PACKED), F32)
    for k, y_ref in enumerate((y0_ref, y1_ref, y2_ref, y3_ref)):
        lo, hi = _unpack_rows(y_ref[...])
        wk = wt[:, k:k + 1]
        acc_lo = acc_lo + wk * lo
        acc_hi = acc_hi + wk * hi
    ffn = jnp.concatenate([acc_lo, acc_hi], axis=-1)
    o_ref[...] = x_ref[...] + _per_batch(ffn, mod_ref[5], jnp.multiply)


def _combine(x2d, y4, wts, mod_l):
    n_tok, d = x2d.shape
    tc = COMBINE_ROWS
    nb = n_tok // tc
    y_specs = [pl.BlockSpec((tc, PACKED), _kmajor_index(k, nb)) for k in range(TOP_K)]
    return pl.pallas_call(
        _combine_kernel,
        out_shape=jax.ShapeDtypeStruct((n_tok, d), F32),
        grid=(nb,),
        in_specs=[pl.BlockSpec((tc, d), lambda i: (i, 0))] + y_specs + [
            pl.BlockSpec((tc, TOP_K), lambda i: (i, 0)),
            _const_spec(mod_l.shape),
        ],
        out_specs=pl.BlockSpec((tc, d), lambda i: (i, 0)),
        compiler_params=pltpu.CompilerParams(
            dimension_semantics=("arbitrary",), vmem_limit_bytes=VMEM_LIMIT),
        name="combine",
    )(x2d, y4, y4, y4, y4, wts, mod_l)


def _kmajor_index(k, nb):
    return lambda i: (k * nb + i, 0)


def _final_norm_kernel(x_ref, g_ref, o_ref):
    x = x_ref[...]
    ms = jnp.mean(x * x, axis=-1, keepdims=True)
    o_ref[...] = x * lax.rsqrt(ms + NORM_EPS) * g_ref[...]


def _final_norm(x2d, g):
    n_tok, d = x2d.shape
    tr = 1024
    return pl.pallas_call(
        _final_norm_kernel,
        out_shape=jax.ShapeDtypeStruct((n_tok, d), F32),
        grid=(n_tok // tr,),
        in_specs=[pl.BlockSpec((tr, d), lambda i: (i, 0)), _const_spec((1, d))],
        out_specs=pl.BlockSpec((tr, d), lambda i: (i, 0)),
        compiler_params=pltpu.CompilerParams(
            dimension_semantics=("arbitrary",), vmem_limit_bytes=VMEM_LIMIT),
        name="final_norm",
    )(x2d, g.reshape(1, d))


def _moe_layer(layer, x2d, mod_l, g2, w_router, b_router, w_e_in, b_e_in, w_e_out, b_e_out):
    n_tok, d = x2d.shape
    tm = EXPERT_ROWS
    h_packed, idx, pos, wts, counts = _router(x2d, mod_l, g2, w_router, b_router)

    counts = counts[0]
    padded = ((counts + tm - 1) // tm) * tm
    pad_end = jnp.cumsum(padded)
    start_pad = pad_end - padded
    experts = jnp.arange(N_EXPERTS, dtype=I32)
    dest = jnp.sum(jnp.where(idx[:, :, None] == experts, start_pad, 0), axis=-1) + pos
    n_blocks = (n_tok * TOP_K) // tm + N_EXPERTS
    n_slots = n_blocks * tm
    block_start = jnp.arange(n_blocks, dtype=I32) * tm
    block_e = jnp.minimum(
        jnp.sum((block_start[:, None] >= pad_end[None, :]).astype(I32), axis=-1),
        N_EXPERTS - 1)
    n_used = (pad_end[-1] // tm).astype(I32).reshape(1)

    per_w = n_tok // SC_WORKERS
    dest_scatter = (dest.reshape(SC_WORKERS, per_w // SC_CHUNK, SC_CHUNK, TOP_K)
                    .transpose(0, 1, 3, 2)
                    .reshape(SC_WORKERS, (per_w // SC_CHUNK) * TOP_K, SC_CHUNK))
    xs = _sc_dispatch(h_packed, dest_scatter, n_slots)
    ys = _experts(layer, block_e, n_used, xs, w_e_in, b_e_in, w_e_out, b_e_out)
    dest_gather = dest.T.reshape(SC_WORKERS, (per_w * TOP_K) // SC_CHUNK, SC_CHUNK)
    y4 = _sc_gather(ys, dest_gather)
    return _combine(x2d, y4, wts, mod_l)


def kernel(x, c, norm1_g, norm2_g, w_ada, b_ada, w_in, conv_w, conv_b, w_rg_a, b_rg_a,
           w_rg_x, b_rg_x, lru_lambda, w_pool, pool_scale, w_up_a, w_up_b, w_out,
           w_router, b_router, w_e_in, b_e_in, w_e_out, b_e_out, final_g):
    depth = w_in.shape[0]
    bsz, seq, d = x.shape
    assert bsz == BATCH and d == D_MODEL
    mod = _ada_mod(c, w_ada, b_ada).reshape(depth, bsz, 6, d).transpose(0, 2, 1, 3)
    w_rg = jnp.concatenate([w_rg_a, w_rg_x], axis=-1).astype(BF16)
    x2d = x.transpose(1, 0, 2).reshape(seq * bsz, d)
    for l in range(depth):
        x2d = _mixer_layer(x2d, mod[l], norm1_g[l], w_in[l].astype(BF16), conv_w[l],
                           conv_b[l], w_rg[l], b_rg_a[l], b_rg_x[l], lru_lambda[l],
                           w_pool[l].astype(BF16), pool_scale[l], w_up_a[l].astype(BF16),
                           w_up_b[l].astype(BF16), w_out[l].astype(BF16))
        x2d = _moe_layer(l, x2d, mod[l], norm2_g[l], w_router[l], b_router[l],
                         w_e_in, b_e_in, w_e_out, b_e_out)
    out = _final_norm(x2d, final_g)
    return out.reshape(seq, bsz, d).transpose(1, 0, 2)
```

```python
import math

import jax
import jax.numpy as jnp
from jax import lax
from jax.experimental import pallas as pl
from jax.experimental.pallas import tpu as pltpu
from jax.experimental.pallas import tpu_sc as plsc

D_MODEL = 1024
BATCH = 8
N_HEADS = 8
HEAD_DIM = 128
CONV_WIDTH = 4
LRU_C = 8.0
POOL_WINDOWS = (2, 4, 8, 16)
POOL_WIDTH = 512
POOL_GROUP_DIM = 128
N_EXPERTS = 32
TOP_K = 4
D_FF = 1024
SWIGLU_LIMIT = 7.0
SWIGLU_ALPHA = 1.702
NORM_EPS = 1e-6

SUBLANES = 8
VMEM_LIMIT = 56 * 1024 * 1024
SC_CORES = 2
SC_SUBCORES = 16
SC_WORKERS = SC_CORES * SC_SUBCORES
SC_CHUNK = 64

MIX_STEPS = 32
ROUTE_ROWS = 512
EXPERT_ROWS = 1024
EXPERT_SUB = 256
ADA_COLS = 1536
NORM_ROWS = 1024
CAST_ROWS = 128
COMBINE_ROWS = 512
PACKED = D_MODEL // 2

F32 = jnp.float32
BF16 = jnp.bfloat16
I32 = jnp.int32
HIGH_MASK = -65536
LOG2_E = math.log2(math.e)


def _const_spec(shape):
    nd = len(shape)
    return pl.BlockSpec(shape, lambda *_: (0,) * nd, pipeline_mode=pl.Buffered(1))


def _pack_rows(v):
    bits = lax.bitcast_convert_type(v.astype(BF16).astype(F32), I32)
    lo, hi = bits[:, :PACKED], bits[:, PACKED:]
    return (hi & HIGH_MASK) | lax.shift_right_logical(lo, 16)


def _unpack_rows(p):
    lo = lax.bitcast_convert_type(lax.shift_left(p, 16), F32)
    hi = lax.bitcast_convert_type(p & HIGH_MASK, F32)
    return lo, hi


def _sigmoid(v):
    return 0.5 * jnp.tanh(0.5 * v) + 0.5


def _per_batch(v, slab, op):
    rows, d = v.shape
    return op(v.reshape(rows // BATCH, BATCH, d), slab[None]).reshape(rows, d)


def _modulated_norm(x, g, shift, scale):
    ms = jnp.mean(x * x, axis=-1, keepdims=True)
    hn = x * lax.rsqrt(ms + NORM_EPS) * g
    return _per_batch(_per_batch(hn, 1.0 + scale, jnp.multiply), shift, jnp.add)


def _ada_kernel(c_ref, w_ref, b_ref, o_ref):
    c = c_ref[...]
    c_act = c * jax.nn.sigmoid(c)
    o_ref[0] = jnp.dot(c_act, w_ref[0], preferred_element_type=F32,
                       precision=lax.Precision.HIGHEST) + b_ref[0]


def _ada_mod(c, w_ada, b_ada):
    depth, d, n = w_ada.shape
    bsz = c.shape[0]
    tn = ADA_COLS
    return pl.pallas_call(
        _ada_kernel,
        out_shape=jax.ShapeDtypeStruct((depth, bsz, n), F32),
        grid=(depth, n // tn),
        in_specs=[
            pl.BlockSpec((bsz, d), lambda l, j: (0, 0)),
            pl.BlockSpec((1, d, tn), lambda l, j: (l, 0, j)),
            pl.BlockSpec((1, 1, tn), lambda l, j: (l, 0, j)),
        ],
        out_specs=pl.BlockSpec((1, bsz, tn), lambda l, j: (l, 0, j)),
        compiler_params=pltpu.CompilerParams(
            dimension_semantics=("arbitrary", "arbitrary"),
            vmem_limit_bytes=VMEM_LIMIT),
        name="ada_mod",
    )(c, w_ada, b_ada.reshape(depth, 1, n))


def _delay(a, steps, carry_ref):
    n = steps * BATCH
    keep = carry_ref.shape[0]
    prev = carry_ref[keep - n:, :]
    return jnp.concatenate([prev, a[:a.shape[0] - n]], axis=0)


def _save_tail(a, carry_ref):
    carry_ref[...] = a[a.shape[0] - carry_ref.shape[0]:]


def _mixer_kernel(x_ref, mod_ref, g1_ref, win_ref, convw_ref, convb_ref, wrg_ref,
                  brga_ref, brgx_ref, lam_ref, wpool_ref, pscale_ref, wupa_ref,
                  wupb_ref, wout_ref, o_ref,
                  conv_c, p1_c, p2_c, p4_c, p8_c, h_c):
    i = pl.program_id(0)
    rows = x_ref.shape[0]
    steps = rows // BATCH
    d = D_MODEL

    @pl.when(i == 0)
    def _():
        for ref in (conv_c, p1_c, p2_c, p4_c, p8_c, h_c):
            ref[...] = jnp.zeros(ref.shape, ref.dtype)

    x = x_ref[...]
    shift1, scale1, gate1 = mod_ref[0], mod_ref[1], mod_ref[2]
    hb = _modulated_norm(x, g1_ref[...], shift1, scale1).astype(BF16)

    def in_proj(lo, hi):
        return jnp.dot(hb, win_ref[:, lo:hi], preferred_element_type=F32)

    x_lru = in_proj(0, d)
    cw = convw_ref[...]
    xc = (convb_ref[...] + cw[3:4] * x_lru
          + cw[2:3] * _delay(x_lru, 1, conv_c)
          + cw[1:2] * _delay(x_lru, 2, conv_c)
          + cw[0:1] * _delay(x_lru, 3, conv_c))
    _save_tail(x_lru, conv_c)
    xcb = xc.astype(BF16)
    pre = [jnp.dot(xcb[:, k * HEAD_DIM:(k + 1) * HEAD_DIM], wrg_ref[k],
                   preferred_element_type=F32) for k in range(N_HEADS)]
    pre_a = jnp.concatenate([p[:, :HEAD_DIM] for p in pre], axis=-1)
    pre_x = jnp.concatenate([p[:, HEAD_DIM:] for p in pre], axis=-1)
    r = _sigmoid(pre_a + brga_ref[...])
    gi = _sigmoid(pre_x + brgx_ref[...])
    rate = (-LRU_C * LOG2_E) * jax.nn.softplus(-lam_ref[...])
    a = jnp.exp2(r * rate)
    y = 1.0 - a * a
    mult = jnp.where(y > 0.0, y * lax.rsqrt(y), 0.0)
    u = mult * (gi * xc)
    h = h_c[...]
    hs = []
    for t in range(steps):
        sl = slice(t * BATCH, (t + 1) * BATCH)
        h = a[sl] * h + u[sl]
        hs.append(h)
    h_c[...] = h
    y_lru = jnp.concatenate(hs, axis=0) * jax.nn.gelu(in_proj(d, 2 * d))
    p_a = jnp.dot(y_lru.astype(BF16), wupa_ref[...], preferred_element_type=F32)
    merged = _sigmoid(in_proj(2 * d + POOL_WIDTH, 3 * d + POOL_WIDTH)) * p_a

    x_pool = in_proj(2 * d, 2 * d + POOL_WIDTH)
    g = POOL_GROUP_DIM
    s2 = x_pool + _delay(x_pool, 1, p1_c)
    s4 = s2[:, g:] + _delay(s2[:, g:], 2, p2_c)
    s8 = s4[:, g:] + _delay(s4[:, g:], 4, p4_c)
    s16 = s8[:, g:] + _delay(s8[:, g:], 8, p8_c)
    _save_tail(x_pool, p1_c)
    _save_tail(s2[:, g:], p2_c)
    _save_tail(s4[:, g:], p4_c)
    _save_tail(s8[:, g:], p8_c)
    wins = (s2[:, :g], s4[:, :g], s8[:, :g], s16)
    row = lax.broadcasted_iota(I32, (rows, 1), 0)
    t1 = (i * steps + row // BATCH + 1).astype(F32)
    pooled = []
    for k, w in enumerate(POOL_WINDOWS):
        cnt = jnp.minimum(t1, float(w))
        pk = wins[k] / cnt - x_pool[:, k * g:(k + 1) * g]
        pooled.append(jnp.dot(pk.astype(BF16), wpool_ref[k], preferred_element_type=F32))
    y_pool = jnp.concatenate(pooled, axis=-1) * pscale_ref[...]
    p_b = jnp.dot(y_pool.astype(BF16), wupb_ref[...], preferred_element_type=F32)
    merged = merged + _sigmoid(in_proj(3 * d + POOL_WIDTH, 4 * d + POOL_WIDTH)) * p_b

    mix = jnp.dot(merged.astype(BF16), wout_ref[...], preferred_element_type=F32)
    o_ref[...] = x + _per_batch(mix, gate1, jnp.multiply)


def _mixer_layer(x2d, mod_l, g1, w_in, conv_w, conv_b, w_rg, b_rg_a, b_rg_x, lam,
                 w_pool, pool_scale, w_up_a, w_up_b, w_out):
    n_tok, d = x2d.shape
    rows = MIX_STEPS * BATCH
    row = lambda v: v.reshape(1, -1)
    args = (x2d, mod_l, row(g1), w_in, conv_w, row(conv_b), w_rg, row(b_rg_a), row(b_rg_x),
            row(lam), w_pool, row(pool_scale), w_up_a, w_up_b, w_out)
    in_specs = [pl.BlockSpec((rows, d), lambda i: (i, 0))]
    in_specs += [_const_spec(a.shape) for a in args[1:]]
    g = POOL_GROUP_DIM
    scratch = [
        pltpu.VMEM(((CONV_WIDTH - 1) * BATCH, d), F32),
        pltpu.VMEM((1 * BATCH, 4 * g), F32), pltpu.VMEM((2 * BATCH, 3 * g), F32),
        pltpu.VMEM((4 * BATCH, 2 * g), F32), pltpu.VMEM((8 * BATCH, g), F32),
        pltpu.VMEM((BATCH, d), F32),
    ]
    return pl.pallas_call(
        _mixer_kernel,
        out_shape=jax.ShapeDtypeStruct(x2d.shape, F32),
        grid=(n_tok // rows,),
        in_specs=in_specs,
        out_specs=pl.BlockSpec((rows, d), lambda i: (i, 0)),
        scratch_shapes=scratch,
        compiler_params=pltpu.CompilerParams(
            dimension_semantics=("arbitrary",), vmem_limit_bytes=VMEM_LIMIT),
        name="mixer",
    )(*args)


def _router_kernel(x_ref, mod_ref, g2_ref, wr_ref, br_ref,
                   h_ref, idx_ref, pos_ref, wt_ref, cnt_ref, run_c):
    i = pl.program_id(0)
    tr = x_ref.shape[0]

    @pl.when(i == 0)
    def _():
        run_c[...] = jnp.zeros(run_c.shape, run_c.dtype)

    h = _modulated_norm(x_ref[...], g2_ref[...], mod_ref[3], mod_ref[4])
    h_ref[...] = _pack_rows(h)

    h_hi = h.astype(BF16)
    h_lo = (h - h_hi.astype(F32)).astype(BF16)
    w = wr_ref[...]
    w_hi = w.astype(BF16)
    w_lo = (w - w_hi.astype(F32)).astype(BF16)
    first = jnp.dot(h_hi, jnp.concatenate([w_hi, w_lo], axis=-1), preferred_element_type=F32)
    logits = (first[:, :N_EXPERTS] + first[:, N_EXPERTS:]
              + jnp.dot(h_lo, w_hi, preferred_element_type=F32) + br_ref[...])

    lane = lax.broadcasted_iota(I32, (tr, N_EXPERTS), 1)
    vals, idxs, hots = [], [], []
    cur = logits
    for _ in range(TOP_K):
        m = jnp.max(cur, axis=-1, keepdims=True)
        am = jnp.min(jnp.where(cur == m, lane, N_EXPERTS), axis=-1, keepdims=True)
        hot = lane == am
        vals.append(m)
        idxs.append(am)
        hots.append(hot)
        cur = jnp.where(hot, -jnp.inf, cur)
    exps = [jnp.exp(v - vals[0]) for v in vals]
    denom = exps[0] + exps[1] + exps[2] + exps[3]

    sel = (hots[0] | hots[1] | hots[2] | hots[3]).astype(F32)
    ri = lax.broadcasted_iota(I32, (tr, tr), 0)
    ci = lax.broadcasted_iota(I32, (tr, tr), 1)
    tri = (ci < ri).astype(BF16)
    before = jnp.dot(tri, sel.astype(BF16), preferred_element_type=F32) + run_c[...]
    run_c[...] = run_c[...] + jnp.sum(sel, axis=0, keepdims=True)
    cnt_ref[...] = run_c[...].astype(I32)

    lane4 = lax.broadcasted_iota(I32, (tr, TOP_K), 1)
    idx_o = jnp.zeros((tr, TOP_K), I32)
    pos_o = jnp.zeros((tr, TOP_K), I32)
    wt_o = jnp.zeros((tr, TOP_K), F32)
    for k in range(TOP_K):
        pk = jnp.sum(jnp.where(hots[k], before, 0.0), axis=-1, keepdims=True)
        idx_o = jnp.where(lane4 == k, idxs[k], idx_o)
        pos_o = jnp.where(lane4 == k, pk.astype(I32), pos_o)
        wt_o = jnp.where(lane4 == k, exps[k] / denom, wt_o)
    idx_ref[...] = idx_o
    pos_ref[...] = pos_o
    wt_ref[...] = wt_o


def _router(x2d, mod_l, g2, w_router, b_router):
    n_tok, d = x2d.shape
    tr = ROUTE_ROWS
    return pl.pallas_call(
        _router_kernel,
        out_shape=(
            jax.ShapeDtypeStruct((n_tok, PACKED), I32),
            jax.ShapeDtypeStruct((n_tok, TOP_K), I32),
            jax.ShapeDtypeStruct((n_tok, TOP_K), I32),
            jax.ShapeDtypeStruct((n_tok, TOP_K), F32),
            jax.ShapeDtypeStruct((1, N_EXPERTS), I32),
        ),
        grid=(n_tok // tr,),
        in_specs=[
            pl.BlockSpec((tr, d), lambda i: (i, 0)),
            _const_spec(mod_l.shape),
            _const_spec((1, d)),
            _const_spec((d, N_EXPERTS)),
            _const_spec((1, N_EXPERTS)),
        ],
        out_specs=(
            pl.BlockSpec((tr, PACKED), lambda i: (i, 0)),
            pl.BlockSpec((tr, TOP_K), lambda i: (i, 0)),
            pl.BlockSpec((tr, TOP_K), lambda i: (i, 0)),
            pl.BlockSpec((tr, TOP_K), lambda i: (i, 0)),
            pl.BlockSpec((1, N_EXPERTS), lambda i: (0, 0)),
        ),
        scratch_shapes=[pltpu.VMEM((1, N_EXPERTS), F32)],
        compiler_params=pltpu.CompilerParams(
            dimension_semantics=("arbitrary",), vmem_limit_bytes=VMEM_LIMIT),
        name="router",
    )(x2d, mod_l, g2.reshape(1, d), w_router, b_router.reshape(1, N_EXPERTS))


def _sc_mesh():
    return plsc.VectorSubcoreMesh(core_axis_name="c", subcore_axis_name="s")


def _sc_worker_id():
    return lax.axis_index("s") * SC_CORES + lax.axis_index("c")


def _sc_dispatch(h_packed, dest_sc, n_slots):
    n_tok, width = h_packed.shape
    per_w = n_tok // SC_WORKERS
    n_chunks = per_w // SC_CHUNK

    def body(h_hbm, dest_hbm, xs_hbm, idx_v, rows_v, sem_in, sem_out):
        base = _sc_worker_id() * per_w
        pltpu.sync_copy(dest_hbm.at[_sc_worker_id()], idx_v)

        def load(j, buf):
            return pltpu.make_async_copy(
                h_hbm.at[pl.ds(base + j * SC_CHUNK, SC_CHUNK)], rows_v.at[buf], sem_in)

        load(0, 0).start()

        @pl.loop(0, n_chunks, step=2)
        def _(j0):
            for buf in range(2):
                j = j0 + buf
                load(j, buf).wait()

                @pl.when(j + 1 < n_chunks)
                def _():
                    load(j + 1, 1 - buf).start()

                copies = [
                    pltpu.make_async_copy(
                        rows_v.at[buf], xs_hbm.at[idx_v.at[j * TOP_K + k]], sem_out)
                    for k in range(TOP_K)]
                for cp in copies:
                    cp.start()
                for cp in copies:
                    cp.wait()

    return pl.kernel(
        body,
        out_type=jax.ShapeDtypeStruct((n_slots, width), I32),
        mesh=_sc_mesh(),
        scratch_types=[
            pltpu.VMEM((n_chunks * TOP_K, SC_CHUNK), I32),
            pltpu.VMEM((2, SC_CHUNK, width), I32),
            pltpu.SemaphoreType.DMA,
            pltpu.SemaphoreType.DMA,
        ],
        name="sc_dispatch",
    )(h_packed, dest_sc)


def _sc_gather(ys_packed, dest_sc):
    width = ys_packed.shape[1]
    n_chunks = dest_sc.shape[1]
    per_w = n_chunks * SC_CHUNK
    n_rows = SC_WORKERS * per_w

    def body(ys_hbm, dest_hbm, out_hbm, idx_v, rows_v, sem_g, sem_w):
        base = _sc_worker_id() * per_w
        pltpu.sync_copy(dest_hbm.at[_sc_worker_id()], idx_v)

        def gather(j, buf):
            return pltpu.make_async_copy(ys_hbm.at[idx_v.at[j]], rows_v.at[buf], sem_g)

        def writeout(j, buf):
            return pltpu.make_async_copy(
                rows_v.at[buf], out_hbm.at[pl.ds(base + j * SC_CHUNK, SC_CHUNK)], sem_w)

        gather(0, 0).start()

        @pl.loop(0, n_chunks, step=2)
        def _(j0):
            for buf in range(2):
                j = j0 + buf
                gather(j, buf).wait()

                @pl.when(j >= 1)
                def _():
                    writeout(j - 1, 1 - buf).wait()

                @pl.when(j + 1 < n_chunks)
                def _():
                    gather(j + 1, 1 - buf).start()

                writeout(j, buf).start()

        writeout(n_chunks - 1, 1).wait()

    return pl.kernel(
        body,
        out_type=jax.ShapeDtypeStruct((n_rows, width), I32),
        mesh=_sc_mesh(),
        scratch_types=[
            pltpu.VMEM((n_chunks, SC_CHUNK), I32),
            pltpu.VMEM((2, SC_CHUNK, width), I32),
            pltpu.SemaphoreType.DMA,
            pltpu.SemaphoreType.DMA,
        ],
        name="sc_gather",
    )(ys_packed, dest_sc)


def _expert_kernel(be_ref, nv_ref, xs_ref, win_ref, bin_ref, wout_ref, bout_ref, ys_ref,
                   win_bf, wout_bf):
    i = pl.program_id(0)
    valid = nv_ref[i]
    prev = be_ref[jnp.maximum(i - 1, 0)]

    @pl.when((valid > 0) & ((i == 0) | (be_ref[i] != prev)))
    def _():
        def cast_in(r, c):
            rows = pl.ds(pl.multiple_of(r * CAST_ROWS, CAST_ROWS), CAST_ROWS)
            win_bf[rows, :] = win_ref[0, rows, :].astype(BF16)
            return c

        def cast_out(r, c):
            rows = pl.ds(pl.multiple_of(r * CAST_ROWS, CAST_ROWS), CAST_ROWS)
            wout_bf[rows, :] = wout_ref[0, rows, :].astype(BF16)
            return c

        lax.fori_loop(0, D_MODEL // CAST_ROWS, cast_in, 0)
        lax.fori_loop(0, D_FF // CAST_ROWS, cast_out, 0)

    n_sub = (valid + EXPERT_SUB - 1) // EXPERT_SUB
    row = lax.broadcasted_iota(I32, (EXPERT_SUB, 1), 0)

    def compute(j, c):
        rows = pl.ds(pl.multiple_of(j * EXPERT_SUB, EXPERT_SUB), EXPERT_SUB)
        packed = jnp.where(row + j * EXPERT_SUB < valid, xs_ref[rows, :], 0)
        lo, hi = _unpack_rows(packed)
        gu = (jnp.dot(lo.astype(BF16), win_bf[:PACKED, :], preferred_element_type=F32)
              + jnp.dot(hi.astype(BF16), win_bf[PACKED:, :], preferred_element_type=F32)
              + bin_ref[0])
        gate = jnp.minimum(gu[:, :D_FF], SWIGLU_LIMIT)
        up = jnp.clip(gu[:, D_FF:], -SWIGLU_LIMIT, SWIGLU_LIMIT)
        act = gate * _sigmoid(SWIGLU_ALPHA * gate) * (up + 1.0)
        out = jnp.dot(act.astype(BF16), wout_bf[...], preferred_element_type=F32) + bout_ref[0]
        ys_ref[rows, :] = _pack_rows(out)
        return c

    def clear(j, c):
        rows = pl.ds(pl.multiple_of(j * EXPERT_SUB, EXPERT_SUB), EXPERT_SUB)
        ys_ref[rows, :] = jnp.zeros((EXPERT_SUB, ys_ref.shape[1]), ys_ref.dtype)
        return c

    lax.fori_loop(0, n_sub, compute, 0)
    lax.fori_loop(n_sub, EXPERT_ROWS // EXPERT_SUB, clear, 0)


def _experts(layer, block_e, n_valid, xs, w_e_in, b_e_in, w_e_out, b_e_out):
    n_slots, width = xs.shape
    d = D_MODEL
    tm = EXPERT_ROWS
    grid_spec = pltpu.PrefetchScalarGridSpec(
        num_scalar_prefetch=2,
        grid=(n_slots // tm,),
        in_specs=[
            pl.BlockSpec((tm, width), lambda i, be, nv: (i, 0)),
            pl.BlockSpec((None, 1, d, 2 * D_FF), lambda i, be, nv: (layer, be[i], 0, 0)),
            pl.BlockSpec((None, 1, 1, 2 * D_FF), lambda i, be, nv: (layer, be[i], 0, 0)),
            pl.BlockSpec((None, 1, D_FF, d), lambda i, be, nv: (layer, be[i], 0, 0)),
            pl.BlockSpec((None, 1, 1, d), lambda i, be, nv: (layer, be[i], 0, 0)),
        ],
        out_specs=pl.BlockSpec((tm, width), lambda i, be, nv: (i, 0)),
        scratch_shapes=[pltpu.VMEM((d, 2 * D_FF), BF16), pltpu.VMEM((D_FF, d), BF16)],
    )
    return pl.pallas_call(
        _expert_kernel,
        out_shape=jax.ShapeDtypeStruct((n_slots, width), I32),
        grid_spec=grid_spec,
        compiler_params=pltpu.CompilerParams(
            dimension_semantics=("arbitrary",), vmem_limit_bytes=VMEM_LIMIT),
        name="experts",
    )(block_e, n_valid, xs, w_e_in, b_e_in.reshape(-1, N_EXPERTS, 1, 2 * D_FF),
      w_e_out, b_e_out.reshape(-1, N_EXPERTS, 1, d))


def _combine_kernel(x_ref, y0_ref, y1_ref, y2_ref, y3_ref, wt_ref, mod_ref, o_ref):
    wt = wt_ref[...]
    acc_lo = jnp.zeros((x_ref.shape[0], PACKED), F32)
    acc_hi = jnp.zeros((x_ref.shape[0], PACKED), F32)
    for k, y_ref in enumerate((y0_ref, y1_ref, y2_ref, y3_ref)):
        lo, hi = _unpack_rows(y_ref[...])
        wk = wt[:, k:k + 1]
        acc_lo = acc_lo + wk * lo
        acc_hi = acc_hi + wk * hi
    ffn = jnp.concatenate([acc_lo, acc_hi], axis=-1)
    o_ref[...] = x_ref[...] + _per_batch(ffn, mod_ref[5], jnp.multiply)


def _combine(x2d, y4, wts, mod_l):
    n_tok, d = x2d.shape
    tc = COMBINE_ROWS
    nb = n_tok // tc
    y_specs = [pl.BlockSpec((tc, PACKED), _kmajor_index(k, nb)) for k in range(TOP_K)]
    return pl.pallas_call(
        _combine_kernel,
        out_shape=jax.ShapeDtypeStruct((n_tok, d), F32),
        grid=(nb,),
        in_specs=[pl.BlockSpec((tc, d), lambda i: (i, 0))] + y_specs + [
            pl.BlockSpec((tc, TOP_K), lambda i: (i, 0)),
            _const_spec(mod_l.shape),
        ],
        out_specs=pl.BlockSpec((tc, d), lambda i: (i, 0)),
        compiler_params=pltpu.CompilerParams(
            dimension_semantics=("arbitrary",), vmem_limit_bytes=VMEM_LIMIT),
        name="combine",
    )(x2d, y4, y4, y4, y4, wts, mod_l)


def _kmajor_index(k, nb):
    return lambda i: (k * nb + i, 0)


def _final_norm_kernel(x_ref, g_ref, o_ref):
    x = x_ref[...]
    ms = jnp.mean(x * x, axis=-1, keepdims=True)
    o_ref[...] = x * lax.rsqrt(ms + NORM_EPS) * g_ref[...]


def _final_norm(x2d, g):
    n_tok, d = x2d.shape
    tr = NORM_ROWS
    return pl.pallas_call(
        _final_norm_kernel,
        out_shape=jax.ShapeDtypeStruct((n_tok, d), F32),
        grid=(n_tok // tr,),
        in_specs=[pl.BlockSpec((tr, d), lambda i: (i, 0)), _const_spec((1, d))],
        out_specs=pl.BlockSpec((tr, d), lambda i: (i, 0)),
        compiler_params=pltpu.CompilerParams(
            dimension_semantics=("arbitrary",), vmem_limit_bytes=VMEM_LIMIT),
        name="final_norm",
    )(x2d, g.reshape(1, d))


def _moe_layer(layer, x2d, mod_l, g2, w_router, b_router, w_e_in, b_e_in, w_e_out, b_e_out):
    n_tok, d = x2d.shape
    tm = EXPERT_ROWS
    h_packed, idx, pos, wts, counts = _router(x2d, mod_l, g2, w_router, b_router)

    counts = counts[0]
    padded = ((counts + tm - 1) // tm) * tm
    pad_end = jnp.cumsum(padded)
    start_pad = pad_end - padded
    experts = jnp.arange(N_EXPERTS, dtype=I32)
    dest = jnp.sum(jnp.where(idx[:, :, None] == experts, start_pad, 0), axis=-1) + pos
    n_blocks = (n_tok * TOP_K) // tm + N_EXPERTS
    n_slots = n_blocks * tm
    block_start = jnp.arange(n_blocks, dtype=I32) * tm
    block_e = jnp.minimum(
        jnp.sum((block_start[:, None] >= pad_end[None, :]).astype(I32), axis=-1),
        N_EXPERTS - 1)
    hot_e = block_e[:, None] == experts[None, :]
    rows_left = jnp.sum(jnp.where(hot_e, (start_pad + counts)[None, :], 0), axis=-1) - block_start
    n_valid = jnp.where(block_start < pad_end[-1], jnp.clip(rows_left, 0, tm), 0).astype(I32)

    per_w = n_tok // SC_WORKERS
    dest_scatter = (dest.reshape(SC_WORKERS, per_w // SC_CHUNK, SC_CHUNK, TOP_K)
                    .transpose(0, 1, 3, 2)
                    .reshape(SC_WORKERS, (per_w // SC_CHUNK) * TOP_K, SC_CHUNK))
    xs = _sc_dispatch(h_packed, dest_scatter, n_slots)
    ys = _experts(layer, block_e, n_valid, xs, w_e_in, b_e_in, w_e_out, b_e_out)
    dest_gather = dest.T.reshape(SC_WORKERS, (per_w * TOP_K) // SC_CHUNK, SC_CHUNK)
    y4 = _sc_gather(ys, dest_gather)
    return _combine(x2d, y4, wts, mod_l)


def kernel(x, c, norm1_g, norm2_g, w_ada, b_ada, w_in, conv_w, conv_b, w_rg_a, b_rg_a,
           w_rg_x, b_rg_x, lru_lambda, w_pool, pool_scale, w_up_a, w_up_b, w_out,
           w_router, b_router, w_e_in, b_e_in, w_e_out, b_e_out, final_g):
    depth = w_in.shape[0]
    bsz, seq, d = x.shape
    assert bsz == BATCH and d == D_MODEL
    mod = _ada_mod(c, w_ada, b_ada).reshape(depth, bsz, 6, d).transpose(0, 2, 1, 3)
    w_rg = jnp.concatenate([w_rg_a, w_rg_x], axis=-1).astype(BF16)
    x2d = x.transpose(1, 0, 2).reshape(seq * bsz, d)
    for l in range(depth):
        x2d = _mixer_layer(x2d, mod[l], norm1_g[l], w_in[l].astype(BF16), conv_w[l],
                           conv_b[l], w_rg[l], b_rg_a[l], b_rg_x[l], lru_lambda[l],
                           w_pool[l].astype(BF16), pool_scale[l], w_up_a[l].astype(BF16),
                           w_up_b[l].astype(BF16), w_out[l].astype(BF16))
        x2d = _moe_layer(l, x2d, mod[l], norm2_g[l], w_router[l], b_router[l],
                         w_e_in, b_e_in, w_e_out, b_e_out)
    out = _final_norm(x2d, final_g)
    return out.reshape(seq, bsz, d).transpose(1, 0, 2)
```

```python
import functools
import math

import jax
import jax.numpy as jnp
from jax import lax
from jax.experimental import pallas as pl
from jax.experimental.pallas import tpu as pltpu
from jax.experimental.pallas import tpu_sc as plsc

D_MODEL = 1024
BATCH = 8
N_HEADS = 8
HEAD_DIM = 128
CONV_WIDTH = 4
LRU_C = 8.0
POOL_WINDOWS = (2, 4, 8, 16)
POOL_WIDTH = 512
POOL_GROUP_DIM = 128
N_EXPERTS = 32
TOP_K = 4
D_FF = 1024
SWIGLU_LIMIT = 7.0
SWIGLU_ALPHA = 1.702
NORM_EPS = 1e-6

SUBLANES = 8
VMEM_LIMIT = 56 * 1024 * 1024
SC_CORES = 2
SC_SUBCORES = 16
SC_WORKERS = SC_CORES * SC_SUBCORES
SC_CHUNK = 64

MIX_STEPS = 32
SEQ_PARTS = 2
ROUTER_COLS = 128
EXPERT_ROWS = 1024
EXPERT_SUB = 256
ADA_COLS = 1536
NORM_ROWS = 512
CAST_ROWS = 128
PACKED = D_MODEL // 2

F32 = jnp.float32
BF16 = jnp.bfloat16
I32 = jnp.int32
HIGH_MASK = -65536
LOG2_E = math.log2(math.e)


def _const_spec(shape):
    nd = len(shape)
    return pl.BlockSpec(shape, lambda *_: (0,) * nd, pipeline_mode=pl.Buffered(1))


def _pack_rows(v):
    bits = lax.bitcast_convert_type(v.astype(BF16).astype(F32), I32)
    lo, hi = bits[:, :PACKED], bits[:, PACKED:]
    return (hi & HIGH_MASK) | lax.shift_right_logical(lo, 16)


def _unpack_rows(p):
    lo = lax.bitcast_convert_type(lax.shift_left(p, 16), F32)
    hi = lax.bitcast_convert_type(p & HIGH_MASK, F32)
    return lo, hi


def _sigmoid(v):
    return 0.5 * jnp.tanh(0.5 * v) + 0.5


def _per_batch(v, slab, op):
    rows, d = v.shape
    return op(v.reshape(rows // BATCH, BATCH, d), slab[None]).reshape(rows, d)


def _modulated_norm(x, g, shift, scale):
    ms = jnp.mean(x * x, axis=-1, keepdims=True)
    hn = x * lax.rsqrt(ms + NORM_EPS) * g
    return _per_batch(_per_batch(hn, 1.0 + scale, jnp.multiply), shift, jnp.add)


def _ada_kernel(c_ref, w_ref, b_ref, o_ref):
    c = c_ref[...]
    c_act = c * jax.nn.sigmoid(c)
    o_ref[0] = jnp.dot(c_act, w_ref[0], preferred_element_type=F32,
                       precision=lax.Precision.HIGHEST) + b_ref[0]


def _ada_mod(c, w_ada, b_ada):
    depth, d, n = w_ada.shape
    bsz = c.shape[0]
    tn = ADA_COLS
    return pl.pallas_call(
        _ada_kernel,
        out_shape=jax.ShapeDtypeStruct((depth, bsz, n), F32),
        grid=(depth, n // tn),
        in_specs=[
            pl.BlockSpec((bsz, d), lambda l, j: (0, 0)),
            pl.BlockSpec((1, d, tn), lambda l, j: (l, 0, j)),
            pl.BlockSpec((1, 1, tn), lambda l, j: (l, 0, j)),
        ],
        out_specs=pl.BlockSpec((1, bsz, tn), lambda l, j: (l, 0, j)),
        compiler_params=pltpu.CompilerParams(
            dimension_semantics=("arbitrary", "arbitrary"),
            vmem_limit_bytes=VMEM_LIMIT),
        name="ada_mod",
    )(c, w_ada, b_ada.reshape(depth, 1, n))


def _delay(a, steps, carry_ref):
    n = steps * BATCH
    keep = carry_ref.shape[0]
    prev = carry_ref[keep - n:, :]
    return jnp.concatenate([prev, a[:a.shape[0] - n]], axis=0)


def _save_tail(a, carry_ref):
    carry_ref[...] = a[a.shape[0] - carry_ref.shape[0]:]


def _combined_rows(x_ref, y_refs, wt_ref, gate2, rs=slice(None)):
    x = x_ref[rs, :]
    wt = wt_ref[rs, :]
    acc_lo = jnp.zeros((x.shape[0], PACKED), F32)
    acc_hi = jnp.zeros((x.shape[0], PACKED), F32)
    for k, y_ref in enumerate(y_refs):
        lo, hi = _unpack_rows(y_ref[rs, :])
        wk = wt[:, k:k + 1]
        acc_lo = acc_lo + wk * lo
        acc_hi = acc_hi + wk * hi
    ffn = jnp.concatenate([acc_lo, acc_hi], axis=-1)
    return x + _per_batch(ffn, gate2, jnp.multiply)


def _route_rows(h, wrt_ref, brc_ref, run_c):
    rows = h.shape[0]
    h_hi = h.astype(BF16)
    h_lo = (h - h_hi.astype(F32)).astype(BF16)
    parts = jnp.dot(jnp.concatenate([h_hi, h_lo], axis=-1), wrt_ref[...],
                    preferred_element_type=F32)
    parts_t = parts.T
    logits = parts_t[:N_EXPERTS] + parts_t[N_EXPERTS:2 * N_EXPERTS] + brc_ref[...]

    e_iota = lax.broadcasted_iota(I32, (N_EXPERTS, rows), 0)
    vals, idxs, hots = [], [], []
    cur = logits
    for _ in range(TOP_K):
        m = jnp.max(cur, axis=0, keepdims=True)
        am = jnp.min(jnp.where(cur == m, e_iota, N_EXPERTS), axis=0, keepdims=True)
        hot = e_iota == am
        vals.append(m)
        idxs.append(am)
        hots.append(hot)
        cur = jnp.where(hot, -jnp.inf, cur)
    exps = [jnp.exp(v - vals[0]) for v in vals]
    denom = exps[0] + exps[1] + exps[2] + exps[3]

    sel = (hots[0] | hots[1] | hots[2] | hots[3]).astype(F32)
    ri = lax.broadcasted_iota(I32, (rows, rows), 0)
    ci = lax.broadcasted_iota(I32, (rows, rows), 1)
    earlier = (ri < ci).astype(BF16)
    before = jnp.dot(sel.astype(BF16), earlier, preferred_element_type=F32) + run_c[...]
    run_c[...] = run_c[...] + jnp.sum(sel, axis=1, keepdims=True)
    poss = [jnp.sum(jnp.where(hot, before, 0.0), axis=0, keepdims=True).astype(I32)
            for hot in hots]
    return (jnp.concatenate(idxs, axis=0), jnp.concatenate(poss, axis=0),
            jnp.concatenate([e / denom for e in exps], axis=0))


N_CARRY = 6


def _mixer_kernel(with_combine, step0, *refs):
    refs = list(refs)
    x_ref = refs.pop(0)
    if with_combine:
        y_refs = [refs.pop(0) for _ in range(TOP_K)]
        wt_in_ref = refs.pop(0)
        modp_ref = refs.pop(0)
    (mod_ref, g1_ref, win_ref, convw_ref, convb_ref, wrg_ref, brga_ref, brgx_ref, lam_ref,
     wpool_ref, pscale_ref, wupa_ref, wupb_ref, wout_ref, g2_ref, wrt_ref, brc_ref) = refs[:17]
    carry_in = refs[17:17 + N_CARRY]
    o_ref, hp_ref, idx_ref, pos_ref, wt_ref, cnt_ref = refs[17 + N_CARRY:23 + N_CARRY]
    carry_out = refs[23 + N_CARRY:23 + 2 * N_CARRY]
    carries = refs[23 + 2 * N_CARRY:23 + 3 * N_CARRY]
    conv_c, p1_c, p2_c, p4_c, p8_c, h_c = carries
    run_c = refs[23 + 3 * N_CARRY]
    i = pl.program_id(0)
    rows = x_ref.shape[0]
    steps = rows // BATCH
    first_step = step0 + i * steps
    d = D_MODEL

    @pl.when(i == 0)
    def _():
        for src, dst in zip(carry_in, carries):
            dst[...] = src[...]
        run_c[...] = jnp.zeros(run_c.shape, run_c.dtype)

    if with_combine:
        x = _combined_rows(x_ref, y_refs, wt_in_ref, modp_ref[5])
    else:
        x = x_ref[...]
    shift1, scale1, gate1 = mod_ref[0], mod_ref[1], mod_ref[2]
    hb = _modulated_norm(x, g1_ref[...], shift1, scale1).astype(BF16)

    def in_proj(lo, hi):
        return jnp.dot(hb, win_ref[:, lo:hi], preferred_element_type=F32)

    x_lru = in_proj(0, d)
    cw = convw_ref[...]
    xc = (convb_ref[...] + cw[3:4] * x_lru
          + cw[2:3] * _delay(x_lru, 1, conv_c)
          + cw[1:2] * _delay(x_lru, 2, conv_c)
          + cw[0:1] * _delay(x_lru, 3, conv_c))
    _save_tail(x_lru, conv_c)
    xcb = xc.astype(BF16)
    pre = [jnp.dot(xcb[:, k * HEAD_DIM:(k + 1) * HEAD_DIM], wrg_ref[k],
                   preferred_element_type=F32) for k in range(N_HEADS)]
    pre_a = jnp.concatenate([p[:, :HEAD_DIM] for p in pre], axis=-1)
    pre_x = jnp.concatenate([p[:, HEAD_DIM:] for p in pre], axis=-1)
    r = _sigmoid(pre_a + brga_ref[...])
    gi = _sigmoid(pre_x + brgx_ref[...])
    rate = (-LRU_C * LOG2_E) * jax.nn.softplus(-lam_ref[...])
    a = jnp.exp2(r * rate)
    y = 1.0 - a * a
    mult = jnp.where(y > 0.0, y * lax.rsqrt(y), 0.0)
    u = mult * (gi * xc)
    h = h_c[...]
    hs = []
    for t in range(steps):
        sl = slice(t * BATCH, (t + 1) * BATCH)
        h = a[sl] * h + u[sl]
        hs.append(h)
    h_c[...] = h
    y_lru = jnp.concatenate(hs, axis=0) * jax.nn.gelu(in_proj(d, 2 * d))
    p_a = jnp.dot(y_lru.astype(BF16), wupa_ref[...], preferred_element_type=F32)
    merged = _sigmoid(in_proj(2 * d + POOL_WIDTH, 3 * d + POOL_WIDTH)) * p_a

    x_pool = in_proj(2 * d, 2 * d + POOL_WIDTH)
    g = POOL_GROUP_DIM
    s2 = x_pool + _delay(x_pool, 1, p1_c)
    s4 = s2[:, g:] + _delay(s2[:, g:], 2, p2_c)
    s8 = s4[:, g:] + _delay(s4[:, g:], 4, p4_c)
    s16 = s8[:, g:] + _delay(s8[:, g:], 8, p8_c)
    _save_tail(x_pool, p1_c)
    _save_tail(s2[:, g:], p2_c)
    _save_tail(s4[:, g:], p4_c)
    _save_tail(s8[:, g:], p8_c)
    wins = (s2[:, :g], s4[:, :g], s8[:, :g], s16)
    row = lax.broadcasted_iota(I32, (rows, 1), 0)
    t1 = (first_step + row // BATCH + 1).astype(F32)
    pooled = []
    for k, w in enumerate(POOL_WINDOWS):
        cnt = jnp.minimum(t1, float(w))
        pk = wins[k] / cnt - x_pool[:, k * g:(k + 1) * g]
        pooled.append(jnp.dot(pk.astype(BF16), wpool_ref[k], preferred_element_type=F32))
    y_pool = jnp.concatenate(pooled, axis=-1) * pscale_ref[...]
    p_b = jnp.dot(y_pool.astype(BF16), wupb_ref[...], preferred_element_type=F32)
    merged = merged + _sigmoid(in_proj(3 * d + POOL_WIDTH, 4 * d + POOL_WIDTH)) * p_b

    mix = jnp.dot(merged.astype(BF16), wout_ref[...], preferred_element_type=F32)
    x_mid = x + _per_batch(mix, gate1, jnp.multiply)
    o_ref[...] = x_mid
    for src, dst in zip(carries, carry_out):
        dst[...] = src[...]

    hr = _modulated_norm(x_mid, g2_ref[...], mod_ref[3], mod_ref[4])
    hp_ref[...] = _pack_rows(hr)
    idx, pos, wts = _route_rows(hr, wrt_ref, brc_ref, run_c)
    idx_ref[...] = idx
    pos_ref[...] = pos
    wt_ref[...] = wts
    cnt_ref[...] = run_c[...].astype(I32)


def _kmajor_index(k, nb):
    return lambda i: (k * nb + i, 0)


def _pending_specs(rows, n_tok):
    nb = n_tok // rows
    specs = [pl.BlockSpec((rows, PACKED), _kmajor_index(k, nb)) for k in range(TOP_K)]
    specs.append(pl.BlockSpec((rows, TOP_K), lambda i: (i, 0)))
    specs.append(_const_spec((6, BATCH, D_MODEL)))
    return specs


def _carry_shapes():
    g = POOL_GROUP_DIM
    return (((CONV_WIDTH - 1) * BATCH, D_MODEL), (1 * BATCH, 4 * g), (2 * BATCH, 3 * g),
            (4 * BATCH, 2 * g), (8 * BATCH, g), (BATCH, D_MODEL))


def _mixer_layer(x2d, pending, carry, step0, mod_l, g1, w_in, conv_w, conv_b, w_rg, b_rg_a,
                 b_rg_x, lam, w_pool, pool_scale, w_up_a, w_up_b, w_out, g2, w_split, b_router):
    n_tok, d = x2d.shape
    rows = MIX_STEPS * BATCH
    row = lambda v: v.reshape(1, -1)
    weights = (mod_l, row(g1), w_in, conv_w, row(conv_b), w_rg, row(b_rg_a), row(b_rg_x),
               row(lam), w_pool, row(pool_scale), w_up_a, w_up_b, w_out, row(g2),
               w_split, b_router.reshape(N_EXPERTS, 1))
    args = [x2d]
    in_specs = [pl.BlockSpec((rows, d), lambda i: (i, 0))]
    if pending is not None:
        y4, wts_prev, mod_prev = pending
        args += [y4, y4, y4, y4, wts_prev, mod_prev]
        in_specs += _pending_specs(rows, n_tok)
    args += list(weights) + list(carry)
    in_specs += [_const_spec(a.shape) for a in weights + tuple(carry)]
    carry_shapes = _carry_shapes()
    scratch = [pltpu.VMEM(shape, F32) for shape in carry_shapes]
    scratch.append(pltpu.VMEM((N_EXPERTS, 1), F32))
    routed = lambda dtype: jax.ShapeDtypeStruct((TOP_K, n_tok), dtype)
    routed_spec = pl.BlockSpec((TOP_K, rows), lambda i: (0, i))
    whole = lambda shape: pl.BlockSpec(shape, lambda i: (0,) * len(shape))
    outs = pl.pallas_call(
        functools.partial(_mixer_kernel, pending is not None, step0),
        out_shape=(
            jax.ShapeDtypeStruct((n_tok, d), F32),
            jax.ShapeDtypeStruct((n_tok, PACKED), I32),
            routed(I32), routed(I32), routed(F32),
            jax.ShapeDtypeStruct((N_EXPERTS, 1), I32),
        ) + tuple(jax.ShapeDtypeStruct(shape, F32) for shape in carry_shapes),
        grid=(n_tok // rows,),
        in_specs=in_specs,
        out_specs=(
            pl.BlockSpec((rows, d), lambda i: (i, 0)),
            pl.BlockSpec((rows, PACKED), lambda i: (i, 0)),
            routed_spec, routed_spec, routed_spec,
            whole((N_EXPERTS, 1)),
        ) + tuple(whole(shape) for shape in carry_shapes),
        scratch_shapes=scratch,
        compiler_params=pltpu.CompilerParams(
            dimension_semantics=("arbitrary",), vmem_limit_bytes=VMEM_LIMIT),
        name="mixer",
    )(*args)
    return outs[:6], outs[6:]


def _split_router_weight(w_router):
    d = w_router.shape[0]
    w_hi = w_router.astype(BF16)
    w_lo = (w_router - w_hi.astype(F32)).astype(BF16)
    zeros = lambda n: jnp.zeros((d, n), BF16)
    return jnp.concatenate([
        jnp.concatenate([w_hi, w_lo, zeros(ROUTER_COLS - 2 * N_EXPERTS)], axis=1),
        jnp.concatenate([w_hi, zeros(ROUTER_COLS - N_EXPERTS)], axis=1)], axis=0)


def _sc_mesh():
    return plsc.VectorSubcoreMesh(core_axis_name="c", subcore_axis_name="s")


def _sc_worker_id():
    return lax.axis_index("s") * SC_CORES + lax.axis_index("c")


def _sc_dispatch(h_packed, dest_sc, n_slots):
    n_tok, width = h_packed.shape
    per_w = n_tok // SC_WORKERS
    n_chunks = per_w // SC_CHUNK

    def body(h_hbm, dest_hbm, xs_hbm, idx_v, rows_v, sem_in, sem_out):
        base = _sc_worker_id() * per_w
        pltpu.sync_copy(dest_hbm.at[_sc_worker_id()], idx_v)

        def load(j, buf):
            return pltpu.make_async_copy(
                h_hbm.at[pl.ds(base + j * SC_CHUNK, SC_CHUNK)], rows_v.at[buf], sem_in)

        load(0, 0).start()

        @pl.loop(0, n_chunks, step=2)
        def _(j0):
            for buf in range(2):
                j = j0 + buf
                load(j, buf).wait()

                @pl.when(j + 1 < n_chunks)
                def _():
                    load(j + 1, 1 - buf).start()

                copies = [
                    pltpu.make_async_copy(
                        rows_v.at[buf], xs_hbm.at[idx_v.at[j * TOP_K + k]], sem_out)
                    for k in range(TOP_K)]
                for cp in copies:
                    cp.start()
                for cp in copies:
                    cp.wait()

    return pl.kernel(
        body,
        out_type=jax.ShapeDtypeStruct((n_slots, width), I32),
        mesh=_sc_mesh(),
        scratch_types=[
            pltpu.VMEM((n_chunks * TOP_K, SC_CHUNK), I32),
            pltpu.VMEM((2, SC_CHUNK, width), I32),
            pltpu.SemaphoreType.DMA,
            pltpu.SemaphoreType.DMA,
        ],
        name="sc_dispatch",
    )(h_packed, dest_sc)


def _sc_gather(ys_packed, dest_sc):
    width = ys_packed.shape[1]
    n_chunks = dest_sc.shape[1]
    per_w = n_chunks * SC_CHUNK
    n_rows = SC_WORKERS * per_w

    def body(ys_hbm, dest_hbm, out_hbm, idx_v, rows_v, sem_g, sem_w):
        base = _sc_worker_id() * per_w
        pltpu.sync_copy(dest_hbm.at[_sc_worker_id()], idx_v)

        def gather(j, buf):
            return pltpu.make_async_copy(ys_hbm.at[idx_v.at[j]], rows_v.at[buf], sem_g)

        def writeout(j, buf):
            return pltpu.make_async_copy(
                rows_v.at[buf], out_hbm.at[pl.ds(base + j * SC_CHUNK, SC_CHUNK)], sem_w)

        gather(0, 0).start()

        @pl.loop(0, n_chunks, step=2)
        def _(j0):
            for buf in range(2):
                j = j0 + buf
                gather(j, buf).wait()

                @pl.when(j >= 1)
                def _():
                    writeout(j - 1, 1 - buf).wait()

                @pl.when(j + 1 < n_chunks)
                def _():
                    gather(j + 1, 1 - buf).start()

                writeout(j, buf).start()

        writeout(n_chunks - 1, 1).wait()

    return pl.kernel(
        body,
        out_type=jax.ShapeDtypeStruct((n_rows, width), I32),
        mesh=_sc_mesh(),
        scratch_types=[
            pltpu.VMEM((n_chunks, SC_CHUNK), I32),
            pltpu.VMEM((2, SC_CHUNK, width), I32),
            pltpu.SemaphoreType.DMA,
            pltpu.SemaphoreType.DMA,
        ],
        name="sc_gather",
    )(ys_packed, dest_sc)


def _expert_kernel(be_ref, nv_ref, xs_ref, win_ref, bin_ref, wout_ref, bout_ref, ys_ref,
                   win_bf, wout_bf):
    i = pl.program_id(0)
    valid = nv_ref[i]
    prev = be_ref[jnp.maximum(i - 1, 0)]

    @pl.when((valid > 0) & ((i == 0) | (be_ref[i] != prev)))
    def _():
        def cast_in(r, c):
            rows = pl.ds(pl.multiple_of(r * CAST_ROWS, CAST_ROWS), CAST_ROWS)
            win_bf[rows, :] = win_ref[0, rows, :].astype(BF16)
            return c

        def cast_out(r, c):
            rows = pl.ds(pl.multiple_of(r * CAST_ROWS, CAST_ROWS), CAST_ROWS)
            wout_bf[rows, :] = wout_ref[0, rows, :].astype(BF16)
            return c

        lax.fori_loop(0, D_MODEL // CAST_ROWS, cast_in, 0)
        lax.fori_loop(0, D_FF // CAST_ROWS, cast_out, 0)

    n_sub = (valid + EXPERT_SUB - 1) // EXPERT_SUB
    row = lax.broadcasted_iota(I32, (EXPERT_SUB, 1), 0)

    def compute(j, c):
        rows = pl.ds(pl.multiple_of(j * EXPERT_SUB, EXPERT_SUB), EXPERT_SUB)
        packed = jnp.where(row + j * EXPERT_SUB < valid, xs_ref[rows, :], 0)
        lo, hi = _unpack_rows(packed)
        gu = (jnp.dot(lo.astype(BF16), win_bf[:PACKED, :], preferred_element_type=F32)
              + jnp.dot(hi.astype(BF16), win_bf[PACKED:, :], preferred_element_type=F32)
              + bin_ref[0])
        gate = jnp.minimum(gu[:, :D_FF], SWIGLU_LIMIT)
        up = jnp.clip(gu[:, D_FF:], -SWIGLU_LIMIT, SWIGLU_LIMIT)
        act = gate * _sigmoid(SWIGLU_ALPHA * gate) * (up + 1.0)
        out = jnp.dot(act.astype(BF16), wout_bf[...], preferred_element_type=F32) + bout_ref[0]
        ys_ref[rows, :] = _pack_rows(out)
        return c

    def clear(j, c):
        rows = pl.ds(pl.multiple_of(j * EXPERT_SUB, EXPERT_SUB), EXPERT_SUB)
        ys_ref[rows, :] = jnp.zeros((EXPERT_SUB, ys_ref.shape[1]), ys_ref.dtype)
        return c

    lax.fori_loop(0, n_sub, compute, 0)
    lax.fori_loop(n_sub, EXPERT_ROWS // EXPERT_SUB, clear, 0)


def _experts(layer, block_e, n_valid, xs, w_e_in, b_e_in, w_e_out, b_e_out):
    n_slots, width = xs.shape
    d = D_MODEL
    tm = EXPERT_ROWS
    grid_spec = pltpu.PrefetchScalarGridSpec(
        num_scalar_prefetch=2,
        grid=(n_slots // tm,),
        in_specs=[
            pl.BlockSpec((tm, width), lambda i, be, nv: (i, 0)),
            pl.BlockSpec((None, 1, d, 2 * D_FF), lambda i, be, nv: (layer, be[i], 0, 0)),
            pl.BlockSpec((None, 1, 1, 2 * D_FF), lambda i, be, nv: (layer, be[i], 0, 0)),
            pl.BlockSpec((None, 1, D_FF, d), lambda i, be, nv: (layer, be[i], 0, 0)),
            pl.BlockSpec((None, 1, 1, d), lambda i, be, nv: (layer, be[i], 0, 0)),
        ],
        out_specs=pl.BlockSpec((tm, width), lambda i, be, nv: (i, 0)),
        scratch_shapes=[pltpu.VMEM((d, 2 * D_FF), BF16), pltpu.VMEM((D_FF, d), BF16)],
    )
    return pl.pallas_call(
        _expert_kernel,
        out_shape=jax.ShapeDtypeStruct((n_slots, width), I32),
        grid_spec=grid_spec,
        compiler_params=pltpu.CompilerParams(
            dimension_semantics=("arbitrary",), vmem_limit_bytes=VMEM_LIMIT),
        name="experts",
    )(block_e, n_valid, xs, w_e_in, b_e_in.reshape(-1, N_EXPERTS, 1, 2 * D_FF),
      w_e_out, b_e_out.reshape(-1, N_EXPERTS, 1, d))


def _final_kernel(x_ref, y0_ref, y1_ref, y2_ref, y3_ref, wt_ref, mod_ref, g_ref, o_ref):
    x = _combined_rows(x_ref, (y0_ref, y1_ref, y2_ref, y3_ref), wt_ref, mod_ref[5])
    ms = jnp.mean(x * x, axis=-1, keepdims=True)
    o_ref[...] = x * lax.rsqrt(ms + NORM_EPS) * g_ref[...]


def _final_norm(x2d, pending, g):
    n_tok, d = x2d.shape
    tr = NORM_ROWS
    y4, wts, mod_l = pending
    return pl.pallas_call(
        _final_kernel,
        out_shape=jax.ShapeDtypeStruct((n_tok, d), F32),
        grid=(n_tok // tr,),
        in_specs=([pl.BlockSpec((tr, d), lambda i: (i, 0))] + _pending_specs(tr, n_tok)
                  + [_const_spec((1, d))]),
        out_specs=pl.BlockSpec((tr, d), lambda i: (i, 0)),
        compiler_params=pltpu.CompilerParams(
            dimension_semantics=("arbitrary",), vmem_limit_bytes=VMEM_LIMIT),
        name="final_norm",
    )(x2d, y4, y4, y4, y4, wts, mod_l, g.reshape(1, d))


def _moe_layer(layer, h_packed, idx, pos, counts, w_e_in, b_e_in, w_e_out, b_e_out):
    n_tok = h_packed.shape[0]
    tm = EXPERT_ROWS
    padded = ((counts + tm - 1) // tm) * tm
    pad_end = jnp.cumsum(padded)
    start_pad = pad_end - padded
    experts = jnp.arange(N_EXPERTS, dtype=I32)
    dest = jnp.sum(jnp.where(idx[:, :, None] == experts, start_pad, 0), axis=-1) + pos
    n_blocks = (n_tok * TOP_K) // tm + N_EXPERTS
    n_slots = n_blocks * tm
    block_start = jnp.arange(n_blocks, dtype=I32) * tm
    block_e = jnp.minimum(
        jnp.sum((block_start[:, None] >= pad_end[None, :]).astype(I32), axis=-1),
        N_EXPERTS - 1)
    hot_e = block_e[:, None] == experts[None, :]
    rows_left = jnp.sum(jnp.where(hot_e, (start_pad + counts)[None, :], 0), axis=-1) - block_start
    n_valid = jnp.where(block_start < pad_end[-1], jnp.clip(rows_left, 0, tm), 0).astype(I32)

    per_w = n_tok // SC_WORKERS
    n_chunks = per_w // SC_CHUNK
    dest_scatter = (dest.reshape(TOP_K, SC_WORKERS, n_chunks, SC_CHUNK)
                    .transpose(1, 2, 0, 3)
                    .reshape(SC_WORKERS, n_chunks * TOP_K, SC_CHUNK))
    xs = _sc_dispatch(h_packed, dest_scatter, n_slots)
    ys = _experts(layer, block_e, n_valid, xs, w_e_in, b_e_in, w_e_out, b_e_out)
    dest_gather = dest.reshape(SC_WORKERS, n_chunks * TOP_K, SC_CHUNK)
    return _sc_gather(ys, dest_gather)


def kernel(x, c, norm1_g, norm2_g, w_ada, b_ada, w_in, conv_w, conv_b, w_rg_a, b_rg_a,
           w_rg_x, b_rg_x, lru_lambda, w_pool, pool_scale, w_up_a, w_up_b, w_out,
           w_router, b_router, w_e_in, b_e_in, w_e_out, b_e_out, final_g):
    depth = w_in.shape[0]
    bsz, seq, d = x.shape
    assert bsz == BATCH and d == D_MODEL
    mod = _ada_mod(c, w_ada, b_ada).reshape(depth, bsz, 6, d).transpose(0, 2, 1, 3)
    w_rg = jnp.concatenate([w_rg_a, w_rg_x], axis=-1).astype(BF16)
    part = seq // SEQ_PARTS
    xs = [x[:, p * part:(p + 1) * part].transpose(1, 0, 2).reshape(part * bsz, d)
          for p in range(SEQ_PARTS)]
    pending = [None] * SEQ_PARTS
    for l in range(depth):
        mixer_weights = (
            mod[l], norm1_g[l], w_in[l].astype(BF16), conv_w[l], conv_b[l], w_rg[l],
            b_rg_a[l], b_rg_x[l], lru_lambda[l], w_pool[l].astype(BF16), pool_scale[l],
            w_up_a[l].astype(BF16), w_up_b[l].astype(BF16), w_out[l].astype(BF16),
            norm2_g[l], _split_router_weight(w_router[l]), b_router[l])
        carry = tuple(jnp.zeros(shape, F32) for shape in _carry_shapes())
        for p in range(SEQ_PARTS):
            (xs[p], h_packed, idx, pos, wts, counts), carry = _mixer_layer(
                xs[p], pending[p], carry, p * part, *mixer_weights)
            y4 = _moe_layer(l, h_packed, idx, pos, counts[:, 0],
                            w_e_in, b_e_in, w_e_out, b_e_out)
            pending[p] = (y4, wts.T, mod[l])
    outs = [_final_norm(xs[p], pending[p], final_g).reshape(part, bsz, d)
            for p in range(SEQ_PARTS)]
    return jnp.concatenate(outs, axis=0).transpose(1, 0, 2)
```

```python
import functools
import math

import jax
import jax.numpy as jnp
from jax import lax
from jax.experimental import pallas as pl
from jax.experimental.pallas import tpu as pltpu
from jax.experimental.pallas import tpu_sc as plsc

D_MODEL = 1024
BATCH = 8
N_HEADS = 8
HEAD_DIM = 128
CONV_WIDTH = 4
LRU_C = 8.0
POOL_WINDOWS = (2, 4, 8, 16)
POOL_WIDTH = 512
POOL_GROUP_DIM = 128
N_EXPERTS = 32
TOP_K = 4
D_FF = 1024
SWIGLU_LIMIT = 7.0
SWIGLU_ALPHA = 1.702
NORM_EPS = 1e-6

SUBLANES = 8
VMEM_LIMIT = 56 * 1024 * 1024
SC_CORES = 2
SC_SUBCORES = 16
SC_WORKERS = SC_CORES * SC_SUBCORES
SC_CHUNK = 64

MIX_STEPS = 32
SEQ_PARTS = 1
ROUTER_COLS = 128
EXPERT_ROWS = 1024
EXPERT_SUB = 256
ADA_COLS = 1536
NORM_ROWS = 512
CAST_ROWS = 128
PACKED = D_MODEL // 2

F32 = jnp.float32
BF16 = jnp.bfloat16
I32 = jnp.int32
HIGH_MASK = -65536
LOG2_E = math.log2(math.e)


def _const_spec(shape):
    nd = len(shape)
    return pl.BlockSpec(shape, lambda *_: (0,) * nd, pipeline_mode=pl.Buffered(1))


def _pack_rows(v):
    bits = lax.bitcast_convert_type(v.astype(BF16).astype(F32), I32)
    lo, hi = bits[:, :PACKED], bits[:, PACKED:]
    return (hi & HIGH_MASK) | lax.shift_right_logical(lo, 16)


def _unpack_rows(p):
    lo = lax.bitcast_convert_type(lax.shift_left(p, 16), F32)
    hi = lax.bitcast_convert_type(p & HIGH_MASK, F32)
    return lo, hi


def _sigmoid(v):
    return 0.5 * jnp.tanh(0.5 * v) + 0.5


def _per_batch(v, slab, op):
    rows, d = v.shape
    return op(v.reshape(rows // BATCH, BATCH, d), slab[None]).reshape(rows, d)


def _modulated_norm(x, g, shift, scale):
    ms = jnp.mean(x * x, axis=-1, keepdims=True)
    hn = x * lax.rsqrt(ms + NORM_EPS) * g
    return _per_batch(_per_batch(hn, 1.0 + scale, jnp.multiply), shift, jnp.add)


def _ada_kernel(c_ref, w_ref, b_ref, o_ref):
    c = c_ref[...]
    c_act = c * jax.nn.sigmoid(c)
    o_ref[0] = jnp.dot(c_act, w_ref[0], preferred_element_type=F32,
                       precision=lax.Precision.HIGHEST) + b_ref[0]


def _ada_mod(c, w_ada, b_ada):
    depth, d, n = w_ada.shape
    bsz = c.shape[0]
    tn = ADA_COLS
    return pl.pallas_call(
        _ada_kernel,
        out_shape=jax.ShapeDtypeStruct((depth, bsz, n), F32),
        grid=(depth, n // tn),
        in_specs=[
            pl.BlockSpec((bsz, d), lambda l, j: (0, 0)),
            pl.BlockSpec((1, d, tn), lambda l, j: (l, 0, j)),
            pl.BlockSpec((1, 1, tn), lambda l, j: (l, 0, j)),
        ],
        out_specs=pl.BlockSpec((1, bsz, tn), lambda l, j: (l, 0, j)),
        compiler_params=pltpu.CompilerParams(
            dimension_semantics=("arbitrary", "arbitrary"),
            vmem_limit_bytes=VMEM_LIMIT),
        name="ada_mod",
    )(c, w_ada, b_ada.reshape(depth, 1, n))


def _delay(a, steps, carry_ref):
    n = steps * BATCH
    keep = carry_ref.shape[0]
    prev = carry_ref[keep - n:, :]
    return jnp.concatenate([prev, a[:a.shape[0] - n]], axis=0)


def _save_tail(a, carry_ref):
    carry_ref[...] = a[a.shape[0] - carry_ref.shape[0]:]


def _combined_rows(x_ref, y_refs, wt_ref, gate2, rs=slice(None)):
    x = x_ref[rs, :]
    wt = wt_ref[rs, :]
    acc_lo = jnp.zeros((x.shape[0], PACKED), F32)
    acc_hi = jnp.zeros((x.shape[0], PACKED), F32)
    for k, y_ref in enumerate(y_refs):
        lo, hi = _unpack_rows(y_ref[rs, :])
        wk = wt[:, k:k + 1]
        acc_lo = acc_lo + wk * lo
        acc_hi = acc_hi + wk * hi
    ffn = jnp.concatenate([acc_lo, acc_hi], axis=-1)
    return x + _per_batch(ffn, gate2, jnp.multiply)


def _route_rows(h, wrt_ref, brc_ref, run_c):
    rows = h.shape[0]
    padded = jnp.dot(h.astype(BF16), wrt_ref[...], preferred_element_type=F32)
    logits = padded.T[:N_EXPERTS] + brc_ref[...]

    e_iota = lax.broadcasted_iota(I32, (N_EXPERTS, rows), 0)
    vals, idxs, hots = [], [], []
    cur = logits
    for _ in range(TOP_K):
        m = jnp.max(cur, axis=0, keepdims=True)
        am = jnp.min(jnp.where(cur == m, e_iota, N_EXPERTS), axis=0, keepdims=True)
        hot = e_iota == am
        vals.append(m)
        idxs.append(am)
        hots.append(hot)
        cur = jnp.where(hot, -jnp.inf, cur)
    exps = [jnp.exp(v - vals[0]) for v in vals]
    denom = exps[0] + exps[1] + exps[2] + exps[3]

    sel = (hots[0] | hots[1] | hots[2] | hots[3]).astype(F32)
    ri = lax.broadcasted_iota(I32, (rows, rows), 0)
    ci = lax.broadcasted_iota(I32, (rows, rows), 1)
    earlier = (ri < ci).astype(BF16)
    before = jnp.dot(sel.astype(BF16), earlier, preferred_element_type=F32) + run_c[...]
    run_c[...] = run_c[...] + jnp.sum(sel, axis=1, keepdims=True)
    poss = [jnp.sum(jnp.where(hot, before, 0.0), axis=0, keepdims=True).astype(I32)
            for hot in hots]
    return (jnp.concatenate(idxs, axis=0), jnp.concatenate(poss, axis=0),
            jnp.concatenate([e / denom for e in exps], axis=0))


N_CARRY = 6


def _mixer_kernel(with_combine, step0, *refs):
    refs = list(refs)
    x_ref = refs.pop(0)
    if with_combine:
        y_refs = [refs.pop(0) for _ in range(TOP_K)]
        wt_in_ref = refs.pop(0)
        modp_ref = refs.pop(0)
    (mod_ref, g1_ref, win_ref, convw_ref, convb_ref, wrg_ref, brga_ref, brgx_ref, lam_ref,
     wpool_ref, pscale_ref, wupa_ref, wupb_ref, wout_ref, g2_ref, wrt_ref, brc_ref) = refs[:17]
    carry_in = refs[17:17 + N_CARRY]
    o_ref, hp_ref, idx_ref, pos_ref, wt_ref, cnt_ref = refs[17 + N_CARRY:23 + N_CARRY]
    carry_out = refs[23 + N_CARRY:23 + 2 * N_CARRY]
    carries = refs[23 + 2 * N_CARRY:23 + 3 * N_CARRY]
    conv_c, p1_c, p2_c, p4_c, p8_c, h_c = carries
    run_c = refs[23 + 3 * N_CARRY]
    i = pl.program_id(0)
    rows = x_ref.shape[0]
    steps = rows // BATCH
    first_step = step0 + i * steps
    d = D_MODEL

    @pl.when(i == 0)
    def _():
        for src, dst in zip(carry_in, carries):
            dst[...] = src[...]
        run_c[...] = jnp.zeros(run_c.shape, run_c.dtype)

    if with_combine:
        x = _combined_rows(x_ref, y_refs, wt_in_ref, modp_ref[5])
    else:
        x = x_ref[...]
    shift1, scale1, gate1 = mod_ref[0], mod_ref[1], mod_ref[2]
    hb = _modulated_norm(x, g1_ref[...], shift1, scale1).astype(BF16)

    def in_proj(lo, hi):
        return jnp.dot(hb, win_ref[:, lo:hi], preferred_element_type=F32)

    x_lru = in_proj(0, d)
    cw = convw_ref[...]
    xc = (convb_ref[...] + cw[3:4] * x_lru
          + cw[2:3] * _delay(x_lru, 1, conv_c)
          + cw[1:2] * _delay(x_lru, 2, conv_c)
          + cw[0:1] * _delay(x_lru, 3, conv_c))
    _save_tail(x_lru, conv_c)
    xcb = xc.astype(BF16)
    pre = [jnp.dot(xcb[:, k * HEAD_DIM:(k + 1) * HEAD_DIM], wrg_ref[k],
                   preferred_element_type=F32) for k in range(N_HEADS)]
    pre_a = jnp.concatenate([p[:, :HEAD_DIM] for p in pre], axis=-1)
    pre_x = jnp.concatenate([p[:, HEAD_DIM:] for p in pre], axis=-1)
    r = _sigmoid(pre_a + brga_ref[...])
    gi = _sigmoid(pre_x + brgx_ref[...])
    rate = (-LRU_C * LOG2_E) * jax.nn.softplus(-lam_ref[...])
    a = jnp.exp2(r * rate)
    y = 1.0 - a * a
    mult = jnp.where(y > 0.0, y * lax.rsqrt(y), 0.0)
    u = mult * (gi * xc)
    h = h_c[...]
    hs = []
    for t in range(steps):
        sl = slice(t * BATCH, (t + 1) * BATCH)
        h = a[sl] * h + u[sl]
        hs.append(h)
    h_c[...] = h
    y_lru = jnp.concatenate(hs, axis=0) * jax.nn.gelu(in_proj(d, 2 * d))
    p_a = jnp.dot(y_lru.astype(BF16), wupa_ref[...], preferred_element_type=F32)
    merged = _sigmoid(in_proj(2 * d + POOL_WIDTH, 3 * d + POOL_WIDTH)) * p_a

    x_pool = in_proj(2 * d, 2 * d + POOL_WIDTH)
    g = POOL_GROUP_DIM
    s2 = x_pool + _delay(x_pool, 1, p1_c)
    s4 = s2[:, g:] + _delay(s2[:, g:], 2, p2_c)
    s8 = s4[:, g:] + _delay(s4[:, g:], 4, p4_c)
    s16 = s8[:, g:] + _delay(s8[:, g:], 8, p8_c)
    _save_tail(x_pool, p1_c)
    _save_tail(s2[:, g:], p2_c)
    _save_tail(s4[:, g:], p4_c)
    _save_tail(s8[:, g:], p8_c)
    wins = (s2[:, :g], s4[:, :g], s8[:, :g], s16)
    row = lax.broadcasted_iota(I32, (rows, 1), 0)
    t1 = (first_step + row // BATCH + 1).astype(F32)
    pooled = []
    for k, w in enumerate(POOL_WINDOWS):
        cnt = jnp.minimum(t1, float(w))
        pk = wins[k] / cnt - x_pool[:, k * g:(k + 1) * g]
        pooled.append(jnp.dot(pk.astype(BF16), wpool_ref[k], preferred_element_type=F32))
    y_pool = jnp.concatenate(pooled, axis=-1) * pscale_ref[...]
    p_b = jnp.dot(y_pool.astype(BF16), wupb_ref[...], preferred_element_type=F32)
    merged = merged + _sigmoid(in_proj(3 * d + POOL_WIDTH, 4 * d + POOL_WIDTH)) * p_b

    mix = jnp.dot(merged.astype(BF16), wout_ref[...], preferred_element_type=F32)
    x_mid = x + _per_batch(mix, gate1, jnp.multiply)
    o_ref[...] = x_mid
    for src, dst in zip(carries, carry_out):
        dst[...] = src[...]

    hr = _modulated_norm(x_mid, g2_ref[...], mod_ref[3], mod_ref[4])
    hp_ref[...] = _pack_rows(hr)
    idx, pos, wts = _route_rows(hr, wrt_ref, brc_ref, run_c)
    idx_ref[...] = idx
    pos_ref[...] = pos
    wt_ref[...] = wts
    cnt_ref[...] = run_c[...].astype(I32)


def _kmajor_index(k, nb):
    return lambda i: (k * nb + i, 0)


def _pending_specs(rows, n_tok):
    nb = n_tok // rows
    specs = [pl.BlockSpec((rows, PACKED), _kmajor_index(k, nb)) for k in range(TOP_K)]
    specs.append(pl.BlockSpec((rows, TOP_K), lambda i: (i, 0)))
    specs.append(_const_spec((6, BATCH, D_MODEL)))
    return specs


def _carry_shapes():
    g = POOL_GROUP_DIM
    return (((CONV_WIDTH - 1) * BATCH, D_MODEL), (1 * BATCH, 4 * g), (2 * BATCH, 3 * g),
            (4 * BATCH, 2 * g), (8 * BATCH, g), (BATCH, D_MODEL))


def _mixer_layer(x2d, pending, carry, step0, mod_l, g1, w_in, conv_w, conv_b, w_rg, b_rg_a,
                 b_rg_x, lam, w_pool, pool_scale, w_up_a, w_up_b, w_out, g2, w_route, b_router):
    n_tok, d = x2d.shape
    rows = MIX_STEPS * BATCH
    row = lambda v: v.reshape(1, -1)
    weights = (mod_l, row(g1), w_in, conv_w, row(conv_b), w_rg, row(b_rg_a), row(b_rg_x),
               row(lam), w_pool, row(pool_scale), w_up_a, w_up_b, w_out, row(g2),
               w_route, b_router.reshape(N_EXPERTS, 1))
    args = [x2d]
    in_specs = [pl.BlockSpec((rows, d), lambda i: (i, 0))]
    if pending is not None:
        y4, wts_prev, mod_prev = pending
        args += [y4, y4, y4, y4, wts_prev, mod_prev]
        in_specs += _pending_specs(rows, n_tok)
    args += list(weights) + list(carry)
    in_specs += [_const_spec(a.shape) for a in weights + tuple(carry)]
    carry_shapes = _carry_shapes()
    scratch = [pltpu.VMEM(shape, F32) for shape in carry_shapes]
    scratch.append(pltpu.VMEM((N_EXPERTS, 1), F32))
    routed = lambda dtype: jax.ShapeDtypeStruct((TOP_K, n_tok), dtype)
    routed_spec = pl.BlockSpec((TOP_K, rows), lambda i: (0, i))
    whole = lambda shape: pl.BlockSpec(shape, lambda i: (0,) * len(shape))
    outs = pl.pallas_call(
        functools.partial(_mixer_kernel, pending is not None, step0),
        out_shape=(
            jax.ShapeDtypeStruct((n_tok, d), F32),
            jax.ShapeDtypeStruct((n_tok, PACKED), I32),
            routed(I32), routed(I32), routed(F32),
            jax.ShapeDtypeStruct((N_EXPERTS, 1), I32),
        ) + tuple(jax.ShapeDtypeStruct(shape, F32) for shape in carry_shapes),
        grid=(n_tok // rows,),
        in_specs=in_specs,
        out_specs=(
            pl.BlockSpec((rows, d), lambda i: (i, 0)),
            pl.BlockSpec((rows, PACKED), lambda i: (i, 0)),
            routed_spec, routed_spec, routed_spec,
            whole((N_EXPERTS, 1)),
        ) + tuple(whole(shape) for shape in carry_shapes),
        scratch_shapes=scratch,
        compiler_params=pltpu.CompilerParams(
            dimension_semantics=("arbitrary",), vmem_limit_bytes=VMEM_LIMIT),
        name="mixer",
    )(*args)
    return outs[:6], outs[6:]


def _padded_router_weight(w_router):
    pad = jnp.zeros((w_router.shape[0], ROUTER_COLS - N_EXPERTS), BF16)
    return jnp.concatenate([w_router.astype(BF16), pad], axis=1)


def _sc_mesh():
    return plsc.VectorSubcoreMesh(core_axis_name="c", subcore_axis_name="s")


def _sc_worker_id():
    return lax.axis_index("s") * SC_CORES + lax.axis_index("c")


def _sc_dispatch(h_packed, dest_sc, n_slots):
    n_tok, width = h_packed.shape
    per_w = n_tok // SC_WORKERS
    n_chunks = per_w // SC_CHUNK

    def body(h_hbm, dest_hbm, xs_hbm, idx_v, rows_v, sem_in, sem_out):
        base = _sc_worker_id() * per_w
        pltpu.sync_copy(dest_hbm.at[_sc_worker_id()], idx_v)

        def load(j, buf):
            return pltpu.make_async_copy(
                h_hbm.at[pl.ds(base + j * SC_CHUNK, SC_CHUNK)], rows_v.at[buf], sem_in)

        load(0, 0).start()

        @pl.loop(0, n_chunks, step=2)
        def _(j0):
            for buf in range(2):
                j = j0 + buf
                load(j, buf).wait()

                @pl.when(j + 1 < n_chunks)
                def _():
                    load(j + 1, 1 - buf).start()

                copies = [
                    pltpu.make_async_copy(
                        rows_v.at[buf], xs_hbm.at[idx_v.at[j * TOP_K + k]], sem_out)
                    for k in range(TOP_K)]
                for cp in copies:
                    cp.start()
                for cp in copies:
                    cp.wait()

    return pl.kernel(
        body,
        out_type=jax.ShapeDtypeStruct((n_slots, width), I32),
        mesh=_sc_mesh(),
        scratch_types=[
            pltpu.VMEM((n_chunks * TOP_K, SC_CHUNK), I32),
            pltpu.VMEM((2, SC_CHUNK, width), I32),
            pltpu.SemaphoreType.DMA,
            pltpu.SemaphoreType.DMA,
        ],
        name="sc_dispatch",
    )(h_packed, dest_sc)


def _sc_gather(ys_packed, dest_sc):
    width = ys_packed.shape[1]
    n_chunks = dest_sc.shape[1]
    per_w = n_chunks * SC_CHUNK
    n_rows = SC_WORKERS * per_w

    def body(ys_hbm, dest_hbm, out_hbm, idx_v, rows_v, sem_g, sem_w):
        base = _sc_worker_id() * per_w
        pltpu.sync_copy(dest_hbm.at[_sc_worker_id()], idx_v)

        def gather(j, buf):
            return pltpu.make_async_copy(ys_hbm.at[idx_v.at[j]], rows_v.at[buf], sem_g)

        def writeout(j, buf):
            return pltpu.make_async_copy(
                rows_v.at[buf], out_hbm.at[pl.ds(base + j * SC_CHUNK, SC_CHUNK)], sem_w)

        gather(0, 0).start()

        @pl.loop(0, n_chunks, step=2)
        def _(j0):
            for buf in range(2):
                j = j0 + buf
                gather(j, buf).wait()

                @pl.when(j >= 1)
                def _():
                    writeout(j - 1, 1 - buf).wait()

                @pl.when(j + 1 < n_chunks)
                def _():
                    gather(j + 1, 1 - buf).start()

                writeout(j, buf).start()

        writeout(n_chunks - 1, 1).wait()

    return pl.kernel(
        body,
        out_type=jax.ShapeDtypeStruct((n_rows, width), I32),
        mesh=_sc_mesh(),
        scratch_types=[
            pltpu.VMEM((n_chunks, SC_CHUNK), I32),
            pltpu.VMEM((2, SC_CHUNK, width), I32),
            pltpu.SemaphoreType.DMA,
            pltpu.SemaphoreType.DMA,
        ],
        name="sc_gather",
    )(ys_packed, dest_sc)


def _expert_kernel(layer, be_ref, nv_ref, first_ref, next_ref, xs_ref, win_hbm, bin_ref,
                   wout_hbm, bout_ref, ys_ref, win_f32, wout_f32, win_bf, wout_bf, sem):
    i = pl.program_id(0)
    valid = nv_ref[i]

    def fetch(e):
        return (pltpu.make_async_copy(win_hbm.at[layer, e], win_f32, sem.at[0]),
                pltpu.make_async_copy(wout_hbm.at[layer, e], wout_f32, sem.at[1]))

    @pl.when(first_ref[i] == 1)
    def _():
        @pl.when(i == 0)
        def _():
            for cp in fetch(be_ref[i]):
                cp.start()

        for cp in fetch(be_ref[i]):
            cp.wait()

        def cast_in(r, c):
            rows = pl.ds(pl.multiple_of(r * CAST_ROWS, CAST_ROWS), CAST_ROWS)
            win_bf[rows, :] = win_f32[rows, :].astype(BF16)
            return c

        def cast_out(r, c):
            rows = pl.ds(pl.multiple_of(r * CAST_ROWS, CAST_ROWS), CAST_ROWS)
            wout_bf[rows, :] = wout_f32[rows, :].astype(BF16)
            return c

        lax.fori_loop(0, D_MODEL // CAST_ROWS, cast_in, 0)
        lax.fori_loop(0, D_FF // CAST_ROWS, cast_out, 0)

        @pl.when(next_ref[i] >= 0)
        def _():
            for cp in fetch(next_ref[i]):
                cp.start()

    n_sub = (valid + EXPERT_SUB - 1) // EXPERT_SUB
    row = lax.broadcasted_iota(I32, (EXPERT_SUB, 1), 0)

    def compute(j, c):
        rows = pl.ds(pl.multiple_of(j * EXPERT_SUB, EXPERT_SUB), EXPERT_SUB)
        packed = jnp.where(row + j * EXPERT_SUB < valid, xs_ref[rows, :], 0)
        lo, hi = _unpack_rows(packed)
        gu = (jnp.dot(lo.astype(BF16), win_bf[:PACKED, :], preferred_element_type=F32)
              + jnp.dot(hi.astype(BF16), win_bf[PACKED:, :], preferred_element_type=F32)
              + bin_ref[0])
        gate = jnp.minimum(gu[:, :D_FF], SWIGLU_LIMIT)
        up = jnp.clip(gu[:, D_FF:], -SWIGLU_LIMIT, SWIGLU_LIMIT)
        act = gate * _sigmoid(SWIGLU_ALPHA * gate) * (up + 1.0)
        out = jnp.dot(act.astype(BF16), wout_bf[...], preferred_element_type=F32) + bout_ref[0]
        ys_ref[rows, :] = _pack_rows(out)
        return c

    def clear(j, c):
        rows = pl.ds(pl.multiple_of(j * EXPERT_SUB, EXPERT_SUB), EXPERT_SUB)
        ys_ref[rows, :] = jnp.zeros((EXPERT_SUB, ys_ref.shape[1]), ys_ref.dtype)
        return c

    lax.fori_loop(0, n_sub, compute, 0)
    lax.fori_loop(n_sub, EXPERT_ROWS // EXPERT_SUB, clear, 0)


def _experts(layer, block_e, n_valid, is_first, next_e, xs, w_e_in, b_e_in, w_e_out, b_e_out):
    n_slots, width = xs.shape
    d = D_MODEL
    tm = EXPERT_ROWS
    bias_index = lambda i, be, nv, fi, nx: (layer, be[i], 0, 0)
    grid_spec = pltpu.PrefetchScalarGridSpec(
        num_scalar_prefetch=4,
        grid=(n_slots // tm,),
        in_specs=[
            pl.BlockSpec((tm, width), lambda i, be, nv, fi, nx: (i, 0)),
            pl.BlockSpec(memory_space=pl.ANY),
            pl.BlockSpec((None, 1, 1, 2 * D_FF), bias_index),
            pl.BlockSpec(memory_space=pl.ANY),
            pl.BlockSpec((None, 1, 1, d), bias_index),
        ],
        out_specs=pl.BlockSpec((tm, width), lambda i, be, nv, fi, nx: (i, 0)),
        scratch_shapes=[
            pltpu.VMEM((d, 2 * D_FF), F32), pltpu.VMEM((D_FF, d), F32),
            pltpu.VMEM((d, 2 * D_FF), BF16), pltpu.VMEM((D_FF, d), BF16),
            pltpu.SemaphoreType.DMA((2,)),
        ],
    )
    return pl.pallas_call(
        functools.partial(_expert_kernel, layer),
        out_shape=jax.ShapeDtypeStruct((n_slots, width), I32),
        grid_spec=grid_spec,
        compiler_params=pltpu.CompilerParams(
            dimension_semantics=("arbitrary",), vmem_limit_bytes=VMEM_LIMIT),
        name="experts",
    )(block_e, n_valid, is_first, next_e, xs, w_e_in,
      b_e_in.reshape(-1, N_EXPERTS, 1, 2 * D_FF), w_e_out, b_e_out.reshape(-1, N_EXPERTS, 1, d))


def _final_kernel(x_ref, y0_ref, y1_ref, y2_ref, y3_ref, wt_ref, mod_ref, g_ref, o_ref):
    x = _combined_rows(x_ref, (y0_ref, y1_ref, y2_ref, y3_ref), wt_ref, mod_ref[5])
    ms = jnp.mean(x * x, axis=-1, keepdims=True)
    o_ref[...] = x * lax.rsqrt(ms + NORM_EPS) * g_ref[...]


def _final_norm(x2d, pending, g):
    n_tok, d = x2d.shape
    tr = NORM_ROWS
    y4, wts, mod_l = pending
    return pl.pallas_call(
        _final_kernel,
        out_shape=jax.ShapeDtypeStruct((n_tok, d), F32),
        grid=(n_tok // tr,),
        in_specs=([pl.BlockSpec((tr, d), lambda i: (i, 0))] + _pending_specs(tr, n_tok)
                  + [_const_spec((1, d))]),
        out_specs=pl.BlockSpec((tr, d), lambda i: (i, 0)),
        compiler_params=pltpu.CompilerParams(
            dimension_semantics=("arbitrary",), vmem_limit_bytes=VMEM_LIMIT),
        name="final_norm",
    )(x2d, y4, y4, y4, y4, wts, mod_l, g.reshape(1, d))


def _moe_layer(layer, h_packed, idx, pos, counts, w_e_in, b_e_in, w_e_out, b_e_out):
    n_tok = h_packed.shape[0]
    tm = EXPERT_ROWS
    padded = ((counts + tm - 1) // tm) * tm
    pad_end = jnp.cumsum(padded)
    start_pad = pad_end - padded
    experts = jnp.arange(N_EXPERTS, dtype=I32)
    dest = jnp.sum(jnp.where(idx[:, :, None] == experts, start_pad, 0), axis=-1) + pos
    n_blocks = (n_tok * TOP_K) // tm + N_EXPERTS
    n_slots = n_blocks * tm
    block_start = jnp.arange(n_blocks, dtype=I32) * tm
    block_e = jnp.minimum(
        jnp.sum((block_start[:, None] >= pad_end[None, :]).astype(I32), axis=-1),
        N_EXPERTS - 1)
    hot_e = block_e[:, None] == experts[None, :]
    rows_left = jnp.sum(jnp.where(hot_e, (start_pad + counts)[None, :], 0), axis=-1) - block_start
    n_valid = jnp.where(block_start < pad_end[-1], jnp.clip(rows_left, 0, tm), 0).astype(I32)
    is_first = jnp.sum(jnp.where(hot_e, start_pad[None, :], 0), axis=-1) == block_start
    is_first = (is_first & (n_valid > 0)).astype(I32)
    later = (experts[None, :] > experts[:, None]) & (counts[None, :] > 0)
    next_of = jnp.min(jnp.where(later, experts[None, :], N_EXPERTS), axis=-1)
    next_of = jnp.where(next_of == N_EXPERTS, -1, next_of)
    next_e = jnp.sum(jnp.where(hot_e, next_of[None, :], 0), axis=-1).astype(I32)

    per_w = n_tok // SC_WORKERS
    n_chunks = per_w // SC_CHUNK
    dest_scatter = (dest.reshape(TOP_K, SC_WORKERS, n_chunks, SC_CHUNK)
                    .transpose(1, 2, 0, 3)
                    .reshape(SC_WORKERS, n_chunks * TOP_K, SC_CHUNK))
    xs = _sc_dispatch(h_packed, dest_scatter, n_slots)
    ys = _experts(layer, block_e, n_valid, is_first, next_e, xs,
                  w_e_in, b_e_in, w_e_out, b_e_out)
    dest_gather = dest.reshape(SC_WORKERS, n_chunks * TOP_K, SC_CHUNK)
    return _sc_gather(ys, dest_gather)


def kernel(x, c, norm1_g, norm2_g, w_ada, b_ada, w_in, conv_w, conv_b, w_rg_a, b_rg_a,
           w_rg_x, b_rg_x, lru_lambda, w_pool, pool_scale, w_up_a, w_up_b, w_out,
           w_router, b_router, w_e_in, b_e_in, w_e_out, b_e_out, final_g):
    depth = w_in.shape[0]
    bsz, seq, d = x.shape
    assert bsz == BATCH and d == D_MODEL
    mod = _ada_mod(c, w_ada, b_ada).reshape(depth, bsz, 6, d).transpose(0, 2, 1, 3)
    w_rg = jnp.concatenate([w_rg_a, w_rg_x], axis=-1).astype(BF16)
    part = seq // SEQ_PARTS
    xs = [x[:, p * part:(p + 1) * part].transpose(1, 0, 2).reshape(part * bsz, d)
          for p in range(SEQ_PARTS)]
    pending = [None] * SEQ_PARTS
    for l in range(depth):
        mixer_weights = (
            mod[l], norm1_g[l], w_in[l].astype(BF16), conv_w[l], conv_b[l], w_rg[l],
            b_rg_a[l], b_rg_x[l], lru_lambda[l], w_pool[l].astype(BF16), pool_scale[l],
            w_up_a[l].astype(BF16), w_up_b[l].astype(BF16), w_out[l].astype(BF16),
            norm2_g[l], _padded_router_weight(w_router[l]), b_router[l])
        carry = tuple(jnp.zeros(shape, F32) for shape in _carry_shapes())
        for p in range(SEQ_PARTS):
            (xs[p], h_packed, idx, pos, wts, counts), carry = _mixer_layer(
                xs[p], pending[p], carry, p * part, *mixer_weights)
            y4 = _moe_layer(l, h_packed, idx, pos, counts[:, 0],
                            w_e_in, b_e_in, w_e_out, b_e_out)
            pending[p] = (y4, wts.T, mod[l])
    outs = [_final_norm(xs[p], pending[p], final_g).reshape(part, bsz, d)
            for p in range(SEQ_PARTS)]
    return jnp.concatenate(outs, axis=0).transpose(1, 0, 2)
```

```python
import functools
import math

import jax
import jax.numpy as jnp
from jax import lax
from jax.experimental import pallas as pl
from jax.experimental.pallas import tpu as pltpu
from jax.experimental.pallas import tpu_sc as plsc

D_MODEL = 1024
BATCH = 8
N_HEADS = 8
HEAD_DIM = 128
CONV_WIDTH = 4
LRU_C = 8.0
POOL_WINDOWS = (2, 4, 8, 16)
POOL_WIDTH = 512
POOL_GROUP_DIM = 128
N_EXPERTS = 32
TOP_K = 4
D_FF = 1024
SWIGLU_LIMIT = 7.0
SWIGLU_ALPHA = 1.702
NORM_EPS = 1e-6

SUBLANES = 8
VMEM_LIMIT = 56 * 1024 * 1024
SC_CORES = 2
SC_SUBCORES = 16
SC_WORKERS = SC_CORES * SC_SUBCORES
SC_CHUNK = 64

MIX_STEPS = 32
SEQ_PARTS = 1
ROUTER_COLS = 128
EXPERT_ROWS = 1024
EXPERT_SUB = 256
ADA_COLS = 1536
NORM_ROWS = 512
CAST_ROWS = 128
PACKED = D_MODEL // 2

F32 = jnp.float32
BF16 = jnp.bfloat16
I32 = jnp.int32
HIGH_MASK = -65536
LOG2_E = math.log2(math.e)


def _const_spec(shape):
    nd = len(shape)
    return pl.BlockSpec(shape, lambda *_: (0,) * nd, pipeline_mode=pl.Buffered(1))


def _pack_rows(v):
    bits = lax.bitcast_convert_type(v.astype(BF16).astype(F32), I32)
    lo, hi = bits[:, :PACKED], bits[:, PACKED:]
    return (hi & HIGH_MASK) | lax.shift_right_logical(lo, 16)


def _unpack_rows(p):
    lo = lax.bitcast_convert_type(lax.shift_left(p, 16), F32)
    hi = lax.bitcast_convert_type(p & HIGH_MASK, F32)
    return lo, hi


def _sigmoid(v):
    return 0.5 * jnp.tanh(0.5 * v) + 0.5


def _per_batch(v, slab, op):
    rows, d = v.shape
    return op(v.reshape(rows // BATCH, BATCH, d), slab[None]).reshape(rows, d)


def _modulated_norm(x, g, shift, scale):
    ms = jnp.mean(x * x, axis=-1, keepdims=True)
    hn = x * lax.rsqrt(ms + NORM_EPS) * g
    return _per_batch(_per_batch(hn, 1.0 + scale, jnp.multiply), shift, jnp.add)


def _ada_kernel(c_ref, w_ref, b_ref, o_ref):
    c = c_ref[...]
    c_act = c * jax.nn.sigmoid(c)
    o_ref[0] = jnp.dot(c_act, w_ref[0], preferred_element_type=F32,
                       precision=lax.Precision.HIGHEST) + b_ref[0]


def _ada_mod(c, w_ada, b_ada):
    depth, d, n = w_ada.shape
    bsz = c.shape[0]
    tn = ADA_COLS
    return pl.pallas_call(
        _ada_kernel,
        out_shape=jax.ShapeDtypeStruct((depth, bsz, n), F32),
        grid=(depth, n // tn),
        in_specs=[
            pl.BlockSpec((bsz, d), lambda l, j: (0, 0)),
            pl.BlockSpec((1, d, tn), lambda l, j: (l, 0, j)),
            pl.BlockSpec((1, 1, tn), lambda l, j: (l, 0, j)),
        ],
        out_specs=pl.BlockSpec((1, bsz, tn), lambda l, j: (l, 0, j)),
        compiler_params=pltpu.CompilerParams(
            dimension_semantics=("arbitrary", "arbitrary"),
            vmem_limit_bytes=VMEM_LIMIT),
        name="ada_mod",
    )(c, w_ada, b_ada.reshape(depth, 1, n))


def _delay(a, steps, carry_ref):
    n = steps * BATCH
    keep = carry_ref.shape[0]
    prev = carry_ref[keep - n:, :]
    return jnp.concatenate([prev, a[:a.shape[0] - n]], axis=0)


def _save_tail(a, carry_ref):
    carry_ref[...] = a[a.shape[0] - carry_ref.shape[0]:]


def _combined_rows(x_ref, y_refs, wt_ref, gate2, rs=slice(None)):
    x = x_ref[rs, :]
    wt = wt_ref[rs, :]
    acc_lo = jnp.zeros((x.shape[0], PACKED), F32)
    acc_hi = jnp.zeros((x.shape[0], PACKED), F32)
    for k, y_ref in enumerate(y_refs):
        lo, hi = _unpack_rows(y_ref[rs, :])
        wk = wt[:, k:k + 1]
        acc_lo = acc_lo + wk * lo
        acc_hi = acc_hi + wk * hi
    ffn = jnp.concatenate([acc_lo, acc_hi], axis=-1)
    return x + _per_batch(ffn, gate2, jnp.multiply)


def _route_rows(h, wrt_ref, brc_ref, run_c):
    rows = h.shape[0]
    padded = jnp.dot(h.astype(BF16), wrt_ref[...], preferred_element_type=F32)
    logits = padded.T[:N_EXPERTS] + brc_ref[...]

    e_iota = lax.broadcasted_iota(I32, (N_EXPERTS, rows), 0)
    vals, idxs, hots = [], [], []
    cur = logits
    for _ in range(TOP_K):
        m = jnp.max(cur, axis=0, keepdims=True)
        am = jnp.min(jnp.where(cur == m, e_iota, N_EXPERTS), axis=0, keepdims=True)
        hot = e_iota == am
        vals.append(m)
        idxs.append(am)
        hots.append(hot)
        cur = jnp.where(hot, -jnp.inf, cur)
    exps = [jnp.exp(v - vals[0]) for v in vals]
    denom = exps[0] + exps[1] + exps[2] + exps[3]

    sel = (hots[0] | hots[1] | hots[2] | hots[3]).astype(F32)
    ri = lax.broadcasted_iota(I32, (rows, rows), 0)
    ci = lax.broadcasted_iota(I32, (rows, rows), 1)
    earlier = (ri < ci).astype(BF16)
    before = jnp.dot(sel.astype(BF16), earlier, preferred_element_type=F32) + run_c[...]
    run_c[...] = run_c[...] + jnp.sum(sel, axis=1, keepdims=True)
    poss = [jnp.sum(jnp.where(hot, before, 0.0), axis=0, keepdims=True).astype(I32)
            for hot in hots]
    return (jnp.concatenate(idxs, axis=0), jnp.concatenate(poss, axis=0),
            jnp.concatenate([e / denom for e in exps], axis=0))


N_CARRY = 6


def _mixer_kernel(with_combine, step0, *refs):
    refs = list(refs)
    x_ref = refs.pop(0)
    if with_combine:
        y_refs = [refs.pop(0) for _ in range(TOP_K)]
        wt_in_ref = refs.pop(0)
        modp_ref = refs.pop(0)
    (mod_ref, g1_ref, win_ref, convw_ref, convb_ref, wrg_ref, brga_ref, brgx_ref, lam_ref,
     wpool_ref, pscale_ref, wupa_ref, wupb_ref, wout_ref, g2_ref, wrt_ref, brc_ref) = refs[:17]
    carry_in = refs[17:17 + N_CARRY]
    o_ref, hp_ref, idx_ref, pos_ref, wt_ref, cnt_ref = refs[17 + N_CARRY:23 + N_CARRY]
    carry_out = refs[23 + N_CARRY:23 + 2 * N_CARRY]
    carries = refs[23 + 2 * N_CARRY:23 + 3 * N_CARRY]
    conv_c, p1_c, p2_c, p4_c, p8_c, h_c = carries
    run_c = refs[23 + 3 * N_CARRY]
    i = pl.program_id(0)
    rows = o_ref.shape[0]
    steps = rows // BATCH
    first_step = step0 + i * steps
    d = D_MODEL

    @pl.when(i == 0)
    def _():
        for src, dst in zip(carry_in, carries):
            dst[...] = src[...]
        run_c[...] = jnp.zeros(run_c.shape, run_c.dtype)

    if with_combine:
        x = _combined_rows(x_ref, y_refs, wt_in_ref, modp_ref[5])
    else:
        x = pltpu.einshape("btd->(tb)d", x_ref[...])
    shift1, scale1, gate1 = mod_ref[0], mod_ref[1], mod_ref[2]
    hb = _modulated_norm(x, g1_ref[...], shift1, scale1).astype(BF16)

    def in_proj(lo, hi):
        return jnp.dot(hb, win_ref[:, lo:hi], preferred_element_type=F32)

    x_lru = in_proj(0, d)
    cw = convw_ref[...]
    xc = (convb_ref[...] + cw[3:4] * x_lru
          + cw[2:3] * _delay(x_lru, 1, conv_c)
          + cw[1:2] * _delay(x_lru, 2, conv_c)
          + cw[0:1] * _delay(x_lru, 3, conv_c))
    _save_tail(x_lru, conv_c)
    xcb = xc.astype(BF16)
    pre = [jnp.dot(xcb[:, k * HEAD_DIM:(k + 1) * HEAD_DIM], wrg_ref[k],
                   preferred_element_type=F32) for k in range(N_HEADS)]
    pre_a = jnp.concatenate([p[:, :HEAD_DIM] for p in pre], axis=-1)
    pre_x = jnp.concatenate([p[:, HEAD_DIM:] for p in pre], axis=-1)
    r = _sigmoid(pre_a + brga_ref[...])
    gi = _sigmoid(pre_x + brgx_ref[...])
    rate = (-LRU_C * LOG2_E) * jax.nn.softplus(-lam_ref[...])
    a = jnp.exp2(r * rate)
    y = 1.0 - a * a
    mult = jnp.where(y > 0.0, y * lax.rsqrt(y), 0.0)
    u = mult * (gi * xc)
    h = h_c[...]
    hs = []
    for t in range(steps):
        sl = slice(t * BATCH, (t + 1) * BATCH)
        h = a[sl] * h + u[sl]
        hs.append(h)
    h_c[...] = h
    y_lru = jnp.concatenate(hs, axis=0) * jax.nn.gelu(in_proj(d, 2 * d))
    p_a = jnp.dot(y_lru.astype(BF16), wupa_ref[...], preferred_element_type=F32)
    merged = _sigmoid(in_proj(2 * d + POOL_WIDTH, 3 * d + POOL_WIDTH)) * p_a

    x_pool = in_proj(2 * d, 2 * d + POOL_WIDTH)
    g = POOL_GROUP_DIM
    s2 = x_pool + _delay(x_pool, 1, p1_c)
    s4 = s2[:, g:] + _delay(s2[:, g:], 2, p2_c)
    s8 = s4[:, g:] + _delay(s4[:, g:], 4, p4_c)
    s16 = s8[:, g:] + _delay(s8[:, g:], 8, p8_c)
    _save_tail(x_pool, p1_c)
    _save_tail(s2[:, g:], p2_c)
    _save_tail(s4[:, g:], p4_c)
    _save_tail(s8[:, g:], p8_c)
    wins = (s2[:, :g], s4[:, :g], s8[:, :g], s16)
    row = lax.broadcasted_iota(I32, (rows, 1), 0)
    t1 = (first_step + row // BATCH + 1).astype(F32)
    pooled = []
    for k, w in enumerate(POOL_WINDOWS):
        cnt = jnp.minimum(t1, float(w))
        pk = wins[k] / cnt - x_pool[:, k * g:(k + 1) * g]
        pooled.append(jnp.dot(pk.astype(BF16), wpool_ref[k], preferred_element_type=F32))
    y_pool = jnp.concatenate(pooled, axis=-1) * pscale_ref[...]
    p_b = jnp.dot(y_pool.astype(BF16), wupb_ref[...], preferred_element_type=F32)
    merged = merged + _sigmoid(in_proj(3 * d + POOL_WIDTH, 4 * d + POOL_WIDTH)) * p_b

    mix = jnp.dot(merged.astype(BF16), wout_ref[...], preferred_element_type=F32)
    x_mid = x + _per_batch(mix, gate1, jnp.multiply)
    o_ref[...] = x_mid
    for src, dst in zip(carries, carry_out):
        dst[...] = src[...]

    hr = _modulated_norm(x_mid, g2_ref[...], mod_ref[3], mod_ref[4])
    hp_ref[...] = _pack_rows(hr)
    idx, pos, wts = _route_rows(hr, wrt_ref, brc_ref, run_c)
    idx_ref[...] = idx
    pos_ref[...] = pos
    wt_ref[...] = wts
    cnt_ref[...] = run_c[...].astype(I32)


def _kmajor_index(k, nb):
    return lambda i: (k * nb + i, 0)


def _pending_specs(rows, n_tok):
    nb = n_tok // rows
    specs = [pl.BlockSpec((rows, PACKED), _kmajor_index(k, nb)) for k in range(TOP_K)]
    specs.append(pl.BlockSpec((rows, TOP_K), lambda i: (i, 0)))
    specs.append(_const_spec((6, BATCH, D_MODEL)))
    return specs


def _carry_shapes():
    g = POOL_GROUP_DIM
    return (((CONV_WIDTH - 1) * BATCH, D_MODEL), (1 * BATCH, 4 * g), (2 * BATCH, 3 * g),
            (4 * BATCH, 2 * g), (8 * BATCH, g), (BATCH, D_MODEL))


def _mixer_layer(x2d, pending, carry, step0, mod_l, g1, w_in, conv_w, conv_b, w_rg, b_rg_a,
                 b_rg_x, lam, w_pool, pool_scale, w_up_a, w_up_b, w_out, g2, w_route, b_router):
    rows = MIX_STEPS * BATCH
    row = lambda v: v.reshape(1, -1)
    weights = (mod_l, row(g1), w_in, conv_w, row(conv_b), w_rg, row(b_rg_a), row(b_rg_x),
               row(lam), w_pool, row(pool_scale), w_up_a, w_up_b, w_out, row(g2),
               w_route, b_router.reshape(N_EXPERTS, 1))
    args = [x2d]
    if pending is None:
        n_tok, d = x2d.shape[0] * x2d.shape[1], x2d.shape[2]
        in_specs = [pl.BlockSpec((BATCH, MIX_STEPS, d), lambda i: (0, i, 0))]
    else:
        n_tok, d = x2d.shape
        in_specs = [pl.BlockSpec((rows, d), lambda i: (i, 0))]
        y4, wts_prev, mod_prev = pending
        args += [y4, y4, y4, y4, wts_prev, mod_prev]
        in_specs += _pending_specs(rows, n_tok)
    args += list(weights) + list(carry)
    in_specs += [_const_spec(a.shape) for a in weights + tuple(carry)]
    carry_shapes = _carry_shapes()
    scratch = [pltpu.VMEM(shape, F32) for shape in carry_shapes]
    scratch.append(pltpu.VMEM((N_EXPERTS, 1), F32))
    routed = lambda dtype: jax.ShapeDtypeStruct((TOP_K, n_tok), dtype)
    routed_spec = pl.BlockSpec((TOP_K, rows), lambda i: (0, i))
    whole = lambda shape: pl.BlockSpec(shape, lambda i: (0,) * len(shape))
    outs = pl.pallas_call(
        functools.partial(_mixer_kernel, pending is not None, step0),
        out_shape=(
            jax.ShapeDtypeStruct((n_tok, d), F32),
            jax.ShapeDtypeStruct((n_tok, PACKED), I32),
            routed(I32), routed(I32), routed(F32),
            jax.ShapeDtypeStruct((N_EXPERTS, 1), I32),
        ) + tuple(jax.ShapeDtypeStruct(shape, F32) for shape in carry_shapes),
        grid=(n_tok // rows,),
        in_specs=in_specs,
        out_specs=(
            pl.BlockSpec((rows, d), lambda i: (i, 0)),
            pl.BlockSpec((rows, PACKED), lambda i: (i, 0)),
            routed_spec, routed_spec, routed_spec,
            whole((N_EXPERTS, 1)),
        ) + tuple(whole(shape) for shape in carry_shapes),
        scratch_shapes=scratch,
        compiler_params=pltpu.CompilerParams(
            dimension_semantics=("arbitrary",), vmem_limit_bytes=VMEM_LIMIT),
        name="mixer",
    )(*args)
    return outs[:6], outs[6:]


def _padded_router_weight(w_router):
    pad = jnp.zeros((w_router.shape[0], ROUTER_COLS - N_EXPERTS), BF16)
    return jnp.concatenate([w_router.astype(BF16), pad], axis=1)


def _sc_mesh():
    return plsc.VectorSubcoreMesh(core_axis_name="c", subcore_axis_name="s")


def _sc_worker_id():
    return lax.axis_index("s") * SC_CORES + lax.axis_index("c")


def _sc_dispatch(h_packed, dest_sc, n_slots):
    n_tok, width = h_packed.shape
    per_w = n_tok // SC_WORKERS
    n_chunks = per_w // SC_CHUNK

    def body(h_hbm, dest_hbm, xs_hbm, idx_v, rows_v, sem_in, sem_out):
        base = _sc_worker_id() * per_w
        pltpu.sync_copy(dest_hbm.at[_sc_worker_id()], idx_v)

        def load(j, buf):
            return pltpu.make_async_copy(
                h_hbm.at[pl.ds(base + j * SC_CHUNK, SC_CHUNK)], rows_v.at[buf], sem_in)

        load(0, 0).start()

        @pl.loop(0, n_chunks, step=2)
        def _(j0):
            for buf in range(2):
                j = j0 + buf
                load(j, buf).wait()

                @pl.when(j + 1 < n_chunks)
                def _():
                    load(j + 1, 1 - buf).start()

                copies = [
                    pltpu.make_async_copy(
                        rows_v.at[buf], xs_hbm.at[idx_v.at[j * TOP_K + k]], sem_out)
                    for k in range(TOP_K)]
                for cp in copies:
                    cp.start()
                for cp in copies:
                    cp.wait()

    return pl.kernel(
        body,
        out_type=jax.ShapeDtypeStruct((n_slots, width), I32),
        mesh=_sc_mesh(),
        scratch_types=[
            pltpu.VMEM((n_chunks * TOP_K, SC_CHUNK), I32),
            pltpu.VMEM((2, SC_CHUNK, width), I32),
            pltpu.SemaphoreType.DMA,
            pltpu.SemaphoreType.DMA,
        ],
        name="sc_dispatch",
    )(h_packed, dest_sc)


def _sc_gather(ys_packed, dest_sc):
    width = ys_packed.shape[1]
    n_chunks = dest_sc.shape[1]
    per_w = n_chunks * SC_CHUNK
    n_rows = SC_WORKERS * per_w

    def body(ys_hbm, dest_hbm, out_hbm, idx_v, rows_v, sem_g, sem_w):
        base = _sc_worker_id() * per_w
        pltpu.sync_copy(dest_hbm.at[_sc_worker_id()], idx_v)

        def gather(j, buf):
            return pltpu.make_async_copy(ys_hbm.at[idx_v.at[j]], rows_v.at[buf], sem_g)

        def writeout(j, buf):
            return pltpu.make_async_copy(
                rows_v.at[buf], out_hbm.at[pl.ds(base + j * SC_CHUNK, SC_CHUNK)], sem_w)

        gather(0, 0).start()

        @pl.loop(0, n_chunks, step=2)
        def _(j0):
            for buf in range(2):
                j = j0 + buf
                gather(j, buf).wait()

                @pl.when(j >= 1)
                def _():
                    writeout(j - 1, 1 - buf).wait()

                @pl.when(j + 1 < n_chunks)
                def _():
                    gather(j + 1, 1 - buf).start()

                writeout(j, buf).start()

        writeout(n_chunks - 1, 1).wait()

    return pl.kernel(
        body,
        out_type=jax.ShapeDtypeStruct((n_rows, width), I32),
        mesh=_sc_mesh(),
        scratch_types=[
            pltpu.VMEM((n_chunks, SC_CHUNK), I32),
            pltpu.VMEM((2, SC_CHUNK, width), I32),
            pltpu.SemaphoreType.DMA,
            pltpu.SemaphoreType.DMA,
        ],
        name="sc_gather",
    )(ys_packed, dest_sc)


def _expert_kernel(layer, be_ref, nv_ref, first_ref, next_ref, xs_ref, win_hbm, bin_ref,
                   wout_hbm, bout_ref, ys_ref, win_f32, wout_f32, win_bf, wout_bf, sem):
    i = pl.program_id(0)
    valid = nv_ref[i]

    def fetch(e):
        return (pltpu.make_async_copy(win_hbm.at[layer, e], win_f32, sem.at[0]),
                pltpu.make_async_copy(wout_hbm.at[layer, e], wout_f32, sem.at[1]))

    @pl.when(first_ref[i] == 1)
    def _():
        @pl.when(i == 0)
        def _():
            for cp in fetch(be_ref[i]):
                cp.start()

        for cp in fetch(be_ref[i]):
            cp.wait()

        def cast_in(r, c):
            rows = pl.ds(pl.multiple_of(r * CAST_ROWS, CAST_ROWS), CAST_ROWS)
            win_bf[rows, :] = win_f32[rows, :].astype(BF16)
            return c

        def cast_out(r, c):
            rows = pl.ds(pl.multiple_of(r * CAST_ROWS, CAST_ROWS), CAST_ROWS)
            wout_bf[rows, :] = wout_f32[rows, :].astype(BF16)
            return c

        lax.fori_loop(0, D_MODEL // CAST_ROWS, cast_in, 0)
        lax.fori_loop(0, D_FF // CAST_ROWS, cast_out, 0)

        @pl.when(next_ref[i] >= 0)
        def _():
            for cp in fetch(next_ref[i]):
                cp.start()

    n_sub = (valid + EXPERT_SUB - 1) // EXPERT_SUB
    row = lax.broadcasted_iota(I32, (EXPERT_SUB, 1), 0)

    def compute(j, c):
        rows = pl.ds(pl.multiple_of(j * EXPERT_SUB, EXPERT_SUB), EXPERT_SUB)
        packed = jnp.where(row + j * EXPERT_SUB < valid, xs_ref[rows, :], 0)
        lo, hi = _unpack_rows(packed)
        gu = (jnp.dot(lo.astype(BF16), win_bf[:PACKED, :], preferred_element_type=F32)
              + jnp.dot(hi.astype(BF16), win_bf[PACKED:, :], preferred_element_type=F32)
              + bin_ref[0])
        gate = jnp.minimum(gu[:, :D_FF], SWIGLU_LIMIT)
        up = jnp.clip(gu[:, D_FF:], -SWIGLU_LIMIT, SWIGLU_LIMIT)
        act = gate * _sigmoid(SWIGLU_ALPHA * gate) * (up + 1.0)
        out = jnp.dot(act.astype(BF16), wout_bf[...], preferred_element_type=F32) + bout_ref[0]
        ys_ref[rows, :] = _pack_rows(out)
        return c

    def clear(j, c):
        rows = pl.ds(pl.multiple_of(j * EXPERT_SUB, EXPERT_SUB), EXPERT_SUB)
        ys_ref[rows, :] = jnp.zeros((EXPERT_SUB, ys_ref.shape[1]), ys_ref.dtype)
        return c

    lax.fori_loop(0, n_sub, compute, 0)
    lax.fori_loop(n_sub, EXPERT_ROWS // EXPERT_SUB, clear, 0)


def _experts(layer, block_e, n_valid, is_first, next_e, xs, w_e_in, b_e_in, w_e_out, b_e_out):
    n_slots, width = xs.shape
    d = D_MODEL
    tm = EXPERT_ROWS
    bias_index = lambda i, be, nv, fi, nx: (layer, be[i], 0, 0)
    grid_spec = pltpu.PrefetchScalarGridSpec(
        num_scalar_prefetch=4,
        grid=(n_slots // tm,),
        in_specs=[
            pl.BlockSpec((tm, width), lambda i, be, nv, fi, nx: (i, 0)),
            pl.BlockSpec(memory_space=pl.ANY),
            pl.BlockSpec((None, 1, 1, 2 * D_FF), bias_index),
            pl.BlockSpec(memory_space=pl.ANY),
            pl.BlockSpec((None, 1, 1, d), bias_index),
        ],
        out_specs=pl.BlockSpec((tm, width), lambda i, be, nv, fi, nx: (i, 0)),
        scratch_shapes=[
            pltpu.VMEM((d, 2 * D_FF), F32), pltpu.VMEM((D_FF, d), F32),
            pltpu.VMEM((d, 2 * D_FF), BF16), pltpu.VMEM((D_FF, d), BF16),
            pltpu.SemaphoreType.DMA((2,)),
        ],
    )
    return pl.pallas_call(
        functools.partial(_expert_kernel, layer),
        out_shape=jax.ShapeDtypeStruct((n_slots, width), I32),
        grid_spec=grid_spec,
        compiler_params=pltpu.CompilerParams(
            dimension_semantics=("arbitrary",), vmem_limit_bytes=VMEM_LIMIT),
        name="experts",
    )(block_e, n_valid, is_first, next_e, xs, w_e_in,
      b_e_in.reshape(-1, N_EXPERTS, 1, 2 * D_FF), w_e_out, b_e_out.reshape(-1, N_EXPERTS, 1, d))


def _final_kernel(x_ref, y0_ref, y1_ref, y2_ref, y3_ref, wt_ref, mod_ref, g_ref, o_ref):
    x = _combined_rows(x_ref, (y0_ref, y1_ref, y2_ref, y3_ref), wt_ref, mod_ref[5])
    ms = jnp.mean(x * x, axis=-1, keepdims=True)
    out = x * lax.rsqrt(ms + NORM_EPS) * g_ref[...]
    o_ref[...] = pltpu.einshape("(tb)d->btd", out, b=BATCH)


def _final_norm(x2d, pending, g):
    n_tok, d = x2d.shape
    tr = NORM_ROWS
    y4, wts, mod_l = pending
    return pl.pallas_call(
        _final_kernel,
        out_shape=jax.ShapeDtypeStruct((BATCH, n_tok // BATCH, d), F32),
        grid=(n_tok // tr,),
        in_specs=([pl.BlockSpec((tr, d), lambda i: (i, 0))] + _pending_specs(tr, n_tok)
                  + [_const_spec((1, d))]),
        out_specs=pl.BlockSpec((BATCH, tr // BATCH, d), lambda i: (0, i, 0)),
        compiler_params=pltpu.CompilerParams(
            dimension_semantics=("arbitrary",), vmem_limit_bytes=VMEM_LIMIT),
        name="final_norm",
    )(x2d, y4, y4, y4, y4, wts, mod_l, g.reshape(1, d))


def _moe_layer(layer, h_packed, idx, pos, counts, w_e_in, b_e_in, w_e_out, b_e_out):
    n_tok = h_packed.shape[0]
    tm = EXPERT_ROWS
    padded = ((counts + tm - 1) // tm) * tm
    pad_end = jnp.cumsum(padded)
    start_pad = pad_end - padded
    experts = jnp.arange(N_EXPERTS, dtype=I32)
    dest = jnp.sum(jnp.where(idx[:, :, None] == experts, start_pad, 0), axis=-1) + pos
    n_blocks = (n_tok * TOP_K) // tm + N_EXPERTS
    n_slots = n_blocks * tm
    block_start = jnp.arange(n_blocks, dtype=I32) * tm
    block_e = jnp.minimum(
        jnp.sum((block_start[:, None] >= pad_end[None, :]).astype(I32), axis=-1),
        N_EXPERTS - 1)
    hot_e = block_e[:, None] == experts[None, :]
    rows_left = jnp.sum(jnp.where(hot_e, (start_pad + counts)[None, :], 0), axis=-1) - block_start
    n_valid = jnp.where(block_start < pad_end[-1], jnp.clip(rows_left, 0, tm), 0).astype(I32)
    is_first = jnp.sum(jnp.where(hot_e, start_pad[None, :], 0), axis=-1) == block_start
    is_first = (is_first & (n_valid > 0)).astype(I32)
    later = (experts[None, :] > experts[:, None]) & (counts[None, :] > 0)
    next_of = jnp.min(jnp.where(later, experts[None, :], N_EXPERTS), axis=-1)
    next_of = jnp.where(next_of == N_EXPERTS, -1, next_of)
    next_e = jnp.sum(jnp.where(hot_e, next_of[None, :], 0), axis=-1).astype(I32)

    per_w = n_tok // SC_WORKERS
    n_chunks = per_w // SC_CHUNK
    dest_scatter = (dest.reshape(TOP_K, SC_WORKERS, n_chunks, SC_CHUNK)
                    .transpose(1, 2, 0, 3)
                    .reshape(SC_WORKERS, n_chunks * TOP_K, SC_CHUNK))
    xs = _sc_dispatch(h_packed, dest_scatter, n_slots)
    ys = _experts(layer, block_e, n_valid, is_first, next_e, xs,
                  w_e_in, b_e_in, w_e_out, b_e_out)
    dest_gather = dest.reshape(SC_WORKERS, n_chunks * TOP_K, SC_CHUNK)
    return _sc_gather(ys, dest_gather)


def kernel(x, c, norm1_g, norm2_g, w_ada, b_ada, w_in, conv_w, conv_b, w_rg_a, b_rg_a,
           w_rg_x, b_rg_x, lru_lambda, w_pool, pool_scale, w_up_a, w_up_b, w_out,
           w_router, b_router, w_e_in, b_e_in, w_e_out, b_e_out, final_g):
    depth = w_in.shape[0]
    bsz, seq, d = x.shape
    assert bsz == BATCH and d == D_MODEL
    mod = _ada_mod(c, w_ada, b_ada).reshape(depth, bsz, 6, d).transpose(0, 2, 1, 3)
    w_rg = jnp.concatenate([w_rg_a, w_rg_x], axis=-1).astype(BF16)
    part = seq // SEQ_PARTS
    xs = [x if SEQ_PARTS == 1 else x[:, p * part:(p + 1) * part] for p in range(SEQ_PARTS)]
    pending = [None] * SEQ_PARTS
    for l in range(depth):
        mixer_weights = (
            mod[l], norm1_g[l], w_in[l].astype(BF16), conv_w[l], conv_b[l], w_rg[l],
            b_rg_a[l], b_rg_x[l], lru_lambda[l], w_pool[l].astype(BF16), pool_scale[l],
            w_up_a[l].astype(BF16), w_up_b[l].astype(BF16), w_out[l].astype(BF16),
            norm2_g[l], _padded_router_weight(w_router[l]), b_router[l])
        carry = tuple(jnp.zeros(shape, F32) for shape in _carry_shapes())
        for p in range(SEQ_PARTS):
            (xs[p], h_packed, idx, pos, wts, counts), carry = _mixer_layer(
                xs[p], pending[p], carry, p * part, *mixer_weights)
            y4 = _moe_layer(l, h_packed, idx, pos, counts[:, 0],
                            w_e_in, b_e_in, w_e_out, b_e_out)
            pending[p] = (y4, wts.T, mod[l])
    outs = [_final_norm(xs[p], pending[p], final_g) for p in range(SEQ_PARTS)]
    return outs[0] if SEQ_PARTS == 1 else jnp.concatenate(outs, axis=1)
```

```python
import functools
import math

import jax
import jax.numpy as jnp
from jax import lax
from jax.experimental import pallas as pl
from jax.experimental.pallas import tpu as pltpu
from jax.experimental.pallas import tpu_sc as plsc

D_MODEL = 1024
BATCH = 8
N_HEADS = 8
HEAD_DIM = 128
CONV_WIDTH = 4
LRU_C = 8.0
POOL_WINDOWS = (2, 4, 8, 16)
POOL_WIDTH = 512
POOL_GROUP_DIM = 128
N_EXPERTS = 32
TOP_K = 4
D_FF = 1024
SWIGLU_LIMIT = 7.0
SWIGLU_ALPHA = 1.702
NORM_EPS = 1e-6

SUBLANES = 8
VMEM_LIMIT = 56 * 1024 * 1024
SC_CORES = 2
SC_SUBCORES = 16
SC_WORKERS = SC_CORES * SC_SUBCORES
SC_CHUNK = 64

MIX_STEPS = 32
SEQ_PARTS = 1
ROUTER_COLS = 128
EXPERT_ROWS = 1024
EXPERT_SUB = 512
EXPERT_TAIL = 256
ADA_COLS = 1536
NORM_ROWS = 512
CAST_ROWS = 128
PACKED = D_MODEL // 2

F32 = jnp.float32
BF16 = jnp.bfloat16
I32 = jnp.int32
HIGH_MASK = -65536
LOG2_E = math.log2(math.e)


def _const_spec(shape):
    nd = len(shape)
    return pl.BlockSpec(shape, lambda *_: (0,) * nd, pipeline_mode=pl.Buffered(1))


def _pack_rows(v):
    bits = lax.bitcast_convert_type(v.astype(BF16).astype(F32), I32)
    lo, hi = bits[:, :PACKED], bits[:, PACKED:]
    return (hi & HIGH_MASK) | lax.shift_right_logical(lo, 16)


def _unpack_rows(p):
    lo = lax.bitcast_convert_type(lax.shift_left(p, 16), F32)
    hi = lax.bitcast_convert_type(p & HIGH_MASK, F32)
    return lo, hi


def _sigmoid(v):
    return 0.5 * jnp.tanh(0.5 * v) + 0.5


def _per_batch(v, slab, op):
    rows, d = v.shape
    return op(v.reshape(rows // BATCH, BATCH, d), slab[None]).reshape(rows, d)


def _modulated_norm(x, g, shift, scale):
    ms = jnp.mean(x * x, axis=-1, keepdims=True)
    hn = x * lax.rsqrt(ms + NORM_EPS) * g
    return _per_batch(_per_batch(hn, 1.0 + scale, jnp.multiply), shift, jnp.add)


def _ada_kernel(c_ref, w_ref, b_ref, o_ref):
    c = c_ref[...]
    c_act = c * jax.nn.sigmoid(c)
    o_ref[0] = jnp.dot(c_act, w_ref[0], preferred_element_type=F32,
                       precision=lax.Precision.HIGHEST) + b_ref[0]


def _ada_mod(c, w_ada, b_ada):
    depth, d, n = w_ada.shape
    bsz = c.shape[0]
    tn = ADA_COLS
    return pl.pallas_call(
        _ada_kernel,
        out_shape=jax.ShapeDtypeStruct((depth, bsz, n), F32),
        grid=(depth, n // tn),
        in_specs=[
            pl.BlockSpec((bsz, d), lambda l, j: (0, 0)),
            pl.BlockSpec((1, d, tn), lambda l, j: (l, 0, j)),
            pl.BlockSpec((1, 1, tn), lambda l, j: (l, 0, j)),
        ],
        out_specs=pl.BlockSpec((1, bsz, tn), lambda l, j: (l, 0, j)),
        compiler_params=pltpu.CompilerParams(
            dimension_semantics=("arbitrary", "arbitrary"),
            vmem_limit_bytes=VMEM_LIMIT),
        name="ada_mod",
    )(c, w_ada, b_ada.reshape(depth, 1, n))


def _delay(a, steps, carry_ref):
    n = steps * BATCH
    keep = carry_ref.shape[0]
    prev = carry_ref[keep - n:, :]
    return jnp.concatenate([prev, a[:a.shape[0] - n]], axis=0)


def _save_tail(a, carry_ref):
    carry_ref[...] = a[a.shape[0] - carry_ref.shape[0]:]


def _combined_rows(x_ref, y_refs, wt_ref, gate2, rs=slice(None)):
    x = x_ref[rs, :]
    wt = wt_ref[rs, :]
    acc_lo = jnp.zeros((x.shape[0], PACKED), F32)
    acc_hi = jnp.zeros((x.shape[0], PACKED), F32)
    for k, y_ref in enumerate(y_refs):
        lo, hi = _unpack_rows(y_ref[rs, :])
        wk = wt[:, k:k + 1]
        acc_lo = acc_lo + wk * lo
        acc_hi = acc_hi + wk * hi
    ffn = jnp.concatenate([acc_lo, acc_hi], axis=-1)
    return x + _per_batch(ffn, gate2, jnp.multiply)


def _route_rows(h, wrt_ref, brc_ref, run_c):
    rows = h.shape[0]
    padded = jnp.dot(h.astype(BF16), wrt_ref[...], preferred_element_type=F32)
    logits = padded.T[:N_EXPERTS] + brc_ref[...]

    e_iota = lax.broadcasted_iota(I32, (N_EXPERTS, rows), 0)
    vals, idxs, hots = [], [], []
    cur = logits
    for _ in range(TOP_K):
        m = jnp.max(cur, axis=0, keepdims=True)
        am = jnp.min(jnp.where(cur == m, e_iota, N_EXPERTS), axis=0, keepdims=True)
        hot = e_iota == am
        vals.append(m)
        idxs.append(am)
        hots.append(hot)
        cur = jnp.where(hot, -jnp.inf, cur)
    exps = [jnp.exp(v - vals[0]) for v in vals]
    denom = exps[0] + exps[1] + exps[2] + exps[3]

    sel = (hots[0] | hots[1] | hots[2] | hots[3]).astype(F32)
    ri = lax.broadcasted_iota(I32, (rows, rows), 0)
    ci = lax.broadcasted_iota(I32, (rows, rows), 1)
    earlier = (ri < ci).astype(BF16)
    before = jnp.dot(sel.astype(BF16), earlier, preferred_element_type=F32) + run_c[...]
    run_c[...] = run_c[...] + jnp.sum(sel, axis=1, keepdims=True)
    poss = [jnp.sum(jnp.where(hot, before, 0.0), axis=0, keepdims=True).astype(I32)
            for hot in hots]
    return (jnp.concatenate(idxs, axis=0), jnp.concatenate(poss, axis=0),
            jnp.concatenate([e / denom for e in exps], axis=0))


N_CARRY = 6


def _mixer_kernel(with_combine, step0, *refs):
    refs = list(refs)
    x_ref = refs.pop(0)
    if with_combine:
        y_refs = [refs.pop(0) for _ in range(TOP_K)]
        wt_in_ref = refs.pop(0)
        modp_ref = refs.pop(0)
    (mod_ref, g1_ref, win_ref, convw_ref, convb_ref, wrg_ref, brga_ref, brgx_ref, lam_ref,
     wpool_ref, pscale_ref, wupa_ref, wupb_ref, wout_ref, g2_ref, wrt_ref, brc_ref) = refs[:17]
    carry_in = refs[17:17 + N_CARRY]
    o_ref, hp_ref, idx_ref, pos_ref, wt_ref, cnt_ref = refs[17 + N_CARRY:23 + N_CARRY]
    carry_out = refs[23 + N_CARRY:23 + 2 * N_CARRY]
    carries = refs[23 + 2 * N_CARRY:23 + 3 * N_CARRY]
    conv_c, p1_c, p2_c, p4_c, p8_c, h_c = carries
    run_c = refs[23 + 3 * N_CARRY]
    i = pl.program_id(0)
    rows = o_ref.shape[0]
    steps = rows // BATCH
    first_step = step0 + i * steps
    d = D_MODEL

    @pl.when(i == 0)
    def _():
        for src, dst in zip(carry_in, carries):
            dst[...] = src[...]
        run_c[...] = jnp.zeros(run_c.shape, run_c.dtype)

    if with_combine:
        x = _combined_rows(x_ref, y_refs, wt_in_ref, modp_ref[5])
    else:
        x = pltpu.einshape("btd->(tb)d", x_ref[...])
    shift1, scale1, gate1 = mod_ref[0], mod_ref[1], mod_ref[2]
    hb = _modulated_norm(x, g1_ref[...], shift1, scale1).astype(BF16)

    def in_proj(lo, hi):
        return jnp.dot(hb, win_ref[:, lo:hi], preferred_element_type=F32)

    x_lru = in_proj(0, d)
    cw = convw_ref[...]
    xc = (convb_ref[...] + cw[3:4] * x_lru
          + cw[2:3] * _delay(x_lru, 1, conv_c)
          + cw[1:2] * _delay(x_lru, 2, conv_c)
          + cw[0:1] * _delay(x_lru, 3, conv_c))
    _save_tail(x_lru, conv_c)
    xcb = xc.astype(BF16)
    pre = [jnp.dot(xcb[:, k * HEAD_DIM:(k + 1) * HEAD_DIM], wrg_ref[k],
                   preferred_element_type=F32) for k in range(N_HEADS)]
    pre_a = jnp.concatenate([p[:, :HEAD_DIM] for p in pre], axis=-1)
    pre_x = jnp.concatenate([p[:, HEAD_DIM:] for p in pre], axis=-1)
    r = _sigmoid(pre_a + brga_ref[...])
    gi = _sigmoid(pre_x + brgx_ref[...])
    rate = (-LRU_C * LOG2_E) * jax.nn.softplus(-lam_ref[...])
    a = jnp.exp2(r * rate)
    y = 1.0 - a * a
    mult = jnp.where(y > 0.0, y * lax.rsqrt(y), 0.0)
    u = mult * (gi * xc)
    h = h_c[...]
    hs = []
    for t in range(steps):
        sl = slice(t * BATCH, (t + 1) * BATCH)
        h = a[sl] * h + u[sl]
        hs.append(h)
    h_c[...] = h
    y_lru = jnp.concatenate(hs, axis=0) * jax.nn.gelu(in_proj(d, 2 * d))
    p_a = jnp.dot(y_lru.astype(BF16), wupa_ref[...], preferred_element_type=F32)
    merged = _sigmoid(in_proj(2 * d + POOL_WIDTH, 3 * d + POOL_WIDTH)) * p_a

    x_pool = in_proj(2 * d, 2 * d + POOL_WIDTH)
    g = POOL_GROUP_DIM
    s2 = x_pool + _delay(x_pool, 1, p1_c)
    s4 = s2[:, g:] + _delay(s2[:, g:], 2, p2_c)
    s8 = s4[:, g:] + _delay(s4[:, g:], 4, p4_c)
    s16 = s8[:, g:] + _delay(s8[:, g:], 8, p8_c)
    _save_tail(x_pool, p1_c)
    _save_tail(s2[:, g:], p2_c)
    _save_tail(s4[:, g:], p4_c)
    _save_tail(s8[:, g:], p8_c)
    wins = (s2[:, :g], s4[:, :g], s8[:, :g], s16)
    row = lax.broadcasted_iota(I32, (rows, 1), 0)
    t1 = (first_step + row // BATCH + 1).astype(F32)
    pooled = []
    for k, w in enumerate(POOL_WINDOWS):
        cnt = jnp.minimum(t1, float(w))
        pk = wins[k] / cnt - x_pool[:, k * g:(k + 1) * g]
        pooled.append(jnp.dot(pk.astype(BF16), wpool_ref[k], preferred_element_type=F32))
    y_pool = jnp.concatenate(pooled, axis=-1) * pscale_ref[...]
    p_b = jnp.dot(y_pool.astype(BF16), wupb_ref[...], preferred_element_type=F32)
    merged = merged + _sigmoid(in_proj(3 * d + POOL_WIDTH, 4 * d + POOL_WIDTH)) * p_b

    mix = jnp.dot(merged.astype(BF16), wout_ref[...], preferred_element_type=F32)
    x_mid = x + _per_batch(mix, gate1, jnp.multiply)
    o_ref[...] = x_mid
    for src, dst in zip(carries, carry_out):
        dst[...] = src[...]

    hr = _modulated_norm(x_mid, g2_ref[...], mod_ref[3], mod_ref[4])
    hp_ref[...] = _pack_rows(hr)
    idx, pos, wts = _route_rows(hr, wrt_ref, brc_ref, run_c)
    idx_ref[...] = idx
    pos_ref[...] = pos
    wt_ref[...] = wts
    cnt_ref[...] = run_c[...].astype(I32)


def _kmajor_index(k, nb):
    return lambda i: (k * nb + i, 0)


def _pending_specs(rows, n_tok):
    nb = n_tok // rows
    specs = [pl.BlockSpec((rows, PACKED), _kmajor_index(k, nb)) for k in range(TOP_K)]
    specs.append(pl.BlockSpec((rows, TOP_K), lambda i: (i, 0)))
    specs.append(_const_spec((6, BATCH, D_MODEL)))
    return specs


def _carry_shapes():
    g = POOL_GROUP_DIM
    return (((CONV_WIDTH - 1) * BATCH, D_MODEL), (1 * BATCH, 4 * g), (2 * BATCH, 3 * g),
            (4 * BATCH, 2 * g), (8 * BATCH, g), (BATCH, D_MODEL))


def _mixer_layer(x2d, pending, carry, step0, mod_l, g1, w_in, conv_w, conv_b, w_rg, b_rg_a,
                 b_rg_x, lam, w_pool, pool_scale, w_up_a, w_up_b, w_out, g2, w_route, b_router):
    rows = MIX_STEPS * BATCH
    row = lambda v: v.reshape(1, -1)
    weights = (mod_l, row(g1), w_in, conv_w, row(conv_b), w_rg, row(b_rg_a), row(b_rg_x),
               row(lam), w_pool, row(pool_scale), w_up_a, w_up_b, w_out, row(g2),
               w_route, b_router.reshape(N_EXPERTS, 1))
    args = [x2d]
    if pending is None:
        n_tok, d = x2d.shape[0] * x2d.shape[1], x2d.shape[2]
        in_specs = [pl.BlockSpec((BATCH, MIX_STEPS, d), lambda i: (0, i, 0))]
    else:
        n_tok, d = x2d.shape
        in_specs = [pl.BlockSpec((rows, d), lambda i: (i, 0))]
        y4, wts_prev, mod_prev = pending
        args += [y4, y4, y4, y4, wts_prev, mod_prev]
        in_specs += _pending_specs(rows, n_tok)
    args += list(weights) + list(carry)
    in_specs += [_const_spec(a.shape) for a in weights + tuple(carry)]
    carry_shapes = _carry_shapes()
    scratch = [pltpu.VMEM(shape, F32) for shape in carry_shapes]
    scratch.append(pltpu.VMEM((N_EXPERTS, 1), F32))
    routed = lambda dtype: jax.ShapeDtypeStruct((TOP_K, n_tok), dtype)
    routed_spec = pl.BlockSpec((TOP_K, rows), lambda i: (0, i))
    whole = lambda shape: pl.BlockSpec(shape, lambda i: (0,) * len(shape))
    outs = pl.pallas_call(
        functools.partial(_mixer_kernel, pending is not None, step0),
        out_shape=(
            jax.ShapeDtypeStruct((n_tok, d), F32),
            jax.ShapeDtypeStruct((n_tok, PACKED), I32),
            routed(I32), routed(I32), routed(F32),
            jax.ShapeDtypeStruct((N_EXPERTS, 1), I32),
        ) + tuple(jax.ShapeDtypeStruct(shape, F32) for shape in carry_shapes),
        grid=(n_tok // rows,),
        in_specs=in_specs,
        out_specs=(
            pl.BlockSpec((rows, d), lambda i: (i, 0)),
            pl.BlockSpec((rows, PACKED), lambda i: (i, 0)),
            routed_spec, routed_spec, routed_spec,
            whole((N_EXPERTS, 1)),
        ) + tuple(whole(shape) for shape in carry_shapes),
        scratch_shapes=scratch,
        compiler_params=pltpu.CompilerParams(
            dimension_semantics=("arbitrary",), vmem_limit_bytes=VMEM_LIMIT),
        name="mixer",
    )(*args)
    return outs[:6], outs[6:]


def _padded_router_weight(w_router):
    pad = jnp.zeros((w_router.shape[0], ROUTER_COLS - N_EXPERTS), BF16)
    return jnp.concatenate([w_router.astype(BF16), pad], axis=1)


def _sc_mesh():
    return plsc.VectorSubcoreMesh(core_axis_name="c", subcore_axis_name="s")


def _sc_worker_id():
    return lax.axis_index("s") * SC_CORES + lax.axis_index("c")


def _sc_dispatch(h_packed, dest_sc, n_slots):
    n_tok, width = h_packed.shape
    per_w = n_tok // SC_WORKERS
    n_chunks = per_w // SC_CHUNK

    def body(h_hbm, dest_hbm, xs_hbm, idx_v, rows_v, sem_in, sem_out):
        base = _sc_worker_id() * per_w
        pltpu.sync_copy(dest_hbm.at[_sc_worker_id()], idx_v)

        def load(j, buf):
            return pltpu.make_async_copy(
                h_hbm.at[pl.ds(base + j * SC_CHUNK, SC_CHUNK)], rows_v.at[buf], sem_in)

        load(0, 0).start()

        @pl.loop(0, n_chunks, step=2)
        def _(j0):
            for buf in range(2):
                j = j0 + buf
                load(j, buf).wait()

                @pl.when(j + 1 < n_chunks)
                def _():
                    load(j + 1, 1 - buf).start()

                copies = [
                    pltpu.make_async_copy(
                        rows_v.at[buf], xs_hbm.at[idx_v.at[j * TOP_K + k]], sem_out)
                    for k in range(TOP_K)]
                for cp in copies:
                    cp.start()
                for cp in copies:
                    cp.wait()

    return pl.kernel(
        body,
        out_type=jax.ShapeDtypeStruct((n_slots, width), I32),
        mesh=_sc_mesh(),
        scratch_types=[
            pltpu.VMEM((n_chunks * TOP_K, SC_CHUNK), I32),
            pltpu.VMEM((2, SC_CHUNK, width), I32),
            pltpu.SemaphoreType.DMA,
            pltpu.SemaphoreType.DMA,
        ],
        name="sc_dispatch",
    )(h_packed, dest_sc)


def _sc_gather(ys_packed, dest_sc):
    width = ys_packed.shape[1]
    n_chunks = dest_sc.shape[1]
    per_w = n_chunks * SC_CHUNK
    n_rows = SC_WORKERS * per_w

    def body(ys_hbm, dest_hbm, out_hbm, idx_v, rows_v, sem_g, sem_w):
        base = _sc_worker_id() * per_w
        pltpu.sync_copy(dest_hbm.at[_sc_worker_id()], idx_v)

        def gather(j, buf):
            return pltpu.make_async_copy(ys_hbm.at[idx_v.at[j]], rows_v.at[buf], sem_g)

        def writeout(j, buf):
            return pltpu.make_async_copy(
                rows_v.at[buf], out_hbm.at[pl.ds(base + j * SC_CHUNK, SC_CHUNK)], sem_w)

        gather(0, 0).start()

        @pl.loop(0, n_chunks, step=2)
        def _(j0):
            for buf in range(2):
                j = j0 + buf
                gather(j, buf).wait()

                @pl.when(j >= 1)
                def _():
                    writeout(j - 1, 1 - buf).wait()

                @pl.when(j + 1 < n_chunks)
                def _():
                    gather(j + 1, 1 - buf).start()

                writeout(j, buf).start()

        writeout(n_chunks - 1, 1).wait()

    return pl.kernel(
        body,
        out_type=jax.ShapeDtypeStruct((n_rows, width), I32),
        mesh=_sc_mesh(),
        scratch_types=[
            pltpu.VMEM((n_chunks, SC_CHUNK), I32),
            pltpu.VMEM((2, SC_CHUNK, width), I32),
            pltpu.SemaphoreType.DMA,
            pltpu.SemaphoreType.DMA,
        ],
        name="sc_gather",
    )(ys_packed, dest_sc)


def _expert_kernel(layer, be_ref, nv_ref, first_ref, next_ref, xs_ref, win_hbm, bin_ref,
                   wout_hbm, bout_ref, ys_ref, win_f32, wout_f32, win_bf, wout_bf, sem):
    i = pl.program_id(0)
    valid = nv_ref[i]

    def fetch(e):
        return (pltpu.make_async_copy(win_hbm.at[layer, e], win_f32, sem.at[0]),
                pltpu.make_async_copy(wout_hbm.at[layer, e], wout_f32, sem.at[1]))

    @pl.when(first_ref[i] == 1)
    def _():
        @pl.when(i == 0)
        def _():
            for cp in fetch(be_ref[i]):
                cp.start()

        for cp in fetch(be_ref[i]):
            cp.wait()

        def cast_in(r, c):
            rows = pl.ds(pl.multiple_of(r * CAST_ROWS, CAST_ROWS), CAST_ROWS)
            win_bf[rows, :] = win_f32[rows, :].astype(BF16)
            return c

        def cast_out(r, c):
            rows = pl.ds(pl.multiple_of(r * CAST_ROWS, CAST_ROWS), CAST_ROWS)
            wout_bf[rows, :] = wout_f32[rows, :].astype(BF16)
            return c

        lax.fori_loop(0, D_MODEL // CAST_ROWS, cast_in, 0)
        lax.fori_loop(0, D_FF // CAST_ROWS, cast_out, 0)

        @pl.when(next_ref[i] >= 0)
        def _():
            for cp in fetch(next_ref[i]):
                cp.start()

    def compute(start, size):
        rows = pl.ds(pl.multiple_of(start, EXPERT_TAIL), size)
        row = lax.broadcasted_iota(I32, (size, 1), 0)
        packed = jnp.where(row + start < valid, xs_ref[rows, :], 0)
        lo, hi = _unpack_rows(packed)
        gu = (jnp.dot(lo.astype(BF16), win_bf[:PACKED, :], preferred_element_type=F32)
              + jnp.dot(hi.astype(BF16), win_bf[PACKED:, :], preferred_element_type=F32)
              + bin_ref[0])
        gate = jnp.minimum(gu[:, :D_FF], SWIGLU_LIMIT)
        up = jnp.clip(gu[:, D_FF:], -SWIGLU_LIMIT, SWIGLU_LIMIT)
        act = gate * _sigmoid(SWIGLU_ALPHA * gate) * (up + 1.0)
        out = jnp.dot(act.astype(BF16), wout_bf[...], preferred_element_type=F32) + bout_ref[0]
        ys_ref[rows, :] = _pack_rows(out)

    n_main = (valid + (EXPERT_SUB - EXPERT_TAIL) - 1) // EXPERT_SUB
    main_rows = n_main * EXPERT_SUB

    def main_pass(j, c):
        compute(j * EXPERT_SUB, EXPERT_SUB)
        return c

    lax.fori_loop(0, n_main, main_pass, 0)

    @pl.when(valid > main_rows)
    def _():
        compute(main_rows, EXPERT_TAIL)

    def clear(j, c):
        rows = pl.ds(pl.multiple_of(j * EXPERT_TAIL, EXPERT_TAIL), EXPERT_TAIL)
        ys_ref[rows, :] = jnp.zeros((EXPERT_TAIL, ys_ref.shape[1]), ys_ref.dtype)
        return c

    done = (main_rows + jnp.where(valid > main_rows, EXPERT_TAIL, 0)) // EXPERT_TAIL
    lax.fori_loop(done, EXPERT_ROWS // EXPERT_TAIL, clear, 0)


def _experts(layer, block_e, n_valid, is_first, next_e, xs, w_e_in, b_e_in, w_e_out, b_e_out):
    n_slots, width = xs.shape
    d = D_MODEL
    tm = EXPERT_ROWS
    bias_index = lambda i, be, nv, fi, nx: (layer, be[i], 0, 0)
    grid_spec = pltpu.PrefetchScalarGridSpec(
        num_scalar_prefetch=4,
        grid=(n_slots // tm,),
        in_specs=[
            pl.BlockSpec((tm, width), lambda i, be, nv, fi, nx: (i, 0)),
            pl.BlockSpec(memory_space=pl.ANY),
            pl.BlockSpec((None, 1, 1, 2 * D_FF), bias_index),
            pl.BlockSpec(memory_space=pl.ANY),
            pl.BlockSpec((None, 1, 1, d), bias_index),
        ],
        out_specs=pl.BlockSpec((tm, width), lambda i, be, nv, fi, nx: (i, 0)),
        scratch_shapes=[
            pltpu.VMEM((d, 2 * D_FF), F32), pltpu.VMEM((D_FF, d), F32),
            pltpu.VMEM((d, 2 * D_FF), BF16), pltpu.VMEM((D_FF, d), BF16),
            pltpu.SemaphoreType.DMA((2,)),
        ],
    )
    return pl.pallas_call(
        functools.partial(_expert_kernel, layer),
        out_shape=jax.ShapeDtypeStruct((n_slots, width), I32),
        grid_spec=grid_spec,
        compiler_params=pltpu.CompilerParams(
            dimension_semantics=("arbitrary",), vmem_limit_bytes=VMEM_LIMIT),
        name="experts",
    )(block_e, n_valid, is_first, next_e, xs, w_e_in,
      b_e_in.reshape(-1, N_EXPERTS, 1, 2 * D_FF), w_e_out, b_e_out.reshape(-1, N_EXPERTS, 1, d))


def _final_kernel(x_ref, y0_ref, y1_ref, y2_ref, y3_ref, wt_ref, mod_ref, g_ref, o_ref):
    x = _combined_rows(x_ref, (y0_ref, y1_ref, y2_ref, y3_ref), wt_ref, mod_ref[5])
    ms = jnp.mean(x * x, axis=-1, keepdims=True)
    out = x * lax.rsqrt(ms + NORM_EPS) * g_ref[...]
    o_ref[...] = pltpu.einshape("(tb)d->btd", out, b=BATCH)


def _final_norm(x2d, pending, g):
    n_tok, d = x2d.shape
    tr = NORM_ROWS
    y4, wts, mod_l = pending
    return pl.pallas_call(
        _final_kernel,
        out_shape=jax.ShapeDtypeStruct((BATCH, n_tok // BATCH, d), F32),
        grid=(n_tok // tr,),
        in_specs=([pl.BlockSpec((tr, d), lambda i: (i, 0))] + _pending_specs(tr, n_tok)
                  + [_const_spec((1, d))]),
        out_specs=pl.BlockSpec((BATCH, tr // BATCH, d), lambda i: (0, i, 0)),
        compiler_params=pltpu.CompilerParams(
            dimension_semantics=("arbitrary",), vmem_limit_bytes=VMEM_LIMIT),
        name="final_norm",
    )(x2d, y4, y4, y4, y4, wts, mod_l, g.reshape(1, d))


def _moe_layer(layer, h_packed, idx, pos, counts, w_e_in, b_e_in, w_e_out, b_e_out):
    n_tok = h_packed.shape[0]
    tm = EXPERT_ROWS
    padded = ((counts + tm - 1) // tm) * tm
    pad_end = jnp.cumsum(padded)
    start_pad = pad_end - padded
    experts = jnp.arange(N_EXPERTS, dtype=I32)
    dest = jnp.sum(jnp.where(idx[:, :, None] == experts, start_pad, 0), axis=-1) + pos
    n_blocks = (n_tok * TOP_K) // tm + N_EXPERTS
    n_slots = n_blocks * tm
    block_start = jnp.arange(n_blocks, dtype=I32) * tm
    block_e = jnp.minimum(
        jnp.sum((block_start[:, None] >= pad_end[None, :]).astype(I32), axis=-1),
        N_EXPERTS - 1)
    hot_e = block_e[:, None] == experts[None, :]
    rows_left = jnp.sum(jnp.where(hot_e, (start_pad + counts)[None, :], 0), axis=-1) - block_start
    n_valid = jnp.where(block_start < pad_end[-1], jnp.clip(rows_left, 0, tm), 0).astype(I32)
    is_first = jnp.sum(jnp.where(hot_e, start_pad[None, :], 0), axis=-1) == block_start
    is_first = (is_first & (n_valid > 0)).astype(I32)
    later = (experts[None, :] > experts[:, None]) & (counts[None, :] > 0)
    next_of = jnp.min(jnp.where(later, experts[None, :], N_EXPERTS), axis=-1)
    next_of = jnp.where(next_of == N_EXPERTS, -1, next_of)
    next_e = jnp.sum(jnp.where(hot_e, next_of[None, :], 0), axis=-1).astype(I32)

    per_w = n_tok // SC_WORKERS
    n_chunks = per_w // SC_CHUNK
    dest_scatter = (dest.reshape(TOP_K, SC_WORKERS, n_chunks, SC_CHUNK)
                    .transpose(1, 2, 0, 3)
                    .reshape(SC_WORKERS, n_chunks * TOP_K, SC_CHUNK))
    xs = _sc_dispatch(h_packed, dest_scatter, n_slots)
    ys = _experts(layer, block_e, n_valid, is_first, next_e, xs,
                  w_e_in, b_e_in, w_e_out, b_e_out)
    dest_gather = dest.reshape(SC_WORKERS, n_chunks * TOP_K, SC_CHUNK)
    return _sc_gather(ys, dest_gather)


def kernel(x, c, norm1_g, norm2_g, w_ada, b_ada, w_in, conv_w, conv_b, w_rg_a, b_rg_a,
           w_rg_x, b_rg_x, lru_lambda, w_pool, pool_scale, w_up_a, w_up_b, w_out,
           w_router, b_router, w_e_in, b_e_in, w_e_out, b_e_out, final_g):
    depth = w_in.shape[0]
    bsz, seq, d = x.shape
    assert bsz == BATCH and d == D_MODEL
    mod = _ada_mod(c, w_ada, b_ada).reshape(depth, bsz, 6, d).transpose(0, 2, 1, 3)
    w_rg = jnp.concatenate([w_rg_a, w_rg_x], axis=-1).astype(BF16)
    part = seq // SEQ_PARTS
    xs = [x if SEQ_PARTS == 1 else x[:, p * part:(p + 1) * part] for p in range(SEQ_PARTS)]
    pending = [None] * SEQ_PARTS
    for l in range(depth):
        mixer_weights = (
            mod[l], norm1_g[l], w_in[l].astype(BF16), conv_w[l], conv_b[l], w_rg[l],
            b_rg_a[l], b_rg_x[l], lru_lambda[l], w_pool[l].astype(BF16), pool_scale[l],
            w_up_a[l].astype(BF16), w_up_b[l].astype(BF16), w_out[l].astype(BF16),
            norm2_g[l], _padded_router_weight(w_router[l]), b_router[l])
        carry = tuple(jnp.zeros(shape, F32) for shape in _carry_shapes())
        for p in range(SEQ_PARTS):
            (xs[p], h_packed, idx, pos, wts, counts), carry = _mixer_layer(
                xs[p], pending[p], carry, p * part, *mixer_weights)
            y4 = _moe_layer(l, h_packed, idx, pos, counts[:, 0],
                            w_e_in, b_e_in, w_e_out, b_e_out)
            pending[p] = (y4, wts.T, mod[l])
    outs = [_final_norm(xs[p], pending[p], final_g) for p in range(SEQ_PARTS)]
    return outs[0] if SEQ_PARTS == 1 else jnp.concatenate(outs, axis=1)
```

```python
import functools
import math

import jax
import jax.numpy as jnp
from jax import lax
from jax.experimental import pallas as pl
from jax.experimental.pallas import tpu as pltpu
from jax.experimental.pallas import tpu_sc as plsc

D_MODEL = 1024
BATCH = 8
N_HEADS = 8
HEAD_DIM = 128
CONV_WIDTH = 4
LRU_C = 8.0
POOL_WINDOWS = (2, 4, 8, 16)
POOL_WIDTH = 512
POOL_GROUP_DIM = 128
N_EXPERTS = 32
TOP_K = 4
D_FF = 1024
SWIGLU_LIMIT = 7.0
SWIGLU_ALPHA = 1.702
NORM_EPS = 1e-6

SUBLANES = 8
VMEM_LIMIT = 56 * 1024 * 1024
SC_CORES = 2
SC_SUBCORES = 16
SC_WORKERS = SC_CORES * SC_SUBCORES
SC_CHUNK = 64

MIX_STEPS = 32
SEQ_PARTS = 1
ROUTER_COLS = 128
EXPERT_ROWS = 1024
EXPERT_SUB = 512
EXPERT_TAIL = 256
ADA_COLS = 1536
NORM_ROWS = 512
CAST_ROWS = 128
PACKED = D_MODEL // 2

F32 = jnp.float32
BF16 = jnp.bfloat16
I32 = jnp.int32
HIGH_MASK = -65536
LOG2_E = math.log2(math.e)


def _const_spec(shape):
    nd = len(shape)
    return pl.BlockSpec(shape, lambda *_: (0,) * nd, pipeline_mode=pl.Buffered(1))


def _pack_rows(v):
    packed = pltpu.pack_elementwise([v[:, :PACKED], v[:, PACKED:]], packed_dtype=BF16)
    return lax.bitcast_convert_type(packed, I32)


def _unpack_rows(p):
    lo = lax.bitcast_convert_type(lax.shift_left(p, 16), F32)
    hi = lax.bitcast_convert_type(p & HIGH_MASK, F32)
    return lo, hi


def _sigmoid(v):
    return 0.5 * jnp.tanh(0.5 * v) + 0.5


def _per_batch(v, slab, op):
    rows, d = v.shape
    return op(v.reshape(rows // BATCH, BATCH, d), slab[None]).reshape(rows, d)


def _modulated_norm(x, g, shift, scale):
    ms = jnp.mean(x * x, axis=-1, keepdims=True)
    hn = x * lax.rsqrt(ms + NORM_EPS)
    return _per_batch(_per_batch(hn, g * (1.0 + scale), jnp.multiply), shift, jnp.add)


def _ada_kernel(c_ref, w_ref, b_ref, o_ref):
    c = c_ref[...]
    c_act = c * jax.nn.sigmoid(c)
    o_ref[0] = jnp.dot(c_act, w_ref[0], preferred_element_type=F32,
                       precision=lax.Precision.HIGHEST) + b_ref[0]


def _ada_mod(c, w_ada, b_ada):
    depth, d, n = w_ada.shape
    bsz = c.shape[0]
    tn = ADA_COLS
    return pl.pallas_call(
        _ada_kernel,
        out_shape=jax.ShapeDtypeStruct((depth, bsz, n), F32),
        grid=(depth, n // tn),
        in_specs=[
            pl.BlockSpec((bsz, d), lambda l, j: (0, 0)),
            pl.BlockSpec((1, d, tn), lambda l, j: (l, 0, j)),
            pl.BlockSpec((1, 1, tn), lambda l, j: (l, 0, j)),
        ],
        out_specs=pl.BlockSpec((1, bsz, tn), lambda l, j: (l, 0, j)),
        compiler_params=pltpu.CompilerParams(
            dimension_semantics=("arbitrary", "arbitrary"),
            vmem_limit_bytes=VMEM_LIMIT),
        name="ada_mod",
    )(c, w_ada, b_ada.reshape(depth, 1, n))


def _delay(a, steps, carry_ref):
    n = steps * BATCH
    keep = carry_ref.shape[0]
    prev = carry_ref[keep - n:, :]
    return jnp.concatenate([prev, a[:a.shape[0] - n]], axis=0)


def _save_tail(a, carry_ref):
    carry_ref[...] = a[a.shape[0] - carry_ref.shape[0]:]


def _combined_rows(x_ref, y_refs, wt_ref, gate2, rs=slice(None)):
    x = x_ref[rs, :]
    wt = wt_ref[rs, :]
    acc_lo = jnp.zeros((x.shape[0], PACKED), F32)
    acc_hi = jnp.zeros((x.shape[0], PACKED), F32)
    for k, y_ref in enumerate(y_refs):
        lo, hi = _unpack_rows(y_ref[rs, :])
        wk = wt[:, k:k + 1]
        acc_lo = acc_lo + wk * lo
        acc_hi = acc_hi + wk * hi
    ffn = jnp.concatenate([acc_lo, acc_hi], axis=-1)
    return x + _per_batch(ffn, gate2, jnp.multiply)


def _route_rows(h, wrt_ref, brc_ref, run_c):
    rows = h.shape[0]
    padded = jnp.dot(h.astype(BF16), wrt_ref[...], preferred_element_type=F32)
    logits = padded.T[:N_EXPERTS] + brc_ref[...]

    e_iota = lax.broadcasted_iota(I32, (N_EXPERTS, rows), 0)
    vals, idxs, hots = [], [], []
    cur = logits
    for _ in range(TOP_K):
        m = jnp.max(cur, axis=0, keepdims=True)
        am = jnp.min(jnp.where(cur == m, e_iota, N_EXPERTS), axis=0, keepdims=True)
        hot = e_iota == am
        vals.append(m)
        idxs.append(am)
        hots.append(hot)
        cur = jnp.where(hot, -jnp.inf, cur)
    exps = [jnp.exp(v - vals[0]) for v in vals]
    denom = exps[0] + exps[1] + exps[2] + exps[3]

    sel = (hots[0] | hots[1] | hots[2] | hots[3]).astype(F32)
    ri = lax.broadcasted_iota(I32, (rows, rows), 0)
    ci = lax.broadcasted_iota(I32, (rows, rows), 1)
    earlier = (ri < ci).astype(BF16)
    before = jnp.dot(sel.astype(BF16), earlier, preferred_element_type=F32) + run_c[...]
    run_c[...] = run_c[...] + jnp.sum(sel, axis=1, keepdims=True)
    poss = [jnp.sum(jnp.where(hot, before, 0.0), axis=0, keepdims=True).astype(I32)
            for hot in hots]
    return (jnp.concatenate(idxs, axis=0), jnp.concatenate(poss, axis=0),
            jnp.concatenate([e / denom for e in exps], axis=0))


N_CARRY = 6


def _mixer_kernel(with_combine, step0, *refs):
    refs = list(refs)
    x_ref = refs.pop(0)
    if with_combine:
        y_refs = [refs.pop(0) for _ in range(TOP_K)]
        wt_in_ref = refs.pop(0)
        modp_ref = refs.pop(0)
    (mod_ref, g1_ref, win_ref, convw_ref, convb_ref, wrg_ref, brga_ref, brgx_ref, lam_ref,
     wpool_ref, pscale_ref, wupa_ref, wupb_ref, wout_ref, g2_ref, wrt_ref, brc_ref) = refs[:17]
    carry_in = refs[17:17 + N_CARRY]
    o_ref, hp_ref, idx_ref, pos_ref, wt_ref, cnt_ref = refs[17 + N_CARRY:23 + N_CARRY]
    carry_out = refs[23 + N_CARRY:23 + 2 * N_CARRY]
    carries = refs[23 + 2 * N_CARRY:23 + 3 * N_CARRY]
    conv_c, p1_c, p2_c, p4_c, p8_c, h_c = carries
    run_c = refs[23 + 3 * N_CARRY]
    i = pl.program_id(0)
    rows = o_ref.shape[0]
    steps = rows // BATCH
    first_step = step0 + i * steps
    d = D_MODEL

    @pl.when(i == 0)
    def _():
        for src, dst in zip(carry_in, carries):
            dst[...] = src[...]
        run_c[...] = jnp.zeros(run_c.shape, run_c.dtype)

    if with_combine:
        x = _combined_rows(x_ref, y_refs, wt_in_ref, modp_ref[5])
    else:
        x = pltpu.einshape("btd->(tb)d", x_ref[...])
    shift1, scale1, gate1 = mod_ref[0], mod_ref[1], mod_ref[2]
    hb = _modulated_norm(x, g1_ref[...], shift1, scale1).astype(BF16)

    def in_proj(lo, hi):
        return jnp.dot(hb, win_ref[:, lo:hi], preferred_element_type=F32)

    x_lru = in_proj(0, d)
    cw = convw_ref[...]
    xc = (convb_ref[...] + cw[3:4] * x_lru
          + cw[2:3] * _delay(x_lru, 1, conv_c)
          + cw[1:2] * _delay(x_lru, 2, conv_c)
          + cw[0:1] * _delay(x_lru, 3, conv_c))
    _save_tail(x_lru, conv_c)
    xcb = xc.astype(BF16)
    pre = [jnp.dot(xcb[:, k * HEAD_DIM:(k + 1) * HEAD_DIM], wrg_ref[k],
                   preferred_element_type=F32) for k in range(N_HEADS)]
    pre_a = jnp.concatenate([p[:, :HEAD_DIM] for p in pre], axis=-1)
    pre_x = jnp.concatenate([p[:, HEAD_DIM:] for p in pre], axis=-1)
    r = _sigmoid(pre_a + brga_ref[...])
    gi = _sigmoid(pre_x + brgx_ref[...])
    rate = (-LRU_C * LOG2_E) * jax.nn.softplus(-lam_ref[...])
    a = jnp.exp2(r * rate)
    y = 1.0 - a * a
    mult = jnp.where(y > 0.0, y * lax.rsqrt(y), 0.0)
    u = mult * (gi * xc)
    h = h_c[...]
    hs = []
    for t in range(steps):
        sl = slice(t * BATCH, (t + 1) * BATCH)
        h = a[sl] * h + u[sl]
        hs.append(h)
    h_c[...] = h
    y_lru = jnp.concatenate(hs, axis=0) * jax.nn.gelu(in_proj(d, 2 * d))
    p_a = jnp.dot(y_lru.astype(BF16), wupa_ref[...], preferred_element_type=F32)
    merged = _sigmoid(in_proj(2 * d + POOL_WIDTH, 3 * d + POOL_WIDTH)) * p_a

    x_pool = in_proj(2 * d, 2 * d + POOL_WIDTH)
    g = POOL_GROUP_DIM
    s2 = x_pool + _delay(x_pool, 1, p1_c)
    s4 = s2[:, g:] + _delay(s2[:, g:], 2, p2_c)
    s8 = s4[:, g:] + _delay(s4[:, g:], 4, p4_c)
    s16 = s8[:, g:] + _delay(s8[:, g:], 8, p8_c)
    _save_tail(x_pool, p1_c)
    _save_tail(s2[:, g:], p2_c)
    _save_tail(s4[:, g:], p4_c)
    _save_tail(s8[:, g:], p8_c)
    wins = (s2[:, :g], s4[:, :g], s8[:, :g], s16)
    row = lax.broadcasted_iota(I32, (rows, 1), 0)
    t1 = (first_step + row // BATCH + 1).astype(F32)
    pooled = []
    for k, w in enumerate(POOL_WINDOWS):
        cnt = jnp.minimum(t1, float(w))
        pk = wins[k] / cnt - x_pool[:, k * g:(k + 1) * g]
        pooled.append(jnp.dot(pk.astype(BF16), wpool_ref[k], preferred_element_type=F32))
    y_pool = jnp.concatenate(pooled, axis=-1) * pscale_ref[...]
    p_b = jnp.dot(y_pool.astype(BF16), wupb_ref[...], preferred_element_type=F32)
    merged = merged + _sigmoid(in_proj(3 * d + POOL_WIDTH, 4 * d + POOL_WIDTH)) * p_b

    mix = jnp.dot(merged.astype(BF16), wout_ref[...], preferred_element_type=F32)
    x_mid = x + _per_batch(mix, gate1, jnp.multiply)
    o_ref[...] = x_mid
    for src, dst in zip(carries, carry_out):
        dst[...] = src[...]

    hr = _modulated_norm(x_mid, g2_ref[...], mod_ref[3], mod_ref[4])
    hp_ref[...] = _pack_rows(hr)
    idx, pos, wts = _route_rows(hr, wrt_ref, brc_ref, run_c)
    idx_ref[...] = idx
    pos_ref[...] = pos
    wt_ref[...] = wts
    cnt_ref[...] = run_c[...].astype(I32)


def _kmajor_index(k, nb):
    return lambda i: (k * nb + i, 0)


def _pending_specs(rows, n_tok):
    nb = n_tok // rows
    specs = [pl.BlockSpec((rows, PACKED), _kmajor_index(k, nb)) for k in range(TOP_K)]
    specs.append(pl.BlockSpec((rows, TOP_K), lambda i: (i, 0)))
    specs.append(_const_spec((6, BATCH, D_MODEL)))
    return specs


def _carry_shapes():
    g = POOL_GROUP_DIM
    return (((CONV_WIDTH - 1) * BATCH, D_MODEL), (1 * BATCH, 4 * g), (2 * BATCH, 3 * g),
            (4 * BATCH, 2 * g), (8 * BATCH, g), (BATCH, D_MODEL))


def _mixer_layer(x2d, pending, carry, step0, mod_l, g1, w_in, conv_w, conv_b, w_rg, b_rg_a,
                 b_rg_x, lam, w_pool, pool_scale, w_up_a, w_up_b, w_out, g2, w_route, b_router):
    rows = MIX_STEPS * BATCH
    row = lambda v: v.reshape(1, -1)
    weights = (mod_l, row(g1), w_in, conv_w, row(conv_b), w_rg, row(b_rg_a), row(b_rg_x),
               row(lam), w_pool, row(pool_scale), w_up_a, w_up_b, w_out, row(g2),
               w_route, b_router.reshape(N_EXPERTS, 1))
    args = [x2d]
    if pending is None:
        n_tok, d = x2d.shape[0] * x2d.shape[1], x2d.shape[2]
        in_specs = [pl.BlockSpec((BATCH, MIX_STEPS, d), lambda i: (0, i, 0))]
    else:
        n_tok, d = x2d.shape
        in_specs = [pl.BlockSpec((rows, d), lambda i: (i, 0))]
        y4, wts_prev, mod_prev = pending
        args += [y4, y4, y4, y4, wts_prev, mod_prev]
        in_specs += _pending_specs(rows, n_tok)
    args += list(weights) + list(carry)
    in_specs += [_const_spec(a.shape) for a in weights + tuple(carry)]
    carry_shapes = _carry_shapes()
    scratch = [pltpu.VMEM(shape, F32) for shape in carry_shapes]
    scratch.append(pltpu.VMEM((N_EXPERTS, 1), F32))
    routed = lambda dtype: jax.ShapeDtypeStruct((TOP_K, n_tok), dtype)
    routed_spec = pl.BlockSpec((TOP_K, rows), lambda i: (0, i))
    whole = lambda shape: pl.BlockSpec(shape, lambda i: (0,) * len(shape))
    outs = pl.pallas_call(
        functools.partial(_mixer_kernel, pending is not None, step0),
        out_shape=(
            jax.ShapeDtypeStruct((n_tok, d), F32),
            jax.ShapeDtypeStruct((n_tok, PACKED), I32),
            routed(I32), routed(I32), routed(F32),
            jax.ShapeDtypeStruct((N_EXPERTS, 1), I32),
        ) + tuple(jax.ShapeDtypeStruct(shape, F32) for shape in carry_shapes),
        grid=(n_tok // rows,),
        in_specs=in_specs,
        out_specs=(
            pl.BlockSpec((rows, d), lambda i: (i, 0)),
            pl.BlockSpec((rows, PACKED), lambda i: (i, 0)),
            routed_spec, routed_spec, routed_spec,
            whole((N_EXPERTS, 1)),
        ) + tuple(whole(shape) for shape in carry_shapes),
        scratch_shapes=scratch,
        compiler_params=pltpu.CompilerParams(
            dimension_semantics=("arbitrary",), vmem_limit_bytes=VMEM_LIMIT),
        name="mixer",
    )(*args)
    return outs[:6], outs[6:]


def _padded_router_weight(w_router):
    pad = jnp.zeros((w_router.shape[0], ROUTER_COLS - N_EXPERTS), BF16)
    return jnp.concatenate([w_router.astype(BF16), pad], axis=1)


def _sc_mesh():
    return plsc.VectorSubcoreMesh(core_axis_name="c", subcore_axis_name="s")


def _sc_worker_id():
    return lax.axis_index("s") * SC_CORES + lax.axis_index("c")


def _sc_dispatch(h_packed, dest_sc, n_slots):
    n_tok, width = h_packed.shape
    per_w = n_tok // SC_WORKERS
    n_chunks = per_w // SC_CHUNK

    def body(h_hbm, dest_hbm, xs_hbm, idx_v, rows_v, sem_in, sem_out):
        base = _sc_worker_id() * per_w
        pltpu.sync_copy(dest_hbm.at[_sc_worker_id()], idx_v)

        def load(j, buf):
            return pltpu.make_async_copy(
                h_hbm.at[pl.ds(base + j * SC_CHUNK, SC_CHUNK)], rows_v.at[buf], sem_in)

        load(0, 0).start()

        @pl.loop(0, n_chunks, step=2)
        def _(j0):
            for buf in range(2):
                j = j0 + buf
                load(j, buf).wait()

                @pl.when(j + 1 < n_chunks)
                def _():
                    load(j + 1, 1 - buf).start()

                copies = [
                    pltpu.make_async_copy(
                        rows_v.at[buf], xs_hbm.at[idx_v.at[j * TOP_K + k]], sem_out)
                    for k in range(TOP_K)]
                for cp in copies:
                    cp.start()
                for cp in copies:
                    cp.wait()

    return pl.kernel(
        body,
        out_type=jax.ShapeDtypeStruct((n_slots, width), I32),
        mesh=_sc_mesh(),
        scratch_types=[
            pltpu.VMEM((n_chunks * TOP_K, SC_CHUNK), I32),
            pltpu.VMEM((2, SC_CHUNK, width), I32),
            pltpu.SemaphoreType.DMA,
            pltpu.SemaphoreType.DMA,
        ],
        name="sc_dispatch",
    )(h_packed, dest_sc)


def _sc_gather(ys_packed, dest_sc):
    width = ys_packed.shape[1]
    n_chunks = dest_sc.shape[1]
    per_w = n_chunks * SC_CHUNK
    n_rows = SC_WORKERS * per_w

    def body(ys_hbm, dest_hbm, out_hbm, idx_v, rows_v, sem_g, sem_w):
        base = _sc_worker_id() * per_w
        pltpu.sync_copy(dest_hbm.at[_sc_worker_id()], idx_v)

        def gather(j, buf):
            return pltpu.make_async_copy(ys_hbm.at[idx_v.at[j]], rows_v.at[buf], sem_g)

        def writeout(j, buf):
            return pltpu.make_async_copy(
                rows_v.at[buf], out_hbm.at[pl.ds(base + j * SC_CHUNK, SC_CHUNK)], sem_w)

        gather(0, 0).start()

        @pl.loop(0, n_chunks, step=2)
        def _(j0):
            for buf in range(2):
                j = j0 + buf
                gather(j, buf).wait()

                @pl.when(j >= 1)
                def _():
                    writeout(j - 1, 1 - buf).wait()

                @pl.when(j + 1 < n_chunks)
                def _():
                    gather(j + 1, 1 - buf).start()

                writeout(j, buf).start()

        writeout(n_chunks - 1, 1).wait()

    return pl.kernel(
        body,
        out_type=jax.ShapeDtypeStruct((n_rows, width), I32),
        mesh=_sc_mesh(),
        scratch_types=[
            pltpu.VMEM((n_chunks, SC_CHUNK), I32),
            pltpu.VMEM((2, SC_CHUNK, width), I32),
            pltpu.SemaphoreType.DMA,
            pltpu.SemaphoreType.DMA,
        ],
        name="sc_gather",
    )(ys_packed, dest_sc)


def _expert_kernel(layer, be_ref, nv_ref, first_ref, next_ref, xs_ref, win_hbm, bin_ref,
                   wout_hbm, bout_ref, ys_ref, win_f32, wout_f32, win_bf, wout_bf, sem):
    i = pl.program_id(0)
    valid = nv_ref[i]

    def fetch(e):
        return (pltpu.make_async_copy(win_hbm.at[layer, e], win_f32, sem.at[0]),
                pltpu.make_async_copy(wout_hbm.at[layer, e], wout_f32, sem.at[1]))

    @pl.when(first_ref[i] == 1)
    def _():
        @pl.when(i == 0)
        def _():
            for cp in fetch(be_ref[i]):
                cp.start()

        for cp in fetch(be_ref[i]):
            cp.wait()

        def cast_in(r, c):
            rows = pl.ds(pl.multiple_of(r * CAST_ROWS, CAST_ROWS), CAST_ROWS)
            win_bf[rows, :] = win_f32[rows, :].astype(BF16)
            return c

        def cast_out(r, c):
            rows = pl.ds(pl.multiple_of(r * CAST_ROWS, CAST_ROWS), CAST_ROWS)
            wout_bf[rows, :] = wout_f32[rows, :].astype(BF16)
            return c

        lax.fori_loop(0, D_MODEL // CAST_ROWS, cast_in, 0)
        lax.fori_loop(0, D_FF // CAST_ROWS, cast_out, 0)

        @pl.when(next_ref[i] >= 0)
        def _():
            for cp in fetch(next_ref[i]):
                cp.start()

    def compute(start, size):
        rows = pl.ds(pl.multiple_of(start, EXPERT_TAIL), size)
        row = lax.broadcasted_iota(I32, (size, 1), 0)
        packed = jnp.where(row + start < valid, xs_ref[rows, :], 0)
        lo, hi = _unpack_rows(packed)
        gu = (jnp.dot(lo.astype(BF16), win_bf[:PACKED, :], preferred_element_type=F32)
              + jnp.dot(hi.astype(BF16), win_bf[PACKED:, :], preferred_element_type=F32)
              + bin_ref[0])
        gate = jnp.minimum(gu[:, :D_FF], SWIGLU_LIMIT)
        up = jnp.clip(gu[:, D_FF:], -SWIGLU_LIMIT, SWIGLU_LIMIT)
        act = gate * _sigmoid(SWIGLU_ALPHA * gate) * (up + 1.0)
        out = jnp.dot(act.astype(BF16), wout_bf[...], preferred_element_type=F32) + bout_ref[0]
        ys_ref[rows, :] = _pack_rows(out)

    n_main = (valid + (EXPERT_SUB - EXPERT_TAIL) - 1) // EXPERT_SUB
    main_rows = n_main * EXPERT_SUB

    def main_pass(j, c):
        compute(j * EXPERT_SUB, EXPERT_SUB)
        return c

    lax.fori_loop(0, n_main, main_pass, 0)

    @pl.when(valid > main_rows)
    def _():
        compute(main_rows, EXPERT_TAIL)

    def clear(j, c):
        rows = pl.ds(pl.multiple_of(j * EXPERT_TAIL, EXPERT_TAIL), EXPERT_TAIL)
        ys_ref[rows, :] = jnp.zeros((EXPERT_TAIL, ys_ref.shape[1]), ys_ref.dtype)
        return c

    done = (main_rows + jnp.where(valid > main_rows, EXPERT_TAIL, 0)) // EXPERT_TAIL
    lax.fori_loop(done, EXPERT_ROWS // EXPERT_TAIL, clear, 0)


def _experts(layer, block_e, n_valid, is_first, next_e, xs, w_e_in, b_e_in, w_e_out, b_e_out):
    n_slots, width = xs.shape
    d = D_MODEL
    tm = EXPERT_ROWS
    bias_index = lambda i, be, nv, fi, nx: (layer, be[i], 0, 0)
    grid_spec = pltpu.PrefetchScalarGridSpec(
        num_scalar_prefetch=4,
        grid=(n_slots // tm,),
        in_specs=[
            pl.BlockSpec((tm, width), lambda i, be, nv, fi, nx: (i, 0)),
            pl.BlockSpec(memory_space=pl.ANY),
            pl.BlockSpec((None, 1, 1, 2 * D_FF), bias_index),
            pl.BlockSpec(memory_space=pl.ANY),
            pl.BlockSpec((None, 1, 1, d), bias_index),
        ],
        out_specs=pl.BlockSpec((tm, width), lambda i, be, nv, fi, nx: (i, 0)),
        scratch_shapes=[
            pltpu.VMEM((d, 2 * D_FF), F32), pltpu.VMEM((D_FF, d), F32),
            pltpu.VMEM((d, 2 * D_FF), BF16), pltpu.VMEM((D_FF, d), BF16),
            pltpu.SemaphoreType.DMA((2,)),
        ],
    )
    return pl.pallas_call(
        functools.partial(_expert_kernel, layer),
        out_shape=jax.ShapeDtypeStruct((n_slots, width), I32),
        grid_spec=grid_spec,
        compiler_params=pltpu.CompilerParams(
            dimension_semantics=("arbitrary",), vmem_limit_bytes=VMEM_LIMIT),
        name="experts",
    )(block_e, n_valid, is_first, next_e, xs, w_e_in,
      b_e_in.reshape(-1, N_EXPERTS, 1, 2 * D_FF), w_e_out, b_e_out.reshape(-1, N_EXPERTS, 1, d))


def _final_kernel(x_ref, y0_ref, y1_ref, y2_ref, y3_ref, wt_ref, mod_ref, g_ref, o_ref):
    x = _combined_rows(x_ref, (y0_ref, y1_ref, y2_ref, y3_ref), wt_ref, mod_ref[5])
    ms = jnp.mean(x * x, axis=-1, keepdims=True)
    out = x * lax.rsqrt(ms + NORM_EPS) * g_ref[...]
    o_ref[...] = pltpu.einshape("(tb)d->btd", out, b=BATCH)


def _final_norm(x2d, pending, g):
    n_tok, d = x2d.shape
    tr = NORM_ROWS
    y4, wts, mod_l = pending
    return pl.pallas_call(
        _final_kernel,
        out_shape=jax.ShapeDtypeStruct((BATCH, n_tok // BATCH, d), F32),
        grid=(n_tok // tr,),
        in_specs=([pl.BlockSpec((tr, d), lambda i: (i, 0))] + _pending_specs(tr, n_tok)
                  + [_const_spec((1, d))]),
        out_specs=pl.BlockSpec((BATCH, tr // BATCH, d), lambda i: (0, i, 0)),
        compiler_params=pltpu.CompilerParams(
            dimension_semantics=("arbitrary",), vmem_limit_bytes=VMEM_LIMIT),
        name="final_norm",
    )(x2d, y4, y4, y4, y4, wts, mod_l, g.reshape(1, d))


def _moe_layer(layer, h_packed, idx, pos, counts, w_e_in, b_e_in, w_e_out, b_e_out):
    n_tok = h_packed.shape[0]
    tm = EXPERT_ROWS
    padded = ((counts + tm - 1) // tm) * tm
    pad_end = jnp.cumsum(padded)
    start_pad = pad_end - padded
    experts = jnp.arange(N_EXPERTS, dtype=I32)
    dest = jnp.sum(jnp.where(idx[:, :, None] == experts, start_pad, 0), axis=-1) + pos
    n_blocks = (n_tok * TOP_K) // tm + N_EXPERTS
    n_slots = n_blocks * tm
    block_start = jnp.arange(n_blocks, dtype=I32) * tm
    block_e = jnp.minimum(
        jnp.sum((block_start[:, None] >= pad_end[None, :]).astype(I32), axis=-1),
        N_EXPERTS - 1)
    hot_e = block_e[:, None] == experts[None, :]
    rows_left = jnp.sum(jnp.where(hot_e, (start_pad + counts)[None, :], 0), axis=-1) - block_start
    n_valid = jnp.where(block_start < pad_end[-1], jnp.clip(rows_left, 0, tm), 0).astype(I32)
    is_first = jnp.sum(jnp.where(hot_e, start_pad[None, :], 0), axis=-1) == block_start
    is_first = (is_first & (n_valid > 0)).astype(I32)
    later = (experts[None, :] > experts[:, None]) & (counts[None, :] > 0)
    next_of = jnp.min(jnp.where(later, experts[None, :], N_EXPERTS), axis=-1)
    next_of = jnp.where(next_of == N_EXPERTS, -1, next_of)
    next_e = jnp.sum(jnp.where(hot_e, next_of[None, :], 0), axis=-1).astype(I32)

    per_w = n_tok // SC_WORKERS
    n_chunks = per_w // SC_CHUNK
    dest_scatter = (dest.reshape(TOP_K, SC_WORKERS, n_chunks, SC_CHUNK)
                    .transpose(1, 2, 0, 3)
                    .reshape(SC_WORKERS, n_chunks * TOP_K, SC_CHUNK))
    xs = _sc_dispatch(h_packed, dest_scatter, n_slots)
    ys = _experts(layer, block_e, n_valid, is_first, next_e, xs,
                  w_e_in, b_e_in, w_e_out, b_e_out)
    dest_gather = dest.reshape(SC_WORKERS, n_chunks * TOP_K, SC_CHUNK)
    return _sc_gather(ys, dest_gather)


def kernel(x, c, norm1_g, norm2_g, w_ada, b_ada, w_in, conv_w, conv_b, w_rg_a, b_rg_a,
           w_rg_x, b_rg_x, lru_lambda, w_pool, pool_scale, w_up_a, w_up_b, w_out,
           w_router, b_router, w_e_in, b_e_in, w_e_out, b_e_out, final_g):
    depth = w_in.shape[0]
    bsz, seq, d = x.shape
    assert bsz == BATCH and d == D_MODEL
    mod = _ada_mod(c, w_ada, b_ada).reshape(depth, bsz, 6, d).transpose(0, 2, 1, 3)
    w_rg = jnp.concatenate([w_rg_a, w_rg_x], axis=-1).astype(BF16)
    part = seq // SEQ_PARTS
    xs = [x if SEQ_PARTS == 1 else x[:, p * part:(p + 1) * part] for p in range(SEQ_PARTS)]
    pending = [None] * SEQ_PARTS
    for l in range(depth):
        mixer_weights = (
            mod[l], norm1_g[l], w_in[l].astype(BF16), conv_w[l], conv_b[l], w_rg[l],
            b_rg_a[l], b_rg_x[l], lru_lambda[l], w_pool[l].astype(BF16), pool_scale[l],
            w_up_a[l].astype(BF16), w_up_b[l].astype(BF16), w_out[l].astype(BF16),
            norm2_g[l], _padded_router_weight(w_router[l]), b_router[l])
        carry = tuple(jnp.zeros(shape, F32) for shape in _carry_shapes())
        for p in range(SEQ_PARTS):
            (xs[p], h_packed, idx, pos, wts, counts), carry = _mixer_layer(
                xs[p], pending[p], carry, p * part, *mixer_weights)
            y4 = _moe_layer(l, h_packed, idx, pos, counts[:, 0],
                            w_e_in, b_e_in, w_e_out, b_e_out)
            pending[p] = (y4, wts.T, mod[l])
    outs = [_final_norm(xs[p], pending[p], final_g) for p in range(SEQ_PARTS)]
    return outs[0] if SEQ_PARTS == 1 else jnp.concatenate(outs, axis=1)
```

```python
import functools
import math

import jax
import jax.numpy as jnp
from jax import lax
from jax.experimental import pallas as pl
from jax.experimental.pallas import tpu as pltpu
from jax.experimental.pallas import tpu_sc as plsc

D_MODEL = 1024
BATCH = 8
N_HEADS = 8
HEAD_DIM = 128
CONV_WIDTH = 4
LRU_C = 8.0
POOL_WINDOWS = (2, 4, 8, 16)
POOL_WIDTH = 512
POOL_GROUP_DIM = 128
N_EXPERTS = 32
TOP_K = 4
D_FF = 1024
SWIGLU_LIMIT = 7.0
SWIGLU_ALPHA = 1.702
NORM_EPS = 1e-6

SUBLANES = 8
VMEM_LIMIT = 56 * 1024 * 1024
SC_CORES = 2
SC_SUBCORES = 16
SC_WORKERS = SC_CORES * SC_SUBCORES
SC_CHUNK = 64

MIX_STEPS = 32
SEQ_PARTS = 1
ROUTER_COLS = 128
EXPERT_ROWS = 1024
EXPERT_SUB = 512
EXPERT_TAIL = 256
ADA_COLS = 1536
NORM_ROWS = 512
CAST_ROWS = 128
PACKED = D_MODEL // 2

F32 = jnp.float32
BF16 = jnp.bfloat16
I32 = jnp.int32
HIGH_MASK = -65536
LOG2_E = math.log2(math.e)


def _const_spec(shape):
    nd = len(shape)
    return pl.BlockSpec(shape, lambda *_: (0,) * nd, pipeline_mode=pl.Buffered(1))


def _pack_rows(v):
    packed = pltpu.pack_elementwise([v[:, :PACKED], v[:, PACKED:]], packed_dtype=BF16)
    return lax.bitcast_convert_type(packed, I32)


def _unpack_rows(p):
    lo = lax.bitcast_convert_type(lax.shift_left(p, 16), F32)
    hi = lax.bitcast_convert_type(p & HIGH_MASK, F32)
    return lo, hi


def _sigmoid(v):
    return 0.5 * jnp.tanh(0.5 * v) + 0.5


def _per_batch(v, slab, op):
    rows, d = v.shape
    return op(v.reshape(rows // BATCH, BATCH, d), slab[None]).reshape(rows, d)


def _modulated_norm(x, g, shift, scale):
    ms = jnp.mean(x * x, axis=-1, keepdims=True)
    hn = x * lax.rsqrt(ms + NORM_EPS)
    return _per_batch(_per_batch(hn, g * (1.0 + scale), jnp.multiply), shift, jnp.add)


def _ada_kernel(c_ref, w_ref, b_ref, o_ref):
    c = c_ref[...]
    c_act = c * jax.nn.sigmoid(c)
    o_ref[0] = jnp.dot(c_act, w_ref[0], preferred_element_type=F32,
                       precision=lax.Precision.HIGHEST) + b_ref[0]


def _ada_mod(c, w_ada, b_ada):
    depth, d, n = w_ada.shape
    bsz = c.shape[0]
    tn = ADA_COLS
    return pl.pallas_call(
        _ada_kernel,
        out_shape=jax.ShapeDtypeStruct((depth, bsz, n), F32),
        grid=(depth, n // tn),
        in_specs=[
            pl.BlockSpec((bsz, d), lambda l, j: (0, 0)),
            pl.BlockSpec((1, d, tn), lambda l, j: (l, 0, j)),
            pl.BlockSpec((1, 1, tn), lambda l, j: (l, 0, j)),
        ],
        out_specs=pl.BlockSpec((1, bsz, tn), lambda l, j: (l, 0, j)),
        compiler_params=pltpu.CompilerParams(
            dimension_semantics=("arbitrary", "arbitrary"),
            vmem_limit_bytes=VMEM_LIMIT),
        name="ada_mod",
    )(c, w_ada, b_ada.reshape(depth, 1, n))


def _delay(a, steps, carry_ref):
    n = steps * BATCH
    keep = carry_ref.shape[0]
    prev = carry_ref[keep - n:, :]
    return jnp.concatenate([prev, a[:a.shape[0] - n]], axis=0)


def _save_tail(a, carry_ref):
    carry_ref[...] = a[a.shape[0] - carry_ref.shape[0]:]


def _combined_rows(x_ref, y_refs, wt_ref, gate2, rs=slice(None)):
    x = x_ref[rs, :]
    wt = wt_ref[rs, :]
    acc_lo = jnp.zeros((x.shape[0], PACKED), F32)
    acc_hi = jnp.zeros((x.shape[0], PACKED), F32)
    for k, y_ref in enumerate(y_refs):
        lo, hi = _unpack_rows(y_ref[rs, :])
        wk = wt[:, k:k + 1]
        acc_lo = acc_lo + wk * lo
        acc_hi = acc_hi + wk * hi
    ffn = jnp.concatenate([acc_lo, acc_hi], axis=-1)
    return x + _per_batch(ffn, gate2, jnp.multiply)


def _route_rows(padded, brc_ref, run_c, live, cols, idx_ref, pos_ref, wt_ref):
    rows = padded.shape[0]
    logits = padded.T[:N_EXPERTS] + brc_ref[...]

    e_iota = lax.broadcasted_iota(I32, (N_EXPERTS, rows), 0)
    vals, idxs, hots = [], [], []
    cur = logits
    for _ in range(TOP_K):
        m = jnp.max(cur, axis=0, keepdims=True)
        am = jnp.min(jnp.where(cur == m, e_iota, N_EXPERTS), axis=0, keepdims=True)
        hot = e_iota == am
        vals.append(m)
        idxs.append(am)
        hots.append(hot)
        cur = jnp.where(hot, -jnp.inf, cur)
    exps = [jnp.exp(v - vals[0]) for v in vals]
    denom = exps[0] + exps[1] + exps[2] + exps[3]

    sel = (hots[0] | hots[1] | hots[2] | hots[3]).astype(F32)
    ri = lax.broadcasted_iota(I32, (rows, rows), 0)
    ci = lax.broadcasted_iota(I32, (rows, rows), 1)
    earlier = (ri < ci).astype(BF16)
    before = jnp.dot(sel.astype(BF16), earlier, preferred_element_type=F32) + run_c[...]
    run_c[...] = run_c[...] + jnp.where(live, jnp.sum(sel, axis=1, keepdims=True), 0.0)
    poss = [jnp.sum(jnp.where(hot, before, 0.0), axis=0, keepdims=True).astype(I32)
            for hot in hots]
    idx_ref[:, cols] = jnp.concatenate(idxs, axis=0)
    pos_ref[:, cols] = jnp.concatenate(poss, axis=0)
    wt_ref[:, cols] = jnp.concatenate([e / denom for e in exps], axis=0)


N_CARRY = 6


def _mixer_kernel(with_combine, step0, *refs):
    refs = list(refs)
    x_ref = refs.pop(0)
    if with_combine:
        y_refs = [refs.pop(0) for _ in range(TOP_K)]
        wt_in_ref = refs.pop(0)
        modp_ref = refs.pop(0)
    (mod_ref, g1_ref, win_ref, convw_ref, convb_ref, wrg_ref, brga_ref, brgx_ref, lam_ref,
     wpool_ref, pscale_ref, wupa_ref, wupb_ref, wout_ref, g2_ref, wrt_ref, brc_ref) = refs[:17]
    carry_in = refs[17:17 + N_CARRY]
    o_ref, hp_ref, idx_ref, pos_ref, wt_ref, cnt_ref = refs[17 + N_CARRY:23 + N_CARRY]
    carry_out = refs[23 + N_CARRY:23 + 2 * N_CARRY]
    carries = refs[23 + 2 * N_CARRY:23 + 3 * N_CARRY]
    conv_c, p1_c, p2_c, p4_c, p8_c, h_c = carries
    run_c, logit_c = refs[23 + 3 * N_CARRY:25 + 3 * N_CARRY]
    i = pl.program_id(0)
    last = pl.num_programs(0) - 1
    rows = o_ref.shape[0]
    steps = rows // BATCH
    first_step = step0 + i * steps
    d = D_MODEL

    @pl.when(i == 0)
    def _():
        for src, dst in zip(carry_in, carries):
            dst[...] = src[...]
        run_c[...] = jnp.zeros(run_c.shape, run_c.dtype)
        logit_c[...] = jnp.zeros(logit_c.shape, logit_c.dtype)

    def route(padded, block, live):
        cols = pl.ds(pl.multiple_of(block * rows, rows), rows)
        _route_rows(padded, brc_ref, run_c, live, cols, idx_ref, pos_ref, wt_ref)

    if with_combine:
        x = _combined_rows(x_ref, y_refs, wt_in_ref, modp_ref[5])
    else:
        x = pltpu.einshape("btd->(tb)d", x_ref[...])
    shift1, scale1, gate1 = mod_ref[0], mod_ref[1], mod_ref[2]
    hb = _modulated_norm(x, g1_ref[...], shift1, scale1).astype(BF16)

    def in_proj(lo, hi):
        return jnp.dot(hb, win_ref[:, lo:hi], preferred_element_type=F32)

    x_lru = in_proj(0, d)
    cw = convw_ref[...]
    xc = (convb_ref[...] + cw[3:4] * x_lru
          + cw[2:3] * _delay(x_lru, 1, conv_c)
          + cw[1:2] * _delay(x_lru, 2, conv_c)
          + cw[0:1] * _delay(x_lru, 3, conv_c))
    _save_tail(x_lru, conv_c)
    xcb = xc.astype(BF16)
    pre = [jnp.dot(xcb[:, k * HEAD_DIM:(k + 1) * HEAD_DIM], wrg_ref[k],
                   preferred_element_type=F32) for k in range(N_HEADS)]
    pre_a = jnp.concatenate([p[:, :HEAD_DIM] for p in pre], axis=-1)
    pre_x = jnp.concatenate([p[:, HEAD_DIM:] for p in pre], axis=-1)
    r = _sigmoid(pre_a + brga_ref[...])
    gi = _sigmoid(pre_x + brgx_ref[...])
    rate = (-LRU_C * LOG2_E) * jax.nn.softplus(-lam_ref[...])
    a = jnp.exp2(r * rate)
    y = 1.0 - a * a
    mult = jnp.where(y > 0.0, y * lax.rsqrt(y), 0.0)
    u = mult * (gi * xc)
    h = h_c[...]
    hs = []
    for t in range(steps):
        sl = slice(t * BATCH, (t + 1) * BATCH)
        h = a[sl] * h + u[sl]
        hs.append(h)
    h_c[...] = h
    y_lru = jnp.concatenate(hs, axis=0) * jax.nn.gelu(in_proj(d, 2 * d))
    p_a = jnp.dot(y_lru.astype(BF16), wupa_ref[...], preferred_element_type=F32)
    merged = _sigmoid(in_proj(2 * d + POOL_WIDTH, 3 * d + POOL_WIDTH)) * p_a

    route(logit_c[...], jnp.maximum(i - 1, 0), i > 0)

    x_pool = in_proj(2 * d, 2 * d + POOL_WIDTH)
    g = POOL_GROUP_DIM
    s2 = x_pool + _delay(x_pool, 1, p1_c)
    s4 = s2[:, g:] + _delay(s2[:, g:], 2, p2_c)
    s8 = s4[:, g:] + _delay(s4[:, g:], 4, p4_c)
    s16 = s8[:, g:] + _delay(s8[:, g:], 8, p8_c)
    _save_tail(x_pool, p1_c)
    _save_tail(s2[:, g:], p2_c)
    _save_tail(s4[:, g:], p4_c)
    _save_tail(s8[:, g:], p8_c)
    wins = (s2[:, :g], s4[:, :g], s8[:, :g], s16)
    row = lax.broadcasted_iota(I32, (rows, 1), 0)
    t1 = (first_step + row // BATCH + 1).astype(F32)
    pooled = []
    for k, w in enumerate(POOL_WINDOWS):
        cnt = jnp.minimum(t1, float(w))
        pk = wins[k] / cnt - x_pool[:, k * g:(k + 1) * g]
        pooled.append(jnp.dot(pk.astype(BF16), wpool_ref[k], preferred_element_type=F32))
    y_pool = jnp.concatenate(pooled, axis=-1) * pscale_ref[...]
    p_b = jnp.dot(y_pool.astype(BF16), wupb_ref[...], preferred_element_type=F32)
    merged = merged + _sigmoid(in_proj(3 * d + POOL_WIDTH, 4 * d + POOL_WIDTH)) * p_b

    mix = jnp.dot(merged.astype(BF16), wout_ref[...], preferred_element_type=F32)
    x_mid = x + _per_batch(mix, gate1, jnp.multiply)
    o_ref[...] = x_mid
    for src, dst in zip(carries, carry_out):
        dst[...] = src[...]

    hr = _modulated_norm(x_mid, g2_ref[...], mod_ref[3], mod_ref[4])
    hp_ref[...] = _pack_rows(hr)
    padded = jnp.dot(hr.astype(BF16), wrt_ref[...], preferred_element_type=F32)
    logit_c[...] = padded

    @pl.when(i == last)
    def _():
        route(logit_c[...], i, True)

    cnt_ref[...] = run_c[...].astype(I32)


def _kmajor_index(k, nb):
    return lambda i: (k * nb + i, 0)


def _pending_specs(rows, n_tok):
    nb = n_tok // rows
    specs = [pl.BlockSpec((rows, PACKED), _kmajor_index(k, nb)) for k in range(TOP_K)]
    specs.append(pl.BlockSpec((rows, TOP_K), lambda i: (i, 0)))
    specs.append(_const_spec((6, BATCH, D_MODEL)))
    return specs


def _carry_shapes():
    g = POOL_GROUP_DIM
    return (((CONV_WIDTH - 1) * BATCH, D_MODEL), (1 * BATCH, 4 * g), (2 * BATCH, 3 * g),
            (4 * BATCH, 2 * g), (8 * BATCH, g), (BATCH, D_MODEL))


def _mixer_layer(x2d, pending, carry, step0, mod_l, g1, w_in, conv_w, conv_b, w_rg, b_rg_a,
                 b_rg_x, lam, w_pool, pool_scale, w_up_a, w_up_b, w_out, g2, w_route, b_router):
    rows = MIX_STEPS * BATCH
    row = lambda v: v.reshape(1, -1)
    weights = (mod_l, row(g1), w_in, conv_w, row(conv_b), w_rg, row(b_rg_a), row(b_rg_x),
               row(lam), w_pool, row(pool_scale), w_up_a, w_up_b, w_out, row(g2),
               w_route, b_router.reshape(N_EXPERTS, 1))
    args = [x2d]
    if pending is None:
        n_tok, d = x2d.shape[0] * x2d.shape[1], x2d.shape[2]
        in_specs = [pl.BlockSpec((BATCH, MIX_STEPS, d), lambda i: (0, i, 0))]
    else:
        n_tok, d = x2d.shape
        in_specs = [pl.BlockSpec((rows, d), lambda i: (i, 0))]
        y4, wts_prev, mod_prev = pending
        args += [y4, y4, y4, y4, wts_prev, mod_prev]
        in_specs += _pending_specs(rows, n_tok)
    args += list(weights) + list(carry)
    in_specs += [_const_spec(a.shape) for a in weights + tuple(carry)]
    carry_shapes = _carry_shapes()
    scratch = [pltpu.VMEM(shape, F32) for shape in carry_shapes]
    scratch.append(pltpu.VMEM((N_EXPERTS, 1), F32))
    scratch.append(pltpu.VMEM((rows, ROUTER_COLS), F32))
    routed = lambda dtype: jax.ShapeDtypeStruct((TOP_K, n_tok), dtype)
    whole = lambda shape: pl.BlockSpec(shape, lambda i: (0,) * len(shape))
    routed_spec = whole((TOP_K, n_tok))
    outs = pl.pallas_call(
        functools.partial(_mixer_kernel, pending is not None, step0),
        out_shape=(
            jax.ShapeDtypeStruct((n_tok, d), F32),
            jax.ShapeDtypeStruct((n_tok, PACKED), I32),
            routed(I32), routed(I32), routed(F32),
            jax.ShapeDtypeStruct((N_EXPERTS, 1), I32),
        ) + tuple(jax.ShapeDtypeStruct(shape, F32) for shape in carry_shapes),
        grid=(n_tok // rows,),
        in_specs=in_specs,
        out_specs=(
            pl.BlockSpec((rows, d), lambda i: (i, 0)),
            pl.BlockSpec((rows, PACKED), lambda i: (i, 0)),
            routed_spec, routed_spec, routed_spec,
            whole((N_EXPERTS, 1)),
        ) + tuple(whole(shape) for shape in carry_shapes),
        scratch_shapes=scratch,
        compiler_params=pltpu.CompilerParams(
            dimension_semantics=("arbitrary",), vmem_limit_bytes=VMEM_LIMIT),
        name="mixer",
    )(*args)
    return outs[:6], outs[6:]


def _padded_router_weight(w_router):
    pad = jnp.zeros((w_router.shape[0], ROUTER_COLS - N_EXPERTS), BF16)
    return jnp.concatenate([w_router.astype(BF16), pad], axis=1)


def _sc_mesh():
    return plsc.VectorSubcoreMesh(core_axis_name="c", subcore_axis_name="s")


def _sc_worker_id():
    return lax.axis_index("s") * SC_CORES + lax.axis_index("c")


def _sc_dispatch(h_packed, dest_sc, n_slots):
    n_tok, width = h_packed.shape
    per_w = n_tok // SC_WORKERS
    n_chunks = per_w // SC_CHUNK

    def body(h_hbm, dest_hbm, xs_hbm, idx_v, rows_v, sem_in, sem_out):
        base = _sc_worker_id() * per_w
        pltpu.sync_copy(dest_hbm.at[_sc_worker_id()], idx_v)

        def load(j, buf):
            return pltpu.make_async_copy(
                h_hbm.at[pl.ds(base + j * SC_CHUNK, SC_CHUNK)], rows_v.at[buf], sem_in)

        load(0, 0).start()

        @pl.loop(0, n_chunks, step=2)
        def _(j0):
            for buf in range(2):
                j = j0 + buf
                load(j, buf).wait()

                @pl.when(j + 1 < n_chunks)
                def _():
                    load(j + 1, 1 - buf).start()

                copies = [
                    pltpu.make_async_copy(
                        rows_v.at[buf], xs_hbm.at[idx_v.at[j * TOP_K + k]], sem_out)
                    for k in range(TOP_K)]
                for cp in copies:
                    cp.start()
                for cp in copies:
                    cp.wait()

    return pl.kernel(
        body,
        out_type=jax.ShapeDtypeStruct((n_slots, width), I32),
        mesh=_sc_mesh(),
        scratch_types=[
            pltpu.VMEM((n_chunks * TOP_K, SC_CHUNK), I32),
            pltpu.VMEM((2, SC_CHUNK, width), I32),
            pltpu.SemaphoreType.DMA,
            pltpu.SemaphoreType.DMA,
        ],
        name="sc_dispatch",
    )(h_packed, dest_sc)


def _sc_gather(ys_packed, dest_sc):
    width = ys_packed.shape[1]
    n_chunks = dest_sc.shape[1]
    per_w = n_chunks * SC_CHUNK
    n_rows = SC_WORKERS * per_w

    def body(ys_hbm, dest_hbm, out_hbm, idx_v, rows_v, sem_g, sem_w):
        base = _sc_worker_id() * per_w
        pltpu.sync_copy(dest_hbm.at[_sc_worker_id()], idx_v)

        def gather(j, buf):
            return pltpu.make_async_copy(ys_hbm.at[idx_v.at[j]], rows_v.at[buf], sem_g)

        def writeout(j, buf):
            return pltpu.make_async_copy(
                rows_v.at[buf], out_hbm.at[pl.ds(base + j * SC_CHUNK, SC_CHUNK)], sem_w)

        gather(0, 0).start()

        @pl.loop(0, n_chunks, step=2)
        def _(j0):
            for buf in range(2):
                j = j0 + buf
                gather(j, buf).wait()

                @pl.when(j >= 1)
                def _():
                    writeout(j - 1, 1 - buf).wait()

                @pl.when(j + 1 < n_chunks)
                def _():
                    gather(j + 1, 1 - buf).start()

                writeout(j, buf).start()

        writeout(n_chunks - 1, 1).wait()

    return pl.kernel(
        body,
        out_type=jax.ShapeDtypeStruct((n_rows, width), I32),
        mesh=_sc_mesh(),
        scratch_types=[
            pltpu.VMEM((n_chunks, SC_CHUNK), I32),
            pltpu.VMEM((2, SC_CHUNK, width), I32),
            pltpu.SemaphoreType.DMA,
            pltpu.SemaphoreType.DMA,
        ],
        name="sc_gather",
    )(ys_packed, dest_sc)


def _expert_kernel(layer, be_ref, nv_ref, first_ref, next_ref, xs_ref, win_hbm, bin_ref,
                   wout_hbm, bout_ref, ys_ref, win_f32, wout_f32, win_bf, wout_bf, sem):
    i = pl.program_id(0)
    valid = nv_ref[i]

    def fetch(e):
        return (pltpu.make_async_copy(win_hbm.at[layer, e], win_f32, sem.at[0]),
                pltpu.make_async_copy(wout_hbm.at[layer, e], wout_f32, sem.at[1]))

    @pl.when(first_ref[i] == 1)
    def _():
        @pl.when(i == 0)
        def _():
            for cp in fetch(be_ref[i]):
                cp.start()

        for cp in fetch(be_ref[i]):
            cp.wait()

        def cast_in(r, c):
            rows = pl.ds(pl.multiple_of(r * CAST_ROWS, CAST_ROWS), CAST_ROWS)
            win_bf[rows, :] = win_f32[rows, :].astype(BF16)
            return c

        def cast_out(r, c):
            rows = pl.ds(pl.multiple_of(r * CAST_ROWS, CAST_ROWS), CAST_ROWS)
            wout_bf[rows, :] = wout_f32[rows, :].astype(BF16)
            return c

        lax.fori_loop(0, D_MODEL // CAST_ROWS, cast_in, 0)
        lax.fori_loop(0, D_FF // CAST_ROWS, cast_out, 0)

        @pl.when(next_ref[i] >= 0)
        def _():
            for cp in fetch(next_ref[i]):
                cp.start()

    def compute(start, size):
        rows = pl.ds(pl.multiple_of(start, EXPERT_TAIL), size)
        row = lax.broadcasted_iota(I32, (size, 1), 0)
        packed = jnp.where(row + start < valid, xs_ref[rows, :], 0)
        lo, hi = _unpack_rows(packed)
        gu = (jnp.dot(lo.astype(BF16), win_bf[:PACKED, :], preferred_element_type=F32)
              + jnp.dot(hi.astype(BF16), win_bf[PACKED:, :], preferred_element_type=F32)
              + bin_ref[0])
        gate = jnp.minimum(gu[:, :D_FF], SWIGLU_LIMIT)
        up = jnp.clip(gu[:, D_FF:], -SWIGLU_LIMIT, SWIGLU_LIMIT)
        act = gate * _sigmoid(SWIGLU_ALPHA * gate) * (up + 1.0)
        out = jnp.dot(act.astype(BF16), wout_bf[...], preferred_element_type=F32) + bout_ref[0]
        ys_ref[rows, :] = _pack_rows(out)

    n_main = (valid + (EXPERT_SUB - EXPERT_TAIL) - 1) // EXPERT_SUB
    main_rows = n_main * EXPERT_SUB

    def main_pass(j, c):
        compute(j * EXPERT_SUB, EXPERT_SUB)
        return c

    lax.fori_loop(0, n_main, main_pass, 0)

    @pl.when(valid > main_rows)
    def _():
        compute(main_rows, EXPERT_TAIL)

    def clear(j, c):
        rows = pl.ds(pl.multiple_of(j * EXPERT_TAIL, EXPERT_TAIL), EXPERT_TAIL)
        ys_ref[rows, :] = jnp.zeros((EXPERT_TAIL, ys_ref.shape[1]), ys_ref.dtype)
        return c

    done = (main_rows + jnp.where(valid > main_rows, EXPERT_TAIL, 0)) // EXPERT_TAIL
    lax.fori_loop(done, EXPERT_ROWS // EXPERT_TAIL, clear, 0)


def _experts(layer, block_e, n_valid, is_first, next_e, xs, w_e_in, b_e_in, w_e_out, b_e_out):
    n_slots, width = xs.shape
    d = D_MODEL
    tm = EXPERT_ROWS
    bias_index = lambda i, be, nv, fi, nx: (layer, be[i], 0, 0)
    grid_spec = pltpu.PrefetchScalarGridSpec(
        num_scalar_prefetch=4,
        grid=(n_slots // tm,),
        in_specs=[
            pl.BlockSpec((tm, width), lambda i, be, nv, fi, nx: (i, 0)),
            pl.BlockSpec(memory_space=pl.ANY),
            pl.BlockSpec((None, 1, 1, 2 * D_FF), bias_index),
            pl.BlockSpec(memory_space=pl.ANY),
            pl.BlockSpec((None, 1, 1, d), bias_index),
        ],
        out_specs=pl.BlockSpec((tm, width), lambda i, be, nv, fi, nx: (i, 0)),
        scratch_shapes=[
            pltpu.VMEM((d, 2 * D_FF), F32), pltpu.VMEM((D_FF, d), F32),
            pltpu.VMEM((d, 2 * D_FF), BF16), pltpu.VMEM((D_FF, d), BF16),
            pltpu.SemaphoreType.DMA((2,)),
        ],
    )
    return pl.pallas_call(
        functools.partial(_expert_kernel, layer),
        out_shape=jax.ShapeDtypeStruct((n_slots, width), I32),
        grid_spec=grid_spec,
        compiler_params=pltpu.CompilerParams(
            dimension_semantics=("arbitrary",), vmem_limit_bytes=VMEM_LIMIT),
        name="experts",
    )(block_e, n_valid, is_first, next_e, xs, w_e_in,
      b_e_in.reshape(-1, N_EXPERTS, 1, 2 * D_FF), w_e_out, b_e_out.reshape(-1, N_EXPERTS, 1, d))


def _final_kernel(x_ref, y0_ref, y1_ref, y2_ref, y3_ref, wt_ref, mod_ref, g_ref, o_ref):
    x = _combined_rows(x_ref, (y0_ref, y1_ref, y2_ref, y3_ref), wt_ref, mod_ref[5])
    ms = jnp.mean(x * x, axis=-1, keepdims=True)
    out = x * lax.rsqrt(ms + NORM_EPS) * g_ref[...]
    o_ref[...] = pltpu.einshape("(tb)d->btd", out, b=BATCH)


def _final_norm(x2d, pending, g):
    n_tok, d = x2d.shape
    tr = NORM_ROWS
    y4, wts, mod_l = pending
    return pl.pallas_call(
        _final_kernel,
        out_shape=jax.ShapeDtypeStruct((BATCH, n_tok // BATCH, d), F32),
        grid=(n_tok // tr,),
        in_specs=([pl.BlockSpec((tr, d), lambda i: (i, 0))] + _pending_specs(tr, n_tok)
                  + [_const_spec((1, d))]),
        out_specs=pl.BlockSpec((BATCH, tr // BATCH, d), lambda i: (0, i, 0)),
        compiler_params=pltpu.CompilerParams(
            dimension_semantics=("arbitrary",), vmem_limit_bytes=VMEM_LIMIT),
        name="final_norm",
    )(x2d, y4, y4, y4, y4, wts, mod_l, g.reshape(1, d))


def _moe_layer(layer, h_packed, idx, pos, counts, w_e_in, b_e_in, w_e_out, b_e_out):
    n_tok = h_packed.shape[0]
    tm = EXPERT_ROWS
    padded = ((counts + tm - 1) // tm) * tm
    pad_end = jnp.cumsum(padded)
    start_pad = pad_end - padded
    experts = jnp.arange(N_EXPERTS, dtype=I32)
    dest = jnp.sum(jnp.where(idx[:, :, None] == experts, start_pad, 0), axis=-1) + pos
    n_blocks = (n_tok * TOP_K) // tm + N_EXPERTS
    n_slots = n_blocks * tm
    block_start = jnp.arange(n_blocks, dtype=I32) * tm
    block_e = jnp.minimum(
        jnp.sum((block_start[:, None] >= pad_end[None, :]).astype(I32), axis=-1),
        N_EXPERTS - 1)
    hot_e = block_e[:, None] == experts[None, :]
    rows_left = jnp.sum(jnp.where(hot_e, (start_pad + counts)[None, :], 0), axis=-1) - block_start
    n_valid = jnp.where(block_start < pad_end[-1], jnp.clip(rows_left, 0, tm), 0).astype(I32)
    is_first = jnp.sum(jnp.where(hot_e, start_pad[None, :], 0), axis=-1) == block_start
    is_first = (is_first & (n_valid > 0)).astype(I32)
    later = (experts[None, :] > experts[:, None]) & (counts[None, :] > 0)
    next_of = jnp.min(jnp.where(later, experts[None, :], N_EXPERTS), axis=-1)
    next_of = jnp.where(next_of == N_EXPERTS, -1, next_of)
    next_e = jnp.sum(jnp.where(hot_e, next_of[None, :], 0), axis=-1).astype(I32)

    per_w = n_tok // SC_WORKERS
    n_chunks = per_w // SC_CHUNK
    dest_scatter = (dest.reshape(TOP_K, SC_WORKERS, n_chunks, SC_CHUNK)
                    .transpose(1, 2, 0, 3)
                    .reshape(SC_WORKERS, n_chunks * TOP_K, SC_CHUNK))
    xs = _sc_dispatch(h_packed, dest_scatter, n_slots)
    ys = _experts(layer, block_e, n_valid, is_first, next_e, xs,
                  w_e_in, b_e_in, w_e_out, b_e_out)
    dest_gather = dest.reshape(SC_WORKERS, n_chunks * TOP_K, SC_CHUNK)
    return _sc_gather(ys, dest_gather)


def kernel(x, c, norm1_g, norm2_g, w_ada, b_ada, w_in, conv_w, conv_b, w_rg_a, b_rg_a,
           w_rg_x, b_rg_x, lru_lambda, w_pool, pool_scale, w_up_a, w_up_b, w_out,
           w_router, b_router, w_e_in, b_e_in, w_e_out, b_e_out, final_g):
    depth = w_in.shape[0]
    bsz, seq, d = x.shape
    assert bsz == BATCH and d == D_MODEL
    mod = _ada_mod(c, w_ada, b_ada).reshape(depth, bsz, 6, d).transpose(0, 2, 1, 3)
    w_rg = jnp.concatenate([w_rg_a, w_rg_x], axis=-1).astype(BF16)
    part = seq // SEQ_PARTS
    xs = [x if SEQ_PARTS == 1 else x[:, p * part:(p + 1) * part] for p in range(SEQ_PARTS)]
    pending = [None] * SEQ_PARTS
    for l in range(depth):
        mixer_weights = (
            mod[l], norm1_g[l], w_in[l].astype(BF16), conv_w[l], conv_b[l], w_rg[l],
            b_rg_a[l], b_rg_x[l], lru_lambda[l], w_pool[l].astype(BF16), pool_scale[l],
            w_up_a[l].astype(BF16), w_up_b[l].astype(BF16), w_out[l].astype(BF16),
            norm2_g[l], _padded_router_weight(w_router[l]), b_router[l])
        carry = tuple(jnp.zeros(shape, F32) for shape in _carry_shapes())
        for p in range(SEQ_PARTS):
            (xs[p], h_packed, idx, pos, wts, counts), carry = _mixer_layer(
                xs[p], pending[p], carry, p * part, *mixer_weights)
            y4 = _moe_layer(l, h_packed, idx, pos, counts[:, 0],
                            w_e_in, b_e_in, w_e_out, b_e_out)
            pending[p] = (y4, wts.T, mod[l])
    outs = [_final_norm(xs[p], pending[p], final_g) for p in range(SEQ_PARTS)]
    return outs[0] if SEQ_PARTS == 1 else jnp.concatenate(outs, axis=1)
```

```python
import functools
import math

import jax
import jax.numpy as jnp
from jax import lax
from jax.experimental import pallas as pl
from jax.experimental.pallas import tpu as pltpu
from jax.experimental.pallas import tpu_sc as plsc

D_MODEL = 1024
BATCH = 8
N_HEADS = 8
HEAD_DIM = 128
CONV_WIDTH = 4
LRU_C = 8.0
POOL_WINDOWS = (2, 4, 8, 16)
POOL_WIDTH = 512
POOL_GROUP_DIM = 128
N_EXPERTS = 32
TOP_K = 4
D_FF = 1024
SWIGLU_LIMIT = 7.0
SWIGLU_ALPHA = 1.702
NORM_EPS = 1e-6

SUBLANES = 8
VMEM_LIMIT = 56 * 1024 * 1024
SC_CORES = 2
SC_SUBCORES = 16
SC_WORKERS = SC_CORES * SC_SUBCORES
SC_CHUNK = 64

MIX_STEPS = 32
ROUTER_COLS = 128
EXPERT_ROWS = 1024
EXPERT_SUB = 512
EXPERT_TAIL = 256
ADA_COLS = 1536
NORM_ROWS = 512
CAST_ROWS = 128
PACKED = D_MODEL // 2

F32 = jnp.float32
BF16 = jnp.bfloat16
I32 = jnp.int32
HIGH_MASK = -65536
LOG2_E = math.log2(math.e)


def _const_spec(shape):
    nd = len(shape)
    return pl.BlockSpec(shape, lambda *_: (0,) * nd, pipeline_mode=pl.Buffered(1))


def _pack_rows(v):
    packed = pltpu.pack_elementwise([v[:, :PACKED], v[:, PACKED:]], packed_dtype=BF16)
    return lax.bitcast_convert_type(packed, I32)


def _unpack_rows(p):
    lo = lax.bitcast_convert_type(lax.shift_left(p, 16), F32)
    hi = lax.bitcast_convert_type(p & HIGH_MASK, F32)
    return lo, hi


def _sigmoid(v):
    return 0.5 * jnp.tanh(0.5 * v) + 0.5


def _per_batch(v, slab, op):
    rows, d = v.shape
    return op(v.reshape(rows // BATCH, BATCH, d), slab[None]).reshape(rows, d)


def _modulated_norm(x, g, shift, scale):
    ms = jnp.mean(x * x, axis=-1, keepdims=True)
    hn = x * lax.rsqrt(ms + NORM_EPS)
    return _per_batch(_per_batch(hn, g * (1.0 + scale), jnp.multiply), shift, jnp.add)


def _ada_kernel(c_ref, w_ref, b_ref, o_ref):
    c = c_ref[...]
    c_act = c * jax.nn.sigmoid(c)
    o_ref[0] = jnp.dot(c_act, w_ref[0], preferred_element_type=F32,
                       precision=lax.Precision.HIGHEST) + b_ref[0]


def _ada_mod(c, w_ada, b_ada):
    depth, d, n = w_ada.shape
    bsz = c.shape[0]
    tn = ADA_COLS
    return pl.pallas_call(
        _ada_kernel,
        out_shape=jax.ShapeDtypeStruct((depth, bsz, n), F32),
        grid=(depth, n // tn),
        in_specs=[
            pl.BlockSpec((bsz, d), lambda l, j: (0, 0)),
            pl.BlockSpec((1, d, tn), lambda l, j: (l, 0, j)),
            pl.BlockSpec((1, 1, tn), lambda l, j: (l, 0, j)),
        ],
        out_specs=pl.BlockSpec((1, bsz, tn), lambda l, j: (l, 0, j)),
        compiler_params=pltpu.CompilerParams(
            dimension_semantics=("arbitrary", "arbitrary"),
            vmem_limit_bytes=VMEM_LIMIT),
        name="ada_mod",
    )(c, w_ada, b_ada.reshape(depth, 1, n))


def _delay(a, steps, carry_ref):
    n = steps * BATCH
    keep = carry_ref.shape[0]
    prev = carry_ref[keep - n:, :]
    return jnp.concatenate([prev, a[:a.shape[0] - n]], axis=0)


def _save_tail(a, carry_ref):
    carry_ref[...] = a[a.shape[0] - carry_ref.shape[0]:]


def _combined_rows(x_ref, y_refs, wt_ref, gate2, rs=slice(None)):
    x = x_ref[rs, :]
    wt = wt_ref[rs, :]
    acc_lo = jnp.zeros((x.shape[0], PACKED), F32)
    acc_hi = jnp.zeros((x.shape[0], PACKED), F32)
    for k, y_ref in enumerate(y_refs):
        lo, hi = _unpack_rows(y_ref[rs, :])
        wk = wt[:, k:k + 1]
        acc_lo = acc_lo + wk * lo
        acc_hi = acc_hi + wk * hi
    ffn = jnp.concatenate([acc_lo, acc_hi], axis=-1)
    return x + _per_batch(ffn, gate2, jnp.multiply)


def _route_rows(padded, brc_ref, run_c, live, cols, idx_ref, pos_ref, wt_ref):
    rows = padded.shape[0]
    logits = padded.T[:N_EXPERTS] + brc_ref[...]

    e_iota = lax.broadcasted_iota(I32, (N_EXPERTS, rows), 0)
    vals, idxs, hots = [], [], []
    cur = logits
    for _ in range(TOP_K):
        m = jnp.max(cur, axis=0, keepdims=True)
        am = jnp.min(jnp.where(cur == m, e_iota, N_EXPERTS), axis=0, keepdims=True)
        hot = e_iota == am
        vals.append(m)
        idxs.append(am)
        hots.append(hot)
        cur = jnp.where(hot, -jnp.inf, cur)
    exps = [jnp.exp(v - vals[0]) for v in vals]
    denom = exps[0] + exps[1] + exps[2] + exps[3]

    sel = (hots[0] | hots[1] | hots[2] | hots[3]).astype(F32)
    ri = lax.broadcasted_iota(I32, (rows, rows), 0)
    ci = lax.broadcasted_iota(I32, (rows, rows), 1)
    earlier = (ri < ci).astype(BF16)
    before = jnp.dot(sel.astype(BF16), earlier, preferred_element_type=F32) + run_c[...]
    run_c[...] = run_c[...] + jnp.where(live, jnp.sum(sel, axis=1, keepdims=True), 0.0)
    poss = [jnp.sum(jnp.where(hot, before, 0.0), axis=0, keepdims=True).astype(I32)
            for hot in hots]
    idx_ref[:, cols] = jnp.concatenate(idxs, axis=0)
    pos_ref[:, cols] = jnp.concatenate(poss, axis=0)
    wt_ref[:, cols] = jnp.concatenate([e / denom for e in exps], axis=0)


VEC_G1, VEC_CONV_B, VEC_BRG_A, VEC_BRG_X, VEC_LAM, VEC_G2, VEC_PSCALE = range(7)


def _mixer_kernel(with_combine, *refs):
    refs = list(refs)
    n_in = 1 + (TOP_K + 1 if with_combine else 0)
    first, nxt = refs[:n_in], refs[n_in:2 * n_in]
    refs = refs[2 * n_in:]
    if with_combine:
        modp_ref = refs.pop(0)
    (mod_ref, vec_ref, win_ref, convw_ref, wrg_ref, wpool_ref, wupa_ref, wupb_ref, wout_ref,
     wrt_ref, brc_ref,
     o_ref, hp_ref, idx_ref, pos_ref, wt_ref, cnt_ref,
     conv_c, p1_c, p2_c, p4_c, p8_c, h_c, run_c, logit_c, x_c, hb_c) = refs
    i = pl.program_id(0)
    last = pl.num_programs(0) - 1
    cur = i % 2
    rows = o_ref.shape[0]
    steps = rows // BATCH
    d = D_MODEL
    vec = vec_ref[...]
    row_of = lambda k: vec[k:k + 1]
    shift1, scale1, gate1 = mod_ref[0], mod_ref[1], mod_ref[2]

    def prologue(in_refs):
        if with_combine:
            x = _combined_rows(in_refs[0], in_refs[1:1 + TOP_K], in_refs[1 + TOP_K],
                               modp_ref[5])
        else:
            x = pltpu.einshape("btd->(tb)d", in_refs[0][...])
        return x, _modulated_norm(x, row_of(VEC_G1), shift1, scale1).astype(BF16)

    @pl.when(i == 0)
    def _():
        for ref in (conv_c, p1_c, p2_c, p4_c, p8_c, h_c, run_c, logit_c):
            ref[...] = jnp.zeros(ref.shape, ref.dtype)
        x_c[0], hb_c[0] = prologue(first)

    def route(padded, block, live):
        cols = pl.ds(pl.multiple_of(block * rows, rows), rows)
        _route_rows(padded, brc_ref, run_c, live, cols, idx_ref, pos_ref, wt_ref)

    def in_proj(lo, hi):
        return jnp.dot(hb_c[cur], win_ref[:, lo:hi], preferred_element_type=F32)

    x_lru = in_proj(0, d)
    cw = convw_ref[...]
    xc = (row_of(VEC_CONV_B) + cw[3:4] * x_lru
          + cw[2:3] * _delay(x_lru, 1, conv_c)
          + cw[1:2] * _delay(x_lru, 2, conv_c)
          + cw[0:1] * _delay(x_lru, 3, conv_c))
    _save_tail(x_lru, conv_c)
    xcb = xc.astype(BF16)
    pre = [jnp.dot(xcb[:, k * HEAD_DIM:(k + 1) * HEAD_DIM], wrg_ref[k],
                   preferred_element_type=F32) for k in range(N_HEADS)]
    pre_a = jnp.concatenate([p[:, :HEAD_DIM] for p in pre], axis=-1)
    pre_x = jnp.concatenate([p[:, HEAD_DIM:] for p in pre], axis=-1)
    r = _sigmoid(pre_a + row_of(VEC_BRG_A))
    gi = _sigmoid(pre_x + row_of(VEC_BRG_X))
    rate = (-LRU_C * LOG2_E) * jax.nn.softplus(-row_of(VEC_LAM))
    a = jnp.exp2(r * rate)
    y = 1.0 - a * a
    mult = jnp.where(y > 0.0, y * lax.rsqrt(y), 0.0)
    u = mult * (gi * xc)
    h = h_c[...]
    hs = []
    for t in range(steps):
        sl = slice(t * BATCH, (t + 1) * BATCH)
        h = a[sl] * h + u[sl]
        hs.append(h)
    h_c[...] = h
    y_lru = jnp.concatenate(hs, axis=0) * jax.nn.gelu(in_proj(d, 2 * d))
    p_a = jnp.dot(y_lru.astype(BF16), wupa_ref[...], preferred_element_type=F32)
    merged = _sigmoid(in_proj(2 * d + POOL_WIDTH, 3 * d + POOL_WIDTH)) * p_a

    route(logit_c[...], jnp.maximum(i - 1, 0), i > 0)
    x_c[1 - cur], hb_c[1 - cur] = prologue(nxt)

    x_pool = in_proj(2 * d, 2 * d + POOL_WIDTH)
    g = POOL_GROUP_DIM
    s2 = x_pool + _delay(x_pool, 1, p1_c)
    s4 = s2[:, g:] + _delay(s2[:, g:], 2, p2_c)
    s8 = s4[:, g:] + _delay(s4[:, g:], 4, p4_c)
    s16 = s8[:, g:] + _delay(s8[:, g:], 8, p8_c)
    _save_tail(x_pool, p1_c)
    _save_tail(s2[:, g:], p2_c)
    _save_tail(s4[:, g:], p4_c)
    _save_tail(s8[:, g:], p8_c)
    wins = (s2[:, :g], s4[:, :g], s8[:, :g], s16)
    row = lax.broadcasted_iota(I32, (rows, 1), 0)
    t1 = (i * steps + row // BATCH + 1).astype(F32)
    pooled = []
    for k, w in enumerate(POOL_WINDOWS):
        cnt = jnp.minimum(t1, float(w))
        pk = wins[k] / cnt - x_pool[:, k * g:(k + 1) * g]
        pooled.append(jnp.dot(pk.astype(BF16), wpool_ref[k], preferred_element_type=F32))
    y_pool = jnp.concatenate(pooled, axis=-1) * row_of(VEC_PSCALE)[:, :POOL_WIDTH]
    p_b = jnp.dot(y_pool.astype(BF16), wupb_ref[...], preferred_element_type=F32)
    merged = merged + _sigmoid(in_proj(3 * d + POOL_WIDTH, 4 * d + POOL_WIDTH)) * p_b

    mix = jnp.dot(merged.astype(BF16), wout_ref[...], preferred_element_type=F32)
    x_mid = x_c[cur] + _per_batch(mix, gate1, jnp.multiply)
    o_ref[...] = x_mid

    hr = _modulated_norm(x_mid, row_of(VEC_G2), mod_ref[3], mod_ref[4])
    hp_ref[...] = _pack_rows(hr)
    padded = jnp.dot(hr.astype(BF16), wrt_ref[...], preferred_element_type=F32)
    logit_c[...] = padded

    @pl.when(i == last)
    def _():
        route(logit_c[...], i, True)

    cnt_ref[...] = run_c[...].astype(I32)


def _pending_specs(rows, n_tok, block_of):
    nb = n_tok // rows
    y_spec = lambda k: pl.BlockSpec((rows, PACKED), lambda i: (k * nb + block_of(i), 0))
    return ([y_spec(k) for k in range(TOP_K)]
            + [pl.BlockSpec((rows, TOP_K), lambda i: (block_of(i), 0))])


def _mixer_layer(x_in, pending, mod_l, vec, w_in, conv_w, w_rg, w_pool, w_up_a, w_up_b, w_out,
                 w_route, b_router):
    rows = MIX_STEPS * BATCH
    weights = [mod_l, vec, w_in, conv_w, w_rg, w_pool, w_up_a, w_up_b, w_out, w_route,
               b_router.reshape(N_EXPERTS, 1)]
    if pending is None:
        n_tok, d = x_in.shape[0] * x_in.shape[1], x_in.shape[2]
        inputs = [x_in]
        specs = lambda block_of: [
            pl.BlockSpec((BATCH, MIX_STEPS, d), lambda i: (0, block_of(i), 0))]
    else:
        n_tok, d = x_in.shape
        y4, wts_prev, mod_prev = pending
        inputs = [x_in, y4, y4, y4, y4, wts_prev]
        weights.insert(0, mod_prev)
        specs = lambda block_of: ([pl.BlockSpec((rows, d), lambda i: (block_of(i), 0))]
                                  + _pending_specs(rows, n_tok, block_of))
    nb = n_tok // rows
    in_specs = (specs(lambda i: 0) + specs(lambda i: jnp.minimum(i + 1, nb - 1))
                + [_const_spec(a.shape) for a in weights])
    g = POOL_GROUP_DIM
    scratch = [
        pltpu.VMEM(((CONV_WIDTH - 1) * BATCH, d), F32),
        pltpu.VMEM((1 * BATCH, 4 * g), F32), pltpu.VMEM((2 * BATCH, 3 * g), F32),
        pltpu.VMEM((4 * BATCH, 2 * g), F32), pltpu.VMEM((8 * BATCH, g), F32),
        pltpu.VMEM((BATCH, d), F32),
        pltpu.VMEM((N_EXPERTS, 1), F32),
        pltpu.VMEM((rows, ROUTER_COLS), F32),
        pltpu.VMEM((2, rows, d), F32), pltpu.VMEM((2, rows, d), BF16),
    ]
    routed = lambda dtype: jax.ShapeDtypeStruct((TOP_K, n_tok), dtype)
    whole = lambda shape: pl.BlockSpec(shape, lambda i: (0,) * len(shape))
    routed_spec = whole((TOP_K, n_tok))
    return pl.pallas_call(
        functools.partial(_mixer_kernel, pending is not None),
        out_shape=(
            jax.ShapeDtypeStruct((n_tok, d), F32),
            jax.ShapeDtypeStruct((n_tok, PACKED), I32),
            routed(I32), routed(I32), routed(F32),
            jax.ShapeDtypeStruct((N_EXPERTS, 1), I32),
        ),
        grid=(nb,),
        in_specs=in_specs,
        out_specs=(
            pl.BlockSpec((rows, d), lambda i: (i, 0)),
            pl.BlockSpec((rows, PACKED), lambda i: (i, 0)),
            routed_spec, routed_spec, routed_spec,
            whole((N_EXPERTS, 1)),
        ),
        scratch_shapes=scratch,
        compiler_params=pltpu.CompilerParams(
            dimension_semantics=("arbitrary",), vmem_limit_bytes=VMEM_LIMIT),
        name="mixer",
    )(*(inputs + inputs + weights))


def _padded_router_weight(w_router):
    pad = jnp.zeros((w_router.shape[0], ROUTER_COLS - N_EXPERTS), BF16)
    return jnp.concatenate([w_router.astype(BF16), pad], axis=1)


def _sc_mesh():
    return plsc.VectorSubcoreMesh(core_axis_name="c", subcore_axis_name="s")


def _sc_worker_id():
    return lax.axis_index("s") * SC_CORES + lax.axis_index("c")


def _sc_dispatch(h_packed, dest_sc, n_slots):
    n_tok, width = h_packed.shape
    per_w = n_tok // SC_WORKERS
    n_chunks = per_w // SC_CHUNK

    def body(h_hbm, dest_hbm, xs_hbm, idx_v, rows_v, sem_in, sem_out):
        base = _sc_worker_id() * per_w
        pltpu.sync_copy(dest_hbm.at[_sc_worker_id()], idx_v)

        def load(j, buf):
            return pltpu.make_async_copy(
                h_hbm.at[pl.ds(base + j * SC_CHUNK, SC_CHUNK)], rows_v.at[buf], sem_in)

        load(0, 0).start()

        @pl.loop(0, n_chunks, step=2)
        def _(j0):
            for buf in range(2):
                j = j0 + buf
                load(j, buf).wait()

                @pl.when(j + 1 < n_chunks)
                def _():
                    load(j + 1, 1 - buf).start()

                copies = [
                    pltpu.make_async_copy(
                        rows_v.at[buf], xs_hbm.at[idx_v.at[j * TOP_K + k]], sem_out)
                    for k in range(TOP_K)]
                for cp in copies:
                    cp.start()
                for cp in copies:
                    cp.wait()

    return pl.kernel(
        body,
        out_type=jax.ShapeDtypeStruct((n_slots, width), I32),
        mesh=_sc_mesh(),
        scratch_types=[
            pltpu.VMEM((n_chunks * TOP_K, SC_CHUNK), I32),
            pltpu.VMEM((2, SC_CHUNK, width), I32),
            pltpu.SemaphoreType.DMA,
            pltpu.SemaphoreType.DMA,
        ],
        name="sc_dispatch",
    )(h_packed, dest_sc)


def _sc_gather(ys_packed, dest_sc):
    width = ys_packed.shape[1]
    n_chunks = dest_sc.shape[1]
    per_w = n_chunks * SC_CHUNK
    n_rows = SC_WORKERS * per_w

    def body(ys_hbm, dest_hbm, out_hbm, idx_v, rows_v, sem_g, sem_w):
        base = _sc_worker_id() * per_w
        pltpu.sync_copy(dest_hbm.at[_sc_worker_id()], idx_v)

        def gather(j, buf):
            return pltpu.make_async_copy(ys_hbm.at[idx_v.at[j]], rows_v.at[buf], sem_g)

        def writeout(j, buf):
            return pltpu.make_async_copy(
                rows_v.at[buf], out_hbm.at[pl.ds(base + j * SC_CHUNK, SC_CHUNK)], sem_w)

        gather(0, 0).start()

        @pl.loop(0, n_chunks, step=2)
        def _(j0):
            for buf in range(2):
                j = j0 + buf
                gather(j, buf).wait()

                @pl.when(j >= 1)
                def _():
                    writeout(j - 1, 1 - buf).wait()

                @pl.when(j + 1 < n_chunks)
                def _():
                    gather(j + 1, 1 - buf).start()

                writeout(j, buf).start()

        writeout(n_chunks - 1, 1).wait()

    return pl.kernel(
        body,
        out_type=jax.ShapeDtypeStruct((n_rows, width), I32),
        mesh=_sc_mesh(),
        scratch_types=[
            pltpu.VMEM((n_chunks, SC_CHUNK), I32),
            pltpu.VMEM((2, SC_CHUNK, width), I32),
            pltpu.SemaphoreType.DMA,
            pltpu.SemaphoreType.DMA,
        ],
        name="sc_gather",
    )(ys_packed, dest_sc)


def _expert_kernel(layer, be_ref, nv_ref, first_ref, next_ref, xs_ref, win_hbm, bin_ref,
                   wout_hbm, bout_ref, ys_ref, win_f32, wout_f32, win_bf, wout_bf, sem):
    i = pl.program_id(0)
    valid = nv_ref[i]

    def fetch(e):
        return (pltpu.make_async_copy(win_hbm.at[layer, e], win_f32, sem.at[0]),
                pltpu.make_async_copy(wout_hbm.at[layer, e], wout_f32, sem.at[1]))

    @pl.when(first_ref[i] == 1)
    def _():
        @pl.when(i == 0)
        def _():
            for cp in fetch(be_ref[i]):
                cp.start()

        for cp in fetch(be_ref[i]):
            cp.wait()

        def cast_in(r, c):
            rows = pl.ds(pl.multiple_of(r * CAST_ROWS, CAST_ROWS), CAST_ROWS)
            win_bf[rows, :] = win_f32[rows, :].astype(BF16)
            return c

        def cast_out(r, c):
            rows = pl.ds(pl.multiple_of(r * CAST_ROWS, CAST_ROWS), CAST_ROWS)
            wout_bf[rows, :] = wout_f32[rows, :].astype(BF16)
            return c

        lax.fori_loop(0, D_MODEL // CAST_ROWS, cast_in, 0)
        lax.fori_loop(0, D_FF // CAST_ROWS, cast_out, 0)

        @pl.when(next_ref[i] >= 0)
        def _():
            for cp in fetch(next_ref[i]):
                cp.start()

    def compute(start, size):
        rows = pl.ds(pl.multiple_of(start, EXPERT_TAIL), size)
        row = lax.broadcasted_iota(I32, (size, 1), 0)
        packed = jnp.where(row + start < valid, xs_ref[rows, :], 0)
        lo, hi = _unpack_rows(packed)
        gu = (jnp.dot(lo.astype(BF16), win_bf[:PACKED, :], preferred_element_type=F32)
              + jnp.dot(hi.astype(BF16), win_bf[PACKED:, :], preferred_element_type=F32)
              + bin_ref[0])
        gate = jnp.minimum(gu[:, :D_FF], SWIGLU_LIMIT)
        up = jnp.clip(gu[:, D_FF:], -SWIGLU_LIMIT, SWIGLU_LIMIT)
        act = gate * _sigmoid(SWIGLU_ALPHA * gate) * (up + 1.0)
        out = jnp.dot(act.astype(BF16), wout_bf[...], preferred_element_type=F32) + bout_ref[0]
        ys_ref[rows, :] = _pack_rows(out)

    n_main = (valid + (EXPERT_SUB - EXPERT_TAIL) - 1) // EXPERT_SUB
    main_rows = n_main * EXPERT_SUB

    def main_pass(j, c):
        compute(j * EXPERT_SUB, EXPERT_SUB)
        return c

    lax.fori_loop(0, n_main, main_pass, 0)

    @pl.when(valid > main_rows)
    def _():
        compute(main_rows, EXPERT_TAIL)

    def clear(j, c):
        rows = pl.ds(pl.multiple_of(j * EXPERT_TAIL, EXPERT_TAIL), EXPERT_TAIL)
        ys_ref[rows, :] = jnp.zeros((EXPERT_TAIL, ys_ref.shape[1]), ys_ref.dtype)
        return c

    done = (main_rows + jnp.where(valid > main_rows, EXPERT_TAIL, 0)) // EXPERT_TAIL
    lax.fori_loop(done, EXPERT_ROWS // EXPERT_TAIL, clear, 0)


def _experts(layer, block_e, n_valid, is_first, next_e, xs, w_e_in, b_e_in, w_e_out, b_e_out):
    n_slots, width = xs.shape
    d = D_MODEL
    tm = EXPERT_ROWS
    bias_index = lambda i, be, nv, fi, nx: (layer, be[i], 0, 0)
    grid_spec = pltpu.PrefetchScalarGridSpec(
        num_scalar_prefetch=4,
        grid=(n_slots // tm,),
        in_specs=[
            pl.BlockSpec((tm, width), lambda i, be, nv, fi, nx: (i, 0)),
            pl.BlockSpec(memory_space=pl.ANY),
            pl.BlockSpec((None, 1, 1, 2 * D_FF), bias_index),
            pl.BlockSpec(memory_space=pl.ANY),
            pl.BlockSpec((None, 1, 1, d), bias_index),
        ],
        out_specs=pl.BlockSpec((tm, width), lambda i, be, nv, fi, nx: (i, 0)),
        scratch_shapes=[
            pltpu.VMEM((d, 2 * D_FF), F32), pltpu.VMEM((D_FF, d), F32),
            pltpu.VMEM((d, 2 * D_FF), BF16), pltpu.VMEM((D_FF, d), BF16),
            pltpu.SemaphoreType.DMA((2,)),
        ],
    )
    return pl.pallas_call(
        functools.partial(_expert_kernel, layer),
        out_shape=jax.ShapeDtypeStruct((n_slots, width), I32),
        grid_spec=grid_spec,
        compiler_params=pltpu.CompilerParams(
            dimension_semantics=("arbitrary",), vmem_limit_bytes=VMEM_LIMIT),
        name="experts",
    )(block_e, n_valid, is_first, next_e, xs, w_e_in,
      b_e_in.reshape(-1, N_EXPERTS, 1, 2 * D_FF), w_e_out, b_e_out.reshape(-1, N_EXPERTS, 1, d))


def _final_kernel(x_ref, y0_ref, y1_ref, y2_ref, y3_ref, wt_ref, mod_ref, g_ref, o_ref):
    x = _combined_rows(x_ref, (y0_ref, y1_ref, y2_ref, y3_ref), wt_ref, mod_ref[5])
    ms = jnp.mean(x * x, axis=-1, keepdims=True)
    out = x * lax.rsqrt(ms + NORM_EPS) * g_ref[...]
    o_ref[...] = pltpu.einshape("(tb)d->btd", out, b=BATCH)


def _final_norm(x2d, pending, g):
    n_tok, d = x2d.shape
    tr = NORM_ROWS
    y4, wts, mod_l = pending
    return pl.pallas_call(
        _final_kernel,
        out_shape=jax.ShapeDtypeStruct((BATCH, n_tok // BATCH, d), F32),
        grid=(n_tok // tr,),
        in_specs=([pl.BlockSpec((tr, d), lambda i: (i, 0))]
                  + _pending_specs(tr, n_tok, lambda i: i)
                  + [_const_spec(mod_l.shape), _const_spec((1, d))]),
        out_specs=pl.BlockSpec((BATCH, tr // BATCH, d), lambda i: (0, i, 0)),
        compiler_params=pltpu.CompilerParams(
            dimension_semantics=("arbitrary",), vmem_limit_bytes=VMEM_LIMIT),
        name="final_norm",
    )(x2d, y4, y4, y4, y4, wts, mod_l, g.reshape(1, d))


def _moe_layer(layer, h_packed, idx, pos, counts, w_e_in, b_e_in, w_e_out, b_e_out):
    n_tok = h_packed.shape[0]
    tm = EXPERT_ROWS
    padded = ((counts + tm - 1) // tm) * tm
    pad_end = jnp.cumsum(padded)
    start_pad = pad_end - padded
    experts = jnp.arange(N_EXPERTS, dtype=I32)
    dest = jnp.sum(jnp.where(idx[:, :, None] == experts, start_pad, 0), axis=-1) + pos
    n_blocks = (n_tok * TOP_K) // tm + N_EXPERTS
    n_slots = n_blocks * tm
    block_start = jnp.arange(n_blocks, dtype=I32) * tm
    block_e = jnp.minimum(
        jnp.sum((block_start[:, None] >= pad_end[None, :]).astype(I32), axis=-1),
        N_EXPERTS - 1)
    hot_e = block_e[:, None] == experts[None, :]
    rows_left = jnp.sum(jnp.where(hot_e, (start_pad + counts)[None, :], 0), axis=-1) - block_start
    n_valid = jnp.where(block_start < pad_end[-1], jnp.clip(rows_left, 0, tm), 0).astype(I32)
    is_first = jnp.sum(jnp.where(hot_e, start_pad[None, :], 0), axis=-1) == block_start
    is_first = (is_first & (n_valid > 0)).astype(I32)
    later = (experts[None, :] > experts[:, None]) & (counts[None, :] > 0)
    next_of = jnp.min(jnp.where(later, experts[None, :], N_EXPERTS), axis=-1)
    next_of = jnp.where(next_of == N_EXPERTS, -1, next_of)
    next_e = jnp.sum(jnp.where(hot_e, next_of[None, :], 0), axis=-1).astype(I32)

    per_w = n_tok // SC_WORKERS
    n_chunks = per_w // SC_CHUNK
    dest_scatter = (dest.reshape(TOP_K, SC_WORKERS, n_chunks, SC_CHUNK)
                    .transpose(1, 2, 0, 3)
                    .reshape(SC_WORKERS, n_chunks * TOP_K, SC_CHUNK))
    xs = _sc_dispatch(h_packed, dest_scatter, n_slots)
    ys = _experts(layer, block_e, n_valid, is_first, next_e, xs,
                  w_e_in, b_e_in, w_e_out, b_e_out)
    dest_gather = dest.reshape(SC_WORKERS, n_chunks * TOP_K, SC_CHUNK)
    return _sc_gather(ys, dest_gather)


def kernel(x, c, norm1_g, norm2_g, w_ada, b_ada, w_in, conv_w, conv_b, w_rg_a, b_rg_a,
           w_rg_x, b_rg_x, lru_lambda, w_pool, pool_scale, w_up_a, w_up_b, w_out,
           w_router, b_router, w_e_in, b_e_in, w_e_out, b_e_out, final_g):
    depth = w_in.shape[0]
    bsz, seq, d = x.shape
    assert bsz == BATCH and d == D_MODEL
    mod = _ada_mod(c, w_ada, b_ada).reshape(depth, bsz, 6, d).transpose(0, 2, 1, 3)
    w_rg = jnp.concatenate([w_rg_a, w_rg_x], axis=-1).astype(BF16)
    pending = None
    xcur = x
    for l in range(depth):
        pad = jnp.zeros((d - pool_scale.shape[1],), F32)
        vec = jnp.stack([norm1_g[l], conv_b[l], b_rg_a[l], b_rg_x[l], lru_lambda[l],
                         norm2_g[l], jnp.concatenate([pool_scale[l], pad]),
                         jnp.zeros((d,), F32)])
        xcur, h_packed, idx, pos, wts, counts = _mixer_layer(
            xcur, pending, mod[l], vec, w_in[l].astype(BF16), conv_w[l], w_rg[l],
            w_pool[l].astype(BF16), w_up_a[l].astype(BF16), w_up_b[l].astype(BF16),
            w_out[l].astype(BF16), _padded_router_weight(w_router[l]), b_router[l])
        y4 = _moe_layer(l, h_packed, idx, pos, counts[:, 0], w_e_in, b_e_in, w_e_out, b_e_out)
        pending = (y4, wts.T, mod[l])
    return _final_norm(xcur, pending, final_g)
```

```python
import functools
import math

import jax
import jax.numpy as jnp
from jax import lax
from jax.experimental import pallas as pl
from jax.experimental.pallas import tpu as pltpu
from jax.experimental.pallas import tpu_sc as plsc

D_MODEL = 1024
BATCH = 8
N_HEADS = 8
HEAD_DIM = 128
CONV_WIDTH = 4
LRU_C = 8.0
POOL_WINDOWS = (2, 4, 8, 16)
POOL_WIDTH = 512
POOL_GROUP_DIM = 128
N_EXPERTS = 32
TOP_K = 4
D_FF = 1024
SWIGLU_LIMIT = 7.0
SWIGLU_ALPHA = 1.702
NORM_EPS = 1e-6

SUBLANES = 8
VMEM_LIMIT = 56 * 1024 * 1024
SC_CORES = 2
SC_SUBCORES = 16
SC_WORKERS = SC_CORES * SC_SUBCORES
SC_CHUNK = 64

MIX_STEPS = 32
ROUTER_COLS = 128
EXPERT_ROWS = 2048
EXPERT_SUB = 512
EXPERT_TAIL = 256
ADA_COLS = 1536
NORM_ROWS = 512
CAST_ROWS = 128
PACKED = D_MODEL // 2

F32 = jnp.float32
BF16 = jnp.bfloat16
I32 = jnp.int32
HIGH_MASK = -65536
LOG2_E = math.log2(math.e)


def _const_spec(shape):
    nd = len(shape)
    return pl.BlockSpec(shape, lambda *_: (0,) * nd, pipeline_mode=pl.Buffered(1))


def _pack_rows(v):
    packed = pltpu.pack_elementwise([v[:, :PACKED], v[:, PACKED:]], packed_dtype=BF16)
    return lax.bitcast_convert_type(packed, I32)


def _unpack_rows(p):
    lo = lax.bitcast_convert_type(lax.shift_left(p, 16), F32)
    hi = lax.bitcast_convert_type(p & HIGH_MASK, F32)
    return lo, hi


def _sigmoid(v):
    return 0.5 * jnp.tanh(0.5 * v) + 0.5


def _per_batch(v, slab, op):
    rows, d = v.shape
    return op(v.reshape(rows // BATCH, BATCH, d), slab[None]).reshape(rows, d)


def _modulated_norm(x, g, shift, scale):
    ms = jnp.mean(x * x, axis=-1, keepdims=True)
    hn = x * lax.rsqrt(ms + NORM_EPS)
    return _per_batch(_per_batch(hn, g * (1.0 + scale), jnp.multiply), shift, jnp.add)


def _ada_kernel(c_ref, w_ref, b_ref, o_ref):
    c = c_ref[...]
    c_act = c * jax.nn.sigmoid(c)
    o_ref[0] = jnp.dot(c_act, w_ref[0], preferred_element_type=F32,
                       precision=lax.Precision.HIGHEST) + b_ref[0]


def _ada_mod(c, w_ada, b_ada):
    depth, d, n = w_ada.shape
    bsz = c.shape[0]
    tn = ADA_COLS
    return pl.pallas_call(
        _ada_kernel,
        out_shape=jax.ShapeDtypeStruct((depth, bsz, n), F32),
        grid=(depth, n // tn),
        in_specs=[
            pl.BlockSpec((bsz, d), lambda l, j: (0, 0)),
            pl.BlockSpec((1, d, tn), lambda l, j: (l, 0, j)),
            pl.BlockSpec((1, 1, tn), lambda l, j: (l, 0, j)),
        ],
        out_specs=pl.BlockSpec((1, bsz, tn), lambda l, j: (l, 0, j)),
        compiler_params=pltpu.CompilerParams(
            dimension_semantics=("arbitrary", "arbitrary"),
            vmem_limit_bytes=VMEM_LIMIT),
        name="ada_mod",
    )(c, w_ada, b_ada.reshape(depth, 1, n))


def _delay(a, steps, carry_ref):
    n = steps * BATCH
    keep = carry_ref.shape[0]
    prev = carry_ref[keep - n:, :]
    return jnp.concatenate([prev, a[:a.shape[0] - n]], axis=0)


def _save_tail(a, carry_ref):
    carry_ref[...] = a[a.shape[0] - carry_ref.shape[0]:]


def _combined_rows(x_ref, y_refs, wt_ref, gate2, rs=slice(None)):
    x = x_ref[rs, :]
    wt = wt_ref[rs, :]
    acc_lo = jnp.zeros((x.shape[0], PACKED), F32)
    acc_hi = jnp.zeros((x.shape[0], PACKED), F32)
    for k, y_ref in enumerate(y_refs):
        lo, hi = _unpack_rows(y_ref[rs, :])
        wk = wt[:, k:k + 1]
        acc_lo = acc_lo + wk * lo
        acc_hi = acc_hi + wk * hi
    ffn = jnp.concatenate([acc_lo, acc_hi], axis=-1)
    return x + _per_batch(ffn, gate2, jnp.multiply)


def _route_rows(padded, brc_ref, run_c, live, cols, idx_ref, pos_ref, wt_ref):
    rows = padded.shape[0]
    logits = padded.T[:N_EXPERTS] + brc_ref[...]

    e_iota = lax.broadcasted_iota(I32, (N_EXPERTS, rows), 0)
    vals, idxs, hots = [], [], []
    cur = logits
    for _ in range(TOP_K):
        m = jnp.max(cur, axis=0, keepdims=True)
        am = jnp.min(jnp.where(cur == m, e_iota, N_EXPERTS), axis=0, keepdims=True)
        hot = e_iota == am
        vals.append(m)
        idxs.append(am)
        hots.append(hot)
        cur = jnp.where(hot, -jnp.inf, cur)
    exps = [jnp.exp(v - vals[0]) for v in vals]
    denom = exps[0] + exps[1] + exps[2] + exps[3]

    sel = (hots[0] | hots[1] | hots[2] | hots[3]).astype(F32)
    ri = lax.broadcasted_iota(I32, (rows, rows), 0)
    ci = lax.broadcasted_iota(I32, (rows, rows), 1)
    earlier = (ri < ci).astype(BF16)
    before = jnp.dot(sel.astype(BF16), earlier, preferred_element_type=F32) + run_c[...]
    run_c[...] = run_c[...] + jnp.where(live, jnp.sum(sel, axis=1, keepdims=True), 0.0)
    poss = [jnp.sum(jnp.where(hot, before, 0.0), axis=0, keepdims=True).astype(I32)
            for hot in hots]
    idx_ref[:, cols] = jnp.concatenate(idxs, axis=0)
    pos_ref[:, cols] = jnp.concatenate(poss, axis=0)
    wt_ref[:, cols] = jnp.concatenate([e / denom for e in exps], axis=0)


VEC_G1, VEC_CONV_B, VEC_BRG_A, VEC_BRG_X, VEC_LAM, VEC_G2, VEC_PSCALE = range(7)


def _mixer_kernel(with_combine, *refs):
    refs = list(refs)
    n_in = 1 + (TOP_K + 1 if with_combine else 0)
    first, nxt = refs[:n_in], refs[n_in:2 * n_in]
    refs = refs[2 * n_in:]
    if with_combine:
        modp_ref = refs.pop(0)
    (mod_ref, vec_ref, win_ref, convw_ref, wrg_ref, wpool_ref, wupa_ref, wupb_ref, wout_ref,
     wrt_ref, brc_ref,
     o_ref, hp_ref, idx_ref, pos_ref, wt_ref, cnt_ref,
     conv_c, p1_c, p2_c, p4_c, p8_c, h_c, run_c, logit_c, x_c, hb_c) = refs
    i = pl.program_id(0)
    last = pl.num_programs(0) - 1
    cur = i % 2
    rows = o_ref.shape[0]
    steps = rows // BATCH
    d = D_MODEL
    vec = vec_ref[...]
    row_of = lambda k: vec[k:k + 1]
    shift1, scale1, gate1 = mod_ref[0], mod_ref[1], mod_ref[2]

    def prologue(in_refs):
        if with_combine:
            x = _combined_rows(in_refs[0], in_refs[1:1 + TOP_K], in_refs[1 + TOP_K],
                               modp_ref[5])
        else:
            x = pltpu.einshape("btd->(tb)d", in_refs[0][...])
        return x, _modulated_norm(x, row_of(VEC_G1), shift1, scale1).astype(BF16)

    @pl.when(i == 0)
    def _():
        for ref in (conv_c, p1_c, p2_c, p4_c, p8_c, h_c, run_c, logit_c):
            ref[...] = jnp.zeros(ref.shape, ref.dtype)
        x_c[0], hb_c[0] = prologue(first)

    def route(padded, block, live):
        cols = pl.ds(pl.multiple_of(block * rows, rows), rows)
        _route_rows(padded, brc_ref, run_c, live, cols, idx_ref, pos_ref, wt_ref)

    def in_proj(lo, hi):
        return jnp.dot(hb_c[cur], win_ref[:, lo:hi], preferred_element_type=F32)

    x_lru = in_proj(0, d)
    cw = convw_ref[...]
    xc = (row_of(VEC_CONV_B) + cw[3:4] * x_lru
          + cw[2:3] * _delay(x_lru, 1, conv_c)
          + cw[1:2] * _delay(x_lru, 2, conv_c)
          + cw[0:1] * _delay(x_lru, 3, conv_c))
    _save_tail(x_lru, conv_c)
    xcb = xc.astype(BF16)
    pre = [jnp.dot(xcb[:, k * HEAD_DIM:(k + 1) * HEAD_DIM], wrg_ref[k],
                   preferred_element_type=F32) for k in range(N_HEADS)]
    pre_a = jnp.concatenate([p[:, :HEAD_DIM] for p in pre], axis=-1)
    pre_x = jnp.concatenate([p[:, HEAD_DIM:] for p in pre], axis=-1)
    r = _sigmoid(pre_a + row_of(VEC_BRG_A))
    gi = _sigmoid(pre_x + row_of(VEC_BRG_X))
    rate = (-LRU_C * LOG2_E) * jax.nn.softplus(-row_of(VEC_LAM))
    a = jnp.exp2(r * rate)
    y = 1.0 - a * a
    mult = jnp.where(y > 0.0, y * lax.rsqrt(y), 0.0)
    u = mult * (gi * xc)
    h = h_c[...]
    hs = []
    for t in range(steps):
        sl = slice(t * BATCH, (t + 1) * BATCH)
        h = a[sl] * h + u[sl]
        hs.append(h)
    h_c[...] = h
    y_lru = jnp.concatenate(hs, axis=0) * jax.nn.gelu(in_proj(d, 2 * d))
    p_a = jnp.dot(y_lru.astype(BF16), wupa_ref[...], preferred_element_type=F32)
    merged = _sigmoid(in_proj(2 * d + POOL_WIDTH, 3 * d + POOL_WIDTH)) * p_a

    route(logit_c[...], jnp.maximum(i - 1, 0), i > 0)
    x_c[1 - cur], hb_c[1 - cur] = prologue(nxt)

    x_pool = in_proj(2 * d, 2 * d + POOL_WIDTH)
    g = POOL_GROUP_DIM
    s2 = x_pool + _delay(x_pool, 1, p1_c)
    s4 = s2[:, g:] + _delay(s2[:, g:], 2, p2_c)
    s8 = s4[:, g:] + _delay(s4[:, g:], 4, p4_c)
    s16 = s8[:, g:] + _delay(s8[:, g:], 8, p8_c)
    _save_tail(x_pool, p1_c)
    _save_tail(s2[:, g:], p2_c)
    _save_tail(s4[:, g:], p4_c)
    _save_tail(s8[:, g:], p8_c)
    wins = (s2[:, :g], s4[:, :g], s8[:, :g], s16)
    row = lax.broadcasted_iota(I32, (rows, 1), 0)
    t1 = (i * steps + row // BATCH + 1).astype(F32)
    pooled = []
    for k, w in enumerate(POOL_WINDOWS):
        cnt = jnp.minimum(t1, float(w))
        pk = wins[k] / cnt - x_pool[:, k * g:(k + 1) * g]
        pooled.append(jnp.dot(pk.astype(BF16), wpool_ref[k], preferred_element_type=F32))
    y_pool = jnp.concatenate(pooled, axis=-1) * row_of(VEC_PSCALE)[:, :POOL_WIDTH]
    p_b = jnp.dot(y_pool.astype(BF16), wupb_ref[...], preferred_element_type=F32)
    merged = merged + _sigmoid(in_proj(3 * d + POOL_WIDTH, 4 * d + POOL_WIDTH)) * p_b

    mix = jnp.dot(merged.astype(BF16), wout_ref[...], preferred_element_type=F32)
    x_mid = x_c[cur] + _per_batch(mix, gate1, jnp.multiply)
    o_ref[...] = x_mid

    hr = _modulated_norm(x_mid, row_of(VEC_G2), mod_ref[3], mod_ref[4])
    hp_ref[...] = _pack_rows(hr)
    padded = jnp.dot(hr.astype(BF16), wrt_ref[...], preferred_element_type=F32)
    logit_c[...] = padded

    @pl.when(i == last)
    def _():
        route(logit_c[...], i, True)

    cnt_ref[...] = run_c[...].astype(I32)


def _pending_specs(rows, n_tok, block_of):
    nb = n_tok // rows
    y_spec = lambda k: pl.BlockSpec((rows, PACKED), lambda i: (k * nb + block_of(i), 0))
    return ([y_spec(k) for k in range(TOP_K)]
            + [pl.BlockSpec((rows, TOP_K), lambda i: (block_of(i), 0))])


def _mixer_layer(x_in, pending, mod_l, vec, w_in, conv_w, w_rg, w_pool, w_up_a, w_up_b, w_out,
                 w_route, b_router):
    rows = MIX_STEPS * BATCH
    weights = [mod_l, vec, w_in, conv_w, w_rg, w_pool, w_up_a, w_up_b, w_out, w_route,
               b_router.reshape(N_EXPERTS, 1)]
    if pending is None:
        n_tok, d = x_in.shape[0] * x_in.shape[1], x_in.shape[2]
        inputs = [x_in]
        specs = lambda block_of: [
            pl.BlockSpec((BATCH, MIX_STEPS, d), lambda i: (0, block_of(i), 0))]
    else:
        n_tok, d = x_in.shape
        y4, wts_prev, mod_prev = pending
        inputs = [x_in, y4, y4, y4, y4, wts_prev]
        weights.insert(0, mod_prev)
        specs = lambda block_of: ([pl.BlockSpec((rows, d), lambda i: (block_of(i), 0))]
                                  + _pending_specs(rows, n_tok, block_of))
    nb = n_tok // rows
    in_specs = (specs(lambda i: 0) + specs(lambda i: jnp.minimum(i + 1, nb - 1))
                + [_const_spec(a.shape) for a in weights])
    g = POOL_GROUP_DIM
    scratch = [
        pltpu.VMEM(((CONV_WIDTH - 1) * BATCH, d), F32),
        pltpu.VMEM((1 * BATCH, 4 * g), F32), pltpu.VMEM((2 * BATCH, 3 * g), F32),
        pltpu.VMEM((4 * BATCH, 2 * g), F32), pltpu.VMEM((8 * BATCH, g), F32),
        pltpu.VMEM((BATCH, d), F32),
        pltpu.VMEM((N_EXPERTS, 1), F32),
        pltpu.VMEM((rows, ROUTER_COLS), F32),
        pltpu.VMEM((2, rows, d), F32), pltpu.VMEM((2, rows, d), BF16),
    ]
    routed = lambda dtype: jax.ShapeDtypeStruct((TOP_K, n_tok), dtype)
    whole = lambda shape: pl.BlockSpec(shape, lambda i: (0,) * len(shape))
    routed_spec = whole((TOP_K, n_tok))
    return pl.pallas_call(
        functools.partial(_mixer_kernel, pending is not None),
        out_shape=(
            jax.ShapeDtypeStruct((n_tok, d), F32),
            jax.ShapeDtypeStruct((n_tok, PACKED), I32),
            routed(I32), routed(I32), routed(F32),
            jax.ShapeDtypeStruct((N_EXPERTS, 1), I32),
        ),
        grid=(nb,),
        in_specs=in_specs,
        out_specs=(
            pl.BlockSpec((rows, d), lambda i: (i, 0)),
            pl.BlockSpec((rows, PACKED), lambda i: (i, 0)),
            routed_spec, routed_spec, routed_spec,
            whole((N_EXPERTS, 1)),
        ),
        scratch_shapes=scratch,
        compiler_params=pltpu.CompilerParams(
            dimension_semantics=("arbitrary",), vmem_limit_bytes=VMEM_LIMIT),
        name="mixer",
    )(*(inputs + inputs + weights))


def _padded_router_weight(w_router):
    pad = jnp.zeros((w_router.shape[0], ROUTER_COLS - N_EXPERTS), BF16)
    return jnp.concatenate([w_router.astype(BF16), pad], axis=1)


def _sc_mesh():
    return plsc.VectorSubcoreMesh(core_axis_name="c", subcore_axis_name="s")


def _sc_worker_id():
    return lax.axis_index("s") * SC_CORES + lax.axis_index("c")


def _sc_dispatch(h_packed, dest_sc, n_slots):
    n_tok, width = h_packed.shape
    per_w = n_tok // SC_WORKERS
    n_chunks = per_w // SC_CHUNK

    def body(h_hbm, dest_hbm, xs_hbm, idx_v, rows_v, sem_in, sem_out):
        base = _sc_worker_id() * per_w
        pltpu.sync_copy(dest_hbm.at[_sc_worker_id()], idx_v)

        def load(j, buf):
            return pltpu.make_async_copy(
                h_hbm.at[pl.ds(base + j * SC_CHUNK, SC_CHUNK)], rows_v.at[buf], sem_in)

        load(0, 0).start()

        @pl.loop(0, n_chunks, step=2)
        def _(j0):
            for buf in range(2):
                j = j0 + buf
                load(j, buf).wait()

                @pl.when(j + 1 < n_chunks)
                def _():
                    load(j + 1, 1 - buf).start()

                copies = [
                    pltpu.make_async_copy(
                        rows_v.at[buf], xs_hbm.at[idx_v.at[j * TOP_K + k]], sem_out)
                    for k in range(TOP_K)]
                for cp in copies:
                    cp.start()
                for cp in copies:
                    cp.wait()

    return pl.kernel(
        body,
        out_type=jax.ShapeDtypeStruct((n_slots, width), I32),
        mesh=_sc_mesh(),
        scratch_types=[
            pltpu.VMEM((n_chunks * TOP_K, SC_CHUNK), I32),
            pltpu.VMEM((2, SC_CHUNK, width), I32),
            pltpu.SemaphoreType.DMA,
            pltpu.SemaphoreType.DMA,
        ],
        name="sc_dispatch",
    )(h_packed, dest_sc)


def _sc_gather(ys_packed, dest_sc):
    width = ys_packed.shape[1]
    n_chunks = dest_sc.shape[1]
    per_w = n_chunks * SC_CHUNK
    n_rows = SC_WORKERS * per_w

    def body(ys_hbm, dest_hbm, out_hbm, idx_v, rows_v, sem_g, sem_w):
        base = _sc_worker_id() * per_w
        pltpu.sync_copy(dest_hbm.at[_sc_worker_id()], idx_v)

        def gather(j, buf):
            return pltpu.make_async_copy(ys_hbm.at[idx_v.at[j]], rows_v.at[buf], sem_g)

        def writeout(j, buf):
            return pltpu.make_async_copy(
                rows_v.at[buf], out_hbm.at[pl.ds(base + j * SC_CHUNK, SC_CHUNK)], sem_w)

        gather(0, 0).start()

        @pl.loop(0, n_chunks, step=2)
        def _(j0):
            for buf in range(2):
                j = j0 + buf
                gather(j, buf).wait()

                @pl.when(j >= 1)
                def _():
                    writeout(j - 1, 1 - buf).wait()

                @pl.when(j + 1 < n_chunks)
                def _():
                    gather(j + 1, 1 - buf).start()

                writeout(j, buf).start()

        writeout(n_chunks - 1, 1).wait()

    return pl.kernel(
        body,
        out_type=jax.ShapeDtypeStruct((n_rows, width), I32),
        mesh=_sc_mesh(),
        scratch_types=[
            pltpu.VMEM((n_chunks, SC_CHUNK), I32),
            pltpu.VMEM((2, SC_CHUNK, width), I32),
            pltpu.SemaphoreType.DMA,
            pltpu.SemaphoreType.DMA,
        ],
        name="sc_gather",
    )(ys_packed, dest_sc)


def _expert_kernel(layer, be_ref, nv_ref, first_ref, next_ref, xs_ref, win_hbm, bin_ref,
                   wout_hbm, bout_ref, ys_ref, win_f32, wout_f32, win_bf, wout_bf, sem):
    i = pl.program_id(0)
    valid = nv_ref[i]

    def fetch(e):
        return (pltpu.make_async_copy(win_hbm.at[layer, e], win_f32, sem.at[0]),
                pltpu.make_async_copy(wout_hbm.at[layer, e], wout_f32, sem.at[1]))

    @pl.when(first_ref[i] == 1)
    def _():
        @pl.when(i == 0)
        def _():
            for cp in fetch(be_ref[i]):
                cp.start()

        for cp in fetch(be_ref[i]):
            cp.wait()

        def cast_in(r, c):
            rows = pl.ds(pl.multiple_of(r * CAST_ROWS, CAST_ROWS), CAST_ROWS)
            win_bf[rows, :] = win_f32[rows, :].astype(BF16)
            return c

        def cast_out(r, c):
            rows = pl.ds(pl.multiple_of(r * CAST_ROWS, CAST_ROWS), CAST_ROWS)
            wout_bf[rows, :] = wout_f32[rows, :].astype(BF16)
            return c

        lax.fori_loop(0, D_MODEL // CAST_ROWS, cast_in, 0)
        lax.fori_loop(0, D_FF // CAST_ROWS, cast_out, 0)

        @pl.when(next_ref[i] >= 0)
        def _():
            for cp in fetch(next_ref[i]):
                cp.start()

    def compute(start, size):
        rows = pl.ds(pl.multiple_of(start, EXPERT_TAIL), size)
        row = lax.broadcasted_iota(I32, (size, 1), 0)
        packed = jnp.where(row + start < valid, xs_ref[rows, :], 0)
        lo, hi = _unpack_rows(packed)
        gu = (jnp.dot(lo.astype(BF16), win_bf[:PACKED, :], preferred_element_type=F32)
              + jnp.dot(hi.astype(BF16), win_bf[PACKED:, :], preferred_element_type=F32)
              + bin_ref[0])
        gate = jnp.minimum(gu[:, :D_FF], SWIGLU_LIMIT)
        up = jnp.clip(gu[:, D_FF:], -SWIGLU_LIMIT, SWIGLU_LIMIT)
        act = gate * _sigmoid(SWIGLU_ALPHA * gate) * (up + 1.0)
        out = jnp.dot(act.astype(BF16), wout_bf[...], preferred_element_type=F32) + bout_ref[0]
        ys_ref[rows, :] = _pack_rows(out)

    n_main = (valid + (EXPERT_SUB - EXPERT_TAIL) - 1) // EXPERT_SUB
    main_rows = n_main * EXPERT_SUB

    def main_pass(j, c):
        compute(j * EXPERT_SUB, EXPERT_SUB)
        return c

    lax.fori_loop(0, n_main, main_pass, 0)

    @pl.when(valid > main_rows)
    def _():
        compute(main_rows, EXPERT_TAIL)

    def clear(j, c):
        rows = pl.ds(pl.multiple_of(j * EXPERT_TAIL, EXPERT_TAIL), EXPERT_TAIL)
        ys_ref[rows, :] = jnp.zeros((EXPERT_TAIL, ys_ref.shape[1]), ys_ref.dtype)
        return c

    done = (main_rows + jnp.where(valid > main_rows, EXPERT_TAIL, 0)) // EXPERT_TAIL
    lax.fori_loop(done, EXPERT_ROWS // EXPERT_TAIL, clear, 0)


def _experts(layer, block_e, n_valid, is_first, next_e, xs, w_e_in, b_e_in, w_e_out, b_e_out):
    n_slots, width = xs.shape
    d = D_MODEL
    tm = EXPERT_ROWS
    bias_index = lambda i, be, nv, fi, nx: (layer, be[i], 0, 0)
    grid_spec = pltpu.PrefetchScalarGridSpec(
        num_scalar_prefetch=4,
        grid=(n_slots // tm,),
        in_specs=[
            pl.BlockSpec((tm, width), lambda i, be, nv, fi, nx: (i, 0)),
            pl.BlockSpec(memory_space=pl.ANY),
            pl.BlockSpec((None, 1, 1, 2 * D_FF), bias_index),
            pl.BlockSpec(memory_space=pl.ANY),
            pl.BlockSpec((None, 1, 1, d), bias_index),
        ],
        out_specs=pl.BlockSpec((tm, width), lambda i, be, nv, fi, nx: (i, 0)),
        scratch_shapes=[
            pltpu.VMEM((d, 2 * D_FF), F32), pltpu.VMEM((D_FF, d), F32),
            pltpu.VMEM((d, 2 * D_FF), BF16), pltpu.VMEM((D_FF, d), BF16),
            pltpu.SemaphoreType.DMA((2,)),
        ],
    )
    return pl.pallas_call(
        functools.partial(_expert_kernel, layer),
        out_shape=jax.ShapeDtypeStruct((n_slots, width), I32),
        grid_spec=grid_spec,
        compiler_params=pltpu.CompilerParams(
            dimension_semantics=("arbitrary",), vmem_limit_bytes=VMEM_LIMIT),
        name="experts",
    )(block_e, n_valid, is_first, next_e, xs, w_e_in,
      b_e_in.reshape(-1, N_EXPERTS, 1, 2 * D_FF), w_e_out, b_e_out.reshape(-1, N_EXPERTS, 1, d))


def _final_kernel(x_ref, y0_ref, y1_ref, y2_ref, y3_ref, wt_ref, mod_ref, g_ref, o_ref):
    x = _combined_rows(x_ref, (y0_ref, y1_ref, y2_ref, y3_ref), wt_ref, mod_ref[5])
    ms = jnp.mean(x * x, axis=-1, keepdims=True)
    out = x * lax.rsqrt(ms + NORM_EPS) * g_ref[...]
    o_ref[...] = pltpu.einshape("(tb)d->btd", out, b=BATCH)


def _final_norm(x2d, pending, g):
    n_tok, d = x2d.shape
    tr = NORM_ROWS
    y4, wts, mod_l = pending
    return pl.pallas_call(
        _final_kernel,
        out_shape=jax.ShapeDtypeStruct((BATCH, n_tok // BATCH, d), F32),
        grid=(n_tok // tr,),
        in_specs=([pl.BlockSpec((tr, d), lambda i: (i, 0))]
                  + _pending_specs(tr, n_tok, lambda i: i)
                  + [_const_spec(mod_l.shape), _const_spec((1, d))]),
        out_specs=pl.BlockSpec((BATCH, tr // BATCH, d), lambda i: (0, i, 0)),
        compiler_params=pltpu.CompilerParams(
            dimension_semantics=("arbitrary",), vmem_limit_bytes=VMEM_LIMIT),
        name="final_norm",
    )(x2d, y4, y4, y4, y4, wts, mod_l, g.reshape(1, d))


def _moe_layer(layer, h_packed, idx, pos, counts, w_e_in, b_e_in, w_e_out, b_e_out):
    n_tok = h_packed.shape[0]
    tm = EXPERT_ROWS
    padded = ((counts + tm - 1) // tm) * tm
    pad_end = jnp.cumsum(padded)
    start_pad = pad_end - padded
    experts = jnp.arange(N_EXPERTS, dtype=I32)
    dest = jnp.sum(jnp.where(idx[:, :, None] == experts, start_pad, 0), axis=-1) + pos
    n_blocks = (n_tok * TOP_K) // tm + N_EXPERTS
    n_slots = n_blocks * tm
    block_start = jnp.arange(n_blocks, dtype=I32) * tm
    block_e = jnp.minimum(
        jnp.sum((block_start[:, None] >= pad_end[None, :]).astype(I32), axis=-1),
        N_EXPERTS - 1)
    hot_e = block_e[:, None] == experts[None, :]
    rows_left = jnp.sum(jnp.where(hot_e, (start_pad + counts)[None, :], 0), axis=-1) - block_start
    n_valid = jnp.where(block_start < pad_end[-1], jnp.clip(rows_left, 0, tm), 0).astype(I32)
    is_first = jnp.sum(jnp.where(hot_e, start_pad[None, :], 0), axis=-1) == block_start
    is_first = (is_first & (n_valid > 0)).astype(I32)
    later = (experts[None, :] > experts[:, None]) & (counts[None, :] > 0)
    next_of = jnp.min(jnp.where(later, experts[None, :], N_EXPERTS), axis=-1)
    next_of = jnp.where(next_of == N_EXPERTS, -1, next_of)
    next_e = jnp.sum(jnp.where(hot_e, next_of[None, :], 0), axis=-1).astype(I32)

    per_w = n_tok // SC_WORKERS
    n_chunks = per_w // SC_CHUNK
    dest_scatter = (dest.reshape(TOP_K, SC_WORKERS, n_chunks, SC_CHUNK)
                    .transpose(1, 2, 0, 3)
                    .reshape(SC_WORKERS, n_chunks * TOP_K, SC_CHUNK))
    xs = _sc_dispatch(h_packed, dest_scatter, n_slots)
    ys = _experts(layer, block_e, n_valid, is_first, next_e, xs,
                  w_e_in, b_e_in, w_e_out, b_e_out)
    dest_gather = dest.reshape(SC_WORKERS, n_chunks * TOP_K, SC_CHUNK)
    return _sc_gather(ys, dest_gather)


def kernel(x, c, norm1_g, norm2_g, w_ada, b_ada, w_in, conv_w, conv_b, w_rg_a, b_rg_a,
           w_rg_x, b_rg_x, lru_lambda, w_pool, pool_scale, w_up_a, w_up_b, w_out,
           w_router, b_router, w_e_in, b_e_in, w_e_out, b_e_out, final_g):
    depth = w_in.shape[0]
    bsz, seq, d = x.shape
    assert bsz == BATCH and d == D_MODEL
    mod = _ada_mod(c, w_ada, b_ada).reshape(depth, bsz, 6, d).transpose(0, 2, 1, 3)
    w_rg = jnp.concatenate([w_rg_a, w_rg_x], axis=-1).astype(BF16)
    pending = None
    xcur = x
    for l in range(depth):
        pad = jnp.zeros((d - pool_scale.shape[1],), F32)
        vec = jnp.stack([norm1_g[l], conv_b[l], b_rg_a[l], b_rg_x[l], lru_lambda[l],
                         norm2_g[l], jnp.concatenate([pool_scale[l], pad]),
                         jnp.zeros((d,), F32)])
        xcur, h_packed, idx, pos, wts, counts = _mixer_layer(
            xcur, pending, mod[l], vec, w_in[l].astype(BF16), conv_w[l], w_rg[l],
            w_pool[l].astype(BF16), w_up_a[l].astype(BF16), w_up_b[l].astype(BF16),
            w_out[l].astype(BF16), _padded_router_weight(w_router[l]), b_router[l])
        y4 = _moe_layer(l, h_packed, idx, pos, counts[:, 0], w_e_in, b_e_in, w_e_out, b_e_out)
        pending = (y4, wts.T, mod[l])
    return _final_norm(xcur, pending, final_g)
```

```python
import functools
import math

import jax
import jax.numpy as jnp
from jax import lax
from jax.experimental import pallas as pl
from jax.experimental.pallas import tpu as pltpu
from jax.experimental.pallas import tpu_sc as plsc

D_MODEL = 1024
BATCH = 8
N_HEADS = 8
HEAD_DIM = 128
CONV_WIDTH = 4
LRU_C = 8.0
POOL_WINDOWS = (2, 4, 8, 16)
POOL_WIDTH = 512
POOL_GROUP_DIM = 128
N_EXPERTS = 32
TOP_K = 4
D_FF = 1024
SWIGLU_LIMIT = 7.0
SWIGLU_ALPHA = 1.702
NORM_EPS = 1e-6

SUBLANES = 8
VMEM_LIMIT = 56 * 1024 * 1024
SC_CORES = 2
SC_SUBCORES = 16
SC_WORKERS = SC_CORES * SC_SUBCORES
SC_CHUNK = 64

MIX_STEPS = 32
ROUTER_COLS = 128
EXPERT_ROWS = 1024
EXPERT_SUB = 512
EXPERT_TAIL = 256
ADA_COLS = 1536
NORM_ROWS = 512
CAST_ROWS = 128
PACKED = D_MODEL // 2

F32 = jnp.float32
BF16 = jnp.bfloat16
I32 = jnp.int32
HIGH_MASK = -65536
LOG2_E = math.log2(math.e)


def _const_spec(shape):
    nd = len(shape)
    return pl.BlockSpec(shape, lambda *_: (0,) * nd, pipeline_mode=pl.Buffered(1))


def _pack_rows(v):
    packed = pltpu.pack_elementwise([v[:, :PACKED], v[:, PACKED:]], packed_dtype=BF16)
    return lax.bitcast_convert_type(packed, I32)


def _unpack_rows(p):
    lo = lax.bitcast_convert_type(lax.shift_left(p, 16), F32)
    hi = lax.bitcast_convert_type(p & HIGH_MASK, F32)
    return lo, hi


def _sigmoid(v):
    return 0.5 * jnp.tanh(0.5 * v) + 0.5


def _per_batch(v, slab, op):
    rows, d = v.shape
    return op(v.reshape(rows // BATCH, BATCH, d), slab[None]).reshape(rows, d)


def _modulated_norm(x, g, shift, scale):
    ms = jnp.mean(x * x, axis=-1, keepdims=True)
    hn = x * lax.rsqrt(ms + NORM_EPS)
    return _per_batch(_per_batch(hn, g * (1.0 + scale), jnp.multiply), shift, jnp.add)


def _ada_kernel(c_ref, w_ref, b_ref, o_ref):
    c = c_ref[...]
    c_act = c * jax.nn.sigmoid(c)
    o_ref[0] = jnp.dot(c_act, w_ref[0], preferred_element_type=F32,
                       precision=lax.Precision.HIGHEST) + b_ref[0]


def _ada_mod(c, w_ada, b_ada):
    depth, d, n = w_ada.shape
    bsz = c.shape[0]
    tn = ADA_COLS
    return pl.pallas_call(
        _ada_kernel,
        out_shape=jax.ShapeDtypeStruct((depth, bsz, n), F32),
        grid=(depth, n // tn),
        in_specs=[
            pl.BlockSpec((bsz, d), lambda l, j: (0, 0)),
            pl.BlockSpec((1, d, tn), lambda l, j: (l, 0, j)),
            pl.BlockSpec((1, 1, tn), lambda l, j: (l, 0, j)),
        ],
        out_specs=pl.BlockSpec((1, bsz, tn), lambda l, j: (l, 0, j)),
        compiler_params=pltpu.CompilerParams(
            dimension_semantics=("arbitrary", "arbitrary"),
            vmem_limit_bytes=VMEM_LIMIT),
        name="ada_mod",
    )(c, w_ada, b_ada.reshape(depth, 1, n))


def _delay(a, steps, carry_ref):
    n = steps * BATCH
    keep = carry_ref.shape[0]
    prev = carry_ref[keep - n:, :]
    return jnp.concatenate([prev, a[:a.shape[0] - n]], axis=0)


def _save_tail(a, carry_ref):
    carry_ref[...] = a[a.shape[0] - carry_ref.shape[0]:]


def _combined_rows(x_ref, y_refs, wt_ref, gate2, rs=slice(None)):
    x = x_ref[rs, :]
    wt = wt_ref[rs, :]
    acc_lo = jnp.zeros((x.shape[0], PACKED), F32)
    acc_hi = jnp.zeros((x.shape[0], PACKED), F32)
    for k, y_ref in enumerate(y_refs):
        lo, hi = _unpack_rows(y_ref[rs, :])
        wk = wt[:, k:k + 1]
        acc_lo = acc_lo + wk * lo
        acc_hi = acc_hi + wk * hi
    ffn = jnp.concatenate([acc_lo, acc_hi], axis=-1)
    return x + _per_batch(ffn, gate2, jnp.multiply)


def _route_rows(padded, brc_ref, run_c, live, cols, idx_ref, pos_ref, wt_ref):
    rows = padded.shape[0]
    logits = padded.T[:N_EXPERTS] + brc_ref[...]

    e_iota = lax.broadcasted_iota(I32, (N_EXPERTS, rows), 0)
    vals, idxs, hots = [], [], []
    cur = logits
    for _ in range(TOP_K):
        m = jnp.max(cur, axis=0, keepdims=True)
        am = jnp.min(jnp.where(cur == m, e_iota, N_EXPERTS), axis=0, keepdims=True)
        hot = e_iota == am
        vals.append(m)
        idxs.append(am)
        hots.append(hot)
        cur = jnp.where(hot, -jnp.inf, cur)
    exps = [jnp.exp(v - vals[0]) for v in vals]
    denom = exps[0] + exps[1] + exps[2] + exps[3]

    sel = (hots[0] | hots[1] | hots[2] | hots[3]).astype(F32)
    ri = lax.broadcasted_iota(I32, (rows, rows), 0)
    ci = lax.broadcasted_iota(I32, (rows, rows), 1)
    earlier = (ri < ci).astype(BF16)
    before = jnp.dot(sel.astype(BF16), earlier, preferred_element_type=F32) + run_c[...]
    run_c[...] = run_c[...] + jnp.where(live, jnp.sum(sel, axis=1, keepdims=True), 0.0)
    poss = [jnp.sum(jnp.where(hot, before, 0.0), axis=0, keepdims=True).astype(I32)
            for hot in hots]
    idx_ref[:, cols] = jnp.concatenate(idxs, axis=0)
    pos_ref[:, cols] = jnp.concatenate(poss, axis=0)
    wt_ref[:, cols] = jnp.concatenate([e / denom for e in exps], axis=0)


VEC_G1, VEC_CONV_B, VEC_BRG_A, VEC_BRG_X, VEC_LAM, VEC_G2, VEC_PSCALE = range(7)


def _mixer_kernel(with_combine, *refs):
    refs = list(refs)
    n_in = 1 + (TOP_K + 1 if with_combine else 0)
    first, nxt = refs[:n_in], refs[n_in:2 * n_in]
    refs = refs[2 * n_in:]
    if with_combine:
        modp_ref = refs.pop(0)
    (mod_ref, vec_ref, win_ref, convw_ref, wrg_ref, wpool_ref, wupa_ref, wupb_ref, wout_ref,
     wrt_ref, brc_ref,
     o_ref, hp_ref, idx_ref, pos_ref, wt_ref, cnt_ref,
     conv_c, p1_c, p2_c, p4_c, p8_c, h_c, run_c, logit_c, x_c, hb_c, xl_c) = refs
    i = pl.program_id(0)
    last = pl.num_programs(0) - 1
    cur = i % 2
    rows = o_ref.shape[0]
    steps = rows // BATCH
    d = D_MODEL
    vec = vec_ref[...]
    row_of = lambda k: vec[k:k + 1]
    shift1, scale1, gate1 = mod_ref[0], mod_ref[1], mod_ref[2]

    def prologue(in_refs):
        if with_combine:
            x = _combined_rows(in_refs[0], in_refs[1:1 + TOP_K], in_refs[1 + TOP_K],
                               modp_ref[5])
        else:
            x = pltpu.einshape("btd->(tb)d", in_refs[0][...])
        return x, _modulated_norm(x, row_of(VEC_G1), shift1, scale1).astype(BF16)

    @pl.when(i == 0)
    def _():
        for ref in (conv_c, p1_c, p2_c, p4_c, p8_c, h_c, run_c, logit_c):
            ref[...] = jnp.zeros(ref.shape, ref.dtype)
        x_c[0], hb_c[0] = prologue(first)
        xl_c[...] = jnp.dot(hb_c[0], win_ref[:, 0:d], preferred_element_type=F32)

    def route(padded, block, live):
        cols = pl.ds(pl.multiple_of(block * rows, rows), rows)
        _route_rows(padded, brc_ref, run_c, live, cols, idx_ref, pos_ref, wt_ref)

    def in_proj(lo, hi):
        return jnp.dot(hb_c[cur], win_ref[:, lo:hi], preferred_element_type=F32)

    x_lru = xl_c[...]
    cw = convw_ref[...]
    xc = (row_of(VEC_CONV_B) + cw[3:4] * x_lru
          + cw[2:3] * _delay(x_lru, 1, conv_c)
          + cw[1:2] * _delay(x_lru, 2, conv_c)
          + cw[0:1] * _delay(x_lru, 3, conv_c))
    _save_tail(x_lru, conv_c)
    xcb = xc.astype(BF16)
    pre = [jnp.dot(xcb[:, k * HEAD_DIM:(k + 1) * HEAD_DIM], wrg_ref[k],
                   preferred_element_type=F32) for k in range(N_HEADS)]
    pre_a = jnp.concatenate([p[:, :HEAD_DIM] for p in pre], axis=-1)
    pre_x = jnp.concatenate([p[:, HEAD_DIM:] for p in pre], axis=-1)
    r = _sigmoid(pre_a + row_of(VEC_BRG_A))
    gi = _sigmoid(pre_x + row_of(VEC_BRG_X))
    rate = (-LRU_C * LOG2_E) * jax.nn.softplus(-row_of(VEC_LAM))
    a = jnp.exp2(r * rate)
    y = 1.0 - a * a
    mult = jnp.where(y > 0.0, y * lax.rsqrt(y), 0.0)
    u = mult * (gi * xc)
    h = h_c[...]
    hs = []
    for t in range(steps):
        sl = slice(t * BATCH, (t + 1) * BATCH)
        h = a[sl] * h + u[sl]
        hs.append(h)
    h_c[...] = h
    y_lru = jnp.concatenate(hs, axis=0) * jax.nn.gelu(in_proj(d, 2 * d))
    p_a = jnp.dot(y_lru.astype(BF16), wupa_ref[...], preferred_element_type=F32)
    merged = _sigmoid(in_proj(2 * d + POOL_WIDTH, 3 * d + POOL_WIDTH)) * p_a

    route(logit_c[...], jnp.maximum(i - 1, 0), i > 0)
    x_c[1 - cur], hb_c[1 - cur] = prologue(nxt)

    x_pool = in_proj(2 * d, 2 * d + POOL_WIDTH)
    g = POOL_GROUP_DIM
    s2 = x_pool + _delay(x_pool, 1, p1_c)
    s4 = s2[:, g:] + _delay(s2[:, g:], 2, p2_c)
    s8 = s4[:, g:] + _delay(s4[:, g:], 4, p4_c)
    s16 = s8[:, g:] + _delay(s8[:, g:], 8, p8_c)
    _save_tail(x_pool, p1_c)
    _save_tail(s2[:, g:], p2_c)
    _save_tail(s4[:, g:], p4_c)
    _save_tail(s8[:, g:], p8_c)
    wins = (s2[:, :g], s4[:, :g], s8[:, :g], s16)
    row = lax.broadcasted_iota(I32, (rows, 1), 0)
    t1 = (i * steps + row // BATCH + 1).astype(F32)
    pooled = []
    for k, w in enumerate(POOL_WINDOWS):
        cnt = jnp.minimum(t1, float(w))
        pk = wins[k] / cnt - x_pool[:, k * g:(k + 1) * g]
        pooled.append(jnp.dot(pk.astype(BF16), wpool_ref[k], preferred_element_type=F32))
    y_pool = jnp.concatenate(pooled, axis=-1) * row_of(VEC_PSCALE)[:, :POOL_WIDTH]
    p_b = jnp.dot(y_pool.astype(BF16), wupb_ref[...], preferred_element_type=F32)
    merged = merged + _sigmoid(in_proj(3 * d + POOL_WIDTH, 4 * d + POOL_WIDTH)) * p_b

    mix = jnp.dot(merged.astype(BF16), wout_ref[...], preferred_element_type=F32)
    x_mid = x_c[cur] + _per_batch(mix, gate1, jnp.multiply)
    o_ref[...] = x_mid
    xl_c[...] = jnp.dot(hb_c[1 - cur], win_ref[:, 0:d], preferred_element_type=F32)

    hr = _modulated_norm(x_mid, row_of(VEC_G2), mod_ref[3], mod_ref[4])
    hp_ref[...] = _pack_rows(hr)
    padded = jnp.dot(hr.astype(BF16), wrt_ref[...], preferred_element_type=F32)
    logit_c[...] = padded

    @pl.when(i == last)
    def _():
        route(logit_c[...], i, True)

    cnt_ref[...] = run_c[...].astype(I32)


def _pending_specs(rows, n_tok, block_of):
    nb = n_tok // rows
    y_spec = lambda k: pl.BlockSpec((rows, PACKED), lambda i: (k * nb + block_of(i), 0))
    return ([y_spec(k) for k in range(TOP_K)]
            + [pl.BlockSpec((rows, TOP_K), lambda i: (block_of(i), 0))])


def _mixer_layer(x_in, pending, mod_l, vec, w_in, conv_w, w_rg, w_pool, w_up_a, w_up_b, w_out,
                 w_route, b_router):
    rows = MIX_STEPS * BATCH
    weights = [mod_l, vec, w_in, conv_w, w_rg, w_pool, w_up_a, w_up_b, w_out, w_route,
               b_router.reshape(N_EXPERTS, 1)]
    if pending is None:
        n_tok, d = x_in.shape[0] * x_in.shape[1], x_in.shape[2]
        inputs = [x_in]
        specs = lambda block_of: [
            pl.BlockSpec((BATCH, MIX_STEPS, d), lambda i: (0, block_of(i), 0))]
    else:
        n_tok, d = x_in.shape
        y4, wts_prev, mod_prev = pending
        inputs = [x_in, y4, y4, y4, y4, wts_prev]
        weights.insert(0, mod_prev)
        specs = lambda block_of: ([pl.BlockSpec((rows, d), lambda i: (block_of(i), 0))]
                                  + _pending_specs(rows, n_tok, block_of))
    nb = n_tok // rows
    in_specs = (specs(lambda i: 0) + specs(lambda i: jnp.minimum(i + 1, nb - 1))
                + [_const_spec(a.shape) for a in weights])
    g = POOL_GROUP_DIM
    scratch = [
        pltpu.VMEM(((CONV_WIDTH - 1) * BATCH, d), F32),
        pltpu.VMEM((1 * BATCH, 4 * g), F32), pltpu.VMEM((2 * BATCH, 3 * g), F32),
        pltpu.VMEM((4 * BATCH, 2 * g), F32), pltpu.VMEM((8 * BATCH, g), F32),
        pltpu.VMEM((BATCH, d), F32),
        pltpu.VMEM((N_EXPERTS, 1), F32),
        pltpu.VMEM((rows, ROUTER_COLS), F32),
        pltpu.VMEM((2, rows, d), F32), pltpu.VMEM((2, rows, d), BF16),
        pltpu.VMEM((rows, d), F32),
    ]
    routed = lambda dtype: jax.ShapeDtypeStruct((TOP_K, n_tok), dtype)
    whole = lambda shape: pl.BlockSpec(shape, lambda i: (0,) * len(shape))
    routed_spec = whole((TOP_K, n_tok))
    return pl.pallas_call(
        functools.partial(_mixer_kernel, pending is not None),
        out_shape=(
            jax.ShapeDtypeStruct((n_tok, d), F32),
            jax.ShapeDtypeStruct((n_tok, PACKED), I32),
            routed(I32), routed(I32), routed(F32),
            jax.ShapeDtypeStruct((N_EXPERTS, 1), I32),
        ),
        grid=(nb,),
        in_specs=in_specs,
        out_specs=(
            pl.BlockSpec((rows, d), lambda i: (i, 0)),
            pl.BlockSpec((rows, PACKED), lambda i: (i, 0)),
            routed_spec, routed_spec, routed_spec,
            whole((N_EXPERTS, 1)),
        ),
        scratch_shapes=scratch,
        compiler_params=pltpu.CompilerParams(
            dimension_semantics=("arbitrary",), vmem_limit_bytes=VMEM_LIMIT),
        name="mixer",
    )(*(inputs + inputs + weights))


def _padded_router_weight(w_router):
    pad = jnp.zeros((w_router.shape[0], ROUTER_COLS - N_EXPERTS), BF16)
    return jnp.concatenate([w_router.astype(BF16), pad], axis=1)


def _sc_mesh():
    return plsc.VectorSubcoreMesh(core_axis_name="c", subcore_axis_name="s")


def _sc_worker_id():
    return lax.axis_index("s") * SC_CORES + lax.axis_index("c")


def _sc_dispatch(h_packed, dest_sc, n_slots):
    n_tok, width = h_packed.shape
    per_w = n_tok // SC_WORKERS
    n_chunks = per_w // SC_CHUNK

    def body(h_hbm, dest_hbm, xs_hbm, idx_v, rows_v, sem_in, sem_out):
        base = _sc_worker_id() * per_w
        pltpu.sync_copy(dest_hbm.at[_sc_worker_id()], idx_v)

        def load(j, buf):
            return pltpu.make_async_copy(
                h_hbm.at[pl.ds(base + j * SC_CHUNK, SC_CHUNK)], rows_v.at[buf], sem_in)

        load(0, 0).start()

        @pl.loop(0, n_chunks, step=2)
        def _(j0):
            for buf in range(2):
                j = j0 + buf
                load(j, buf).wait()

                @pl.when(j + 1 < n_chunks)
                def _():
                    load(j + 1, 1 - buf).start()

                copies = [
                    pltpu.make_async_copy(
                        rows_v.at[buf], xs_hbm.at[idx_v.at[j * TOP_K + k]], sem_out)
                    for k in range(TOP_K)]
                for cp in copies:
                    cp.start()
                for cp in copies:
                    cp.wait()

    return pl.kernel(
        body,
        out_type=jax.ShapeDtypeStruct((n_slots, width), I32),
        mesh=_sc_mesh(),
        scratch_types=[
            pltpu.VMEM((n_chunks * TOP_K, SC_CHUNK), I32),
            pltpu.VMEM((2, SC_CHUNK, width), I32),
            pltpu.SemaphoreType.DMA,
            pltpu.SemaphoreType.DMA,
        ],
        name="sc_dispatch",
    )(h_packed, dest_sc)


def _sc_gather(ys_packed, dest_sc):
    width = ys_packed.shape[1]
    n_chunks = dest_sc.shape[1]
    per_w = n_chunks * SC_CHUNK
    n_rows = SC_WORKERS * per_w

    def body(ys_hbm, dest_hbm, out_hbm, idx_v, rows_v, sem_g, sem_w):
        base = _sc_worker_id() * per_w
        pltpu.sync_copy(dest_hbm.at[_sc_worker_id()], idx_v)

        def gather(j, buf):
            return pltpu.make_async_copy(ys_hbm.at[idx_v.at[j]], rows_v.at[buf], sem_g)

        def writeout(j, buf):
            return pltpu.make_async_copy(
                rows_v.at[buf], out_hbm.at[pl.ds(base + j * SC_CHUNK, SC_CHUNK)], sem_w)

        gather(0, 0).start()

        @pl.loop(0, n_chunks, step=2)
        def _(j0):
            for buf in range(2):
                j = j0 + buf
                gather(j, buf).wait()

                @pl.when(j >= 1)
                def _():
                    writeout(j - 1, 1 - buf).wait()

                @pl.when(j + 1 < n_chunks)
                def _():
                    gather(j + 1, 1 - buf).start()

                writeout(j, buf).start()

        writeout(n_chunks - 1, 1).wait()

    return pl.kernel(
        body,
        out_type=jax.ShapeDtypeStruct((n_rows, width), I32),
        mesh=_sc_mesh(),
        scratch_types=[
            pltpu.VMEM((n_chunks, SC_CHUNK), I32),
            pltpu.VMEM((2, SC_CHUNK, width), I32),
            pltpu.SemaphoreType.DMA,
            pltpu.SemaphoreType.DMA,
        ],
        name="sc_gather",
    )(ys_packed, dest_sc)


def _expert_kernel(layer, be_ref, nv_ref, first_ref, next_ref, xs_ref, win_hbm, bin_ref,
                   wout_hbm, bout_ref, ys_ref, win_f32, wout_f32, win_bf, wout_bf, sem):
    i = pl.program_id(0)
    valid = nv_ref[i]

    def fetch(e):
        return (pltpu.make_async_copy(win_hbm.at[layer, e], win_f32, sem.at[0]),
                pltpu.make_async_copy(wout_hbm.at[layer, e], wout_f32, sem.at[1]))

    @pl.when(first_ref[i] == 1)
    def _():
        @pl.when(i == 0)
        def _():
            for cp in fetch(be_ref[i]):
                cp.start()

        for cp in fetch(be_ref[i]):
            cp.wait()

        def cast_in(r, c):
            rows = pl.ds(pl.multiple_of(r * CAST_ROWS, CAST_ROWS), CAST_ROWS)
            win_bf[rows, :] = win_f32[rows, :].astype(BF16)
            return c

        def cast_out(r, c):
            rows = pl.ds(pl.multiple_of(r * CAST_ROWS, CAST_ROWS), CAST_ROWS)
            wout_bf[rows, :] = wout_f32[rows, :].astype(BF16)
            return c

        lax.fori_loop(0, D_MODEL // CAST_ROWS, cast_in, 0)
        lax.fori_loop(0, D_FF // CAST_ROWS, cast_out, 0)

        @pl.when(next_ref[i] >= 0)
        def _():
            for cp in fetch(next_ref[i]):
                cp.start()

    def compute(start, size):
        rows = pl.ds(pl.multiple_of(start, EXPERT_TAIL), size)
        row = lax.broadcasted_iota(I32, (size, 1), 0)
        packed = jnp.where(row + start < valid, xs_ref[rows, :], 0)
        lo, hi = _unpack_rows(packed)
        gu = (jnp.dot(lo.astype(BF16), win_bf[:PACKED, :], preferred_element_type=F32)
              + jnp.dot(hi.astype(BF16), win_bf[PACKED:, :], preferred_element_type=F32)
              + bin_ref[0])
        gate = jnp.minimum(gu[:, :D_FF], SWIGLU_LIMIT)
        up = jnp.clip(gu[:, D_FF:], -SWIGLU_LIMIT, SWIGLU_LIMIT)
        act = gate * _sigmoid(SWIGLU_ALPHA * gate) * (up + 1.0)
        out = jnp.dot(act.astype(BF16), wout_bf[...], preferred_element_type=F32) + bout_ref[0]
        ys_ref[rows, :] = _pack_rows(out)

    n_main = (valid + (EXPERT_SUB - EXPERT_TAIL) - 1) // EXPERT_SUB
    main_rows = n_main * EXPERT_SUB

    def main_pass(j, c):
        compute(j * EXPERT_SUB, EXPERT_SUB)
        return c

    lax.fori_loop(0, n_main, main_pass, 0)

    @pl.when(valid > main_rows)
    def _():
        compute(main_rows, EXPERT_TAIL)

    def clear(j, c):
        rows = pl.ds(pl.multiple_of(j * EXPERT_TAIL, EXPERT_TAIL), EXPERT_TAIL)
        ys_ref[rows, :] = jnp.zeros((EXPERT_TAIL, ys_ref.shape[1]), ys_ref.dtype)
        return c

    done = (main_rows + jnp.where(valid > main_rows, EXPERT_TAIL, 0)) // EXPERT_TAIL
    lax.fori_loop(done, EXPERT_ROWS // EXPERT_TAIL, clear, 0)


def _experts(layer, block_e, n_valid, is_first, next_e, xs, w_e_in, b_e_in, w_e_out, b_e_out):
    n_slots, width = xs.shape
    d = D_MODEL
    tm = EXPERT_ROWS
    bias_index = lambda i, be, nv, fi, nx: (layer, be[i], 0, 0)
    grid_spec = pltpu.PrefetchScalarGridSpec(
        num_scalar_prefetch=4,
        grid=(n_slots // tm,),
        in_specs=[
            pl.BlockSpec((tm, width), lambda i, be, nv, fi, nx: (i, 0)),
            pl.BlockSpec(memory_space=pl.ANY),
            pl.BlockSpec((None, 1, 1, 2 * D_FF), bias_index),
            pl.BlockSpec(memory_space=pl.ANY),
            pl.BlockSpec((None, 1, 1, d), bias_index),
        ],
        out_specs=pl.BlockSpec((tm, width), lambda i, be, nv, fi, nx: (i, 0)),
        scratch_shapes=[
            pltpu.VMEM((d, 2 * D_FF), F32), pltpu.VMEM((D_FF, d), F32),
            pltpu.VMEM((d, 2 * D_FF), BF16), pltpu.VMEM((D_FF, d), BF16),
            pltpu.SemaphoreType.DMA((2,)),
        ],
    )
    return pl.pallas_call(
        functools.partial(_expert_kernel, layer),
        out_shape=jax.ShapeDtypeStruct((n_slots, width), I32),
        grid_spec=grid_spec,
        compiler_params=pltpu.CompilerParams(
            dimension_semantics=("arbitrary",), vmem_limit_bytes=VMEM_LIMIT),
        name="experts",
    )(block_e, n_valid, is_first, next_e, xs, w_e_in,
      b_e_in.reshape(-1, N_EXPERTS, 1, 2 * D_FF), w_e_out, b_e_out.reshape(-1, N_EXPERTS, 1, d))


def _final_kernel(x_ref, y0_ref, y1_ref, y2_ref, y3_ref, wt_ref, mod_ref, g_ref, o_ref):
    x = _combined_rows(x_ref, (y0_ref, y1_ref, y2_ref, y3_ref), wt_ref, mod_ref[5])
    ms = jnp.mean(x * x, axis=-1, keepdims=True)
    out = x * lax.rsqrt(ms + NORM_EPS) * g_ref[...]
    o_ref[...] = pltpu.einshape("(tb)d->btd", out, b=BATCH)


def _final_norm(x2d, pending, g):
    n_tok, d = x2d.shape
    tr = NORM_ROWS
    y4, wts, mod_l = pending
    return pl.pallas_call(
        _final_kernel,
        out_shape=jax.ShapeDtypeStruct((BATCH, n_tok // BATCH, d), F32),
        grid=(n_tok // tr,),
        in_specs=([pl.BlockSpec((tr, d), lambda i: (i, 0))]
                  + _pending_specs(tr, n_tok, lambda i: i)
                  + [_const_spec(mod_l.shape), _const_spec((1, d))]),
        out_specs=pl.BlockSpec((BATCH, tr // BATCH, d), lambda i: (0, i, 0)),
        compiler_params=pltpu.CompilerParams(
            dimension_semantics=("arbitrary",), vmem_limit_bytes=VMEM_LIMIT),
        name="final_norm",
    )(x2d, y4, y4, y4, y4, wts, mod_l, g.reshape(1, d))


def _moe_layer(layer, h_packed, idx, pos, counts, w_e_in, b_e_in, w_e_out, b_e_out):
    n_tok = h_packed.shape[0]
    tm = EXPERT_ROWS
    padded = ((counts + tm - 1) // tm) * tm
    pad_end = jnp.cumsum(padded)
    start_pad = pad_end - padded
    experts = jnp.arange(N_EXPERTS, dtype=I32)
    dest = jnp.sum(jnp.where(idx[:, :, None] == experts, start_pad, 0), axis=-1) + pos
    n_blocks = (n_tok * TOP_K) // tm + N_EXPERTS
    n_slots = n_blocks * tm
    block_start = jnp.arange(n_blocks, dtype=I32) * tm
    block_e = jnp.minimum(
        jnp.sum((block_start[:, None] >= pad_end[None, :]).astype(I32), axis=-1),
        N_EXPERTS - 1)
    hot_e = block_e[:, None] == experts[None, :]
    rows_left = jnp.sum(jnp.where(hot_e, (start_pad + counts)[None, :], 0), axis=-1) - block_start
    n_valid = jnp.where(block_start < pad_end[-1], jnp.clip(rows_left, 0, tm), 0).astype(I32)
    is_first = jnp.sum(jnp.where(hot_e, start_pad[None, :], 0), axis=-1) == block_start
    is_first = (is_first & (n_valid > 0)).astype(I32)
    later = (experts[None, :] > experts[:, None]) & (counts[None, :] > 0)
    next_of = jnp.min(jnp.where(later, experts[None, :], N_EXPERTS), axis=-1)
    next_of = jnp.where(next_of == N_EXPERTS, -1, next_of)
    next_e = jnp.sum(jnp.where(hot_e, next_of[None, :], 0), axis=-1).astype(I32)

    per_w = n_tok // SC_WORKERS
    n_chunks = per_w // SC_CHUNK
    dest_scatter = (dest.reshape(TOP_K, SC_WORKERS, n_chunks, SC_CHUNK)
                    .transpose(1, 2, 0, 3)
                    .reshape(SC_WORKERS, n_chunks * TOP_K, SC_CHUNK))
    xs = _sc_dispatch(h_packed, dest_scatter, n_slots)
    ys = _experts(layer, block_e, n_valid, is_first, next_e, xs,
                  w_e_in, b_e_in, w_e_out, b_e_out)
    dest_gather = dest.reshape(SC_WORKERS, n_chunks * TOP_K, SC_CHUNK)
    return _sc_gather(ys, dest_gather)


def kernel(x, c, norm1_g, norm2_g, w_ada, b_ada, w_in, conv_w, conv_b, w_rg_a, b_rg_a,
           w_rg_x, b_rg_x, lru_lambda, w_pool, pool_scale, w_up_a, w_up_b, w_out,
           w_router, b_router, w_e_in, b_e_in, w_e_out, b_e_out, final_g):
    depth = w_in.shape[0]
    bsz, seq, d = x.shape
    assert bsz == BATCH and d == D_MODEL
    mod = _ada_mod(c, w_ada, b_ada).reshape(depth, bsz, 6, d).transpose(0, 2, 1, 3)
    w_rg = jnp.concatenate([w_rg_a, w_rg_x], axis=-1).astype(BF16)
    pending = None
    xcur = x
    for l in range(depth):
        pad = jnp.zeros((d - pool_scale.shape[1],), F32)
        vec = jnp.stack([norm1_g[l], conv_b[l], b_rg_a[l], b_rg_x[l], lru_lambda[l],
                         norm2_g[l], jnp.concatenate([pool_scale[l], pad]),
                         jnp.zeros((d,), F32)])
        xcur, h_packed, idx, pos, wts, counts = _mixer_layer(
            xcur, pending, mod[l], vec, w_in[l].astype(BF16), conv_w[l], w_rg[l],
            w_pool[l].astype(BF16), w_up_a[l].astype(BF16), w_up_b[l].astype(BF16),
            w_out[l].astype(BF16), _padded_router_weight(w_router[l]), b_router[l])
        y4 = _moe_layer(l, h_packed, idx, pos, counts[:, 0], w_e_in, b_e_in, w_e_out, b_e_out)
        pending = (y4, wts.T, mod[l])
    return _final_norm(xcur, pending, final_g)
```

```python
import functools
import math

import jax
import jax.numpy as jnp
from jax import lax
from jax.experimental import pallas as pl
from jax.experimental.pallas import tpu as pltpu
from jax.experimental.pallas import tpu_sc as plsc

D_MODEL = 1024
BATCH = 8
N_HEADS = 8
HEAD_DIM = 128
CONV_WIDTH = 4
LRU_C = 8.0
POOL_WINDOWS = (2, 4, 8, 16)
POOL_WIDTH = 512
POOL_GROUP_DIM = 128
N_EXPERTS = 32
TOP_K = 4
D_FF = 1024
SWIGLU_LIMIT = 7.0
SWIGLU_ALPHA = 1.702
NORM_EPS = 1e-6

SUBLANES = 8
VMEM_LIMIT = 56 * 1024 * 1024
SC_CORES = 2
SC_SUBCORES = 16
SC_WORKERS = SC_CORES * SC_SUBCORES
SC_CHUNK = 64

MIX_STEPS = 32
ROUTER_COLS = 128
EXPERT_ROWS = 1024
EXPERT_SUB = 512
EXPERT_TAIL = 256
ADA_COLS = 1536
NORM_ROWS = 512
CAST_ROWS = 128
PACKED = D_MODEL // 2

F32 = jnp.float32
BF16 = jnp.bfloat16
I32 = jnp.int32
HIGH_MASK = -65536
LOG2_E = math.log2(math.e)


def _const_spec(shape):
    nd = len(shape)
    return pl.BlockSpec(shape, lambda *_: (0,) * nd, pipeline_mode=pl.Buffered(1))


def _pack_rows(v):
    packed = pltpu.pack_elementwise([v[:, :PACKED], v[:, PACKED:]], packed_dtype=BF16)
    return lax.bitcast_convert_type(packed, I32)


def _unpack_rows(p):
    lo = lax.bitcast_convert_type(lax.shift_left(p, 16), F32)
    hi = lax.bitcast_convert_type(p & HIGH_MASK, F32)
    return lo, hi


def _sigmoid(v):
    return 0.5 * jnp.tanh(0.5 * v) + 0.5


def _per_batch(v, slab, op):
    rows, d = v.shape
    return op(v.reshape(rows // BATCH, BATCH, d), slab[None]).reshape(rows, d)


def _modulated_norm(x, g, shift, scale):
    ms = jnp.mean(x * x, axis=-1, keepdims=True)
    hn = x * lax.rsqrt(ms + NORM_EPS)
    return _per_batch(_per_batch(hn, g * (1.0 + scale), jnp.multiply), shift, jnp.add)


def _ada_kernel(c_ref, w_ref, b_ref, o_ref):
    c = c_ref[...]
    c_act = c * jax.nn.sigmoid(c)
    o_ref[0] = jnp.dot(c_act, w_ref[0], preferred_element_type=F32,
                       precision=lax.Precision.HIGHEST) + b_ref[0]


def _ada_mod(c, w_ada, b_ada):
    depth, d, n = w_ada.shape
    bsz = c.shape[0]
    tn = ADA_COLS
    return pl.pallas_call(
        _ada_kernel,
        out_shape=jax.ShapeDtypeStruct((depth, bsz, n), F32),
        grid=(depth, n // tn),
        in_specs=[
            pl.BlockSpec((bsz, d), lambda l, j: (0, 0)),
            pl.BlockSpec((1, d, tn), lambda l, j: (l, 0, j)),
            pl.BlockSpec((1, 1, tn), lambda l, j: (l, 0, j)),
        ],
        out_specs=pl.BlockSpec((1, bsz, tn), lambda l, j: (l, 0, j)),
        compiler_params=pltpu.CompilerParams(
            dimension_semantics=("arbitrary", "arbitrary"),
            vmem_limit_bytes=VMEM_LIMIT),
        name="ada_mod",
    )(c, w_ada, b_ada.reshape(depth, 1, n))


def _delay(a, steps, carry_ref):
    n = steps * BATCH
    keep = carry_ref.shape[0]
    prev = carry_ref[keep - n:, :]
    return jnp.concatenate([prev, a[:a.shape[0] - n]], axis=0)


def _save_tail(a, carry_ref):
    carry_ref[...] = a[a.shape[0] - carry_ref.shape[0]:]


def _combined_rows(x_ref, y_refs, wt_ref, gate2, rs=slice(None)):
    x = x_ref[rs, :]
    wt = wt_ref[rs, :]
    acc_lo = jnp.zeros((x.shape[0], PACKED), F32)
    acc_hi = jnp.zeros((x.shape[0], PACKED), F32)
    for k, y_ref in enumerate(y_refs):
        lo, hi = _unpack_rows(y_ref[rs, :])
        wk = wt[:, k:k + 1]
        acc_lo = acc_lo + wk * lo
        acc_hi = acc_hi + wk * hi
    ffn = jnp.concatenate([acc_lo, acc_hi], axis=-1)
    return x + _per_batch(ffn, gate2, jnp.multiply)


def _route_rows(padded, brc_ref, run_c, live, cols, idx_ref, pos_ref, wt_ref):
    rows = padded.shape[0]
    logits = padded.T[:N_EXPERTS] + brc_ref[...]

    e_iota = lax.broadcasted_iota(I32, (N_EXPERTS, rows), 0)
    vals, idxs, hots = [], [], []
    cur = logits
    for _ in range(TOP_K):
        m = jnp.max(cur, axis=0, keepdims=True)
        am = jnp.min(jnp.where(cur == m, e_iota, N_EXPERTS), axis=0, keepdims=True)
        hot = e_iota == am
        vals.append(m)
        idxs.append(am)
        hots.append(hot)
        cur = jnp.where(hot, -jnp.inf, cur)
    exps = [jnp.exp(v - vals[0]) for v in vals]
    denom = exps[0] + exps[1] + exps[2] + exps[3]

    sel = (hots[0] | hots[1] | hots[2] | hots[3]).astype(F32)
    ri = lax.broadcasted_iota(I32, (rows, rows), 0)
    ci = lax.broadcasted_iota(I32, (rows, rows), 1)
    earlier = (ri < ci).astype(BF16)
    before = jnp.dot(sel.astype(BF16), earlier, preferred_element_type=F32) + run_c[...]
    run_c[...] = run_c[...] + jnp.where(live, jnp.sum(sel, axis=1, keepdims=True), 0.0)
    poss = [jnp.sum(jnp.where(hot, before, 0.0), axis=0, keepdims=True).astype(I32)
            for hot in hots]
    idx_ref[:, cols] = jnp.concatenate(idxs, axis=0)
    pos_ref[:, cols] = jnp.concatenate(poss, axis=0)
    wt_ref[:, cols] = jnp.concatenate([e / denom for e in exps], axis=0)


VEC_G1, VEC_CONV_B, VEC_BRG_A, VEC_BRG_X, VEC_LAM, VEC_G2, VEC_PSCALE = range(7)


def _mixer_kernel(with_combine, *refs):
    refs = list(refs)
    n_in = 1 + (TOP_K + 1 if with_combine else 0)
    first, nxt = refs[:n_in], refs[n_in:2 * n_in]
    refs = refs[2 * n_in:]
    if with_combine:
        modp_ref = refs.pop(0)
    (mod_ref, vec_ref, win_ref, convw_ref, wrg_ref, wpool_ref, wupa_ref, wupb_ref, wout_ref,
     wrt_ref, brc_ref,
     o_ref, hp_ref, idx_ref, pos_ref, wt_ref, cnt_ref,
     conv_c, p1_c, p2_c, p4_c, p8_c, h_c, run_c, logit_c, x_c, hb_c, xl_c) = refs
    i = pl.program_id(0)
    last = pl.num_programs(0) - 1
    cur = i % 2
    rows = o_ref.shape[0]
    steps = rows // BATCH
    d = D_MODEL
    vec = vec_ref[...]
    row_of = lambda k: vec[k:k + 1]
    shift1, scale1, gate1 = mod_ref[0], mod_ref[1], mod_ref[2]

    def prologue(in_refs):
        if with_combine:
            x = _combined_rows(in_refs[0], in_refs[1:1 + TOP_K], in_refs[1 + TOP_K],
                               modp_ref[5])
        else:
            x = pltpu.einshape("btd->(tb)d", in_refs[0][...])
        return x, _modulated_norm(x, row_of(VEC_G1), shift1, scale1).astype(BF16)

    @pl.when(i == 0)
    def _():
        for ref in (conv_c, p1_c, p2_c, p4_c, p8_c, h_c, run_c, logit_c):
            ref[...] = jnp.zeros(ref.shape, ref.dtype)
        x_c[0], hb_c[0] = prologue(first)
        xl_c[...] = jnp.dot(hb_c[0], win_ref[:, 0:d], preferred_element_type=F32)

    def route(padded, block, live):
        cols = pl.ds(pl.multiple_of(block * rows, rows), rows)
        _route_rows(padded, brc_ref, run_c, live, cols, idx_ref, pos_ref, wt_ref)

    def in_proj(lo, hi):
        return jnp.dot(hb_c[cur], win_ref[:, lo:hi], preferred_element_type=F32)

    x_lru = xl_c[...]
    cw = convw_ref[...]
    xc = (row_of(VEC_CONV_B) + cw[3:4] * x_lru
          + cw[2:3] * _delay(x_lru, 1, conv_c)
          + cw[1:2] * _delay(x_lru, 2, conv_c)
          + cw[0:1] * _delay(x_lru, 3, conv_c))
    _save_tail(x_lru, conv_c)
    xcb = xc.astype(BF16)
    pre = [jnp.dot(xcb[:, k * HEAD_DIM:(k + 1) * HEAD_DIM], wrg_ref[k],
                   preferred_element_type=F32) for k in range(N_HEADS)]
    pre_a = jnp.concatenate([p[:, :HEAD_DIM] for p in pre], axis=-1)
    pre_x = jnp.concatenate([p[:, HEAD_DIM:] for p in pre], axis=-1)
    r = _sigmoid(pre_a + row_of(VEC_BRG_A))
    gi = _sigmoid(pre_x + row_of(VEC_BRG_X))
    rate = (-LRU_C * LOG2_E) * jax.nn.softplus(-row_of(VEC_LAM))
    a = jnp.exp2(r * rate)
    y = 1.0 - a * a
    mult = jnp.where(y > 0.0, y * lax.rsqrt(y), 0.0)
    u = mult * (gi * xc)
    h = h_c[...]
    hs = []
    for t in range(steps):
        sl = slice(t * BATCH, (t + 1) * BATCH)
        h = a[sl] * h + u[sl]
        hs.append(h)
    h_c[...] = h
    y_lru = jnp.concatenate(hs, axis=0) * jax.nn.gelu(in_proj(d, 2 * d))
    p_a = jnp.dot(y_lru.astype(BF16), wupa_ref[...], preferred_element_type=F32)
    merged = _sigmoid(in_proj(2 * d + POOL_WIDTH, 3 * d + POOL_WIDTH)) * p_a

    route(logit_c[...], jnp.maximum(i - 1, 0), i > 0)
    x_c[1 - cur], hb_c[1 - cur] = prologue(nxt)

    x_pool = in_proj(2 * d, 2 * d + POOL_WIDTH)
    g = POOL_GROUP_DIM
    s2 = x_pool + _delay(x_pool, 1, p1_c)
    s4 = s2[:, g:] + _delay(s2[:, g:], 2, p2_c)
    s8 = s4[:, g:] + _delay(s4[:, g:], 4, p4_c)
    s16 = s8[:, g:] + _delay(s8[:, g:], 8, p8_c)
    _save_tail(x_pool, p1_c)
    _save_tail(s2[:, g:], p2_c)
    _save_tail(s4[:, g:], p4_c)
    _save_tail(s8[:, g:], p8_c)
    wins = (s2[:, :g], s4[:, :g], s8[:, :g], s16)
    row = lax.broadcasted_iota(I32, (rows, 1), 0)
    t1 = (i * steps + row // BATCH + 1).astype(F32)
    pooled = []
    for k, w in enumerate(POOL_WINDOWS):
        cnt = jnp.minimum(t1, float(w))
        pk = wins[k] / cnt - x_pool[:, k * g:(k + 1) * g]
        pooled.append(jnp.dot(pk.astype(BF16), wpool_ref[k], preferred_element_type=F32))
    y_pool = jnp.concatenate(pooled, axis=-1) * row_of(VEC_PSCALE)[:, :POOL_WIDTH]
    p_b = jnp.dot(y_pool.astype(BF16), wupb_ref[...], preferred_element_type=F32)
    merged = merged + _sigmoid(in_proj(3 * d + POOL_WIDTH, 4 * d + POOL_WIDTH)) * p_b

    mix = jnp.dot(merged.astype(BF16), wout_ref[...], preferred_element_type=F32)
    x_mid = x_c[cur] + _per_batch(mix, gate1, jnp.multiply)
    o_ref[...] = x_mid
    xl_c[...] = jnp.dot(hb_c[1 - cur], win_ref[:, 0:d], preferred_element_type=F32)

    hr = _modulated_norm(x_mid, row_of(VEC_G2), mod_ref[3], mod_ref[4])
    hp_ref[...] = _pack_rows(hr)
    padded = jnp.dot(hr.astype(BF16), wrt_ref[...], preferred_element_type=F32)
    logit_c[...] = padded

    @pl.when(i == last)
    def _():
        route(logit_c[...], i, True)

    cnt_ref[...] = run_c[...].astype(I32)


def _pending_specs(rows, n_tok, block_of):
    nb = n_tok // rows
    y_spec = lambda k: pl.BlockSpec((rows, PACKED), lambda i: (k * nb + block_of(i), 0))
    return ([y_spec(k) for k in range(TOP_K)]
            + [pl.BlockSpec((rows, TOP_K), lambda i: (block_of(i), 0))])


def _mixer_layer(x_in, pending, mod_l, vec, w_in, conv_w, w_rg, w_pool, w_up_a, w_up_b, w_out,
                 w_route, b_router):
    rows = MIX_STEPS * BATCH
    weights = [mod_l, vec, w_in, conv_w, w_rg, w_pool, w_up_a, w_up_b, w_out, w_route,
               b_router.reshape(N_EXPERTS, 1)]
    if pending is None:
        n_tok, d = x_in.shape[0] * x_in.shape[1], x_in.shape[2]
        inputs = [x_in]
        specs = lambda block_of: [
            pl.BlockSpec((BATCH, MIX_STEPS, d), lambda i: (0, block_of(i), 0))]
    else:
        n_tok, d = x_in.shape
        y4, wts_prev, mod_prev = pending
        inputs = [x_in, y4, y4, y4, y4, wts_prev]
        weights.insert(0, mod_prev)
        specs = lambda block_of: ([pl.BlockSpec((rows, d), lambda i: (block_of(i), 0))]
                                  + _pending_specs(rows, n_tok, block_of))
    nb = n_tok // rows
    in_specs = (specs(lambda i: 0) + specs(lambda i: jnp.minimum(i + 1, nb - 1))
                + [_const_spec(a.shape) for a in weights])
    g = POOL_GROUP_DIM
    scratch = [
        pltpu.VMEM(((CONV_WIDTH - 1) * BATCH, d), F32),
        pltpu.VMEM((1 * BATCH, 4 * g), F32), pltpu.VMEM((2 * BATCH, 3 * g), F32),
        pltpu.VMEM((4 * BATCH, 2 * g), F32), pltpu.VMEM((8 * BATCH, g), F32),
        pltpu.VMEM((BATCH, d), F32),
        pltpu.VMEM((N_EXPERTS, 1), F32),
        pltpu.VMEM((rows, ROUTER_COLS), F32),
        pltpu.VMEM((2, rows, d), F32), pltpu.VMEM((2, rows, d), BF16),
        pltpu.VMEM((rows, d), F32),
    ]
    routed = lambda dtype: jax.ShapeDtypeStruct((TOP_K, n_tok), dtype)
    whole = lambda shape: pl.BlockSpec(shape, lambda i: (0,) * len(shape))
    routed_spec = whole((TOP_K, n_tok))
    return pl.pallas_call(
        functools.partial(_mixer_kernel, pending is not None),
        out_shape=(
            jax.ShapeDtypeStruct((n_tok, d), F32),
            jax.ShapeDtypeStruct((n_tok, PACKED), I32),
            routed(I32), routed(I32), routed(F32),
            jax.ShapeDtypeStruct((N_EXPERTS, 1), I32),
        ),
        grid=(nb,),
        in_specs=in_specs,
        out_specs=(
            pl.BlockSpec((rows, d), lambda i: (i, 0)),
            pl.BlockSpec((rows, PACKED), lambda i: (i, 0)),
            routed_spec, routed_spec, routed_spec,
            whole((N_EXPERTS, 1)),
        ),
        scratch_shapes=scratch,
        compiler_params=pltpu.CompilerParams(
            dimension_semantics=("arbitrary",), vmem_limit_bytes=VMEM_LIMIT),
        name="mixer",
    )(*(inputs + inputs + weights))


def _padded_router_weight(w_router):
    pad = jnp.zeros((w_router.shape[0], ROUTER_COLS - N_EXPERTS), BF16)
    return jnp.concatenate([w_router.astype(BF16), pad], axis=1)


def _sc_mesh():
    return plsc.VectorSubcoreMesh(core_axis_name="c", subcore_axis_name="s")


def _sc_worker_id():
    return lax.axis_index("s") * SC_CORES + lax.axis_index("c")


def _sc_dispatch(h_packed, dest_sc, n_slots):
    n_tok, width = h_packed.shape
    per_w = n_tok // SC_WORKERS
    n_chunks = per_w // SC_CHUNK

    def body(h_hbm, dest_hbm, xs_hbm, idx_v, rows_v, sem_in, sem_out):
        base = _sc_worker_id() * per_w
        pltpu.sync_copy(dest_hbm.at[_sc_worker_id()], idx_v)

        def load(j, buf):
            return pltpu.make_async_copy(
                h_hbm.at[pl.ds(base + j * SC_CHUNK, SC_CHUNK)], rows_v.at[buf], sem_in)

        load(0, 0).start()

        @pl.loop(0, n_chunks, step=2)
        def _(j0):
            for buf in range(2):
                j = j0 + buf
                load(j, buf).wait()

                @pl.when(j + 1 < n_chunks)
                def _():
                    load(j + 1, 1 - buf).start()

                copies = [
                    pltpu.make_async_copy(
                        rows_v.at[buf], xs_hbm.at[idx_v.at[j * TOP_K + k]], sem_out)
                    for k in range(TOP_K)]
                for cp in copies:
                    cp.start()
                for cp in copies:
                    cp.wait()

    return pl.kernel(
        body,
        out_type=jax.ShapeDtypeStruct((n_slots, width), I32),
        mesh=_sc_mesh(),
        scratch_types=[
            pltpu.VMEM((n_chunks * TOP_K, SC_CHUNK), I32),
            pltpu.VMEM((2, SC_CHUNK, width), I32),
            pltpu.SemaphoreType.DMA,
            pltpu.SemaphoreType.DMA,
        ],
        name="sc_dispatch",
    )(h_packed, dest_sc)


def _sc_gather(ys_packed, dest_sc):
    width = ys_packed.shape[1]
    n_chunks = dest_sc.shape[1]
    per_w = n_chunks * SC_CHUNK
    n_rows = SC_WORKERS * per_w

    def body(ys_hbm, dest_hbm, out_hbm, idx_v, rows_v, sem_g, sem_w):
        base = _sc_worker_id() * per_w
        pltpu.sync_copy(dest_hbm.at[_sc_worker_id()], idx_v)

        def gather(j, buf):
            return pltpu.make_async_copy(ys_hbm.at[idx_v.at[j]], rows_v.at[buf], sem_g)

        def writeout(j, buf):
            return pltpu.make_async_copy(
                rows_v.at[buf], out_hbm.at[pl.ds(base + j * SC_CHUNK, SC_CHUNK)], sem_w)

        gather(0, 0).start()

        @pl.loop(0, n_chunks, step=2)
        def _(j0):
            for buf in range(2):
                j = j0 + buf
                gather(j, buf).wait()

                @pl.when(j >= 1)
                def _():
                    writeout(j - 1, 1 - buf).wait()

                @pl.when(j + 1 < n_chunks)
                def _():
                    gather(j + 1, 1 - buf).start()

                writeout(j, buf).start()

        writeout(n_chunks - 1, 1).wait()

    return pl.kernel(
        body,
        out_type=jax.ShapeDtypeStruct((n_rows, width), I32),
        mesh=_sc_mesh(),
        scratch_types=[
            pltpu.VMEM((n_chunks, SC_CHUNK), I32),
            pltpu.VMEM((2, SC_CHUNK, width), I32),
            pltpu.SemaphoreType.DMA,
            pltpu.SemaphoreType.DMA,
        ],
        name="sc_gather",
    )(ys_packed, dest_sc)


def _expert_kernel(layer, be_ref, nv_ref, first_ref, next_ref, xs_ref, win_hbm, bin_ref,
                   wout_hbm, bout_ref, ys_ref, win_f32, wout_f32, win_bf, wout_bf, sem):
    i = pl.program_id(0)
    valid = nv_ref[i]

    def fetch(e):
        return (pltpu.make_async_copy(win_hbm.at[layer, e], win_f32, sem.at[0]),
                pltpu.make_async_copy(wout_hbm.at[layer, e], wout_f32, sem.at[1]))

    @pl.when(first_ref[i] == 1)
    def _():
        @pl.when(i == 0)
        def _():
            for cp in fetch(be_ref[i]):
                cp.start()

        for cp in fetch(be_ref[i]):
            cp.wait()

        def cast_in(r, c):
            rows = pl.ds(pl.multiple_of(r * CAST_ROWS, CAST_ROWS), CAST_ROWS)
            win_bf[rows, :] = win_f32[rows, :].astype(BF16)
            return c

        def cast_out(r, c):
            rows = pl.ds(pl.multiple_of(r * CAST_ROWS, CAST_ROWS), CAST_ROWS)
            wout_bf[rows, :] = wout_f32[rows, :].astype(BF16)
            return c

        lax.fori_loop(0, D_MODEL // CAST_ROWS, cast_in, 0)
        lax.fori_loop(0, D_FF // CAST_ROWS, cast_out, 0)

        @pl.when(next_ref[i] >= 0)
        def _():
            for cp in fetch(next_ref[i]):
                cp.start()

    def load_rows(start, size):
        rows = pl.ds(pl.multiple_of(start, EXPERT_TAIL), size)
        row = lax.broadcasted_iota(I32, (size, 1), 0)
        lo, hi = _unpack_rows(jnp.where(row + start < valid, xs_ref[rows, :], 0))
        return lo.astype(BF16), hi.astype(BF16)

    def up_proj(lo, hi):
        return (jnp.dot(lo, win_bf[:PACKED, :], preferred_element_type=F32)
                + jnp.dot(hi, win_bf[PACKED:, :], preferred_element_type=F32) + bin_ref[0])

    def activation(gu):
        gate = jnp.minimum(gu[:, :D_FF], SWIGLU_LIMIT)
        up = jnp.clip(gu[:, D_FF:], -SWIGLU_LIMIT, SWIGLU_LIMIT)
        return (gate * _sigmoid(SWIGLU_ALPHA * gate) * (up + 1.0)).astype(BF16)

    def down_proj(act):
        return jnp.dot(act, wout_bf[...], preferred_element_type=F32) + bout_ref[0]

    def store_rows(start, size, out):
        rows = pl.ds(pl.multiple_of(start, EXPERT_TAIL), size)
        ys_ref[rows, :] = _pack_rows(out)

    def compute(start, size):
        store_rows(start, size, down_proj(activation(up_proj(*load_rows(start, size)))))

    n_main = (valid + (EXPERT_SUB - EXPERT_TAIL) - 1) // EXPERT_SUB
    main_rows = n_main * EXPERT_SUB
    assert EXPERT_ROWS == 2 * EXPERT_SUB

    @pl.when(n_main == 2)
    def _():
        in0, in1 = load_rows(0, EXPERT_SUB), load_rows(EXPERT_SUB, EXPERT_SUB)
        gu0 = up_proj(*in0)
        gu1 = up_proj(*in1)
        out0 = down_proj(activation(gu0))
        out1 = down_proj(activation(gu1))
        store_rows(0, EXPERT_SUB, out0)
        store_rows(EXPERT_SUB, EXPERT_SUB, out1)

    @pl.when(n_main == 1)
    def _():
        compute(0, EXPERT_SUB)

    @pl.when((n_main < 2) & (valid > main_rows))
    def _():
        compute(main_rows, EXPERT_TAIL)


    def clear(j, c):
        rows = pl.ds(pl.multiple_of(j * EXPERT_TAIL, EXPERT_TAIL), EXPERT_TAIL)
        ys_ref[rows, :] = jnp.zeros((EXPERT_TAIL, ys_ref.shape[1]), ys_ref.dtype)
        return c

    done = (main_rows + jnp.where(valid > main_rows, EXPERT_TAIL, 0)) // EXPERT_TAIL
    lax.fori_loop(done, EXPERT_ROWS // EXPERT_TAIL, clear, 0)


def _experts(layer, block_e, n_valid, is_first, next_e, xs, w_e_in, b_e_in, w_e_out, b_e_out):
    n_slots, width = xs.shape
    d = D_MODEL
    tm = EXPERT_ROWS
    bias_index = lambda i, be, nv, fi, nx: (layer, be[i], 0, 0)
    grid_spec = pltpu.PrefetchScalarGridSpec(
        num_scalar_prefetch=4,
        grid=(n_slots // tm,),
        in_specs=[
            pl.BlockSpec((tm, width), lambda i, be, nv, fi, nx: (i, 0)),
            pl.BlockSpec(memory_space=pl.ANY),
            pl.BlockSpec((None, 1, 1, 2 * D_FF), bias_index),
            pl.BlockSpec(memory_space=pl.ANY),
            pl.BlockSpec((None, 1, 1, d), bias_index),
        ],
        out_specs=pl.BlockSpec((tm, width), lambda i, be, nv, fi, nx: (i, 0)),
        scratch_shapes=[
            pltpu.VMEM((d, 2 * D_FF), F32), pltpu.VMEM((D_FF, d), F32),
            pltpu.VMEM((d, 2 * D_FF), BF16), pltpu.VMEM((D_FF, d), BF16),
            pltpu.SemaphoreType.DMA((2,)),
        ],
    )
    return pl.pallas_call(
        functools.partial(_expert_kernel, layer),
        out_shape=jax.ShapeDtypeStruct((n_slots, width), I32),
        grid_spec=grid_spec,
        compiler_params=pltpu.CompilerParams(
            dimension_semantics=("arbitrary",), vmem_limit_bytes=VMEM_LIMIT),
        name="experts",
    )(block_e, n_valid, is_first, next_e, xs, w_e_in,
      b_e_in.reshape(-1, N_EXPERTS, 1, 2 * D_FF), w_e_out, b_e_out.reshape(-1, N_EXPERTS, 1, d))


def _final_kernel(x_ref, y0_ref, y1_ref, y2_ref, y3_ref, wt_ref, mod_ref, g_ref, o_ref):
    x = _combined_rows(x_ref, (y0_ref, y1_ref, y2_ref, y3_ref), wt_ref, mod_ref[5])
    ms = jnp.mean(x * x, axis=-1, keepdims=True)
    out = x * lax.rsqrt(ms + NORM_EPS) * g_ref[...]
    o_ref[...] = pltpu.einshape("(tb)d->btd", out, b=BATCH)


def _final_norm(x2d, pending, g):
    n_tok, d = x2d.shape
    tr = NORM_ROWS
    y4, wts, mod_l = pending
    return pl.pallas_call(
        _final_kernel,
        out_shape=jax.ShapeDtypeStruct((BATCH, n_tok // BATCH, d), F32),
        grid=(n_tok // tr,),
        in_specs=([pl.BlockSpec((tr, d), lambda i: (i, 0))]
                  + _pending_specs(tr, n_tok, lambda i: i)
                  + [_const_spec(mod_l.shape), _const_spec((1, d))]),
        out_specs=pl.BlockSpec((BATCH, tr // BATCH, d), lambda i: (0, i, 0)),
        compiler_params=pltpu.CompilerParams(
            dimension_semantics=("arbitrary",), vmem_limit_bytes=VMEM_LIMIT),
        name="final_norm",
    )(x2d, y4, y4, y4, y4, wts, mod_l, g.reshape(1, d))


def _moe_layer(layer, h_packed, idx, pos, counts, w_e_in, b_e_in, w_e_out, b_e_out):
    n_tok = h_packed.shape[0]
    tm = EXPERT_ROWS
    padded = ((counts + tm - 1) // tm) * tm
    pad_end = jnp.cumsum(padded)
    start_pad = pad_end - padded
    experts = jnp.arange(N_EXPERTS, dtype=I32)
    dest = jnp.sum(jnp.where(idx[:, :, None] == experts, start_pad, 0), axis=-1) + pos
    n_blocks = (n_tok * TOP_K) // tm + N_EXPERTS
    n_slots = n_blocks * tm
    block_start = jnp.arange(n_blocks, dtype=I32) * tm
    block_e = jnp.minimum(
        jnp.sum((block_start[:, None] >= pad_end[None, :]).astype(I32), axis=-1),
        N_EXPERTS - 1)
    hot_e = block_e[:, None] == experts[None, :]
    rows_left = jnp.sum(jnp.where(hot_e, (start_pad + counts)[None, :], 0), axis=-1) - block_start
    n_valid = jnp.where(block_start < pad_end[-1], jnp.clip(rows_left, 0, tm), 0).astype(I32)
    is_first = jnp.sum(jnp.where(hot_e, start_pad[None, :], 0), axis=-1) == block_start
    is_first = (is_first & (n_valid > 0)).astype(I32)
    later = (experts[None, :] > experts[:, None]) & (counts[None, :] > 0)
    next_of = jnp.min(jnp.where(later, experts[None, :], N_EXPERTS), axis=-1)
    next_of = jnp.where(next_of == N_EXPERTS, -1, next_of)
    next_e = jnp.sum(jnp.where(hot_e, next_of[None, :], 0), axis=-1).astype(I32)

    per_w = n_tok // SC_WORKERS
    n_chunks = per_w // SC_CHUNK
    dest_scatter = (dest.reshape(TOP_K, SC_WORKERS, n_chunks, SC_CHUNK)
                    .transpose(1, 2, 0, 3)
                    .reshape(SC_WORKERS, n_chunks * TOP_K, SC_CHUNK))
    xs = _sc_dispatch(h_packed, dest_scatter, n_slots)
    ys = _experts(layer, block_e, n_valid, is_first, next_e, xs,
                  w_e_in, b_e_in, w_e_out, b_e_out)
    dest_gather = dest.reshape(SC_WORKERS, n_chunks * TOP_K, SC_CHUNK)
    return _sc_gather(ys, dest_gather)


def kernel(x, c, norm1_g, norm2_g, w_ada, b_ada, w_in, conv_w, conv_b, w_rg_a, b_rg_a,
           w_rg_x, b_rg_x, lru_lambda, w_pool, pool_scale, w_up_a, w_up_b, w_out,
           w_router, b_router, w_e_in, b_e_in, w_e_out, b_e_out, final_g):
    depth = w_in.shape[0]
    bsz, seq, d = x.shape
    assert bsz == BATCH and d == D_MODEL
    mod = _ada_mod(c, w_ada, b_ada).reshape(depth, bsz, 6, d).transpose(0, 2, 1, 3)
    w_rg = jnp.concatenate([w_rg_a, w_rg_x], axis=-1).astype(BF16)
    pending = None
    xcur = x
    for l in range(depth):
        pad = jnp.zeros((d - pool_scale.shape[1],), F32)
        vec = jnp.stack([norm1_g[l], conv_b[l], b_rg_a[l], b_rg_x[l], lru_lambda[l],
                         norm2_g[l], jnp.concatenate([pool_scale[l], pad]),
                         jnp.zeros((d,), F32)])
        xcur, h_packed, idx, pos, wts, counts = _mixer_layer(
            xcur, pending, mod[l], vec, w_in[l].astype(BF16), conv_w[l], w_rg[l],
            w_pool[l].astype(BF16), w_up_a[l].astype(BF16), w_up_b[l].astype(BF16),
            w_out[l].astype(BF16), _padded_router_weight(w_router[l]), b_router[l])
        y4 = _moe_layer(l, h_packed, idx, pos, counts[:, 0], w_e_in, b_e_in, w_e_out, b_e_out)
        pending = (y4, wts.T, mod[l])
    return _final_norm(xcur, pending, final_g)
```

```python
import functools
import math

import jax
import jax.numpy as jnp
from jax import lax
from jax.experimental import pallas as pl
from jax.experimental.pallas import tpu as pltpu
from jax.experimental.pallas import tpu_sc as plsc

D_MODEL = 1024
BATCH = 8
N_HEADS = 8
HEAD_DIM = 128
CONV_WIDTH = 4
LRU_C = 8.0
POOL_WINDOWS = (2, 4, 8, 16)
POOL_WIDTH = 512
POOL_GROUP_DIM = 128
N_EXPERTS = 32
TOP_K = 4
D_FF = 1024
SWIGLU_LIMIT = 7.0
SWIGLU_ALPHA = 1.702
NORM_EPS = 1e-6

SUBLANES = 8
VMEM_LIMIT = 56 * 1024 * 1024
SC_CORES = 2
SC_SUBCORES = 16
SC_WORKERS = SC_CORES * SC_SUBCORES
SC_CHUNK = 64

MIX_STEPS = 32
ROUTER_COLS = 128
EXPERT_ROWS = 1024
EXPERT_SUB = 512
EXPERT_TAIL = 256
EXPERT_FULL_PASS = 256
ADA_COLS = 1536
NORM_ROWS = 512
CAST_ROWS = 128
PACKED = D_MODEL // 2

F32 = jnp.float32
BF16 = jnp.bfloat16
I32 = jnp.int32
HIGH_MASK = -65536
LOG2_E = math.log2(math.e)


def _const_spec(shape):
    nd = len(shape)
    return pl.BlockSpec(shape, lambda *_: (0,) * nd, pipeline_mode=pl.Buffered(1))


def _pack_rows(v):
    packed = pltpu.pack_elementwise([v[:, :PACKED], v[:, PACKED:]], packed_dtype=BF16)
    return lax.bitcast_convert_type(packed, I32)


def _unpack_rows(p):
    lo = lax.bitcast_convert_type(lax.shift_left(p, 16), F32)
    hi = lax.bitcast_convert_type(p & HIGH_MASK, F32)
    return lo, hi


def _sigmoid(v):
    return 0.5 * jnp.tanh(0.5 * v) + 0.5


def _per_batch(v, slab, op):
    rows, d = v.shape
    return op(v.reshape(rows // BATCH, BATCH, d), slab[None]).reshape(rows, d)


def _modulated_norm(x, g, shift, scale):
    ms = jnp.mean(x * x, axis=-1, keepdims=True)
    hn = x * lax.rsqrt(ms + NORM_EPS)
    return _per_batch(_per_batch(hn, g * (1.0 + scale), jnp.multiply), shift, jnp.add)


def _ada_kernel(c_ref, w_ref, b_ref, o_ref):
    c = c_ref[...]
    c_act = c * jax.nn.sigmoid(c)
    o_ref[0] = jnp.dot(c_act, w_ref[0], preferred_element_type=F32,
                       precision=lax.Precision.HIGHEST) + b_ref[0]


def _ada_mod(c, w_ada, b_ada):
    depth, d, n = w_ada.shape
    bsz = c.shape[0]
    tn = ADA_COLS
    return pl.pallas_call(
        _ada_kernel,
        out_shape=jax.ShapeDtypeStruct((depth, bsz, n), F32),
        grid=(depth, n // tn),
        in_specs=[
            pl.BlockSpec((bsz, d), lambda l, j: (0, 0)),
            pl.BlockSpec((1, d, tn), lambda l, j: (l, 0, j)),
            pl.BlockSpec((1, 1, tn), lambda l, j: (l, 0, j)),
        ],
        out_specs=pl.BlockSpec((1, bsz, tn), lambda l, j: (l, 0, j)),
        compiler_params=pltpu.CompilerParams(
            dimension_semantics=("arbitrary", "arbitrary"),
            vmem_limit_bytes=VMEM_LIMIT),
        name="ada_mod",
    )(c, w_ada, b_ada.reshape(depth, 1, n))


def _delay(a, steps, carry_ref):
    n = steps * BATCH
    keep = carry_ref.shape[0]
    prev = carry_ref[keep - n:, :]
    return jnp.concatenate([prev, a[:a.shape[0] - n]], axis=0)


def _save_tail(a, carry_ref):
    carry_ref[...] = a[a.shape[0] - carry_ref.shape[0]:]


def _combined_rows(x_ref, y_refs, wt_ref, gate2, rs=slice(None)):
    x = x_ref[rs, :]
    wt = wt_ref[rs, :]
    acc_lo = jnp.zeros((x.shape[0], PACKED), F32)
    acc_hi = jnp.zeros((x.shape[0], PACKED), F32)
    for k, y_ref in enumerate(y_refs):
        lo, hi = _unpack_rows(y_ref[rs, :])
        wk = wt[:, k:k + 1]
        acc_lo = acc_lo + wk * lo
        acc_hi = acc_hi + wk * hi
    ffn = jnp.concatenate([acc_lo, acc_hi], axis=-1)
    return x + _per_batch(ffn, gate2, jnp.multiply)


def _route_rows(padded, brc_ref, run_c, live, cols, idx_ref, pos_ref, wt_ref):
    rows = padded.shape[0]
    logits = padded.T[:N_EXPERTS] + brc_ref[...]

    e_iota = lax.broadcasted_iota(I32, (N_EXPERTS, rows), 0)
    vals, idxs, hots = [], [], []
    cur = logits
    for _ in range(TOP_K):
        m = jnp.max(cur, axis=0, keepdims=True)
        am = jnp.min(jnp.where(cur == m, e_iota, N_EXPERTS), axis=0, keepdims=True)
        hot = e_iota == am
        vals.append(m)
        idxs.append(am)
        hots.append(hot)
        cur = jnp.where(hot, -jnp.inf, cur)
    exps = [jnp.exp(v - vals[0]) for v in vals]
    denom = exps[0] + exps[1] + exps[2] + exps[3]

    sel = (hots[0] | hots[1] | hots[2] | hots[3]).astype(F32)
    ri = lax.broadcasted_iota(I32, (rows, rows), 0)
    ci = lax.broadcasted_iota(I32, (rows, rows), 1)
    earlier = (ri < ci).astype(BF16)
    before = jnp.dot(sel.astype(BF16), earlier, preferred_element_type=F32) + run_c[...]
    run_c[...] = run_c[...] + jnp.where(live, jnp.sum(sel, axis=1, keepdims=True), 0.0)
    poss = [jnp.sum(jnp.where(hot, before, 0.0), axis=0, keepdims=True).astype(I32)
            for hot in hots]
    idx_ref[:, cols] = jnp.concatenate(idxs, axis=0)
    pos_ref[:, cols] = jnp.concatenate(poss, axis=0)
    wt_ref[:, cols] = jnp.concatenate([e / denom for e in exps], axis=0)


VEC_G1, VEC_CONV_B, VEC_BRG_A, VEC_BRG_X, VEC_LAM, VEC_G2, VEC_PSCALE = range(7)


def _mixer_kernel(with_combine, *refs):
    refs = list(refs)
    n_in = 1 + (TOP_K + 1 if with_combine else 0)
    first, nxt = refs[:n_in], refs[n_in:2 * n_in]
    refs = refs[2 * n_in:]
    if with_combine:
        modp_ref = refs.pop(0)
    (mod_ref, vec_ref, win_ref, convw_ref, wrg_ref, wpool_ref, wupa_ref, wupb_ref, wout_ref,
     wrt_ref, brc_ref,
     o_ref, hp_ref, idx_ref, pos_ref, wt_ref, cnt_ref,
     conv_c, p1_c, p2_c, p4_c, p8_c, h_c, run_c, logit_c, x_c, hb_c, xl_c) = refs
    i = pl.program_id(0)
    last = pl.num_programs(0) - 1
    cur = i % 2
    rows = o_ref.shape[0]
    steps = rows // BATCH
    d = D_MODEL
    vec = vec_ref[...]
    row_of = lambda k: vec[k:k + 1]
    shift1, scale1, gate1 = mod_ref[0], mod_ref[1], mod_ref[2]

    def prologue(in_refs):
        if with_combine:
            x = _combined_rows(in_refs[0], in_refs[1:1 + TOP_K], in_refs[1 + TOP_K],
                               modp_ref[5])
        else:
            x = pltpu.einshape("btd->(tb)d", in_refs[0][...])
        return x, _modulated_norm(x, row_of(VEC_G1), shift1, scale1).astype(BF16)

    @pl.when(i == 0)
    def _():
        for ref in (conv_c, p1_c, p2_c, p4_c, p8_c, h_c, run_c, logit_c):
            ref[...] = jnp.zeros(ref.shape, ref.dtype)
        x_c[0], hb_c[0] = prologue(first)
        xl_c[...] = jnp.dot(hb_c[0], win_ref[:, 0:d], preferred_element_type=F32)

    def route(padded, block, live):
        cols = pl.ds(pl.multiple_of(block * rows, rows), rows)
        _route_rows(padded, brc_ref, run_c, live, cols, idx_ref, pos_ref, wt_ref)

    def in_proj(lo, hi):
        return jnp.dot(hb_c[cur], win_ref[:, lo:hi], preferred_element_type=F32)

    x_lru = xl_c[...]
    cw = convw_ref[...]
    xc = (row_of(VEC_CONV_B) + cw[3:4] * x_lru
          + cw[2:3] * _delay(x_lru, 1, conv_c)
          + cw[1:2] * _delay(x_lru, 2, conv_c)
          + cw[0:1] * _delay(x_lru, 3, conv_c))
    _save_tail(x_lru, conv_c)
    xcb = xc.astype(BF16)
    pre = [jnp.dot(xcb[:, k * HEAD_DIM:(k + 1) * HEAD_DIM], wrg_ref[k],
                   preferred_element_type=F32) for k in range(N_HEADS)]
    pre_a = jnp.concatenate([p[:, :HEAD_DIM] for p in pre], axis=-1)
    pre_x = jnp.concatenate([p[:, HEAD_DIM:] for p in pre], axis=-1)
    r = _sigmoid(pre_a + row_of(VEC_BRG_A))
    gi = _sigmoid(pre_x + row_of(VEC_BRG_X))
    rate = (-LRU_C * LOG2_E) * jax.nn.softplus(-row_of(VEC_LAM))
    a = jnp.exp2(r * rate)
    y = 1.0 - a * a
    mult = jnp.where(y > 0.0, y * lax.rsqrt(y), 0.0)
    u = mult * (gi * xc)
    h = h_c[...]
    hs = []
    for t in range(steps):
        sl = slice(t * BATCH, (t + 1) * BATCH)
        h = a[sl] * h + u[sl]
        hs.append(h)
    h_c[...] = h
    y_lru = jnp.concatenate(hs, axis=0) * jax.nn.gelu(in_proj(d, 2 * d))
    p_a = jnp.dot(y_lru.astype(BF16), wupa_ref[...], preferred_element_type=F32)
    merged = _sigmoid(in_proj(2 * d + POOL_WIDTH, 3 * d + POOL_WIDTH)) * p_a

    route(logit_c[...], jnp.maximum(i - 1, 0), i > 0)
    x_c[1 - cur], hb_c[1 - cur] = prologue(nxt)

    x_pool = in_proj(2 * d, 2 * d + POOL_WIDTH)
    g = POOL_GROUP_DIM
    s2 = x_pool + _delay(x_pool, 1, p1_c)
    s4 = s2[:, g:] + _delay(s2[:, g:], 2, p2_c)
    s8 = s4[:, g:] + _delay(s4[:, g:], 4, p4_c)
    s16 = s8[:, g:] + _delay(s8[:, g:], 8, p8_c)
    _save_tail(x_pool, p1_c)
    _save_tail(s2[:, g:], p2_c)
    _save_tail(s4[:, g:], p4_c)
    _save_tail(s8[:, g:], p8_c)
    wins = (s2[:, :g], s4[:, :g], s8[:, :g], s16)
    row = lax.broadcasted_iota(I32, (rows, 1), 0)
    t1 = (i * steps + row // BATCH + 1).astype(F32)
    pooled = []
    for k, w in enumerate(POOL_WINDOWS):
        cnt = jnp.minimum(t1, float(w))
        pk = wins[k] / cnt - x_pool[:, k * g:(k + 1) * g]
        pooled.append(jnp.dot(pk.astype(BF16), wpool_ref[k], preferred_element_type=F32))
    y_pool = jnp.concatenate(pooled, axis=-1) * row_of(VEC_PSCALE)[:, :POOL_WIDTH]
    p_b = jnp.dot(y_pool.astype(BF16), wupb_ref[...], preferred_element_type=F32)
    merged = merged + _sigmoid(in_proj(3 * d + POOL_WIDTH, 4 * d + POOL_WIDTH)) * p_b

    mix = jnp.dot(merged.astype(BF16), wout_ref[...], preferred_element_type=F32)
    x_mid = x_c[cur] + _per_batch(mix, gate1, jnp.multiply)
    o_ref[...] = x_mid
    xl_c[...] = jnp.dot(hb_c[1 - cur], win_ref[:, 0:d], preferred_element_type=F32)

    hr = _modulated_norm(x_mid, row_of(VEC_G2), mod_ref[3], mod_ref[4])
    hp_ref[...] = _pack_rows(hr)
    padded = jnp.dot(hr.astype(BF16), wrt_ref[...], preferred_element_type=F32)
    logit_c[...] = padded

    @pl.when(i == last)
    def _():
        route(logit_c[...], i, True)

    cnt_ref[...] = run_c[...].astype(I32)


def _pending_specs(rows, n_tok, block_of):
    nb = n_tok // rows
    y_spec = lambda k: pl.BlockSpec((rows, PACKED), lambda i: (k * nb + block_of(i), 0))
    return ([y_spec(k) for k in range(TOP_K)]
            + [pl.BlockSpec((rows, TOP_K), lambda i: (block_of(i), 0))])


def _mixer_layer(x_in, pending, mod_l, vec, w_in, conv_w, w_rg, w_pool, w_up_a, w_up_b, w_out,
                 w_route, b_router):
    rows = MIX_STEPS * BATCH
    weights = [mod_l, vec, w_in, conv_w, w_rg, w_pool, w_up_a, w_up_b, w_out, w_route,
               b_router.reshape(N_EXPERTS, 1)]
    if pending is None:
        n_tok, d = x_in.shape[0] * x_in.shape[1], x_in.shape[2]
        inputs = [x_in]
        specs = lambda block_of: [
            pl.BlockSpec((BATCH, MIX_STEPS, d), lambda i: (0, block_of(i), 0))]
    else:
        n_tok, d = x_in.shape
        y4, wts_prev, mod_prev = pending
        inputs = [x_in, y4, y4, y4, y4, wts_prev]
        weights.insert(0, mod_prev)
        specs = lambda block_of: ([pl.BlockSpec((rows, d), lambda i: (block_of(i), 0))]
                                  + _pending_specs(rows, n_tok, block_of))
    nb = n_tok // rows
    in_specs = (specs(lambda i: 0) + specs(lambda i: jnp.minimum(i + 1, nb - 1))
                + [_const_spec(a.shape) for a in weights])
    g = POOL_GROUP_DIM
    scratch = [
        pltpu.VMEM(((CONV_WIDTH - 1) * BATCH, d), F32),
        pltpu.VMEM((1 * BATCH, 4 * g), F32), pltpu.VMEM((2 * BATCH, 3 * g), F32),
        pltpu.VMEM((4 * BATCH, 2 * g), F32), pltpu.VMEM((8 * BATCH, g), F32),
        pltpu.VMEM((BATCH, d), F32),
        pltpu.VMEM((N_EXPERTS, 1), F32),
        pltpu.VMEM((rows, ROUTER_COLS), F32),
        pltpu.VMEM((2, rows, d), F32), pltpu.VMEM((2, rows, d), BF16),
        pltpu.VMEM((rows, d), F32),
    ]
    routed = lambda dtype: jax.ShapeDtypeStruct((TOP_K, n_tok), dtype)
    whole = lambda shape: pl.BlockSpec(shape, lambda i: (0,) * len(shape))
    routed_spec = whole((TOP_K, n_tok))
    return pl.pallas_call(
        functools.partial(_mixer_kernel, pending is not None),
        out_shape=(
            jax.ShapeDtypeStruct((n_tok, d), F32),
            jax.ShapeDtypeStruct((n_tok, PACKED), I32),
            routed(I32), routed(I32), routed(F32),
            jax.ShapeDtypeStruct((N_EXPERTS, 1), I32),
        ),
        grid=(nb,),
        in_specs=in_specs,
        out_specs=(
            pl.BlockSpec((rows, d), lambda i: (i, 0)),
            pl.BlockSpec((rows, PACKED), lambda i: (i, 0)),
            routed_spec, routed_spec, routed_spec,
            whole((N_EXPERTS, 1)),
        ),
        scratch_shapes=scratch,
        compiler_params=pltpu.CompilerParams(
            dimension_semantics=("arbitrary",), vmem_limit_bytes=VMEM_LIMIT),
        name="mixer",
    )(*(inputs + inputs + weights))


def _padded_router_weight(w_router):
    pad = jnp.zeros((w_router.shape[0], ROUTER_COLS - N_EXPERTS), BF16)
    return jnp.concatenate([w_router.astype(BF16), pad], axis=1)


def _sc_mesh():
    return plsc.VectorSubcoreMesh(core_axis_name="c", subcore_axis_name="s")


def _sc_worker_id():
    return lax.axis_index("s") * SC_CORES + lax.axis_index("c")


def _sc_dispatch(h_packed, dest_sc, n_slots):
    n_tok, width = h_packed.shape
    per_w = n_tok // SC_WORKERS
    n_chunks = per_w // SC_CHUNK

    def body(h_hbm, dest_hbm, xs_hbm, idx_v, rows_v, sem_in, sem_out):
        base = _sc_worker_id() * per_w
        pltpu.sync_copy(dest_hbm.at[_sc_worker_id()], idx_v)

        def load(j, buf):
            return pltpu.make_async_copy(
                h_hbm.at[pl.ds(base + j * SC_CHUNK, SC_CHUNK)], rows_v.at[buf], sem_in)

        load(0, 0).start()

        @pl.loop(0, n_chunks, step=2)
        def _(j0):
            for buf in range(2):
                j = j0 + buf
                load(j, buf).wait()

                @pl.when(j + 1 < n_chunks)
                def _():
                    load(j + 1, 1 - buf).start()

                copies = [
                    pltpu.make_async_copy(
                        rows_v.at[buf], xs_hbm.at[idx_v.at[j * TOP_K + k]], sem_out)
                    for k in range(TOP_K)]
                for cp in copies:
                    cp.start()
                for cp in copies:
                    cp.wait()

    return pl.kernel(
        body,
        out_type=jax.ShapeDtypeStruct((n_slots, width), I32),
        mesh=_sc_mesh(),
        scratch_types=[
            pltpu.VMEM((n_chunks * TOP_K, SC_CHUNK), I32),
            pltpu.VMEM((2, SC_CHUNK, width), I32),
            pltpu.SemaphoreType.DMA,
            pltpu.SemaphoreType.DMA,
        ],
        name="sc_dispatch",
    )(h_packed, dest_sc)


def _sc_gather(ys_packed, dest_sc):
    width = ys_packed.shape[1]
    n_chunks = dest_sc.shape[1]
    per_w = n_chunks * SC_CHUNK
    n_rows = SC_WORKERS * per_w

    def body(ys_hbm, dest_hbm, out_hbm, idx_v, rows_v, sem_g, sem_w):
        base = _sc_worker_id() * per_w
        pltpu.sync_copy(dest_hbm.at[_sc_worker_id()], idx_v)

        def gather(j, buf):
            return pltpu.make_async_copy(ys_hbm.at[idx_v.at[j]], rows_v.at[buf], sem_g)

        def writeout(j, buf):
            return pltpu.make_async_copy(
                rows_v.at[buf], out_hbm.at[pl.ds(base + j * SC_CHUNK, SC_CHUNK)], sem_w)

        gather(0, 0).start()

        @pl.loop(0, n_chunks, step=2)
        def _(j0):
            for buf in range(2):
                j = j0 + buf
                gather(j, buf).wait()

                @pl.when(j >= 1)
                def _():
                    writeout(j - 1, 1 - buf).wait()

                @pl.when(j + 1 < n_chunks)
                def _():
                    gather(j + 1, 1 - buf).start()

                writeout(j, buf).start()

        writeout(n_chunks - 1, 1).wait()

    return pl.kernel(
        body,
        out_type=jax.ShapeDtypeStruct((n_rows, width), I32),
        mesh=_sc_mesh(),
        scratch_types=[
            pltpu.VMEM((n_chunks, SC_CHUNK), I32),
            pltpu.VMEM((2, SC_CHUNK, width), I32),
            pltpu.SemaphoreType.DMA,
            pltpu.SemaphoreType.DMA,
        ],
        name="sc_gather",
    )(ys_packed, dest_sc)


def _expert_kernel(layer, be_ref, nv_ref, first_ref, next_ref, xs_ref, win_hbm, bin_ref,
                   wout_hbm, bout_ref, ys_ref, win_f32, wout_f32, win_bf, wout_bf, sem):
    i = pl.program_id(0)
    valid = nv_ref[i]

    def fetch(e):
        return (pltpu.make_async_copy(win_hbm.at[layer, e], win_f32, sem.at[0]),
                pltpu.make_async_copy(wout_hbm.at[layer, e], wout_f32, sem.at[1]))

    @pl.when(first_ref[i] == 1)
    def _():
        @pl.when(i == 0)
        def _():
            for cp in fetch(be_ref[i]):
                cp.start()

        for cp in fetch(be_ref[i]):
            cp.wait()

        def cast_in(r, c):
            rows = pl.ds(pl.multiple_of(r * CAST_ROWS, CAST_ROWS), CAST_ROWS)
            win_bf[rows, :] = win_f32[rows, :].astype(BF16)
            return c

        def cast_out(r, c):
            rows = pl.ds(pl.multiple_of(r * CAST_ROWS, CAST_ROWS), CAST_ROWS)
            wout_bf[rows, :] = wout_f32[rows, :].astype(BF16)
            return c

        lax.fori_loop(0, D_MODEL // CAST_ROWS, cast_in, 0)
        lax.fori_loop(0, D_FF // CAST_ROWS, cast_out, 0)

        @pl.when(next_ref[i] >= 0)
        def _():
            for cp in fetch(next_ref[i]):
                cp.start()

    def load_rows(start, size):
        rows = pl.ds(pl.multiple_of(start, EXPERT_TAIL), size)
        row = lax.broadcasted_iota(I32, (size, 1), 0)
        lo, hi = _unpack_rows(jnp.where(row + start < valid, xs_ref[rows, :], 0))
        return lo.astype(BF16), hi.astype(BF16)

    def up_proj(lo, hi):
        return (jnp.dot(lo, win_bf[:PACKED, :], preferred_element_type=F32)
                + jnp.dot(hi, win_bf[PACKED:, :], preferred_element_type=F32) + bin_ref[0])

    def activation(gu):
        gate = jnp.minimum(gu[:, :D_FF], SWIGLU_LIMIT)
        up = jnp.clip(gu[:, D_FF:], -SWIGLU_LIMIT, SWIGLU_LIMIT)
        return (gate * _sigmoid(SWIGLU_ALPHA * gate) * (up + 1.0)).astype(BF16)

    def down_proj(act):
        return jnp.dot(act, wout_bf[...], preferred_element_type=F32) + bout_ref[0]

    def store_rows(start, size, out):
        rows = pl.ds(pl.multiple_of(start, EXPERT_TAIL), size)
        ys_ref[rows, :] = _pack_rows(out)

    def compute(start, size):
        store_rows(start, size, down_proj(activation(up_proj(*load_rows(start, size)))))

    n_main = (valid + (EXPERT_SUB - EXPERT_TAIL) - 1) // EXPERT_SUB
    main_rows = n_main * EXPERT_SUB
    assert EXPERT_ROWS == 2 * EXPERT_SUB

    @pl.when(n_main == 2)
    def _():
        size = EXPERT_FULL_PASS
        starts = range(0, EXPERT_ROWS, size)
        gus, outs = {}, {}
        for n, start in enumerate(starts):
            gus[start] = up_proj(*load_rows(start, size))
            if n > 0:
                prev = starts[n - 1]
                outs[prev] = down_proj(activation(gus.pop(prev)))
            if n > 1:
                store_rows(starts[n - 2], size, outs.pop(starts[n - 2]))
        outs[starts[-1]] = down_proj(activation(gus.pop(starts[-1])))
        for start in list(outs):
            store_rows(start, size, outs.pop(start))

    @pl.when(n_main == 1)
    def _():
        compute(0, EXPERT_SUB)

    @pl.when((n_main < 2) & (valid > main_rows))
    def _():
        compute(main_rows, EXPERT_TAIL)


    def clear(j, c):
        rows = pl.ds(pl.multiple_of(j * EXPERT_TAIL, EXPERT_TAIL), EXPERT_TAIL)
        ys_ref[rows, :] = jnp.zeros((EXPERT_TAIL, ys_ref.shape[1]), ys_ref.dtype)
        return c

    done = (main_rows + jnp.where(valid > main_rows, EXPERT_TAIL, 0)) // EXPERT_TAIL
    lax.fori_loop(done, EXPERT_ROWS // EXPERT_TAIL, clear, 0)


def _experts(layer, block_e, n_valid, is_first, next_e, xs, w_e_in, b_e_in, w_e_out, b_e_out):
    n_slots, width = xs.shape
    d = D_MODEL
    tm = EXPERT_ROWS
    bias_index = lambda i, be, nv, fi, nx: (layer, be[i], 0, 0)
    grid_spec = pltpu.PrefetchScalarGridSpec(
        num_scalar_prefetch=4,
        grid=(n_slots // tm,),
        in_specs=[
            pl.BlockSpec((tm, width), lambda i, be, nv, fi, nx: (i, 0)),
            pl.BlockSpec(memory_space=pl.ANY),
            pl.BlockSpec((None, 1, 1, 2 * D_FF), bias_index),
            pl.BlockSpec(memory_space=pl.ANY),
            pl.BlockSpec((None, 1, 1, d), bias_index),
        ],
        out_specs=pl.BlockSpec((tm, width), lambda i, be, nv, fi, nx: (i, 0)),
        scratch_shapes=[
            pltpu.VMEM((d, 2 * D_FF), F32), pltpu.VMEM((D_FF, d), F32),
            pltpu.VMEM((d, 2 * D_FF), BF16), pltpu.VMEM((D_FF, d), BF16),
            pltpu.SemaphoreType.DMA((2,)),
        ],
    )
    return pl.pallas_call(
        functools.partial(_expert_kernel, layer),
        out_shape=jax.ShapeDtypeStruct((n_slots, width), I32),
        grid_spec=grid_spec,
        compiler_params=pltpu.CompilerParams(
            dimension_semantics=("arbitrary",), vmem_limit_bytes=VMEM_LIMIT),
        name="experts",
    )(block_e, n_valid, is_first, next_e, xs, w_e_in,
      b_e_in.reshape(-1, N_EXPERTS, 1, 2 * D_FF), w_e_out, b_e_out.reshape(-1, N_EXPERTS, 1, d))


def _final_kernel(x_ref, y0_ref, y1_ref, y2_ref, y3_ref, wt_ref, mod_ref, g_ref, o_ref):
    x = _combined_rows(x_ref, (y0_ref, y1_ref, y2_ref, y3_ref), wt_ref, mod_ref[5])
    ms = jnp.mean(x * x, axis=-1, keepdims=True)
    out = x * lax.rsqrt(ms + NORM_EPS) * g_ref[...]
    o_ref[...] = pltpu.einshape("(tb)d->btd", out, b=BATCH)


def _final_norm(x2d, pending, g):
    n_tok, d = x2d.shape
    tr = NORM_ROWS
    y4, wts, mod_l = pending
    return pl.pallas_call(
        _final_kernel,
        out_shape=jax.ShapeDtypeStruct((BATCH, n_tok // BATCH, d), F32),
        grid=(n_tok // tr,),
        in_specs=([pl.BlockSpec((tr, d), lambda i: (i, 0))]
                  + _pending_specs(tr, n_tok, lambda i: i)
                  + [_const_spec(mod_l.shape), _const_spec((1, d))]),
        out_specs=pl.BlockSpec((BATCH, tr // BATCH, d), lambda i: (0, i, 0)),
        compiler_params=pltpu.CompilerParams(
            dimension_semantics=("arbitrary",), vmem_limit_bytes=VMEM_LIMIT),
        name="final_norm",
    )(x2d, y4, y4, y4, y4, wts, mod_l, g.reshape(1, d))


def _moe_layer(layer, h_packed, idx, pos, counts, w_e_in, b_e_in, w_e_out, b_e_out):
    n_tok = h_packed.shape[0]
    tm = EXPERT_ROWS
    padded = ((counts + tm - 1) // tm) * tm
    pad_end = jnp.cumsum(padded)
    start_pad = pad_end - padded
    experts = jnp.arange(N_EXPERTS, dtype=I32)
    dest = jnp.sum(jnp.where(idx[:, :, None] == experts, start_pad, 0), axis=-1) + pos
    n_blocks = (n_tok * TOP_K) // tm + N_EXPERTS
    n_slots = n_blocks * tm
    block_start = jnp.arange(n_blocks, dtype=I32) * tm
    block_e = jnp.minimum(
        jnp.sum((block_start[:, None] >= pad_end[None, :]).astype(I32), axis=-1),
        N_EXPERTS - 1)
    hot_e = block_e[:, None] == experts[None, :]
    rows_left = jnp.sum(jnp.where(hot_e, (start_pad + counts)[None, :], 0), axis=-1) - block_start
    n_valid = jnp.where(block_start < pad_end[-1], jnp.clip(rows_left, 0, tm), 0).astype(I32)
    is_first = jnp.sum(jnp.where(hot_e, start_pad[None, :], 0), axis=-1) == block_start
    is_first = (is_first & (n_valid > 0)).astype(I32)
    later = (experts[None, :] > experts[:, None]) & (counts[None, :] > 0)
    next_of = jnp.min(jnp.where(later, experts[None, :], N_EXPERTS), axis=-1)
    next_of = jnp.where(next_of == N_EXPERTS, -1, next_of)
    next_e = jnp.sum(jnp.where(hot_e, next_of[None, :], 0), axis=-1).astype(I32)

    per_w = n_tok // SC_WORKERS
    n_chunks = per_w // SC_CHUNK
    dest_scatter = (dest.reshape(TOP_K, SC_WORKERS, n_chunks, SC_CHUNK)
                    .transpose(1, 2, 0, 3)
                    .reshape(SC_WORKERS, n_chunks * TOP_K, SC_CHUNK))
    xs = _sc_dispatch(h_packed, dest_scatter, n_slots)
    ys = _experts(layer, block_e, n_valid, is_first, next_e, xs,
                  w_e_in, b_e_in, w_e_out, b_e_out)
    dest_gather = dest.reshape(SC_WORKERS, n_chunks * TOP_K, SC_CHUNK)
    return _sc_gather(ys, dest_gather)


def kernel(x, c, norm1_g, norm2_g, w_ada, b_ada, w_in, conv_w, conv_b, w_rg_a, b_rg_a,
           w_rg_x, b_rg_x, lru_lambda, w_pool, pool_scale, w_up_a, w_up_b, w_out,
           w_router, b_router, w_e_in, b_e_in, w_e_out, b_e_out, final_g):
    depth = w_in.shape[0]
    bsz, seq, d = x.shape
    assert bsz == BATCH and d == D_MODEL
    mod = _ada_mod(c, w_ada, b_ada).reshape(depth, bsz, 6, d).transpose(0, 2, 1, 3)
    w_rg = jnp.concatenate([w_rg_a, w_rg_x], axis=-1).astype(BF16)
    pending = None
    xcur = x
    for l in range(depth):
        pad = jnp.zeros((d - pool_scale.shape[1],), F32)
        vec = jnp.stack([norm1_g[l], conv_b[l], b_rg_a[l], b_rg_x[l], lru_lambda[l],
                         norm2_g[l], jnp.concatenate([pool_scale[l], pad]),
                         jnp.zeros((d,), F32)])
        xcur, h_packed, idx, pos, wts, counts = _mixer_layer(
            xcur, pending, mod[l], vec, w_in[l].astype(BF16), conv_w[l], w_rg[l],
            w_pool[l].astype(BF16), w_up_a[l].astype(BF16), w_up_b[l].astype(BF16),
            w_out[l].astype(BF16), _padded_router_weight(w_router[l]), b_router[l])
        y4 = _moe_layer(l, h_packed, idx, pos, counts[:, 0], w_e_in, b_e_in, w_e_out, b_e_out)
        pending = (y4, wts.T, mod[l])
    return _final_norm(xcur, pending, final_g)
```

```python
import functools
import math

import jax
import jax.numpy as jnp
from jax import lax
from jax.experimental import pallas as pl
from jax.experimental.pallas import tpu as pltpu
from jax.experimental.pallas import tpu_sc as plsc

D_MODEL = 1024
BATCH = 8
N_HEADS = 8
HEAD_DIM = 128
CONV_WIDTH = 4
LRU_C = 8.0
POOL_WINDOWS = (2, 4, 8, 16)
POOL_WIDTH = 512
POOL_GROUP_DIM = 128
N_EXPERTS = 32
TOP_K = 4
D_FF = 1024
SWIGLU_LIMIT = 7.0
SWIGLU_ALPHA = 1.702
NORM_EPS = 1e-6

SUBLANES = 8
VMEM_LIMIT = 56 * 1024 * 1024
SC_CORES = 2
SC_SUBCORES = 16
SC_WORKERS = SC_CORES * SC_SUBCORES
SC_CHUNK = 64

MIX_STEPS = 32
ROUTER_COLS = 128
EXPERT_ROWS = 1024
EXPERT_PASS = 256
ADA_COLS = 3072
NORM_ROWS = 512
CAST_ROWS = 128
PACKED = D_MODEL // 2

F32 = jnp.float32
BF16 = jnp.bfloat16
I32 = jnp.int32
HIGH_MASK = -65536
LOG2_E = math.log2(math.e)


def _const_spec(shape):
    nd = len(shape)
    return pl.BlockSpec(shape, lambda *_: (0,) * nd, pipeline_mode=pl.Buffered(1))


def _pack_rows(v):
    packed = pltpu.pack_elementwise([v[:, :PACKED], v[:, PACKED:]], packed_dtype=BF16)
    return lax.bitcast_convert_type(packed, I32)


def _unpack_rows(p):
    lo = lax.bitcast_convert_type(lax.shift_left(p, 16), F32)
    hi = lax.bitcast_convert_type(p & HIGH_MASK, F32)
    return lo, hi


def _sigmoid(v):
    return 0.5 * jnp.tanh(0.5 * v) + 0.5


def _per_batch(v, slab, op):
    rows, d = v.shape
    return op(v.reshape(rows // BATCH, BATCH, d), slab[None]).reshape(rows, d)


def _modulated_norm(x, g, shift, scale):
    ms = jnp.mean(x * x, axis=-1, keepdims=True)
    hn = x * lax.rsqrt(ms + NORM_EPS)
    return _per_batch(_per_batch(hn, g * (1.0 + scale), jnp.multiply), shift, jnp.add)


def _ada_kernel(c_ref, w_ref, b_ref, o_ref):
    c = c_ref[...]
    c_act = c * jax.nn.sigmoid(c)
    o_ref[0] = jnp.dot(c_act, w_ref[0], preferred_element_type=F32,
                       precision=lax.Precision.HIGHEST) + b_ref[0]


def _ada_mod(c, w_ada, b_ada):
    depth, d, n = w_ada.shape
    bsz = c.shape[0]
    tn = ADA_COLS
    return pl.pallas_call(
        _ada_kernel,
        out_shape=jax.ShapeDtypeStruct((depth, bsz, n), F32),
        grid=(depth, n // tn),
        in_specs=[
            pl.BlockSpec((bsz, d), lambda l, j: (0, 0)),
            pl.BlockSpec((1, d, tn), lambda l, j: (l, 0, j)),
            pl.BlockSpec((1, 1, tn), lambda l, j: (l, 0, j)),
        ],
        out_specs=pl.BlockSpec((1, bsz, tn), lambda l, j: (l, 0, j)),
        compiler_params=pltpu.CompilerParams(
            dimension_semantics=("arbitrary", "arbitrary"),
            vmem_limit_bytes=VMEM_LIMIT),
        name="ada_mod",
    )(c, w_ada, b_ada.reshape(depth, 1, n))


def _delay(a, steps, carry_ref):
    n = steps * BATCH
    keep = carry_ref.shape[0]
    prev = carry_ref[keep - n:, :]
    return jnp.concatenate([prev, a[:a.shape[0] - n]], axis=0)


def _save_tail(a, carry_ref):
    carry_ref[...] = a[a.shape[0] - carry_ref.shape[0]:]


def _combined_rows(x_ref, y_refs, wt_ref, gate2, rs=slice(None)):
    x = x_ref[rs, :]
    wt = wt_ref[rs, :]
    acc_lo = jnp.zeros((x.shape[0], PACKED), F32)
    acc_hi = jnp.zeros((x.shape[0], PACKED), F32)
    for k, y_ref in enumerate(y_refs):
        lo, hi = _unpack_rows(y_ref[rs, :])
        wk = wt[:, k:k + 1]
        acc_lo = acc_lo + wk * lo
        acc_hi = acc_hi + wk * hi
    ffn = jnp.concatenate([acc_lo, acc_hi], axis=-1)
    return x + _per_batch(ffn, gate2, jnp.multiply)


def _route_rows(padded, brc_ref, run_c, live, cols, idx_ref, pos_ref, wt_ref):
    rows = padded.shape[0]
    logits = padded.T[:N_EXPERTS] + brc_ref[...]

    e_iota = lax.broadcasted_iota(I32, (N_EXPERTS, rows), 0)
    vals, idxs, hots = [], [], []
    cur = logits
    for _ in range(TOP_K):
        m = jnp.max(cur, axis=0, keepdims=True)
        am = jnp.min(jnp.where(cur == m, e_iota, N_EXPERTS), axis=0, keepdims=True)
        hot = e_iota == am
        vals.append(m)
        idxs.append(am)
        hots.append(hot)
        cur = jnp.where(hot, -jnp.inf, cur)
    exps = [jnp.exp(v - vals[0]) for v in vals]
    denom = exps[0] + exps[1] + exps[2] + exps[3]

    sel = (hots[0] | hots[1] | hots[2] | hots[3]).astype(F32)
    ri = lax.broadcasted_iota(I32, (rows, rows), 0)
    ci = lax.broadcasted_iota(I32, (rows, rows), 1)
    earlier = (ri < ci).astype(BF16)
    before = jnp.dot(sel.astype(BF16), earlier, preferred_element_type=F32) + run_c[...]
    run_c[...] = run_c[...] + jnp.where(live, jnp.sum(sel, axis=1, keepdims=True), 0.0)
    poss = [jnp.sum(jnp.where(hot, before, 0.0), axis=0, keepdims=True).astype(I32)
            for hot in hots]
    idx_ref[:, cols] = jnp.concatenate(idxs, axis=0)
    pos_ref[:, cols] = jnp.concatenate(poss, axis=0)
    wt_ref[:, cols] = jnp.concatenate([e / denom for e in exps], axis=0)


VEC_G1, VEC_CONV_B, VEC_BRG_A, VEC_BRG_X, VEC_LAM, VEC_G2, VEC_PSCALE = range(7)


def _mixer_kernel(with_combine, *refs):
    refs = list(refs)
    n_in = 1 + (TOP_K + 1 if with_combine else 0)
    first, nxt = refs[:n_in], refs[n_in:2 * n_in]
    refs = refs[2 * n_in:]
    if with_combine:
        modp_ref = refs.pop(0)
    (mod_ref, vec_ref, win_ref, convw_ref, wrg_ref, wpool_ref, wupa_ref, wupb_ref, wout_ref,
     wrt_ref, brc_ref,
     o_ref, hp_ref, idx_ref, pos_ref, wt_ref, cnt_ref,
     conv_c, p1_c, p2_c, p4_c, p8_c, h_c, run_c, logit_c, x_c, hb_c, xl_c) = refs
    i = pl.program_id(0)
    last = pl.num_programs(0) - 1
    cur = i % 2
    rows = o_ref.shape[0]
    steps = rows // BATCH
    d = D_MODEL
    vec = vec_ref[...]
    row_of = lambda k: vec[k:k + 1]
    shift1, scale1, gate1 = mod_ref[0], mod_ref[1], mod_ref[2]

    def prologue(in_refs):
        if with_combine:
            x = _combined_rows(in_refs[0], in_refs[1:1 + TOP_K], in_refs[1 + TOP_K],
                               modp_ref[5])
        else:
            x = pltpu.einshape("btd->(tb)d", in_refs[0][...])
        return x, _modulated_norm(x, row_of(VEC_G1), shift1, scale1).astype(BF16)

    @pl.when(i == 0)
    def _():
        for ref in (conv_c, p1_c, p2_c, p4_c, p8_c, h_c, run_c, logit_c):
            ref[...] = jnp.zeros(ref.shape, ref.dtype)
        x_c[0], hb_c[0] = prologue(first)
        xl_c[...] = jnp.dot(hb_c[0], win_ref[:, 0:d], preferred_element_type=F32)

    def route(padded, block, live):
        cols = pl.ds(pl.multiple_of(block * rows, rows), rows)
        _route_rows(padded, brc_ref, run_c, live, cols, idx_ref, pos_ref, wt_ref)

    def in_proj(lo, hi):
        return jnp.dot(hb_c[cur], win_ref[:, lo:hi], preferred_element_type=F32)

    x_lru = xl_c[...]
    cw = convw_ref[...]
    xc = (row_of(VEC_CONV_B) + cw[3:4] * x_lru
          + cw[2:3] * _delay(x_lru, 1, conv_c)
          + cw[1:2] * _delay(x_lru, 2, conv_c)
          + cw[0:1] * _delay(x_lru, 3, conv_c))
    _save_tail(x_lru, conv_c)
    xcb = xc.astype(BF16)
    pre = [jnp.dot(xcb[:, k * HEAD_DIM:(k + 1) * HEAD_DIM], wrg_ref[k],
                   preferred_element_type=F32) for k in range(N_HEADS)]
    pre_a = jnp.concatenate([p[:, :HEAD_DIM] for p in pre], axis=-1)
    pre_x = jnp.concatenate([p[:, HEAD_DIM:] for p in pre], axis=-1)
    r = _sigmoid(pre_a + row_of(VEC_BRG_A))
    gi = _sigmoid(pre_x + row_of(VEC_BRG_X))
    rate = (-LRU_C * LOG2_E) * jax.nn.softplus(-row_of(VEC_LAM))
    a = jnp.exp2(r * rate)
    y = 1.0 - a * a
    mult = jnp.where(y > 0.0, y * lax.rsqrt(y), 0.0)
    u = mult * (gi * xc)
    h = h_c[...]
    hs = []
    for t in range(steps):
        sl = slice(t * BATCH, (t + 1) * BATCH)
        h = a[sl] * h + u[sl]
        hs.append(h)
    h_c[...] = h
    y_lru = jnp.concatenate(hs, axis=0) * jax.nn.gelu(in_proj(d, 2 * d))
    p_a = jnp.dot(y_lru.astype(BF16), wupa_ref[...], preferred_element_type=F32)
    merged = _sigmoid(in_proj(2 * d + POOL_WIDTH, 3 * d + POOL_WIDTH)) * p_a

    route(logit_c[...], jnp.maximum(i - 1, 0), i > 0)
    x_c[1 - cur], hb_c[1 - cur] = prologue(nxt)

    x_pool = in_proj(2 * d, 2 * d + POOL_WIDTH)
    g = POOL_GROUP_DIM
    s2 = x_pool + _delay(x_pool, 1, p1_c)
    s4 = s2[:, g:] + _delay(s2[:, g:], 2, p2_c)
    s8 = s4[:, g:] + _delay(s4[:, g:], 4, p4_c)
    s16 = s8[:, g:] + _delay(s8[:, g:], 8, p8_c)
    _save_tail(x_pool, p1_c)
    _save_tail(s2[:, g:], p2_c)
    _save_tail(s4[:, g:], p4_c)
    _save_tail(s8[:, g:], p8_c)
    wins = (s2[:, :g], s4[:, :g], s8[:, :g], s16)
    row = lax.broadcasted_iota(I32, (rows, 1), 0)
    t1 = (i * steps + row // BATCH + 1).astype(F32)
    pooled = []
    for k, w in enumerate(POOL_WINDOWS):
        cnt = jnp.minimum(t1, float(w))
        pk = wins[k] / cnt - x_pool[:, k * g:(k + 1) * g]
        pooled.append(jnp.dot(pk.astype(BF16), wpool_ref[k], preferred_element_type=F32))
    y_pool = jnp.concatenate(pooled, axis=-1) * row_of(VEC_PSCALE)[:, :POOL_WIDTH]
    p_b = jnp.dot(y_pool.astype(BF16), wupb_ref[...], preferred_element_type=F32)
    merged = merged + _sigmoid(in_proj(3 * d + POOL_WIDTH, 4 * d + POOL_WIDTH)) * p_b

    mix = jnp.dot(merged.astype(BF16), wout_ref[...], preferred_element_type=F32)
    x_mid = x_c[cur] + _per_batch(mix, gate1, jnp.multiply)
    o_ref[...] = x_mid
    xl_c[...] = jnp.dot(hb_c[1 - cur], win_ref[:, 0:d], preferred_element_type=F32)

    hr = _modulated_norm(x_mid, row_of(VEC_G2), mod_ref[3], mod_ref[4])
    hp_ref[...] = _pack_rows(hr)
    padded = jnp.dot(hr.astype(BF16), wrt_ref[...], preferred_element_type=F32)
    logit_c[...] = padded

    @pl.when(i == last)
    def _():
        route(logit_c[...], i, True)

    cnt_ref[...] = run_c[...].astype(I32)


def _pending_specs(rows, n_tok, block_of):
    nb = n_tok // rows
    y_spec = lambda k: pl.BlockSpec((rows, PACKED), lambda i: (k * nb + block_of(i), 0))
    return ([y_spec(k) for k in range(TOP_K)]
            + [pl.BlockSpec((rows, TOP_K), lambda i: (block_of(i), 0))])


def _mixer_layer(x_in, pending, mod_l, vec, w_in, conv_w, w_rg, w_pool, w_up_a, w_up_b, w_out,
                 w_route, b_router):
    rows = MIX_STEPS * BATCH
    weights = [mod_l, vec, w_in, conv_w, w_rg, w_pool, w_up_a, w_up_b, w_out, w_route,
               b_router.reshape(N_EXPERTS, 1)]
    if pending is None:
        n_tok, d = x_in.shape[0] * x_in.shape[1], x_in.shape[2]
        inputs = [x_in]
        specs = lambda block_of: [
            pl.BlockSpec((BATCH, MIX_STEPS, d), lambda i: (0, block_of(i), 0))]
    else:
        n_tok, d = x_in.shape
        y4, wts_prev, mod_prev = pending
        inputs = [x_in, y4, y4, y4, y4, wts_prev]
        weights.insert(0, mod_prev)
        specs = lambda block_of: ([pl.BlockSpec((rows, d), lambda i: (block_of(i), 0))]
                                  + _pending_specs(rows, n_tok, block_of))
    nb = n_tok // rows
    in_specs = (specs(lambda i: 0) + specs(lambda i: jnp.minimum(i + 1, nb - 1))
                + [_const_spec(a.shape) for a in weights])
    g = POOL_GROUP_DIM
    scratch = [
        pltpu.VMEM(((CONV_WIDTH - 1) * BATCH, d), F32),
        pltpu.VMEM((1 * BATCH, 4 * g), F32), pltpu.VMEM((2 * BATCH, 3 * g), F32),
        pltpu.VMEM((4 * BATCH, 2 * g), F32), pltpu.VMEM((8 * BATCH, g), F32),
        pltpu.VMEM((BATCH, d), F32),
        pltpu.VMEM((N_EXPERTS, 1), F32),
        pltpu.VMEM((rows, ROUTER_COLS), F32),
        pltpu.VMEM((2, rows, d), F32), pltpu.VMEM((2, rows, d), BF16),
        pltpu.VMEM((rows, d), F32),
    ]
    routed = lambda dtype: jax.ShapeDtypeStruct((TOP_K, n_tok), dtype)
    whole = lambda shape: pl.BlockSpec(shape, lambda i: (0,) * len(shape))
    routed_spec = whole((TOP_K, n_tok))
    return pl.pallas_call(
        functools.partial(_mixer_kernel, pending is not None),
        out_shape=(
            jax.ShapeDtypeStruct((n_tok, d), F32),
            jax.ShapeDtypeStruct((n_tok, PACKED), I32),
            routed(I32), routed(I32), routed(F32),
            jax.ShapeDtypeStruct((N_EXPERTS, 1), I32),
        ),
        grid=(nb,),
        in_specs=in_specs,
        out_specs=(
            pl.BlockSpec((rows, d), lambda i: (i, 0)),
            pl.BlockSpec((rows, PACKED), lambda i: (i, 0)),
            routed_spec, routed_spec, routed_spec,
            whole((N_EXPERTS, 1)),
        ),
        scratch_shapes=scratch,
        compiler_params=pltpu.CompilerParams(
            dimension_semantics=("arbitrary",), vmem_limit_bytes=VMEM_LIMIT),
        name="mixer",
    )(*(inputs + inputs + weights))


def _padded_router_weight(w_router):
    pad = jnp.zeros((w_router.shape[0], ROUTER_COLS - N_EXPERTS), BF16)
    return jnp.concatenate([w_router.astype(BF16), pad], axis=1)


def _sc_mesh():
    return plsc.VectorSubcoreMesh(core_axis_name="c", subcore_axis_name="s")


def _sc_worker_id():
    return lax.axis_index("s") * SC_CORES + lax.axis_index("c")


def _sc_dispatch(h_packed, dest_sc, n_slots):
    n_tok, width = h_packed.shape
    per_w = n_tok // SC_WORKERS
    n_chunks = per_w // SC_CHUNK

    def body(h_hbm, dest_hbm, xs_hbm, idx_v, rows_v, sem_in, sem_out):
        base = _sc_worker_id() * per_w
        pltpu.sync_copy(dest_hbm.at[_sc_worker_id()], idx_v)

        def load(j, buf):
            return pltpu.make_async_copy(
                h_hbm.at[pl.ds(base + j * SC_CHUNK, SC_CHUNK)], rows_v.at[buf], sem_in)

        load(0, 0).start()

        @pl.loop(0, n_chunks, step=2)
        def _(j0):
            for buf in range(2):
                j = j0 + buf
                load(j, buf).wait()

                @pl.when(j + 1 < n_chunks)
                def _():
                    load(j + 1, 1 - buf).start()

                copies = [
                    pltpu.make_async_copy(
                        rows_v.at[buf], xs_hbm.at[idx_v.at[j * TOP_K + k]], sem_out)
                    for k in range(TOP_K)]
                for cp in copies:
                    cp.start()
                for cp in copies:
                    cp.wait()

    return pl.kernel(
        body,
        out_type=jax.ShapeDtypeStruct((n_slots, width), I32),
        mesh=_sc_mesh(),
        scratch_types=[
            pltpu.VMEM((n_chunks * TOP_K, SC_CHUNK), I32),
            pltpu.VMEM((2, SC_CHUNK, width), I32),
            pltpu.SemaphoreType.DMA,
            pltpu.SemaphoreType.DMA,
        ],
        name="sc_dispatch",
    )(h_packed, dest_sc)


def _sc_gather(ys_packed, dest_sc):
    width = ys_packed.shape[1]
    n_chunks = dest_sc.shape[1]
    per_w = n_chunks * SC_CHUNK
    n_rows = SC_WORKERS * per_w

    def body(ys_hbm, dest_hbm, out_hbm, idx_v, rows_v, sem_g, sem_w):
        base = _sc_worker_id() * per_w
        pltpu.sync_copy(dest_hbm.at[_sc_worker_id()], idx_v)

        def gather(j, buf):
            return pltpu.make_async_copy(ys_hbm.at[idx_v.at[j]], rows_v.at[buf], sem_g)

        def writeout(j, buf):
            return pltpu.make_async_copy(
                rows_v.at[buf], out_hbm.at[pl.ds(base + j * SC_CHUNK, SC_CHUNK)], sem_w)

        gather(0, 0).start()

        @pl.loop(0, n_chunks, step=2)
        def _(j0):
            for buf in range(2):
                j = j0 + buf
                gather(j, buf).wait()

                @pl.when(j >= 1)
                def _():
                    writeout(j - 1, 1 - buf).wait()

                @pl.when(j + 1 < n_chunks)
                def _():
                    gather(j + 1, 1 - buf).start()

                writeout(j, buf).start()

        writeout(n_chunks - 1, 1).wait()

    return pl.kernel(
        body,
        out_type=jax.ShapeDtypeStruct((n_rows, width), I32),
        mesh=_sc_mesh(),
        scratch_types=[
            pltpu.VMEM((n_chunks, SC_CHUNK), I32),
            pltpu.VMEM((2, SC_CHUNK, width), I32),
            pltpu.SemaphoreType.DMA,
            pltpu.SemaphoreType.DMA,
        ],
        name="sc_gather",
    )(ys_packed, dest_sc)


def _expert_kernel(layer, be_ref, nv_ref, first_ref, next_ref, xs_ref, win_hbm, bin_ref,
                   wout_hbm, bout_ref, ys_ref, win_f32, wout_f32, win_bf, wout_bf, sem):
    i = pl.program_id(0)
    valid = nv_ref[i]

    def fetch(e):
        return (pltpu.make_async_copy(win_hbm.at[layer, e], win_f32, sem.at[0]),
                pltpu.make_async_copy(wout_hbm.at[layer, e], wout_f32, sem.at[1]))

    @pl.when(first_ref[i] == 1)
    def _():
        @pl.when(i == 0)
        def _():
            for cp in fetch(be_ref[i]):
                cp.start()

        for cp in fetch(be_ref[i]):
            cp.wait()

        def cast_in(r, c):
            rows = pl.ds(pl.multiple_of(r * CAST_ROWS, CAST_ROWS), CAST_ROWS)
            win_bf[rows, :] = win_f32[rows, :].astype(BF16)
            return c

        def cast_out(r, c):
            rows = pl.ds(pl.multiple_of(r * CAST_ROWS, CAST_ROWS), CAST_ROWS)
            wout_bf[rows, :] = wout_f32[rows, :].astype(BF16)
            return c

        lax.fori_loop(0, D_MODEL // CAST_ROWS, cast_in, 0)
        lax.fori_loop(0, D_FF // CAST_ROWS, cast_out, 0)

        @pl.when(next_ref[i] >= 0)
        def _():
            for cp in fetch(next_ref[i]):
                cp.start()

    def load_rows(start, size):
        rows = pl.ds(start, size)
        row = lax.broadcasted_iota(I32, (size, 1), 0)
        lo, hi = _unpack_rows(jnp.where(row + start < valid, xs_ref[rows, :], 0))
        return lo.astype(BF16), hi.astype(BF16)

    def up_proj(lo, hi):
        return (jnp.dot(lo, win_bf[:PACKED, :], preferred_element_type=F32)
                + jnp.dot(hi, win_bf[PACKED:, :], preferred_element_type=F32) + bin_ref[0])

    def activation(gu):
        gate = jnp.minimum(gu[:, :D_FF], SWIGLU_LIMIT)
        up = jnp.clip(gu[:, D_FF:], -SWIGLU_LIMIT, SWIGLU_LIMIT)
        return (gate * _sigmoid(SWIGLU_ALPHA * gate) * (up + 1.0)).astype(BF16)

    def down_proj(act):
        return jnp.dot(act, wout_bf[...], preferred_element_type=F32) + bout_ref[0]

    def store_rows(start, size, out):
        rows = pl.ds(start, size)
        ys_ref[rows, :] = _pack_rows(out)

    def pipelined(n_pass):
        size = EXPERT_PASS
        starts = range(0, n_pass * size, size)
        gus, outs = {}, {}
        for n, start in enumerate(starts):
            gus[start] = up_proj(*load_rows(start, size))
            if n > 0:
                prev = starts[n - 1]
                outs[prev] = down_proj(activation(gus.pop(prev)))
            if n > 1:
                store_rows(starts[n - 2], size, outs.pop(starts[n - 2]))
        outs[starts[-1]] = down_proj(activation(gus.pop(starts[-1])))
        for start in list(outs):
            store_rows(start, size, outs.pop(start))

    passes = (valid + EXPERT_PASS - 1) // EXPERT_PASS
    for n_pass in range(1, EXPERT_ROWS // EXPERT_PASS + 1):
        pl.when(passes == n_pass)(functools.partial(pipelined, n_pass))

    def clear(j, c):
        rows = pl.ds(pl.multiple_of(j * EXPERT_PASS, EXPERT_PASS), EXPERT_PASS)
        ys_ref[rows, :] = jnp.zeros((EXPERT_PASS, ys_ref.shape[1]), ys_ref.dtype)
        return c

    lax.fori_loop(passes, EXPERT_ROWS // EXPERT_PASS, clear, 0)


def _experts(layer, block_e, n_valid, is_first, next_e, xs, w_e_in, b_e_in, w_e_out, b_e_out):
    n_slots, width = xs.shape
    d = D_MODEL
    tm = EXPERT_ROWS
    bias_index = lambda i, be, nv, fi, nx: (layer, be[i], 0, 0)
    grid_spec = pltpu.PrefetchScalarGridSpec(
        num_scalar_prefetch=4,
        grid=(n_slots // tm,),
        in_specs=[
            pl.BlockSpec((tm, width), lambda i, be, nv, fi, nx: (i, 0)),
            pl.BlockSpec(memory_space=pl.ANY),
            pl.BlockSpec((None, 1, 1, 2 * D_FF), bias_index),
            pl.BlockSpec(memory_space=pl.ANY),
            pl.BlockSpec((None, 1, 1, d), bias_index),
        ],
        out_specs=pl.BlockSpec((tm, width), lambda i, be, nv, fi, nx: (i, 0)),
        scratch_shapes=[
            pltpu.VMEM((d, 2 * D_FF), F32), pltpu.VMEM((D_FF, d), F32),
            pltpu.VMEM((d, 2 * D_FF), BF16), pltpu.VMEM((D_FF, d), BF16),
            pltpu.SemaphoreType.DMA((2,)),
        ],
    )
    return pl.pallas_call(
        functools.partial(_expert_kernel, layer),
        out_shape=jax.ShapeDtypeStruct((n_slots, width), I32),
        grid_spec=grid_spec,
        compiler_params=pltpu.CompilerParams(
            dimension_semantics=("arbitrary",), vmem_limit_bytes=VMEM_LIMIT),
        name="experts",
    )(block_e, n_valid, is_first, next_e, xs, w_e_in,
      b_e_in.reshape(-1, N_EXPERTS, 1, 2 * D_FF), w_e_out, b_e_out.reshape(-1, N_EXPERTS, 1, d))


def _final_kernel(x_ref, y0_ref, y1_ref, y2_ref, y3_ref, wt_ref, mod_ref, g_ref, o_ref):
    x = _combined_rows(x_ref, (y0_ref, y1_ref, y2_ref, y3_ref), wt_ref, mod_ref[5])
    ms = jnp.mean(x * x, axis=-1, keepdims=True)
    out = x * lax.rsqrt(ms + NORM_EPS) * g_ref[...]
    o_ref[...] = pltpu.einshape("(tb)d->btd", out, b=BATCH)


def _final_norm(x2d, pending, g):
    n_tok, d = x2d.shape
    tr = NORM_ROWS
    y4, wts, mod_l = pending
    return pl.pallas_call(
        _final_kernel,
        out_shape=jax.ShapeDtypeStruct((BATCH, n_tok // BATCH, d), F32),
        grid=(n_tok // tr,),
        in_specs=([pl.BlockSpec((tr, d), lambda i: (i, 0))]
                  + _pending_specs(tr, n_tok, lambda i: i)
                  + [_const_spec(mod_l.shape), _const_spec((1, d))]),
        out_specs=pl.BlockSpec((BATCH, tr // BATCH, d), lambda i: (0, i, 0)),
        compiler_params=pltpu.CompilerParams(
            dimension_semantics=("arbitrary",), vmem_limit_bytes=VMEM_LIMIT),
        name="final_norm",
    )(x2d, y4, y4, y4, y4, wts, mod_l, g.reshape(1, d))


def _moe_layer(layer, h_packed, idx, pos, counts, w_e_in, b_e_in, w_e_out, b_e_out):
    n_tok = h_packed.shape[0]
    tm = EXPERT_ROWS
    padded = ((counts + tm - 1) // tm) * tm
    pad_end = jnp.cumsum(padded)
    start_pad = pad_end - padded
    experts = jnp.arange(N_EXPERTS, dtype=I32)
    dest = jnp.sum(jnp.where(idx[:, :, None] == experts, start_pad, 0), axis=-1) + pos
    n_blocks = (n_tok * TOP_K) // tm + N_EXPERTS
    n_slots = n_blocks * tm
    block_start = jnp.arange(n_blocks, dtype=I32) * tm
    block_e = jnp.minimum(
        jnp.sum((block_start[:, None] >= pad_end[None, :]).astype(I32), axis=-1),
        N_EXPERTS - 1)
    hot_e = block_e[:, None] == experts[None, :]
    rows_left = jnp.sum(jnp.where(hot_e, (start_pad + counts)[None, :], 0), axis=-1) - block_start
    n_valid = jnp.where(block_start < pad_end[-1], jnp.clip(rows_left, 0, tm), 0).astype(I32)
    is_first = jnp.sum(jnp.where(hot_e, start_pad[None, :], 0), axis=-1) == block_start
    is_first = (is_first & (n_valid > 0)).astype(I32)
    later = (experts[None, :] > experts[:, None]) & (counts[None, :] > 0)
    next_of = jnp.min(jnp.where(later, experts[None, :], N_EXPERTS), axis=-1)
    next_of = jnp.where(next_of == N_EXPERTS, -1, next_of)
    next_e = jnp.sum(jnp.where(hot_e, next_of[None, :], 0), axis=-1).astype(I32)

    per_w = n_tok // SC_WORKERS
    n_chunks = per_w // SC_CHUNK
    dest_scatter = (dest.reshape(TOP_K, SC_WORKERS, n_chunks, SC_CHUNK)
                    .transpose(1, 2, 0, 3)
                    .reshape(SC_WORKERS, n_chunks * TOP_K, SC_CHUNK))
    xs = _sc_dispatch(h_packed, dest_scatter, n_slots)
    ys = _experts(layer, block_e, n_valid, is_first, next_e, xs,
                  w_e_in, b_e_in, w_e_out, b_e_out)
    dest_gather = dest.reshape(SC_WORKERS, n_chunks * TOP_K, SC_CHUNK)
    return _sc_gather(ys, dest_gather)


def kernel(x, c, norm1_g, norm2_g, w_ada, b_ada, w_in, conv_w, conv_b, w_rg_a, b_rg_a,
           w_rg_x, b_rg_x, lru_lambda, w_pool, pool_scale, w_up_a, w_up_b, w_out,
           w_router, b_router, w_e_in, b_e_in, w_e_out, b_e_out, final_g):
    depth = w_in.shape[0]
    bsz, seq, d = x.shape
    assert bsz == BATCH and d == D_MODEL
    mod = _ada_mod(c, w_ada, b_ada).reshape(depth, bsz, 6, d).transpose(0, 2, 1, 3)
    w_rg = jnp.concatenate([w_rg_a, w_rg_x], axis=-1).astype(BF16)
    pending = None
    xcur = x
    for l in range(depth):
        pad = jnp.zeros((d - pool_scale.shape[1],), F32)
        vec = jnp.stack([norm1_g[l], conv_b[l], b_rg_a[l], b_rg_x[l], lru_lambda[l],
                         norm2_g[l], jnp.concatenate([pool_scale[l], pad]),
                         jnp.zeros((d,), F32)])
        xcur, h_packed, idx, pos, wts, counts = _mixer_layer(
            xcur, pending, mod[l], vec, w_in[l].astype(BF16), conv_w[l], w_rg[l],
            w_pool[l].astype(BF16), w_up_a[l].astype(BF16), w_up_b[l].astype(BF16),
            w_out[l].astype(BF16), _padded_router_weight(w_router[l]), b_router[l])
        y4 = _moe_layer(l, h_packed, idx, pos, counts[:, 0], w_e_in, b_e_in, w_e_out, b_e_out)
        pending = (y4, wts.T, mod[l])
    return _final_norm(xcur, pending, final_g)
```

```python
import functools
import math

import jax
import jax.numpy as jnp
from jax import lax
from jax.experimental import pallas as pl
from jax.experimental.pallas import tpu as pltpu
from jax.experimental.pallas import tpu_sc as plsc

D_MODEL = 1024
BATCH = 8
N_HEADS = 8
HEAD_DIM = 128
CONV_WIDTH = 4
LRU_C = 8.0
POOL_WINDOWS = (2, 4, 8, 16)
POOL_WIDTH = 512
POOL_GROUP_DIM = 128
N_EXPERTS = 32
TOP_K = 4
D_FF = 1024
SWIGLU_LIMIT = 7.0
SWIGLU_ALPHA = 1.702
NORM_EPS = 1e-6

VMEM_LIMIT = 56 * 1024 * 1024
SC_CORES = 2
SC_SUBCORES = 16
SC_WORKERS = SC_CORES * SC_SUBCORES
SC_CHUNK = 64

MIX_STEPS = 32
ROUTER_COLS = 128
EXPERT_ROWS = 1024
EXPERT_PASS = 256
ADA_COLS = 3072
NORM_ROWS = 512
CAST_ROWS = 128
PACKED = D_MODEL // 2

F32 = jnp.float32
BF16 = jnp.bfloat16
I32 = jnp.int32
HIGH_MASK = -65536
LOG2_E = math.log2(math.e)


def _const_spec(shape):
    nd = len(shape)
    return pl.BlockSpec(shape, lambda *_: (0,) * nd, pipeline_mode=pl.Buffered(1))


def _pack_rows(v):
    packed = pltpu.pack_elementwise([v[:, :PACKED], v[:, PACKED:]], packed_dtype=BF16)
    return lax.bitcast_convert_type(packed, I32)


def _unpack_rows(p):
    lo = lax.bitcast_convert_type(lax.shift_left(p, 16), F32)
    hi = lax.bitcast_convert_type(p & HIGH_MASK, F32)
    return lo, hi


def _sigmoid_of_twice(t):
    return 0.5 * jnp.tanh(t) + 0.5


def _per_batch(v, slab, op):
    rows, d = v.shape
    return op(v.reshape(rows // BATCH, BATCH, d), slab[None]).reshape(rows, d)


def _modulated_norm(x, g, shift, scale):
    ms = jnp.mean(x * x, axis=-1, keepdims=True)
    hn = x * lax.rsqrt(ms + NORM_EPS)
    return _per_batch(_per_batch(hn, g * (1.0 + scale), jnp.multiply), shift, jnp.add)


def _ada_kernel(c_ref, w_ref, b_ref, o_ref):
    c = c_ref[...]
    c_act = c * jax.nn.sigmoid(c)
    o_ref[0] = jnp.dot(c_act, w_ref[0], preferred_element_type=F32,
                       precision=lax.Precision.HIGHEST) + b_ref[0]


def _ada_mod(c, w_ada, b_ada):
    depth, d, n = w_ada.shape
    bsz = c.shape[0]
    tn = ADA_COLS
    return pl.pallas_call(
        _ada_kernel,
        out_shape=jax.ShapeDtypeStruct((depth, bsz, n), F32),
        grid=(depth, n // tn),
        in_specs=[
            pl.BlockSpec((bsz, d), lambda l, j: (0, 0)),
            pl.BlockSpec((1, d, tn), lambda l, j: (l, 0, j)),
            pl.BlockSpec((1, 1, tn), lambda l, j: (l, 0, j)),
        ],
        out_specs=pl.BlockSpec((1, bsz, tn), lambda l, j: (l, 0, j)),
        compiler_params=pltpu.CompilerParams(
            dimension_semantics=("arbitrary", "arbitrary"),
            vmem_limit_bytes=VMEM_LIMIT),
        name="ada_mod",
    )(c, w_ada, b_ada.reshape(depth, 1, n))


def _delay(a, steps, carry_ref):
    n = steps * BATCH
    keep = carry_ref.shape[0]
    prev = carry_ref[keep - n:, :]
    return jnp.concatenate([prev, a[:a.shape[0] - n]], axis=0)


def _save_tail(a, carry_ref):
    carry_ref[...] = a[a.shape[0] - carry_ref.shape[0]:]


def _combined_rows(x_ref, y_refs, wt_ref, gate2):
    x = x_ref[...]
    wt = wt_ref[...]
    acc_lo = jnp.zeros((x.shape[0], PACKED), F32)
    acc_hi = jnp.zeros((x.shape[0], PACKED), F32)
    for k, y_ref in enumerate(y_refs):
        lo, hi = _unpack_rows(y_ref[...])
        wk = wt[:, k:k + 1]
        acc_lo = acc_lo + wk * lo
        acc_hi = acc_hi + wk * hi
    ffn = jnp.concatenate([acc_lo, acc_hi], axis=-1)
    return x + _per_batch(ffn, gate2, jnp.multiply)


def _route_rows(padded, brc_ref, run_c, live, cols, idx_ref, pos_ref, wt_ref):
    rows = padded.shape[0]
    logits = padded.T[:N_EXPERTS] + brc_ref[...]

    e_iota = lax.broadcasted_iota(I32, (N_EXPERTS, rows), 0)
    vals, idxs, hots = [], [], []
    cur = logits
    for _ in range(TOP_K):
        m = jnp.max(cur, axis=0, keepdims=True)
        am = jnp.min(jnp.where(cur == m, e_iota, N_EXPERTS), axis=0, keepdims=True)
        hot = e_iota == am
        vals.append(m)
        idxs.append(am)
        hots.append(hot)
        cur = jnp.where(hot, -jnp.inf, cur)
    exps = [jnp.exp(v - vals[0]) for v in vals]
    denom = exps[0] + exps[1] + exps[2] + exps[3]

    sel = (hots[0] | hots[1] | hots[2] | hots[3]).astype(F32)
    ri = lax.broadcasted_iota(I32, (rows, rows), 0)
    ci = lax.broadcasted_iota(I32, (rows, rows), 1)
    earlier = (ri < ci).astype(BF16)
    before = jnp.dot(sel.astype(BF16), earlier, preferred_element_type=F32) + run_c[...]
    run_c[...] = run_c[...] + jnp.where(live, jnp.sum(sel, axis=1, keepdims=True), 0.0)
    poss = [jnp.sum(jnp.where(hot, before, 0.0), axis=0, keepdims=True).astype(I32)
            for hot in hots]
    idx_ref[:, cols] = jnp.concatenate(idxs, axis=0)
    pos_ref[:, cols] = jnp.concatenate(poss, axis=0)
    wt_ref[:, cols] = jnp.concatenate([e / denom for e in exps], axis=0)


VEC_G1, VEC_CONV_B, VEC_BRG_A, VEC_BRG_X, VEC_LAM, VEC_G2, VEC_PSCALE = range(7)


def _mixer_kernel(with_combine, *refs):
    refs = list(refs)
    n_in = 1 + (TOP_K + 1 if with_combine else 0)
    first, nxt = refs[:n_in], refs[n_in:2 * n_in]
    refs = refs[2 * n_in:]
    if with_combine:
        modp_ref = refs.pop(0)
    (mod_ref, vec_ref, win_ref, convw_ref, wrg_ref, wpool_ref, wupa_ref, wupb_ref, wout_ref,
     wrt_ref, brc_ref,
     o_ref, hp_ref, idx_ref, pos_ref, wt_ref, cnt_ref,
     conv_c, p1_c, p2_c, p4_c, p8_c, h_c, run_c, logit_c, x_c, hb_c, xl_c) = refs
    i = pl.program_id(0)
    last = pl.num_programs(0) - 1
    cur = i % 2
    rows = o_ref.shape[0]
    steps = rows // BATCH
    d = D_MODEL
    vec = vec_ref[...]
    row_of = lambda k: vec[k:k + 1]
    shift1, scale1, gate1 = mod_ref[0], mod_ref[1], mod_ref[2]

    def prologue(in_refs):
        if with_combine:
            x = _combined_rows(in_refs[0], in_refs[1:1 + TOP_K], in_refs[1 + TOP_K],
                               modp_ref[5])
        else:
            x = pltpu.einshape("btd->(tb)d", in_refs[0][...])
        return x, _modulated_norm(x, row_of(VEC_G1), shift1, scale1).astype(BF16)

    @pl.when(i == 0)
    def _():
        for ref in (conv_c, p1_c, p2_c, p4_c, p8_c, h_c, run_c, logit_c):
            ref[...] = jnp.zeros(ref.shape, ref.dtype)
        x_c[0], hb_c[0] = prologue(first)
        xl_c[...] = jnp.dot(hb_c[0], win_ref[:, 0:d], preferred_element_type=F32)

    def route(padded, block, live):
        cols = pl.ds(pl.multiple_of(block * rows, rows), rows)
        _route_rows(padded, brc_ref, run_c, live, cols, idx_ref, pos_ref, wt_ref)

    def in_proj(lo, hi):
        return jnp.dot(hb_c[cur], win_ref[:, lo:hi], preferred_element_type=F32)

    x_lru = xl_c[...]
    cw = convw_ref[...]
    xc = (row_of(VEC_CONV_B) + cw[3:4] * x_lru
          + cw[2:3] * _delay(x_lru, 1, conv_c)
          + cw[1:2] * _delay(x_lru, 2, conv_c)
          + cw[0:1] * _delay(x_lru, 3, conv_c))
    _save_tail(x_lru, conv_c)
    xcb = xc.astype(BF16)
    pre = [jnp.dot(xcb[:, k * HEAD_DIM:(k + 1) * HEAD_DIM], wrg_ref[k],
                   preferred_element_type=F32) for k in range(N_HEADS)]
    pre_a = jnp.concatenate([p[:, :HEAD_DIM] for p in pre], axis=-1)
    pre_x = jnp.concatenate([p[:, HEAD_DIM:] for p in pre], axis=-1)
    r = _sigmoid_of_twice(pre_a + row_of(VEC_BRG_A))
    gi = _sigmoid_of_twice(pre_x + row_of(VEC_BRG_X))
    rate = (-LRU_C * LOG2_E) * jax.nn.softplus(-row_of(VEC_LAM))
    a = jnp.exp2(r * rate)
    y = 1.0 - a * a
    mult = jnp.where(y > 0.0, y * lax.rsqrt(y), 0.0)
    u = mult * (gi * xc)
    h = h_c[...]
    hs = []
    for t in range(steps):
        sl = slice(t * BATCH, (t + 1) * BATCH)
        h = a[sl] * h + u[sl]
        hs.append(h)
    h_c[...] = h
    y_lru = jnp.concatenate(hs, axis=0) * jax.nn.gelu(in_proj(d, 2 * d))
    p_a = jnp.dot(y_lru.astype(BF16), wupa_ref[...], preferred_element_type=F32)
    merged = _sigmoid_of_twice(in_proj(2 * d + POOL_WIDTH, 3 * d + POOL_WIDTH)) * p_a

    route(logit_c[...], jnp.maximum(i - 1, 0), i > 0)
    x_c[1 - cur], hb_c[1 - cur] = prologue(nxt)

    x_pool = in_proj(2 * d, 2 * d + POOL_WIDTH)
    g = POOL_GROUP_DIM
    s2 = x_pool + _delay(x_pool, 1, p1_c)
    s4 = s2[:, g:] + _delay(s2[:, g:], 2, p2_c)
    s8 = s4[:, g:] + _delay(s4[:, g:], 4, p4_c)
    s16 = s8[:, g:] + _delay(s8[:, g:], 8, p8_c)
    _save_tail(x_pool, p1_c)
    _save_tail(s2[:, g:], p2_c)
    _save_tail(s4[:, g:], p4_c)
    _save_tail(s8[:, g:], p8_c)
    wins = (s2[:, :g], s4[:, :g], s8[:, :g], s16)
    row = lax.broadcasted_iota(I32, (rows, 1), 0)
    t1 = (i * steps + row // BATCH + 1).astype(F32)
    pooled = []
    for k, w in enumerate(POOL_WINDOWS):
        cnt = jnp.minimum(t1, float(w))
        pk = wins[k] / cnt - x_pool[:, k * g:(k + 1) * g]
        pooled.append(jnp.dot(pk.astype(BF16), wpool_ref[k], preferred_element_type=F32))
    y_pool = jnp.concatenate(pooled, axis=-1) * row_of(VEC_PSCALE)[:, :POOL_WIDTH]
    p_b = jnp.dot(y_pool.astype(BF16), wupb_ref[...], preferred_element_type=F32)
    gates_b = _sigmoid_of_twice(in_proj(3 * d + POOL_WIDTH, 4 * d + POOL_WIDTH))
    merged = merged + gates_b * p_b

    mix = jnp.dot(merged.astype(BF16), wout_ref[...], preferred_element_type=F32)
    x_mid = x_c[cur] + _per_batch(mix, gate1, jnp.multiply)
    o_ref[...] = x_mid
    xl_c[...] = jnp.dot(hb_c[1 - cur], win_ref[:, 0:d], preferred_element_type=F32)

    hr = _modulated_norm(x_mid, row_of(VEC_G2), mod_ref[3], mod_ref[4])
    hp_ref[...] = _pack_rows(hr)
    padded = jnp.dot(hr.astype(BF16), wrt_ref[...], preferred_element_type=F32)
    logit_c[...] = padded

    @pl.when(i == last)
    def _():
        route(logit_c[...], i, True)

    cnt_ref[...] = run_c[...].astype(I32)


def _pending_specs(rows, n_tok, block_of):
    nb = n_tok // rows
    y_spec = lambda k: pl.BlockSpec((rows, PACKED), lambda i: (k * nb + block_of(i), 0))
    return ([y_spec(k) for k in range(TOP_K)]
            + [pl.BlockSpec((rows, TOP_K), lambda i: (block_of(i), 0))])


def _mixer_layer(x_in, pending, mod_l, vec, w_in, conv_w, w_rg, w_pool, w_up_a, w_up_b, w_out,
                 w_route, b_router):
    rows = MIX_STEPS * BATCH
    weights = [mod_l, vec, w_in, conv_w, w_rg, w_pool, w_up_a, w_up_b, w_out, w_route,
               b_router.reshape(N_EXPERTS, 1)]
    if pending is None:
        n_tok, d = x_in.shape[0] * x_in.shape[1], x_in.shape[2]
        inputs = [x_in]
        specs = lambda block_of: [
            pl.BlockSpec((BATCH, MIX_STEPS, d), lambda i: (0, block_of(i), 0))]
    else:
        n_tok, d = x_in.shape
        y4, wts_prev, mod_prev = pending
        inputs = [x_in, y4, y4, y4, y4, wts_prev]
        weights.insert(0, mod_prev)
        specs = lambda block_of: ([pl.BlockSpec((rows, d), lambda i: (block_of(i), 0))]
                                  + _pending_specs(rows, n_tok, block_of))
    nb = n_tok // rows
    in_specs = (specs(lambda i: 0) + specs(lambda i: jnp.minimum(i + 1, nb - 1))
                + [_const_spec(a.shape) for a in weights])
    g = POOL_GROUP_DIM
    scratch = [
        pltpu.VMEM(((CONV_WIDTH - 1) * BATCH, d), F32),
        pltpu.VMEM((1 * BATCH, 4 * g), F32), pltpu.VMEM((2 * BATCH, 3 * g), F32),
        pltpu.VMEM((4 * BATCH, 2 * g), F32), pltpu.VMEM((8 * BATCH, g), F32),
        pltpu.VMEM((BATCH, d), F32),
        pltpu.VMEM((N_EXPERTS, 1), F32),
        pltpu.VMEM((rows, ROUTER_COLS), F32),
        pltpu.VMEM((2, rows, d), F32), pltpu.VMEM((2, rows, d), BF16),
        pltpu.VMEM((rows, d), F32),
    ]
    routed = lambda dtype: jax.ShapeDtypeStruct((TOP_K, n_tok), dtype)
    whole = lambda shape: pl.BlockSpec(shape, lambda i: (0,) * len(shape))
    routed_spec = whole((TOP_K, n_tok))
    return pl.pallas_call(
        functools.partial(_mixer_kernel, pending is not None),
        out_shape=(
            jax.ShapeDtypeStruct((n_tok, d), F32),
            jax.ShapeDtypeStruct((n_tok, PACKED), I32),
            routed(I32), routed(I32), routed(F32),
            jax.ShapeDtypeStruct((N_EXPERTS, 1), I32),
        ),
        grid=(nb,),
        in_specs=in_specs,
        out_specs=(
            pl.BlockSpec((rows, d), lambda i: (i, 0)),
            pl.BlockSpec((rows, PACKED), lambda i: (i, 0)),
            routed_spec, routed_spec, routed_spec,
            whole((N_EXPERTS, 1)),
        ),
        scratch_shapes=scratch,
        compiler_params=pltpu.CompilerParams(
            dimension_semantics=("arbitrary",), vmem_limit_bytes=VMEM_LIMIT),
        name="mixer",
    )(*(inputs + inputs + weights))


def _padded_router_weight(w_router):
    pad = jnp.zeros((w_router.shape[0], ROUTER_COLS - N_EXPERTS), BF16)
    return jnp.concatenate([w_router.astype(BF16), pad], axis=1)


def _sc_mesh():
    return plsc.VectorSubcoreMesh(core_axis_name="c", subcore_axis_name="s")


def _sc_worker_id():
    return lax.axis_index("s") * SC_CORES + lax.axis_index("c")


def _sc_dispatch(h_packed, dest_sc, n_slots):
    n_tok, width = h_packed.shape
    per_w = n_tok // SC_WORKERS
    n_chunks = per_w // SC_CHUNK

    def body(h_hbm, dest_hbm, xs_hbm, idx_v, rows_v, sem_in, sem_out):
        base = _sc_worker_id() * per_w
        pltpu.sync_copy(dest_hbm.at[_sc_worker_id()], idx_v)

        def load(j, buf):
            return pltpu.make_async_copy(
                h_hbm.at[pl.ds(base + j * SC_CHUNK, SC_CHUNK)], rows_v.at[buf], sem_in)

        load(0, 0).start()

        @pl.loop(0, n_chunks, step=2)
        def _(j0):
            for buf in range(2):
                j = j0 + buf
                load(j, buf).wait()

                @pl.when(j + 1 < n_chunks)
                def _():
                    load(j + 1, 1 - buf).start()

                copies = [
                    pltpu.make_async_copy(
                        rows_v.at[buf], xs_hbm.at[idx_v.at[j * TOP_K + k]], sem_out)
                    for k in range(TOP_K)]
                for cp in copies:
                    cp.start()
                for cp in copies:
                    cp.wait()

    return pl.kernel(
        body,
        out_type=jax.ShapeDtypeStruct((n_slots, width), I32),
        mesh=_sc_mesh(),
        scratch_types=[
            pltpu.VMEM((n_chunks * TOP_K, SC_CHUNK), I32),
            pltpu.VMEM((2, SC_CHUNK, width), I32),
            pltpu.SemaphoreType.DMA,
            pltpu.SemaphoreType.DMA,
        ],
        name="sc_dispatch",
    )(h_packed, dest_sc)


def _sc_gather(ys_packed, dest_sc):
    width = ys_packed.shape[1]
    n_chunks = dest_sc.shape[1]
    per_w = n_chunks * SC_CHUNK
    n_rows = SC_WORKERS * per_w

    def body(ys_hbm, dest_hbm, out_hbm, idx_v, rows_v, sem_g, sem_w):
        base = _sc_worker_id() * per_w
        pltpu.sync_copy(dest_hbm.at[_sc_worker_id()], idx_v)

        def gather(j, buf):
            return pltpu.make_async_copy(ys_hbm.at[idx_v.at[j]], rows_v.at[buf], sem_g)

        def writeout(j, buf):
            return pltpu.make_async_copy(
                rows_v.at[buf], out_hbm.at[pl.ds(base + j * SC_CHUNK, SC_CHUNK)], sem_w)

        gather(0, 0).start()

        @pl.loop(0, n_chunks, step=2)
        def _(j0):
            for buf in range(2):
                j = j0 + buf
                gather(j, buf).wait()

                @pl.when(j >= 1)
                def _():
                    writeout(j - 1, 1 - buf).wait()

                @pl.when(j + 1 < n_chunks)
                def _():
                    gather(j + 1, 1 - buf).start()

                writeout(j, buf).start()

        writeout(n_chunks - 1, 1).wait()

    return pl.kernel(
        body,
        out_type=jax.ShapeDtypeStruct((n_rows, width), I32),
        mesh=_sc_mesh(),
        scratch_types=[
            pltpu.VMEM((n_chunks, SC_CHUNK), I32),
            pltpu.VMEM((2, SC_CHUNK, width), I32),
            pltpu.SemaphoreType.DMA,
            pltpu.SemaphoreType.DMA,
        ],
        name="sc_gather",
    )(ys_packed, dest_sc)


def _expert_kernel(layer, be_ref, nv_ref, first_ref, next_ref, xs_ref, win_hbm, bin_ref,
                   wout_hbm, bout_ref, ys_ref, win_f32, wout_f32, win_bf, wout_bf, sem):
    i = pl.program_id(0)
    valid = nv_ref[i]

    def fetch(e):
        return (pltpu.make_async_copy(win_hbm.at[layer, e], win_f32, sem.at[0]),
                pltpu.make_async_copy(wout_hbm.at[layer, e], wout_f32, sem.at[1]))

    @pl.when(first_ref[i] == 1)
    def _():
        @pl.when(i == 0)
        def _():
            for cp in fetch(be_ref[i]):
                cp.start()

        for cp in fetch(be_ref[i]):
            cp.wait()

        def cast_in(r, c):
            rows = pl.ds(pl.multiple_of(r * CAST_ROWS, CAST_ROWS), CAST_ROWS)
            win_bf[rows, :] = win_f32[rows, :].astype(BF16)
            return c

        def cast_out(r, c):
            rows = pl.ds(pl.multiple_of(r * CAST_ROWS, CAST_ROWS), CAST_ROWS)
            wout_bf[rows, :] = wout_f32[rows, :].astype(BF16)
            return c

        lax.fori_loop(0, D_MODEL // CAST_ROWS, cast_in, 0)
        lax.fori_loop(0, D_FF // CAST_ROWS, cast_out, 0)

        @pl.when(next_ref[i] >= 0)
        def _():
            for cp in fetch(next_ref[i]):
                cp.start()

    def load_rows(start, size):
        rows = pl.ds(start, size)
        row = lax.broadcasted_iota(I32, (size, 1), 0)
        lo, hi = _unpack_rows(jnp.where(row + start < valid, xs_ref[rows, :], 0))
        return lo.astype(BF16), hi.astype(BF16)

    def up_proj(lo, hi):
        return (jnp.dot(lo, win_bf[:PACKED, :], preferred_element_type=F32)
                + jnp.dot(hi, win_bf[PACKED:, :], preferred_element_type=F32) + bin_ref[0])

    def activation(gu):
        gate = jnp.minimum(gu[:, :D_FF], SWIGLU_LIMIT)
        up = jnp.clip(gu[:, D_FF:], -SWIGLU_LIMIT, SWIGLU_LIMIT)
        gated = gate * _sigmoid_of_twice(0.5 * (SWIGLU_ALPHA * gate))
        return (gated * (up + 1.0)).astype(BF16)

    def down_proj(act):
        return jnp.dot(act, wout_bf[...], preferred_element_type=F32) + bout_ref[0]

    def store_rows(start, size, out):
        rows = pl.ds(start, size)
        ys_ref[rows, :] = _pack_rows(out)

    def pipelined(n_pass):
        size = EXPERT_PASS
        starts = range(0, n_pass * size, size)
        gus, outs = {}, {}
        for n, start in enumerate(starts):
            gus[start] = up_proj(*load_rows(start, size))
            if n > 0:
                prev = starts[n - 1]
                outs[prev] = down_proj(activation(gus.pop(prev)))
            if n > 1:
                store_rows(starts[n - 2], size, outs.pop(starts[n - 2]))
        outs[starts[-1]] = down_proj(activation(gus.pop(starts[-1])))
        for start in list(outs):
            store_rows(start, size, outs.pop(start))

    passes = (valid + EXPERT_PASS - 1) // EXPERT_PASS
    for n_pass in range(1, EXPERT_ROWS // EXPERT_PASS + 1):
        pl.when(passes == n_pass)(functools.partial(pipelined, n_pass))

    def clear(j, c):
        rows = pl.ds(pl.multiple_of(j * EXPERT_PASS, EXPERT_PASS), EXPERT_PASS)
        ys_ref[rows, :] = jnp.zeros((EXPERT_PASS, ys_ref.shape[1]), ys_ref.dtype)
        return c

    lax.fori_loop(passes, EXPERT_ROWS // EXPERT_PASS, clear, 0)


def _experts(layer, block_e, n_valid, is_first, next_e, xs, w_e_in, b_e_in, w_e_out, b_e_out):
    n_slots, width = xs.shape
    d = D_MODEL
    tm = EXPERT_ROWS
    bias_index = lambda i, be, nv, fi, nx: (layer, be[i], 0, 0)
    grid_spec = pltpu.PrefetchScalarGridSpec(
        num_scalar_prefetch=4,
        grid=(n_slots // tm,),
        in_specs=[
            pl.BlockSpec((tm, width), lambda i, be, nv, fi, nx: (i, 0)),
            pl.BlockSpec(memory_space=pl.ANY),
            pl.BlockSpec((None, 1, 1, 2 * D_FF), bias_index),
            pl.BlockSpec(memory_space=pl.ANY),
            pl.BlockSpec((None, 1, 1, d), bias_index),
        ],
        out_specs=pl.BlockSpec((tm, width), lambda i, be, nv, fi, nx: (i, 0)),
        scratch_shapes=[
            pltpu.VMEM((d, 2 * D_FF), F32), pltpu.VMEM((D_FF, d), F32),
            pltpu.VMEM((d, 2 * D_FF), BF16), pltpu.VMEM((D_FF, d), BF16),
            pltpu.SemaphoreType.DMA((2,)),
        ],
    )
    return pl.pallas_call(
        functools.partial(_expert_kernel, layer),
        out_shape=jax.ShapeDtypeStruct((n_slots, width), I32),
        grid_spec=grid_spec,
        compiler_params=pltpu.CompilerParams(
            dimension_semantics=("arbitrary",), vmem_limit_bytes=VMEM_LIMIT),
        name="experts",
    )(block_e, n_valid, is_first, next_e, xs, w_e_in,
      b_e_in.reshape(-1, N_EXPERTS, 1, 2 * D_FF), w_e_out, b_e_out.reshape(-1, N_EXPERTS, 1, d))


def _final_kernel(x_ref, y0_ref, y1_ref, y2_ref, y3_ref, wt_ref, mod_ref, g_ref, o_ref):
    x = _combined_rows(x_ref, (y0_ref, y1_ref, y2_ref, y3_ref), wt_ref, mod_ref[5])
    ms = jnp.mean(x * x, axis=-1, keepdims=True)
    out = x * lax.rsqrt(ms + NORM_EPS) * g_ref[...]
    o_ref[...] = pltpu.einshape("(tb)d->btd", out, b=BATCH)


def _final_norm(x2d, pending, g):
    n_tok, d = x2d.shape
    tr = NORM_ROWS
    y4, wts, mod_l = pending
    return pl.pallas_call(
        _final_kernel,
        out_shape=jax.ShapeDtypeStruct((BATCH, n_tok // BATCH, d), F32),
        grid=(n_tok // tr,),
        in_specs=([pl.BlockSpec((tr, d), lambda i: (i, 0))]
                  + _pending_specs(tr, n_tok, lambda i: i)
                  + [_const_spec(mod_l.shape), _const_spec((1, d))]),
        out_specs=pl.BlockSpec((BATCH, tr // BATCH, d), lambda i: (0, i, 0)),
        compiler_params=pltpu.CompilerParams(
            dimension_semantics=("arbitrary",), vmem_limit_bytes=VMEM_LIMIT),
        name="final_norm",
    )(x2d, y4, y4, y4, y4, wts, mod_l, g.reshape(1, d))


def _moe_layer(layer, h_packed, idx, pos, counts, w_e_in, b_e_in, w_e_out, b_e_out):
    n_tok = h_packed.shape[0]
    tm = EXPERT_ROWS
    padded = ((counts + tm - 1) // tm) * tm
    pad_end = jnp.cumsum(padded)
    start_pad = pad_end - padded
    experts = jnp.arange(N_EXPERTS, dtype=I32)
    dest = jnp.sum(jnp.where(idx[:, :, None] == experts, start_pad, 0), axis=-1) + pos
    n_blocks = (n_tok * TOP_K) // tm + N_EXPERTS
    n_slots = n_blocks * tm
    block_start = jnp.arange(n_blocks, dtype=I32) * tm
    block_e = jnp.minimum(
        jnp.sum((block_start[:, None] >= pad_end[None, :]).astype(I32), axis=-1),
        N_EXPERTS - 1)
    hot_e = block_e[:, None] == experts[None, :]
    rows_left = jnp.sum(jnp.where(hot_e, (start_pad + counts)[None, :], 0), axis=-1) - block_start
    n_valid = jnp.where(block_start < pad_end[-1], jnp.clip(rows_left, 0, tm), 0).astype(I32)
    is_first = jnp.sum(jnp.where(hot_e, start_pad[None, :], 0), axis=-1) == block_start
    is_first = (is_first & (n_valid > 0)).astype(I32)
    later = (experts[None, :] > experts[:, None]) & (counts[None, :] > 0)
    next_of = jnp.min(jnp.where(later, experts[None, :], N_EXPERTS), axis=-1)
    next_of = jnp.where(next_of == N_EXPERTS, -1, next_of)
    next_e = jnp.sum(jnp.where(hot_e, next_of[None, :], 0), axis=-1).astype(I32)

    per_w = n_tok // SC_WORKERS
    n_chunks = per_w // SC_CHUNK
    dest_scatter = (dest.reshape(TOP_K, SC_WORKERS, n_chunks, SC_CHUNK)
                    .transpose(1, 2, 0, 3)
                    .reshape(SC_WORKERS, n_chunks * TOP_K, SC_CHUNK))
    xs = _sc_dispatch(h_packed, dest_scatter, n_slots)
    ys = _experts(layer, block_e, n_valid, is_first, next_e, xs,
                  w_e_in, b_e_in, w_e_out, b_e_out)
    dest_gather = dest.reshape(SC_WORKERS, n_chunks * TOP_K, SC_CHUNK)
    return _sc_gather(ys, dest_gather)


def kernel(x, c, norm1_g, norm2_g, w_ada, b_ada, w_in, conv_w, conv_b, w_rg_a, b_rg_a,
           w_rg_x, b_rg_x, lru_lambda, w_pool, pool_scale, w_up_a, w_up_b, w_out,
           w_router, b_router, w_e_in, b_e_in, w_e_out, b_e_out, final_g):
    depth = w_in.shape[0]
    bsz, seq, d = x.shape
    assert bsz == BATCH and d == D_MODEL
    mod = _ada_mod(c, w_ada, b_ada).reshape(depth, bsz, 6, d).transpose(0, 2, 1, 3)
    w_rg = (0.5 * jnp.concatenate([w_rg_a, w_rg_x], axis=-1)).astype(BF16)
    gate_cols = jnp.arange(w_in.shape[2]) >= 2 * d + POOL_WIDTH
    w_in_scaled = jnp.where(gate_cols, 0.5, 1.0) * w_in
    pending = None
    xcur = x
    for l in range(depth):
        pad = jnp.zeros((d - pool_scale.shape[1],), F32)
        vec = jnp.stack([norm1_g[l], conv_b[l], 0.5 * b_rg_a[l], 0.5 * b_rg_x[l],
                         lru_lambda[l], norm2_g[l], jnp.concatenate([pool_scale[l], pad]),
                         jnp.zeros((d,), F32)])
        xcur, h_packed, idx, pos, wts, counts = _mixer_layer(
            xcur, pending, mod[l], vec, w_in_scaled[l].astype(BF16), conv_w[l], w_rg[l],
            w_pool[l].astype(BF16), w_up_a[l].astype(BF16), w_up_b[l].astype(BF16),
            w_out[l].astype(BF16), _padded_router_weight(w_router[l]), b_router[l])
        y4 = _moe_layer(l, h_packed, idx, pos, counts[:, 0], w_e_in, b_e_in, w_e_out, b_e_out)
        pending = (y4, wts.T, mod[l])
    return _final_norm(xcur, pending, final_g)
```

```python
import functools
import math

import jax
import jax.numpy as jnp
from jax import lax
from jax.experimental import pallas as pl
from jax.experimental.pallas import tpu as pltpu
from jax.experimental.pallas import tpu_sc as plsc

D_MODEL = 1024
BATCH = 8
N_HEADS = 8
HEAD_DIM = 128
CONV_WIDTH = 4
LRU_C = 8.0
POOL_WINDOWS = (2, 4, 8, 16)
POOL_WIDTH = 512
POOL_GROUP_DIM = 128
N_EXPERTS = 32
TOP_K = 4
D_FF = 1024
SWIGLU_LIMIT = 7.0
SWIGLU_ALPHA = 1.702
NORM_EPS = 1e-6

VMEM_LIMIT = 56 * 1024 * 1024
SC_CORES = 2
SC_SUBCORES = 16
SC_WORKERS = SC_CORES * SC_SUBCORES
SC_CHUNK = 64

MIX_STEPS = 32
ROUTER_COLS = 128
EXPERT_ROWS = 1024
EXPERT_PASS = 256
ADA_COLS = 3072
NORM_ROWS = 1024
CAST_ROWS = 128
PACKED = D_MODEL // 2

F32 = jnp.float32
BF16 = jnp.bfloat16
I32 = jnp.int32
HIGH_MASK = -65536
LOG2_E = math.log2(math.e)


def _const_spec(shape):
    nd = len(shape)
    return pl.BlockSpec(shape, lambda *_: (0,) * nd, pipeline_mode=pl.Buffered(1))


def _pack_rows(v):
    packed = pltpu.pack_elementwise([v[:, :PACKED], v[:, PACKED:]], packed_dtype=BF16)
    return lax.bitcast_convert_type(packed, I32)


def _unpack_rows(p):
    lo = lax.bitcast_convert_type(lax.shift_left(p, 16), F32)
    hi = lax.bitcast_convert_type(p & HIGH_MASK, F32)
    return lo, hi


def _sigmoid_of_twice(t):
    return 0.5 * jnp.tanh(t) + 0.5


def _per_batch(v, slab, op):
    rows, d = v.shape
    return op(v.reshape(rows // BATCH, BATCH, d), slab[None]).reshape(rows, d)


def _modulated_norm(x, g, shift, scale):
    ms = jnp.mean(x * x, axis=-1, keepdims=True)
    hn = x * lax.rsqrt(ms + NORM_EPS)
    return _per_batch(_per_batch(hn, g * (1.0 + scale), jnp.multiply), shift, jnp.add)


def _ada_kernel(c_ref, w_ref, b_ref, o_ref):
    c = c_ref[...]
    c_act = c * jax.nn.sigmoid(c)
    o_ref[0] = jnp.dot(c_act, w_ref[0], preferred_element_type=F32,
                       precision=lax.Precision.HIGHEST) + b_ref[0]


def _ada_mod(c, w_ada, b_ada):
    depth, d, n = w_ada.shape
    bsz = c.shape[0]
    tn = ADA_COLS
    return pl.pallas_call(
        _ada_kernel,
        out_shape=jax.ShapeDtypeStruct((depth, bsz, n), F32),
        grid=(depth, n // tn),
        in_specs=[
            pl.BlockSpec((bsz, d), lambda l, j: (0, 0)),
            pl.BlockSpec((1, d, tn), lambda l, j: (l, 0, j)),
            pl.BlockSpec((1, 1, tn), lambda l, j: (l, 0, j)),
        ],
        out_specs=pl.BlockSpec((1, bsz, tn), lambda l, j: (l, 0, j)),
        compiler_params=pltpu.CompilerParams(
            dimension_semantics=("arbitrary", "arbitrary"),
            vmem_limit_bytes=VMEM_LIMIT),
        name="ada_mod",
    )(c, w_ada, b_ada.reshape(depth, 1, n))


def _delay(a, steps, carry_ref):
    n = steps * BATCH
    keep = carry_ref.shape[0]
    prev = carry_ref[keep - n:, :]
    return jnp.concatenate([prev, a[:a.shape[0] - n]], axis=0)


def _save_tail(a, carry_ref):
    carry_ref[...] = a[a.shape[0] - carry_ref.shape[0]:]


def _combined_rows(x_ref, y_refs, wt_ref, gate2):
    x = x_ref[...]
    wt = wt_ref[...]
    acc_lo = jnp.zeros((x.shape[0], PACKED), F32)
    acc_hi = jnp.zeros((x.shape[0], PACKED), F32)
    for k, y_ref in enumerate(y_refs):
        lo, hi = _unpack_rows(y_ref[...])
        wk = wt[:, k:k + 1]
        acc_lo = acc_lo + wk * lo
        acc_hi = acc_hi + wk * hi
    ffn = jnp.concatenate([acc_lo, acc_hi], axis=-1)
    return x + _per_batch(ffn, gate2, jnp.multiply)


def _route_rows(padded, brc_ref, run_c, live, cols, idx_ref, pos_ref, wt_ref):
    rows = padded.shape[0]
    logits = padded.T[:N_EXPERTS] + brc_ref[...]

    e_iota = lax.broadcasted_iota(I32, (N_EXPERTS, rows), 0)
    vals, idxs, hots = [], [], []
    cur = logits
    for _ in range(TOP_K):
        m = jnp.max(cur, axis=0, keepdims=True)
        am = jnp.min(jnp.where(cur == m, e_iota, N_EXPERTS), axis=0, keepdims=True)
        hot = e_iota == am
        vals.append(m)
        idxs.append(am)
        hots.append(hot)
        cur = jnp.where(hot, -jnp.inf, cur)
    exps = [jnp.exp(v - vals[0]) for v in vals]
    denom = exps[0] + exps[1] + exps[2] + exps[3]

    sel = (hots[0] | hots[1] | hots[2] | hots[3]).astype(F32)
    ri = lax.broadcasted_iota(I32, (rows, rows), 0)
    ci = lax.broadcasted_iota(I32, (rows, rows), 1)
    earlier = (ri < ci).astype(BF16)
    before = jnp.dot(sel.astype(BF16), earlier, preferred_element_type=F32) + run_c[...]
    run_c[...] = run_c[...] + jnp.where(live, jnp.sum(sel, axis=1, keepdims=True), 0.0)
    poss = [jnp.sum(jnp.where(hot, before, 0.0), axis=0, keepdims=True).astype(I32)
            for hot in hots]
    idx_ref[:, cols] = jnp.concatenate(idxs, axis=0)
    pos_ref[:, cols] = jnp.concatenate(poss, axis=0)
    wt_ref[:, cols] = jnp.concatenate([e / denom for e in exps], axis=0)


VEC_G1, VEC_CONV_B, VEC_BRG_A, VEC_BRG_X, VEC_LAM, VEC_G2, VEC_PSCALE = range(7)


def _mixer_kernel(with_combine, *refs):
    refs = list(refs)
    n_in = 1 + (TOP_K + 1 if with_combine else 0)
    first, nxt = refs[:n_in], refs[n_in:2 * n_in]
    refs = refs[2 * n_in:]
    if with_combine:
        modp_ref = refs.pop(0)
    (mod_ref, vec_ref, win_ref, convw_ref, wrg_ref, wpool_ref, wupa_ref, wupb_ref, wout_ref,
     wrt_ref, brc_ref,
     o_ref, hp_ref, idx_ref, pos_ref, wt_ref, cnt_ref,
     conv_c, p1_c, p2_c, p4_c, p8_c, h_c, run_c, logit_c, x_c, hb_c, xl_c) = refs
    i = pl.program_id(0)
    last = pl.num_programs(0) - 1
    cur = i % 2
    rows = o_ref.shape[0]
    steps = rows // BATCH
    d = D_MODEL
    vec = vec_ref[...]
    row_of = lambda k: vec[k:k + 1]
    shift1, scale1, gate1 = mod_ref[0], mod_ref[1], mod_ref[2]

    def prologue(in_refs):
        if with_combine:
            x = _combined_rows(in_refs[0], in_refs[1:1 + TOP_K], in_refs[1 + TOP_K],
                               modp_ref[5])
        else:
            x = pltpu.einshape("btd->(tb)d", in_refs[0][...])
        return x, _modulated_norm(x, row_of(VEC_G1), shift1, scale1).astype(BF16)

    @pl.when(i == 0)
    def _():
        for ref in (conv_c, p1_c, p2_c, p4_c, p8_c, h_c, run_c, logit_c):
            ref[...] = jnp.zeros(ref.shape, ref.dtype)
        x_c[0], hb_c[0] = prologue(first)
        xl_c[...] = jnp.dot(hb_c[0], win_ref[:, 0:d], preferred_element_type=F32)

    def route(padded, block, live):
        cols = pl.ds(pl.multiple_of(block * rows, rows), rows)
        _route_rows(padded, brc_ref, run_c, live, cols, idx_ref, pos_ref, wt_ref)

    def in_proj(lo, hi):
        return jnp.dot(hb_c[cur], win_ref[:, lo:hi], preferred_element_type=F32)

    x_lru = xl_c[...]
    cw = convw_ref[...]
    xc = (row_of(VEC_CONV_B) + cw[3:4] * x_lru
          + cw[2:3] * _delay(x_lru, 1, conv_c)
          + cw[1:2] * _delay(x_lru, 2, conv_c)
          + cw[0:1] * _delay(x_lru, 3, conv_c))
    _save_tail(x_lru, conv_c)
    xcb = xc.astype(BF16)
    pre = [jnp.dot(xcb[:, k * HEAD_DIM:(k + 1) * HEAD_DIM], wrg_ref[k],
                   preferred_element_type=F32) for k in range(N_HEADS)]
    pre_a = jnp.concatenate([p[:, :HEAD_DIM] for p in pre], axis=-1)
    pre_x = jnp.concatenate([p[:, HEAD_DIM:] for p in pre], axis=-1)
    r = _sigmoid_of_twice(pre_a + row_of(VEC_BRG_A))
    gi = _sigmoid_of_twice(pre_x + row_of(VEC_BRG_X))
    rate = (-LRU_C * LOG2_E) * jax.nn.softplus(-row_of(VEC_LAM))
    a = jnp.exp2(r * rate)
    y = 1.0 - a * a
    mult = jnp.where(y > 0.0, y * lax.rsqrt(y), 0.0)
    u = mult * (gi * xc)
    h = h_c[...]
    hs = []
    for t in range(steps):
        sl = slice(t * BATCH, (t + 1) * BATCH)
        h = a[sl] * h + u[sl]
        hs.append(h)
    h_c[...] = h
    y_lru = jnp.concatenate(hs, axis=0) * jax.nn.gelu(in_proj(d, 2 * d))
    p_a = jnp.dot(y_lru.astype(BF16), wupa_ref[...], preferred_element_type=F32)
    merged = _sigmoid_of_twice(in_proj(2 * d + POOL_WIDTH, 3 * d + POOL_WIDTH)) * p_a

    route(logit_c[...], jnp.maximum(i - 1, 0), i > 0)
    x_c[1 - cur], hb_c[1 - cur] = prologue(nxt)

    x_pool = in_proj(2 * d, 2 * d + POOL_WIDTH)
    g = POOL_GROUP_DIM
    s2 = x_pool + _delay(x_pool, 1, p1_c)
    s4 = s2[:, g:] + _delay(s2[:, g:], 2, p2_c)
    s8 = s4[:, g:] + _delay(s4[:, g:], 4, p4_c)
    s16 = s8[:, g:] + _delay(s8[:, g:], 8, p8_c)
    _save_tail(x_pool, p1_c)
    _save_tail(s2[:, g:], p2_c)
    _save_tail(s4[:, g:], p4_c)
    _save_tail(s8[:, g:], p8_c)
    wins = (s2[:, :g], s4[:, :g], s8[:, :g], s16)
    row = lax.broadcasted_iota(I32, (rows, 1), 0)
    t1 = (i * steps + row // BATCH + 1).astype(F32)
    pooled = []
    for k, w in enumerate(POOL_WINDOWS):
        cnt = jnp.minimum(t1, float(w))
        pk = wins[k] / cnt - x_pool[:, k * g:(k + 1) * g]
        pooled.append(jnp.dot(pk.astype(BF16), wpool_ref[k], preferred_element_type=F32))
    y_pool = jnp.concatenate(pooled, axis=-1) * row_of(VEC_PSCALE)[:, :POOL_WIDTH]
    p_b = jnp.dot(y_pool.astype(BF16), wupb_ref[...], preferred_element_type=F32)
    gates_b = _sigmoid_of_twice(in_proj(3 * d + POOL_WIDTH, 4 * d + POOL_WIDTH))
    merged = merged + gates_b * p_b

    mix = jnp.dot(merged.astype(BF16), wout_ref[...], preferred_element_type=F32)
    x_mid = x_c[cur] + _per_batch(mix, gate1, jnp.multiply)
    o_ref[...] = x_mid
    xl_c[...] = jnp.dot(hb_c[1 - cur], win_ref[:, 0:d], preferred_element_type=F32)

    hr = _modulated_norm(x_mid, row_of(VEC_G2), mod_ref[3], mod_ref[4])
    hp_ref[...] = _pack_rows(hr)
    padded = jnp.dot(hr.astype(BF16), wrt_ref[...], preferred_element_type=F32)
    logit_c[...] = padded

    @pl.when(i == last)
    def _():
        route(logit_c[...], i, True)

    cnt_ref[...] = run_c[...].astype(I32)


def _pending_specs(rows, n_tok, block_of):
    nb = n_tok // rows
    y_spec = lambda k: pl.BlockSpec((rows, PACKED), lambda i: (k * nb + block_of(i), 0))
    return ([y_spec(k) for k in range(TOP_K)]
            + [pl.BlockSpec((rows, TOP_K), lambda i: (block_of(i), 0))])


def _mixer_layer(x_in, pending, mod_l, vec, w_in, conv_w, w_rg, w_pool, w_up_a, w_up_b, w_out,
                 w_route, b_router):
    rows = MIX_STEPS * BATCH
    weights = [mod_l, vec, w_in, conv_w, w_rg, w_pool, w_up_a, w_up_b, w_out, w_route,
               b_router.reshape(N_EXPERTS, 1)]
    if pending is None:
        n_tok, d = x_in.shape[0] * x_in.shape[1], x_in.shape[2]
        inputs = [x_in]
        specs = lambda block_of: [
            pl.BlockSpec((BATCH, MIX_STEPS, d), lambda i: (0, block_of(i), 0))]
    else:
        n_tok, d = x_in.shape
        y4, wts_prev, mod_prev = pending
        inputs = [x_in, y4, y4, y4, y4, wts_prev]
        weights.insert(0, mod_prev)
        specs = lambda block_of: ([pl.BlockSpec((rows, d), lambda i: (block_of(i), 0))]
                                  + _pending_specs(rows, n_tok, block_of))
    nb = n_tok // rows
    in_specs = (specs(lambda i: 0) + specs(lambda i: jnp.minimum(i + 1, nb - 1))
                + [_const_spec(a.shape) for a in weights])
    g = POOL_GROUP_DIM
    scratch = [
        pltpu.VMEM(((CONV_WIDTH - 1) * BATCH, d), F32),
        pltpu.VMEM((1 * BATCH, 4 * g), F32), pltpu.VMEM((2 * BATCH, 3 * g), F32),
        pltpu.VMEM((4 * BATCH, 2 * g), F32), pltpu.VMEM((8 * BATCH, g), F32),
        pltpu.VMEM((BATCH, d), F32),
        pltpu.VMEM((N_EXPERTS, 1), F32),
        pltpu.VMEM((rows, ROUTER_COLS), F32),
        pltpu.VMEM((2, rows, d), F32), pltpu.VMEM((2, rows, d), BF16),
        pltpu.VMEM((rows, d), F32),
    ]
    routed = lambda dtype: jax.ShapeDtypeStruct((TOP_K, n_tok), dtype)
    whole = lambda shape: pl.BlockSpec(shape, lambda i: (0,) * len(shape))
    routed_spec = whole((TOP_K, n_tok))
    return pl.pallas_call(
        functools.partial(_mixer_kernel, pending is not None),
        out_shape=(
            jax.ShapeDtypeStruct((n_tok, d), F32),
            jax.ShapeDtypeStruct((n_tok, PACKED), I32),
            routed(I32), routed(I32), routed(F32),
            jax.ShapeDtypeStruct((N_EXPERTS, 1), I32),
        ),
        grid=(nb,),
        in_specs=in_specs,
        out_specs=(
            pl.BlockSpec((rows, d), lambda i: (i, 0)),
            pl.BlockSpec((rows, PACKED), lambda i: (i, 0)),
            routed_spec, routed_spec, routed_spec,
            whole((N_EXPERTS, 1)),
        ),
        scratch_shapes=scratch,
        compiler_params=pltpu.CompilerParams(
            dimension_semantics=("arbitrary",), vmem_limit_bytes=VMEM_LIMIT),
        name="mixer",
    )(*(inputs + inputs + weights))


def _padded_router_weight(w_router):
    pad = jnp.zeros((w_router.shape[0], ROUTER_COLS - N_EXPERTS), BF16)
    return jnp.concatenate([w_router.astype(BF16), pad], axis=1)


def _sc_mesh():
    return plsc.VectorSubcoreMesh(core_axis_name="c", subcore_axis_name="s")


def _sc_worker_id():
    return lax.axis_index("s") * SC_CORES + lax.axis_index("c")


def _sc_dispatch(h_packed, dest_sc, n_slots):
    n_tok, width = h_packed.shape
    per_w = n_tok // SC_WORKERS
    n_chunks = per_w // SC_CHUNK

    def body(h_hbm, dest_hbm, xs_hbm, idx_v, rows_v, sem_in, sem_out):
        base = _sc_worker_id() * per_w
        pltpu.sync_copy(dest_hbm.at[_sc_worker_id()], idx_v)

        def load(j, buf):
            return pltpu.make_async_copy(
                h_hbm.at[pl.ds(base + j * SC_CHUNK, SC_CHUNK)], rows_v.at[buf], sem_in)

        load(0, 0).start()

        @pl.loop(0, n_chunks, step=2)
        def _(j0):
            for buf in range(2):
                j = j0 + buf
                load(j, buf).wait()

                @pl.when(j + 1 < n_chunks)
                def _():
                    load(j + 1, 1 - buf).start()

                copies = [
                    pltpu.make_async_copy(
                        rows_v.at[buf], xs_hbm.at[idx_v.at[j * TOP_K + k]], sem_out)
                    for k in range(TOP_K)]
                for cp in copies:
                    cp.start()
                for cp in copies:
                    cp.wait()

    return pl.kernel(
        body,
        out_type=jax.ShapeDtypeStruct((n_slots, width), I32),
        mesh=_sc_mesh(),
        scratch_types=[
            pltpu.VMEM((n_chunks * TOP_K, SC_CHUNK), I32),
            pltpu.VMEM((2, SC_CHUNK, width), I32),
            pltpu.SemaphoreType.DMA,
            pltpu.SemaphoreType.DMA,
        ],
        name="sc_dispatch",
    )(h_packed, dest_sc)


def _sc_gather(ys_packed, dest_sc):
    width = ys_packed.shape[1]
    n_chunks = dest_sc.shape[1]
    per_w = n_chunks * SC_CHUNK
    n_rows = SC_WORKERS * per_w

    def body(ys_hbm, dest_hbm, out_hbm, idx_v, rows_v, sem_g, sem_w):
        base = _sc_worker_id() * per_w
        pltpu.sync_copy(dest_hbm.at[_sc_worker_id()], idx_v)

        def gather(j, buf):
            return pltpu.make_async_copy(ys_hbm.at[idx_v.at[j]], rows_v.at[buf], sem_g)

        def writeout(j, buf):
            return pltpu.make_async_copy(
                rows_v.at[buf], out_hbm.at[pl.ds(base + j * SC_CHUNK, SC_CHUNK)], sem_w)

        gather(0, 0).start()

        @pl.loop(0, n_chunks, step=2)
        def _(j0):
            for buf in range(2):
                j = j0 + buf
                gather(j, buf).wait()

                @pl.when(j >= 1)
                def _():
                    writeout(j - 1, 1 - buf).wait()

                @pl.when(j + 1 < n_chunks)
                def _():
                    gather(j + 1, 1 - buf).start()

                writeout(j, buf).start()

        writeout(n_chunks - 1, 1).wait()

    return pl.kernel(
        body,
        out_type=jax.ShapeDtypeStruct((n_rows, width), I32),
        mesh=_sc_mesh(),
        scratch_types=[
            pltpu.VMEM((n_chunks, SC_CHUNK), I32),
            pltpu.VMEM((2, SC_CHUNK, width), I32),
            pltpu.SemaphoreType.DMA,
            pltpu.SemaphoreType.DMA,
        ],
        name="sc_gather",
    )(ys_packed, dest_sc)


def _expert_kernel(layer, be_ref, nv_ref, first_ref, next_ref, xs_ref, win_hbm, bin_ref,
                   wout_hbm, bout_ref, ys_ref, win_f32, wout_f32, win_bf, wout_bf, sem):
    i = pl.program_id(0)
    valid = nv_ref[i]

    def fetch(e):
        return (pltpu.make_async_copy(win_hbm.at[layer, e], win_f32, sem.at[0]),
                pltpu.make_async_copy(wout_hbm.at[layer, e], wout_f32, sem.at[1]))

    @pl.when(first_ref[i] == 1)
    def _():
        @pl.when(i == 0)
        def _():
            for cp in fetch(be_ref[i]):
                cp.start()

        for cp in fetch(be_ref[i]):
            cp.wait()

        def cast_in(r, c):
            rows = pl.ds(pl.multiple_of(r * CAST_ROWS, CAST_ROWS), CAST_ROWS)
            win_bf[rows, :] = win_f32[rows, :].astype(BF16)
            return c

        def cast_out(r, c):
            rows = pl.ds(pl.multiple_of(r * CAST_ROWS, CAST_ROWS), CAST_ROWS)
            wout_bf[rows, :] = wout_f32[rows, :].astype(BF16)
            return c

        lax.fori_loop(0, D_MODEL // CAST_ROWS, cast_in, 0)
        lax.fori_loop(0, D_FF // CAST_ROWS, cast_out, 0)

        @pl.when(next_ref[i] >= 0)
        def _():
            for cp in fetch(next_ref[i]):
                cp.start()

    def load_rows(start, size):
        rows = pl.ds(start, size)
        row = lax.broadcasted_iota(I32, (size, 1), 0)
        lo, hi = _unpack_rows(jnp.where(row + start < valid, xs_ref[rows, :], 0))
        return lo.astype(BF16), hi.astype(BF16)

    def up_proj(lo, hi):
        return (jnp.dot(lo, win_bf[:PACKED, :], preferred_element_type=F32)
                + jnp.dot(hi, win_bf[PACKED:, :], preferred_element_type=F32) + bin_ref[0])

    def activation(gu):
        gate = jnp.minimum(gu[:, :D_FF], SWIGLU_LIMIT)
        up = jnp.clip(gu[:, D_FF:], -SWIGLU_LIMIT, SWIGLU_LIMIT)
        gated = gate * _sigmoid_of_twice(0.5 * (SWIGLU_ALPHA * gate))
        return (gated * (up + 1.0)).astype(BF16)

    def down_proj(act):
        return jnp.dot(act, wout_bf[...], preferred_element_type=F32) + bout_ref[0]

    def store_rows(start, size, out):
        rows = pl.ds(start, size)
        ys_ref[rows, :] = _pack_rows(out)

    def pipelined(n_pass):
        size = EXPERT_PASS
        starts = range(0, n_pass * size, size)
        gus, outs = {}, {}
        for n, start in enumerate(starts):
            gus[start] = up_proj(*load_rows(start, size))
            if n > 0:
                prev = starts[n - 1]
                outs[prev] = down_proj(activation(gus.pop(prev)))
            if n > 1:
                store_rows(starts[n - 2], size, outs.pop(starts[n - 2]))
        outs[starts[-1]] = down_proj(activation(gus.pop(starts[-1])))
        for start in list(outs):
            store_rows(start, size, outs.pop(start))

    passes = (valid + EXPERT_PASS - 1) // EXPERT_PASS
    for n_pass in range(1, EXPERT_ROWS // EXPERT_PASS + 1):
        pl.when(passes == n_pass)(functools.partial(pipelined, n_pass))

    def clear(j, c):
        rows = pl.ds(pl.multiple_of(j * EXPERT_PASS, EXPERT_PASS), EXPERT_PASS)
        ys_ref[rows, :] = jnp.zeros((EXPERT_PASS, ys_ref.shape[1]), ys_ref.dtype)
        return c

    lax.fori_loop(passes, EXPERT_ROWS // EXPERT_PASS, clear, 0)


def _experts(layer, block_e, n_valid, is_first, next_e, xs, w_e_in, b_e_in, w_e_out, b_e_out):
    n_slots, width = xs.shape
    d = D_MODEL
    tm = EXPERT_ROWS
    bias_index = lambda i, be, nv, fi, nx: (layer, be[i], 0, 0)
    grid_spec = pltpu.PrefetchScalarGridSpec(
        num_scalar_prefetch=4,
        grid=(n_slots // tm,),
        in_specs=[
            pl.BlockSpec((tm, width), lambda i, be, nv, fi, nx: (i, 0)),
            pl.BlockSpec(memory_space=pl.ANY),
            pl.BlockSpec((None, 1, 1, 2 * D_FF), bias_index),
            pl.BlockSpec(memory_space=pl.ANY),
            pl.BlockSpec((None, 1, 1, d), bias_index),
        ],
        out_specs=pl.BlockSpec((tm, width), lambda i, be, nv, fi, nx: (i, 0)),
        scratch_shapes=[
            pltpu.VMEM((d, 2 * D_FF), F32), pltpu.VMEM((D_FF, d), F32),
            pltpu.VMEM((d, 2 * D_FF), BF16), pltpu.VMEM((D_FF, d), BF16),
            pltpu.SemaphoreType.DMA((2,)),
        ],
    )
    return pl.pallas_call(
        functools.partial(_expert_kernel, layer),
        out_shape=jax.ShapeDtypeStruct((n_slots, width), I32),
        grid_spec=grid_spec,
        compiler_params=pltpu.CompilerParams(
            dimension_semantics=("arbitrary",), vmem_limit_bytes=VMEM_LIMIT),
        name="experts",
    )(block_e, n_valid, is_first, next_e, xs, w_e_in,
      b_e_in.reshape(-1, N_EXPERTS, 1, 2 * D_FF), w_e_out, b_e_out.reshape(-1, N_EXPERTS, 1, d))


def _final_kernel(x_ref, y0_ref, y1_ref, y2_ref, y3_ref, wt_ref, mod_ref, g_ref, o_ref):
    x = _combined_rows(x_ref, (y0_ref, y1_ref, y2_ref, y3_ref), wt_ref, mod_ref[5])
    ms = jnp.mean(x * x, axis=-1, keepdims=True)
    out = x * lax.rsqrt(ms + NORM_EPS) * g_ref[...]
    o_ref[...] = pltpu.einshape("(tb)d->btd", out, b=BATCH)


def _final_norm(x2d, pending, g):
    n_tok, d = x2d.shape
    tr = NORM_ROWS
    y4, wts, mod_l = pending
    return pl.pallas_call(
        _final_kernel,
        out_shape=jax.ShapeDtypeStruct((BATCH, n_tok // BATCH, d), F32),
        grid=(n_tok // tr,),
        in_specs=([pl.BlockSpec((tr, d), lambda i: (i, 0))]
                  + _pending_specs(tr, n_tok, lambda i: i)
                  + [_const_spec(mod_l.shape), _const_spec((1, d))]),
        out_specs=pl.BlockSpec((BATCH, tr // BATCH, d), lambda i: (0, i, 0)),
        compiler_params=pltpu.CompilerParams(
            dimension_semantics=("arbitrary",), vmem_limit_bytes=VMEM_LIMIT),
        name="final_norm",
    )(x2d, y4, y4, y4, y4, wts, mod_l, g.reshape(1, d))


def _moe_layer(layer, h_packed, idx, pos, counts, w_e_in, b_e_in, w_e_out, b_e_out):
    n_tok = h_packed.shape[0]
    tm = EXPERT_ROWS
    padded = ((counts + tm - 1) // tm) * tm
    pad_end = jnp.cumsum(padded)
    start_pad = pad_end - padded
    experts = jnp.arange(N_EXPERTS, dtype=I32)
    dest = jnp.sum(jnp.where(idx[:, :, None] == experts, start_pad, 0), axis=-1) + pos
    n_blocks = (n_tok * TOP_K) // tm + N_EXPERTS
    n_slots = n_blocks * tm
    block_start = jnp.arange(n_blocks, dtype=I32) * tm
    block_e = jnp.minimum(
        jnp.sum((block_start[:, None] >= pad_end[None, :]).astype(I32), axis=-1),
        N_EXPERTS - 1)
    hot_e = block_e[:, None] == experts[None, :]
    rows_left = jnp.sum(jnp.where(hot_e, (start_pad + counts)[None, :], 0), axis=-1) - block_start
    n_valid = jnp.where(block_start < pad_end[-1], jnp.clip(rows_left, 0, tm), 0).astype(I32)
    is_first = jnp.sum(jnp.where(hot_e, start_pad[None, :], 0), axis=-1) == block_start
    is_first = (is_first & (n_valid > 0)).astype(I32)
    later = (experts[None, :] > experts[:, None]) & (counts[None, :] > 0)
    next_of = jnp.min(jnp.where(later, experts[None, :], N_EXPERTS), axis=-1)
    next_of = jnp.where(next_of == N_EXPERTS, -1, next_of)
    next_e = jnp.sum(jnp.where(hot_e, next_of[None, :], 0), axis=-1).astype(I32)

    per_w = n_tok // SC_WORKERS
    n_chunks = per_w // SC_CHUNK
    dest_scatter = (dest.reshape(TOP_K, SC_WORKERS, n_chunks, SC_CHUNK)
                    .transpose(1, 2, 0, 3)
                    .reshape(SC_WORKERS, n_chunks * TOP_K, SC_CHUNK))
    xs = _sc_dispatch(h_packed, dest_scatter, n_slots)
    ys = _experts(layer, block_e, n_valid, is_first, next_e, xs,
                  w_e_in, b_e_in, w_e_out, b_e_out)
    dest_gather = dest.reshape(SC_WORKERS, n_chunks * TOP_K, SC_CHUNK)
    return _sc_gather(ys, dest_gather)


def kernel(x, c, norm1_g, norm2_g, w_ada, b_ada, w_in, conv_w, conv_b, w_rg_a, b_rg_a,
           w_rg_x, b_rg_x, lru_lambda, w_pool, pool_scale, w_up_a, w_up_b, w_out,
           w_router, b_router, w_e_in, b_e_in, w_e_out, b_e_out, final_g):
    depth = w_in.shape[0]
    bsz, seq, d = x.shape
    assert bsz == BATCH and d == D_MODEL
    mod = _ada_mod(c, w_ada, b_ada).reshape(depth, bsz, 6, d).transpose(0, 2, 1, 3)
    w_rg = (0.5 * jnp.concatenate([w_rg_a, w_rg_x], axis=-1)).astype(BF16)
    gate_cols = jnp.arange(w_in.shape[2]) >= 2 * d + POOL_WIDTH
    w_in_scaled = jnp.where(gate_cols, 0.5, 1.0) * w_in
    pending = None
    xcur = x
    for l in range(depth):
        pad = jnp.zeros((d - pool_scale.shape[1],), F32)
        vec = jnp.stack([norm1_g[l], conv_b[l], 0.5 * b_rg_a[l], 0.5 * b_rg_x[l],
                         lru_lambda[l], norm2_g[l], jnp.concatenate([pool_scale[l], pad]),
                         jnp.zeros((d,), F32)])
        xcur, h_packed, idx, pos, wts, counts = _mixer_layer(
            xcur, pending, mod[l], vec, w_in_scaled[l].astype(BF16), conv_w[l], w_rg[l],
            w_pool[l].astype(BF16), w_up_a[l].astype(BF16), w_up_b[l].astype(BF16),
            w_out[l].astype(BF16), _padded_router_weight(w_router[l]), b_router[l])
        y4 = _moe_layer(l, h_packed, idx, pos, counts[:, 0], w_e_in, b_e_in, w_e_out, b_e_out)
        pending = (y4, wts.T, mod[l])
    return _final_norm(xcur, pending, final_g)
```

```python
import functools
import math

import jax
import jax.numpy as jnp
from jax import lax
from jax.experimental import pallas as pl
from jax.experimental.pallas import tpu as pltpu
from jax.experimental.pallas import tpu_sc as plsc

D_MODEL = 1024
BATCH = 8
N_HEADS = 8
HEAD_DIM = 128
CONV_WIDTH = 4
LRU_C = 8.0
POOL_WINDOWS = (2, 4, 8, 16)
POOL_WIDTH = 512
POOL_GROUP_DIM = 128
N_EXPERTS = 32
TOP_K = 4
D_FF = 1024
SWIGLU_LIMIT = 7.0
SWIGLU_ALPHA = 1.702
NORM_EPS = 1e-6

VMEM_LIMIT = 56 * 1024 * 1024
SC_CORES = 2
SC_SUBCORES = 16
SC_WORKERS = SC_CORES * SC_SUBCORES
SC_CHUNK = 64
SC_BUFFERS = 3

MIX_STEPS = 32
ROUTER_COLS = 128
EXPERT_ROWS = 1024
EXPERT_PASS = 256
ADA_COLS = 3072
NORM_ROWS = 512
CAST_ROWS = 128
PACKED = D_MODEL // 2

F32 = jnp.float32
BF16 = jnp.bfloat16
I32 = jnp.int32
HIGH_MASK = -65536
LOG2_E = math.log2(math.e)


def _const_spec(shape):
    nd = len(shape)
    return pl.BlockSpec(shape, lambda *_: (0,) * nd, pipeline_mode=pl.Buffered(1))


def _pack_rows(v):
    packed = pltpu.pack_elementwise([v[:, :PACKED], v[:, PACKED:]], packed_dtype=BF16)
    return lax.bitcast_convert_type(packed, I32)


def _unpack_rows(p):
    lo = lax.bitcast_convert_type(lax.shift_left(p, 16), F32)
    hi = lax.bitcast_convert_type(p & HIGH_MASK, F32)
    return lo, hi


def _sigmoid_of_twice(t):
    return 0.5 * jnp.tanh(t) + 0.5


def _per_batch(v, slab, op):
    rows, d = v.shape
    return op(v.reshape(rows // BATCH, BATCH, d), slab[None]).reshape(rows, d)


def _modulated_norm(x, g, shift, scale):
    ms = jnp.mean(x * x, axis=-1, keepdims=True)
    hn = x * lax.rsqrt(ms + NORM_EPS)
    return _per_batch(_per_batch(hn, g * (1.0 + scale), jnp.multiply), shift, jnp.add)


def _ada_kernel(c_ref, w_ref, b_ref, o_ref):
    c = c_ref[...]
    c_act = c * jax.nn.sigmoid(c)
    o_ref[0] = jnp.dot(c_act, w_ref[0], preferred_element_type=F32,
                       precision=lax.Precision.HIGHEST) + b_ref[0]


def _ada_mod(c, w_ada, b_ada):
    depth, d, n = w_ada.shape
    bsz = c.shape[0]
    tn = ADA_COLS
    return pl.pallas_call(
        _ada_kernel,
        out_shape=jax.ShapeDtypeStruct((depth, bsz, n), F32),
        grid=(depth, n // tn),
        in_specs=[
            pl.BlockSpec((bsz, d), lambda l, j: (0, 0)),
            pl.BlockSpec((1, d, tn), lambda l, j: (l, 0, j)),
            pl.BlockSpec((1, 1, tn), lambda l, j: (l, 0, j)),
        ],
        out_specs=pl.BlockSpec((1, bsz, tn), lambda l, j: (l, 0, j)),
        compiler_params=pltpu.CompilerParams(
            dimension_semantics=("arbitrary", "arbitrary"),
            vmem_limit_bytes=VMEM_LIMIT),
        name="ada_mod",
    )(c, w_ada, b_ada.reshape(depth, 1, n))


def _delay(a, steps, carry_ref):
    n = steps * BATCH
    keep = carry_ref.shape[0]
    prev = carry_ref[keep - n:, :]
    return jnp.concatenate([prev, a[:a.shape[0] - n]], axis=0)


def _save_tail(a, carry_ref):
    carry_ref[...] = a[a.shape[0] - carry_ref.shape[0]:]


def _combined_rows(x_ref, y_refs, wt_ref, gate2):
    x = x_ref[...]
    wt = wt_ref[...]
    acc_lo = jnp.zeros((x.shape[0], PACKED), F32)
    acc_hi = jnp.zeros((x.shape[0], PACKED), F32)
    for k, y_ref in enumerate(y_refs):
        lo, hi = _unpack_rows(y_ref[...])
        wk = wt[:, k:k + 1]
        acc_lo = acc_lo + wk * lo
        acc_hi = acc_hi + wk * hi
    ffn = jnp.concatenate([acc_lo, acc_hi], axis=-1)
    return x + _per_batch(ffn, gate2, jnp.multiply)


def _route_rows(padded, brc_ref, run_c, live, cols, idx_ref, pos_ref, wt_ref):
    rows = padded.shape[0]
    logits = padded.T[:N_EXPERTS] + brc_ref[...]

    e_iota = lax.broadcasted_iota(I32, (N_EXPERTS, rows), 0)
    vals, idxs, hots = [], [], []
    cur = logits
    for _ in range(TOP_K):
        m = jnp.max(cur, axis=0, keepdims=True)
        am = jnp.min(jnp.where(cur == m, e_iota, N_EXPERTS), axis=0, keepdims=True)
        hot = e_iota == am
        vals.append(m)
        idxs.append(am)
        hots.append(hot)
        cur = jnp.where(hot, -jnp.inf, cur)
    exps = [jnp.exp(v - vals[0]) for v in vals]
    denom = exps[0] + exps[1] + exps[2] + exps[3]

    sel = (hots[0] | hots[1] | hots[2] | hots[3]).astype(F32)
    ri = lax.broadcasted_iota(I32, (rows, rows), 0)
    ci = lax.broadcasted_iota(I32, (rows, rows), 1)
    earlier = (ri < ci).astype(BF16)
    before = jnp.dot(sel.astype(BF16), earlier, preferred_element_type=F32) + run_c[...]
    run_c[...] = run_c[...] + jnp.where(live, jnp.sum(sel, axis=1, keepdims=True), 0.0)
    poss = [jnp.sum(jnp.where(hot, before, 0.0), axis=0, keepdims=True).astype(I32)
            for hot in hots]
    idx_ref[:, cols] = jnp.concatenate(idxs, axis=0)
    pos_ref[:, cols] = jnp.concatenate(poss, axis=0)
    wt_ref[:, cols] = jnp.concatenate([e / denom for e in exps], axis=0)


VEC_G1, VEC_CONV_B, VEC_BRG_A, VEC_BRG_X, VEC_LAM, VEC_G2, VEC_PSCALE = range(7)


def _mixer_kernel(with_combine, *refs):
    refs = list(refs)
    n_in = 1 + (TOP_K + 1 if with_combine else 0)
    first, nxt = refs[:n_in], refs[n_in:2 * n_in]
    refs = refs[2 * n_in:]
    if with_combine:
        modp_ref = refs.pop(0)
    (mod_ref, vec_ref, win_ref, convw_ref, wrg_ref, wpool_ref, wupa_ref, wupb_ref, wout_ref,
     wrt_ref, brc_ref,
     o_ref, hp_ref, idx_ref, pos_ref, wt_ref, cnt_ref,
     conv_c, p1_c, p2_c, p4_c, p8_c, h_c, run_c, logit_c, x_c, hb_c, xl_c) = refs
    i = pl.program_id(0)
    last = pl.num_programs(0) - 1
    cur = i % 2
    rows = o_ref.shape[0]
    steps = rows // BATCH
    d = D_MODEL
    vec = vec_ref[...]
    row_of = lambda k: vec[k:k + 1]
    shift1, scale1, gate1 = mod_ref[0], mod_ref[1], mod_ref[2]

    def prologue(in_refs):
        if with_combine:
            x = _combined_rows(in_refs[0], in_refs[1:1 + TOP_K], in_refs[1 + TOP_K],
                               modp_ref[5])
        else:
            x = pltpu.einshape("btd->(tb)d", in_refs[0][...])
        return x, _modulated_norm(x, row_of(VEC_G1), shift1, scale1).astype(BF16)

    @pl.when(i == 0)
    def _():
        for ref in (conv_c, p1_c, p2_c, p4_c, p8_c, h_c, run_c, logit_c):
            ref[...] = jnp.zeros(ref.shape, ref.dtype)
        x_c[0], hb_c[0] = prologue(first)
        xl_c[...] = jnp.dot(hb_c[0], win_ref[:, 0:d], preferred_element_type=F32)

    def route(padded, block, live):
        cols = pl.ds(pl.multiple_of(block * rows, rows), rows)
        _route_rows(padded, brc_ref, run_c, live, cols, idx_ref, pos_ref, wt_ref)

    def in_proj(lo, hi):
        return jnp.dot(hb_c[cur], win_ref[:, lo:hi], preferred_element_type=F32)

    x_lru = xl_c[...]
    cw = convw_ref[...]
    xc = (row_of(VEC_CONV_B) + cw[3:4] * x_lru
          + cw[2:3] * _delay(x_lru, 1, conv_c)
          + cw[1:2] * _delay(x_lru, 2, conv_c)
          + cw[0:1] * _delay(x_lru, 3, conv_c))
    _save_tail(x_lru, conv_c)
    xcb = xc.astype(BF16)
    pre = [jnp.dot(xcb[:, k * HEAD_DIM:(k + 1) * HEAD_DIM], wrg_ref[k],
                   preferred_element_type=F32) for k in range(N_HEADS)]
    pre_a = jnp.concatenate([p[:, :HEAD_DIM] for p in pre], axis=-1)
    pre_x = jnp.concatenate([p[:, HEAD_DIM:] for p in pre], axis=-1)
    r = _sigmoid_of_twice(pre_a + row_of(VEC_BRG_A))
    gi = _sigmoid_of_twice(pre_x + row_of(VEC_BRG_X))
    rate = (-LRU_C * LOG2_E) * jax.nn.softplus(-row_of(VEC_LAM))
    a = jnp.exp2(r * rate)
    y = 1.0 - a * a
    mult = jnp.where(y > 0.0, y * lax.rsqrt(y), 0.0)
    u = mult * (gi * xc)
    h = h_c[...]
    hs = []
    for t in range(steps):
        sl = slice(t * BATCH, (t + 1) * BATCH)
        h = a[sl] * h + u[sl]
        hs.append(h)
    h_c[...] = h
    y_lru = jnp.concatenate(hs, axis=0) * jax.nn.gelu(in_proj(d, 2 * d))
    p_a = jnp.dot(y_lru.astype(BF16), wupa_ref[...], preferred_element_type=F32)
    merged = _sigmoid_of_twice(in_proj(2 * d + POOL_WIDTH, 3 * d + POOL_WIDTH)) * p_a

    route(logit_c[...], jnp.maximum(i - 1, 0), i > 0)
    x_c[1 - cur], hb_c[1 - cur] = prologue(nxt)

    x_pool = in_proj(2 * d, 2 * d + POOL_WIDTH)
    g = POOL_GROUP_DIM
    s2 = x_pool + _delay(x_pool, 1, p1_c)
    s4 = s2[:, g:] + _delay(s2[:, g:], 2, p2_c)
    s8 = s4[:, g:] + _delay(s4[:, g:], 4, p4_c)
    s16 = s8[:, g:] + _delay(s8[:, g:], 8, p8_c)
    _save_tail(x_pool, p1_c)
    _save_tail(s2[:, g:], p2_c)
    _save_tail(s4[:, g:], p4_c)
    _save_tail(s8[:, g:], p8_c)
    wins = (s2[:, :g], s4[:, :g], s8[:, :g], s16)
    row = lax.broadcasted_iota(I32, (rows, 1), 0)
    t1 = (i * steps + row // BATCH + 1).astype(F32)
    pooled = []
    for k, w in enumerate(POOL_WINDOWS):
        cnt = jnp.minimum(t1, float(w))
        pk = wins[k] / cnt - x_pool[:, k * g:(k + 1) * g]
        pooled.append(jnp.dot(pk.astype(BF16), wpool_ref[k], preferred_element_type=F32))
    y_pool = jnp.concatenate(pooled, axis=-1) * row_of(VEC_PSCALE)[:, :POOL_WIDTH]
    p_b = jnp.dot(y_pool.astype(BF16), wupb_ref[...], preferred_element_type=F32)
    gates_b = _sigmoid_of_twice(in_proj(3 * d + POOL_WIDTH, 4 * d + POOL_WIDTH))
    merged = merged + gates_b * p_b

    mix = jnp.dot(merged.astype(BF16), wout_ref[...], preferred_element_type=F32)
    x_mid = x_c[cur] + _per_batch(mix, gate1, jnp.multiply)
    o_ref[...] = x_mid
    xl_c[...] = jnp.dot(hb_c[1 - cur], win_ref[:, 0:d], preferred_element_type=F32)

    hr = _modulated_norm(x_mid, row_of(VEC_G2), mod_ref[3], mod_ref[4])
    hp_ref[...] = _pack_rows(hr)
    padded = jnp.dot(hr.astype(BF16), wrt_ref[...], preferred_element_type=F32)
    logit_c[...] = padded

    @pl.when(i == last)
    def _():
        route(logit_c[...], i, True)

    cnt_ref[...] = run_c[...].astype(I32)


def _pending_specs(rows, n_tok, block_of):
    nb = n_tok // rows
    y_spec = lambda k: pl.BlockSpec((rows, PACKED), lambda i: (k * nb + block_of(i), 0))
    return ([y_spec(k) for k in range(TOP_K)]
            + [pl.BlockSpec((rows, TOP_K), lambda i: (block_of(i), 0))])


def _mixer_layer(x_in, pending, mod_l, vec, w_in, conv_w, w_rg, w_pool, w_up_a, w_up_b, w_out,
                 w_route, b_router):
    rows = MIX_STEPS * BATCH
    weights = [mod_l, vec, w_in, conv_w, w_rg, w_pool, w_up_a, w_up_b, w_out, w_route,
               b_router.reshape(N_EXPERTS, 1)]
    if pending is None:
        n_tok, d = x_in.shape[0] * x_in.shape[1], x_in.shape[2]
        inputs = [x_in]
        specs = lambda block_of: [
            pl.BlockSpec((BATCH, MIX_STEPS, d), lambda i: (0, block_of(i), 0))]
    else:
        n_tok, d = x_in.shape
        y4, wts_prev, mod_prev = pending
        inputs = [x_in, y4, y4, y4, y4, wts_prev]
        weights.insert(0, mod_prev)
        specs = lambda block_of: ([pl.BlockSpec((rows, d), lambda i: (block_of(i), 0))]
                                  + _pending_specs(rows, n_tok, block_of))
    nb = n_tok // rows
    in_specs = (specs(lambda i: 0) + specs(lambda i: jnp.minimum(i + 1, nb - 1))
                + [_const_spec(a.shape) for a in weights])
    g = POOL_GROUP_DIM
    scratch = [
        pltpu.VMEM(((CONV_WIDTH - 1) * BATCH, d), F32),
        pltpu.VMEM((1 * BATCH, 4 * g), F32), pltpu.VMEM((2 * BATCH, 3 * g), F32),
        pltpu.VMEM((4 * BATCH, 2 * g), F32), pltpu.VMEM((8 * BATCH, g), F32),
        pltpu.VMEM((BATCH, d), F32),
        pltpu.VMEM((N_EXPERTS, 1), F32),
        pltpu.VMEM((rows, ROUTER_COLS), F32),
        pltpu.VMEM((2, rows, d), F32), pltpu.VMEM((2, rows, d), BF16),
        pltpu.VMEM((rows, d), F32),
    ]
    routed = lambda dtype: jax.ShapeDtypeStruct((TOP_K, n_tok), dtype)
    whole = lambda shape: pl.BlockSpec(shape, lambda i: (0,) * len(shape))
    routed_spec = whole((TOP_K, n_tok))
    return pl.pallas_call(
        functools.partial(_mixer_kernel, pending is not None),
        out_shape=(
            jax.ShapeDtypeStruct((n_tok, d), F32),
            jax.ShapeDtypeStruct((n_tok, PACKED), I32),
            routed(I32), routed(I32), routed(F32),
            jax.ShapeDtypeStruct((N_EXPERTS, 1), I32),
        ),
        grid=(nb,),
        in_specs=in_specs,
        out_specs=(
            pl.BlockSpec((rows, d), lambda i: (i, 0)),
            pl.BlockSpec((rows, PACKED), lambda i: (i, 0)),
            routed_spec, routed_spec, routed_spec,
            whole((N_EXPERTS, 1)),
        ),
        scratch_shapes=scratch,
        compiler_params=pltpu.CompilerParams(
            dimension_semantics=("arbitrary",), vmem_limit_bytes=VMEM_LIMIT),
        name="mixer",
    )(*(inputs + inputs + weights))


def _padded_router_weight(w_router):
    pad = jnp.zeros((w_router.shape[0], ROUTER_COLS - N_EXPERTS), BF16)
    return jnp.concatenate([w_router.astype(BF16), pad], axis=1)


def _sc_mesh():
    return plsc.VectorSubcoreMesh(core_axis_name="c", subcore_axis_name="s")


def _sc_worker_id():
    return lax.axis_index("s") * SC_CORES + lax.axis_index("c")


def _maybe(cond, fn):
    if isinstance(cond, bool):
        if cond:
            fn()
    else:
        pl.when(cond)(fn)


def _ring_loop(n_chunks, step):
    main = (n_chunks // SC_BUFFERS) * SC_BUFFERS

    @pl.loop(0, main, step=SC_BUFFERS)
    def _(j0):
        for buf in range(SC_BUFFERS):
            step(j0 + buf, buf)

    for j in range(main, n_chunks):
        step(j, j % SC_BUFFERS)


def _sc_dispatch(h_packed, dest_sc, n_slots):
    n_tok, width = h_packed.shape
    per_w = n_tok // SC_WORKERS
    n_chunks = per_w // SC_CHUNK
    ahead = SC_BUFFERS - 1

    def body(h_hbm, dest_hbm, xs_hbm, idx_v, rows_v, sem_in, sem_out):
        base = _sc_worker_id() * per_w
        pltpu.sync_copy(dest_hbm.at[_sc_worker_id()], idx_v)

        def load(j, buf):
            return pltpu.make_async_copy(
                h_hbm.at[pl.ds(base + j * SC_CHUNK, SC_CHUNK)], rows_v.at[buf], sem_in.at[buf])

        def scatters(j, buf):
            return [pltpu.make_async_copy(
                rows_v.at[buf], xs_hbm.at[idx_v.at[j * TOP_K + k]], sem_out.at[buf])
                for k in range(TOP_K)]

        for j in range(min(ahead, n_chunks)):
            load(j, j).start()

        def step(j, buf):
            prev = (buf - 1) % SC_BUFFERS
            load(j, buf).wait()
            for cp in scatters(j, buf):
                cp.start()

            def drain_prev():
                for cp in scatters(j - 1, prev):
                    cp.wait()

            _maybe(j >= 1, drain_prev)
            _maybe(j + ahead < n_chunks, lambda: load(j + ahead, prev).start())

        _ring_loop(n_chunks, step)
        for cp in scatters(n_chunks - 1, (n_chunks - 1) % SC_BUFFERS):
            cp.wait()

    return pl.kernel(
        body,
        out_type=jax.ShapeDtypeStruct((n_slots, width), I32),
        mesh=_sc_mesh(),
        scratch_types=[
            pltpu.VMEM((n_chunks * TOP_K, SC_CHUNK), I32),
            pltpu.VMEM((SC_BUFFERS, SC_CHUNK, width), I32),
            pltpu.SemaphoreType.DMA((SC_BUFFERS,)),
            pltpu.SemaphoreType.DMA((SC_BUFFERS,)),
        ],
        name="sc_dispatch",
    )(h_packed, dest_sc)


def _sc_gather(ys_packed, dest_sc):
    width = ys_packed.shape[1]
    n_chunks = dest_sc.shape[1]
    per_w = n_chunks * SC_CHUNK
    n_rows = SC_WORKERS * per_w
    ahead = SC_BUFFERS - 1

    def body(ys_hbm, dest_hbm, out_hbm, idx_v, rows_v, sem_g, sem_w):
        base = _sc_worker_id() * per_w
        pltpu.sync_copy(dest_hbm.at[_sc_worker_id()], idx_v)

        def gather(j, buf):
            return pltpu.make_async_copy(
                ys_hbm.at[idx_v.at[j]], rows_v.at[buf], sem_g.at[buf])

        def writeout(j, buf):
            return pltpu.make_async_copy(
                rows_v.at[buf], out_hbm.at[pl.ds(base + j * SC_CHUNK, SC_CHUNK)],
                sem_w.at[buf])

        for j in range(min(ahead, n_chunks)):
            gather(j, j).start()

        def step(j, buf):
            prev = (buf - 1) % SC_BUFFERS
            gather(j, buf).wait()
            writeout(j, buf).start()
            _maybe(j >= 1, lambda: writeout(j - 1, prev).wait())
            _maybe(j + ahead < n_chunks, lambda: gather(j + ahead, prev).start())

        _ring_loop(n_chunks, step)
        writeout(n_chunks - 1, (n_chunks - 1) % SC_BUFFERS).wait()

    return pl.kernel(
        body,
        out_type=jax.ShapeDtypeStruct((n_rows, width), I32),
        mesh=_sc_mesh(),
        scratch_types=[
            pltpu.VMEM((n_chunks, SC_CHUNK), I32),
            pltpu.VMEM((SC_BUFFERS, SC_CHUNK, width), I32),
            pltpu.SemaphoreType.DMA((SC_BUFFERS,)),
            pltpu.SemaphoreType.DMA((SC_BUFFERS,)),
        ],
        name="sc_gather",
    )(ys_packed, dest_sc)


def _expert_kernel(layer, be_ref, nv_ref, first_ref, next_ref, xs_ref, win_hbm, bin_ref,
                   wout_hbm, bout_ref, ys_ref, win_f32, wout_f32, win_bf, wout_bf, sem):
    i = pl.program_id(0)
    valid = nv_ref[i]

    def fetch(e):
        return (pltpu.make_async_copy(win_hbm.at[layer, e], win_f32, sem.at[0]),
                pltpu.make_async_copy(wout_hbm.at[layer, e], wout_f32, sem.at[1]))

    @pl.when(first_ref[i] == 1)
    def _():
        @pl.when(i == 0)
        def _():
            for cp in fetch(be_ref[i]):
                cp.start()

        for cp in fetch(be_ref[i]):
            cp.wait()

        def cast_in(r, c):
            rows = pl.ds(pl.multiple_of(r * CAST_ROWS, CAST_ROWS), CAST_ROWS)
            win_bf[rows, :] = win_f32[rows, :].astype(BF16)
            return c

        def cast_out(r, c):
            rows = pl.ds(pl.multiple_of(r * CAST_ROWS, CAST_ROWS), CAST_ROWS)
            wout_bf[rows, :] = wout_f32[rows, :].astype(BF16)
            return c

        lax.fori_loop(0, D_MODEL // CAST_ROWS, cast_in, 0)
        lax.fori_loop(0, D_FF // CAST_ROWS, cast_out, 0)

        @pl.when(next_ref[i] >= 0)
        def _():
            for cp in fetch(next_ref[i]):
                cp.start()

    def load_rows(start, size):
        rows = pl.ds(start, size)
        row = lax.broadcasted_iota(I32, (size, 1), 0)
        lo, hi = _unpack_rows(jnp.where(row + start < valid, xs_ref[rows, :], 0))
        return lo.astype(BF16), hi.astype(BF16)

    def up_proj(lo, hi):
        return (jnp.dot(lo, win_bf[:PACKED, :], preferred_element_type=F32)
                + jnp.dot(hi, win_bf[PACKED:, :], preferred_element_type=F32) + bin_ref[0])

    def activation(gu):
        gate = jnp.minimum(gu[:, :D_FF], SWIGLU_LIMIT)
        up = jnp.clip(gu[:, D_FF:], -SWIGLU_LIMIT, SWIGLU_LIMIT)
        gated = gate * _sigmoid_of_twice(0.5 * (SWIGLU_ALPHA * gate))
        return (gated * (up + 1.0)).astype(BF16)

    def down_proj(act):
        return jnp.dot(act, wout_bf[...], preferred_element_type=F32) + bout_ref[0]

    def store_rows(start, size, out):
        rows = pl.ds(start, size)
        ys_ref[rows, :] = _pack_rows(out)

    def pipelined(n_pass):
        size = EXPERT_PASS
        starts = range(0, n_pass * size, size)
        gus, outs = {}, {}
        for n, start in enumerate(starts):
            gus[start] = up_proj(*load_rows(start, size))
            if n > 0:
                prev = starts[n - 1]
                outs[prev] = down_proj(activation(gus.pop(prev)))
            if n > 1:
                store_rows(starts[n - 2], size, outs.pop(starts[n - 2]))
        outs[starts[-1]] = down_proj(activation(gus.pop(starts[-1])))
        for start in list(outs):
            store_rows(start, size, outs.pop(start))

    passes = (valid + EXPERT_PASS - 1) // EXPERT_PASS
    for n_pass in range(1, EXPERT_ROWS // EXPERT_PASS + 1):
        pl.when(passes == n_pass)(functools.partial(pipelined, n_pass))

    def clear(j, c):
        rows = pl.ds(pl.multiple_of(j * EXPERT_PASS, EXPERT_PASS), EXPERT_PASS)
        ys_ref[rows, :] = jnp.zeros((EXPERT_PASS, ys_ref.shape[1]), ys_ref.dtype)
        return c

    lax.fori_loop(passes, EXPERT_ROWS // EXPERT_PASS, clear, 0)


def _experts(layer, block_e, n_valid, is_first, next_e, xs, w_e_in, b_e_in, w_e_out, b_e_out):
    n_slots, width = xs.shape
    d = D_MODEL
    tm = EXPERT_ROWS
    bias_index = lambda i, be, nv, fi, nx: (layer, be[i], 0, 0)
    grid_spec = pltpu.PrefetchScalarGridSpec(
        num_scalar_prefetch=4,
        grid=(n_slots // tm,),
        in_specs=[
            pl.BlockSpec((tm, width), lambda i, be, nv, fi, nx: (i, 0)),
            pl.BlockSpec(memory_space=pl.ANY),
            pl.BlockSpec((None, 1, 1, 2 * D_FF), bias_index),
            pl.BlockSpec(memory_space=pl.ANY),
            pl.BlockSpec((None, 1, 1, d), bias_index),
        ],
        out_specs=pl.BlockSpec((tm, width), lambda i, be, nv, fi, nx: (i, 0)),
        scratch_shapes=[
            pltpu.VMEM((d, 2 * D_FF), F32), pltpu.VMEM((D_FF, d), F32),
            pltpu.VMEM((d, 2 * D_FF), BF16), pltpu.VMEM((D_FF, d), BF16),
            pltpu.SemaphoreType.DMA((2,)),
        ],
    )
    return pl.pallas_call(
        functools.partial(_expert_kernel, layer),
        out_shape=jax.ShapeDtypeStruct((n_slots, width), I32),
        grid_spec=grid_spec,
        compiler_params=pltpu.CompilerParams(
            dimension_semantics=("arbitrary",), vmem_limit_bytes=VMEM_LIMIT),
        name="experts",
    )(block_e, n_valid, is_first, next_e, xs, w_e_in,
      b_e_in.reshape(-1, N_EXPERTS, 1, 2 * D_FF), w_e_out, b_e_out.reshape(-1, N_EXPERTS, 1, d))


def _final_kernel(x_ref, y0_ref, y1_ref, y2_ref, y3_ref, wt_ref, mod_ref, g_ref, o_ref):
    x = _combined_rows(x_ref, (y0_ref, y1_ref, y2_ref, y3_ref), wt_ref, mod_ref[5])
    ms = jnp.mean(x * x, axis=-1, keepdims=True)
    out = x * lax.rsqrt(ms + NORM_EPS) * g_ref[...]
    o_ref[...] = pltpu.einshape("(tb)d->btd", out, b=BATCH)


def _final_norm(x2d, pending, g):
    n_tok, d = x2d.shape
    tr = NORM_ROWS
    y4, wts, mod_l = pending
    return pl.pallas_call(
        _final_kernel,
        out_shape=jax.ShapeDtypeStruct((BATCH, n_tok // BATCH, d), F32),
        grid=(n_tok // tr,),
        in_specs=([pl.BlockSpec((tr, d), lambda i: (i, 0))]
                  + _pending_specs(tr, n_tok, lambda i: i)
                  + [_const_spec(mod_l.shape), _const_spec((1, d))]),
        out_specs=pl.BlockSpec((BATCH, tr // BATCH, d), lambda i: (0, i, 0)),
        compiler_params=pltpu.CompilerParams(
            dimension_semantics=("arbitrary",), vmem_limit_bytes=VMEM_LIMIT),
        name="final_norm",
    )(x2d, y4, y4, y4, y4, wts, mod_l, g.reshape(1, d))


def _moe_layer(layer, h_packed, idx, pos, counts, w_e_in, b_e_in, w_e_out, b_e_out):
    n_tok = h_packed.shape[0]
    tm = EXPERT_ROWS
    padded = ((counts + tm - 1) // tm) * tm
    pad_end = jnp.cumsum(padded)
    start_pad = pad_end - padded
    experts = jnp.arange(N_EXPERTS, dtype=I32)
    dest = jnp.sum(jnp.where(idx[:, :, None] == experts, start_pad, 0), axis=-1) + pos
    n_blocks = (n_tok * TOP_K) // tm + N_EXPERTS
    n_slots = n_blocks * tm
    block_start = jnp.arange(n_blocks, dtype=I32) * tm
    block_e = jnp.minimum(
        jnp.sum((block_start[:, None] >= pad_end[None, :]).astype(I32), axis=-1),
        N_EXPERTS - 1)
    hot_e = block_e[:, None] == experts[None, :]
    rows_left = jnp.sum(jnp.where(hot_e, (start_pad + counts)[None, :], 0), axis=-1) - block_start
    n_valid = jnp.where(block_start < pad_end[-1], jnp.clip(rows_left, 0, tm), 0).astype(I32)
    is_first = jnp.sum(jnp.where(hot_e, start_pad[None, :], 0), axis=-1) == block_start
    is_first = (is_first & (n_valid > 0)).astype(I32)
    later = (experts[None, :] > experts[:, None]) & (counts[None, :] > 0)
    next_of = jnp.min(jnp.where(later, experts[None, :], N_EXPERTS), axis=-1)
    next_of = jnp.where(next_of == N_EXPERTS, -1, next_of)
    next_e = jnp.sum(jnp.where(hot_e, next_of[None, :], 0), axis=-1).astype(I32)

    per_w = n_tok // SC_WORKERS
    n_chunks = per_w // SC_CHUNK
    dest_scatter = (dest.reshape(TOP_K, SC_WORKERS, n_chunks, SC_CHUNK)
                    .transpose(1, 2, 0, 3)
                    .reshape(SC_WORKERS, n_chunks * TOP_K, SC_CHUNK))
    xs = _sc_dispatch(h_packed, dest_scatter, n_slots)
    ys = _experts(layer, block_e, n_valid, is_first, next_e, xs,
                  w_e_in, b_e_in, w_e_out, b_e_out)
    dest_gather = dest.reshape(SC_WORKERS, n_chunks * TOP_K, SC_CHUNK)
    return _sc_gather(ys, dest_gather)


def kernel(x, c, norm1_g, norm2_g, w_ada, b_ada, w_in, conv_w, conv_b, w_rg_a, b_rg_a,
           w_rg_x, b_rg_x, lru_lambda, w_pool, pool_scale, w_up_a, w_up_b, w_out,
           w_router, b_router, w_e_in, b_e_in, w_e_out, b_e_out, final_g):
    depth = w_in.shape[0]
    bsz, seq, d = x.shape
    assert bsz == BATCH and d == D_MODEL
    mod = _ada_mod(c, w_ada, b_ada).reshape(depth, bsz, 6, d).transpose(0, 2, 1, 3)
    w_rg = (0.5 * jnp.concatenate([w_rg_a, w_rg_x], axis=-1)).astype(BF16)
    gate_cols = jnp.arange(w_in.shape[2]) >= 2 * d + POOL_WIDTH
    w_in_scaled = jnp.where(gate_cols, 0.5, 1.0) * w_in
    pending = None
    xcur = x
    for l in range(depth):
        pad = jnp.zeros((d - pool_scale.shape[1],), F32)
        vec = jnp.stack([norm1_g[l], conv_b[l], 0.5 * b_rg_a[l], 0.5 * b_rg_x[l],
                         lru_lambda[l], norm2_g[l], jnp.concatenate([pool_scale[l], pad]),
                         jnp.zeros((d,), F32)])
        xcur, h_packed, idx, pos, wts, counts = _mixer_layer(
            xcur, pending, mod[l], vec, w_in_scaled[l].astype(BF16), conv_w[l], w_rg[l],
            w_pool[l].astype(BF16), w_up_a[l].astype(BF16), w_up_b[l].astype(BF16),
            w_out[l].astype(BF16), _padded_router_weight(w_router[l]), b_router[l])
        y4 = _moe_layer(l, h_packed, idx, pos, counts[:, 0], w_e_in, b_e_in, w_e_out, b_e_out)
        pending = (y4, wts.T, mod[l])
    return _final_norm(xcur, pending, final_g)
```

```python
import functools
import math

import jax
import jax.numpy as jnp
from jax import lax
from jax.experimental import pallas as pl
from jax.experimental.pallas import tpu as pltpu
from jax.experimental.pallas import tpu_sc as plsc

D_MODEL = 1024
BATCH = 8
N_HEADS = 8
HEAD_DIM = 128
CONV_WIDTH = 4
LRU_C = 8.0
POOL_WINDOWS = (2, 4, 8, 16)
POOL_WIDTH = 512
POOL_GROUP_DIM = 128
N_EXPERTS = 32
TOP_K = 4
D_FF = 1024
SWIGLU_LIMIT = 7.0
SWIGLU_ALPHA = 1.702
NORM_EPS = 1e-6

VMEM_LIMIT = 56 * 1024 * 1024
SC_CORES = 2
SC_SUBCORES = 16
SC_WORKERS = SC_CORES * SC_SUBCORES
SC_CHUNK = 64

MIX_STEPS = 32
ROUTER_COLS = 128
EXPERT_ROWS = 1024
EXPERT_PASS = 256
ADA_COLS = 3072
NORM_ROWS = 512
CAST_ROWS = 128
PACKED = D_MODEL // 2

F32 = jnp.float32
BF16 = jnp.bfloat16
I32 = jnp.int32
HIGH_MASK = -65536
LOG2_E = math.log2(math.e)


def _const_spec(shape):
    nd = len(shape)
    return pl.BlockSpec(shape, lambda *_: (0,) * nd, pipeline_mode=pl.Buffered(1))


def _pack_rows(v):
    packed = pltpu.pack_elementwise([v[:, :PACKED], v[:, PACKED:]], packed_dtype=BF16)
    return lax.bitcast_convert_type(packed, I32)


def _unpack_rows(p):
    lo = lax.bitcast_convert_type(lax.shift_left(p, 16), F32)
    hi = lax.bitcast_convert_type(p & HIGH_MASK, F32)
    return lo, hi


def _sigmoid_of_twice(t):
    return 0.5 * jnp.tanh(t) + 0.5


def _per_batch(v, slab, op):
    rows, d = v.shape
    return op(v.reshape(rows // BATCH, BATCH, d), slab[None]).reshape(rows, d)


def _modulated_norm(x, g, shift, scale):
    ms = jnp.mean(x * x, axis=-1, keepdims=True)
    hn = x * lax.rsqrt(ms + NORM_EPS)
    return _per_batch(_per_batch(hn, g * (1.0 + scale), jnp.multiply), shift, jnp.add)


def _ada_kernel(c_ref, w_ref, b_ref, o_ref):
    c = c_ref[...]
    c_act = c * jax.nn.sigmoid(c)
    o_ref[0] = jnp.dot(c_act, w_ref[0], preferred_element_type=F32,
                       precision=lax.Precision.HIGHEST) + b_ref[0]


def _ada_mod(c, w_ada, b_ada):
    depth, d, n = w_ada.shape
    bsz = c.shape[0]
    tn = ADA_COLS
    return pl.pallas_call(
        _ada_kernel,
        out_shape=jax.ShapeDtypeStruct((depth, bsz, n), F32),
        grid=(depth, n // tn),
        in_specs=[
            pl.BlockSpec((bsz, d), lambda l, j: (0, 0)),
            pl.BlockSpec((1, d, tn), lambda l, j: (l, 0, j)),
            pl.BlockSpec((1, 1, tn), lambda l, j: (l, 0, j)),
        ],
        out_specs=pl.BlockSpec((1, bsz, tn), lambda l, j: (l, 0, j)),
        compiler_params=pltpu.CompilerParams(
            dimension_semantics=("arbitrary", "arbitrary"),
            vmem_limit_bytes=VMEM_LIMIT),
        name="ada_mod",
    )(c, w_ada, b_ada.reshape(depth, 1, n))


def _delay(a, steps, carry_ref):
    n = steps * BATCH
    keep = carry_ref.shape[0]
    prev = carry_ref[keep - n:, :]
    return jnp.concatenate([prev, a[:a.shape[0] - n]], axis=0)


def _save_tail(a, carry_ref):
    carry_ref[...] = a[a.shape[0] - carry_ref.shape[0]:]


def _combined_rows(x_ref, y_refs, wt_ref, gate2):
    x = x_ref[...]
    wt = wt_ref[...]
    acc_lo = jnp.zeros((x.shape[0], PACKED), F32)
    acc_hi = jnp.zeros((x.shape[0], PACKED), F32)
    for k, y_ref in enumerate(y_refs):
        lo, hi = _unpack_rows(y_ref[...])
        wk = wt[:, k:k + 1]
        acc_lo = acc_lo + wk * lo
        acc_hi = acc_hi + wk * hi
    ffn = jnp.concatenate([acc_lo, acc_hi], axis=-1)
    return x + _per_batch(ffn, gate2, jnp.multiply)


def _route_rows(padded, brc_ref, run_c, live, cols, idx_ref, pos_ref, wt_ref):
    rows = padded.shape[0]
    logits = padded.T[:N_EXPERTS] + brc_ref[...]

    e_iota = lax.broadcasted_iota(I32, (N_EXPERTS, rows), 0)
    vals, idxs, hots = [], [], []
    cur = logits
    for _ in range(TOP_K):
        m = jnp.max(cur, axis=0, keepdims=True)
        am = jnp.min(jnp.where(cur == m, e_iota, N_EXPERTS), axis=0, keepdims=True)
        hot = e_iota == am
        vals.append(m)
        idxs.append(am)
        hots.append(hot)
        cur = jnp.where(hot, -jnp.inf, cur)
    exps = [jnp.exp(v - vals[0]) for v in vals]
    denom = exps[0] + exps[1] + exps[2] + exps[3]

    sel = (hots[0] | hots[1] | hots[2] | hots[3]).astype(F32)
    ri = lax.broadcasted_iota(I32, (rows, rows), 0)
    ci = lax.broadcasted_iota(I32, (rows, rows), 1)
    earlier = (ri < ci).astype(BF16)
    before = jnp.dot(sel.astype(BF16), earlier, preferred_element_type=F32) + run_c[...]
    run_c[...] = run_c[...] + jnp.where(live, jnp.sum(sel, axis=1, keepdims=True), 0.0)
    poss = [jnp.sum(jnp.where(hot, before, 0.0), axis=0, keepdims=True).astype(I32)
            for hot in hots]
    idx_ref[:, cols] = jnp.concatenate(idxs, axis=0)
    pos_ref[:, cols] = jnp.concatenate(poss, axis=0)
    wt_ref[:, cols] = jnp.concatenate([e / denom for e in exps], axis=0)


VEC_G1, VEC_CONV_B, VEC_BRG_A, VEC_BRG_X, VEC_LAM, VEC_G2, VEC_PSCALE = range(7)


def _mixer_kernel(with_combine, *refs):
    refs = list(refs)
    n_in = 1 + (TOP_K + 1 if with_combine else 0)
    first, nxt = refs[:n_in], refs[n_in:2 * n_in]
    refs = refs[2 * n_in:]
    if with_combine:
        modp_ref = refs.pop(0)
    (mod_ref, vec_ref, win_ref, convw_ref, wrg_ref, wpool_ref, wupa_ref, wupb_ref, wout_ref,
     wrt_ref, brc_ref,
     o_ref, hp_ref, idx_ref, pos_ref, wt_ref, cnt_ref,
     conv_c, p1_c, p2_c, p4_c, p8_c, h_c, run_c, logit_c, x_c, hb_c, xl_c) = refs
    i = pl.program_id(0)
    last = pl.num_programs(0) - 1
    cur = i % 2
    rows = o_ref.shape[0]
    steps = rows // BATCH
    d = D_MODEL
    vec = vec_ref[...]
    row_of = lambda k: vec[k:k + 1]
    shift1, scale1, gate1 = mod_ref[0], mod_ref[1], mod_ref[2]

    def prologue(in_refs):
        if with_combine:
            x = _combined_rows(in_refs[0], in_refs[1:1 + TOP_K], in_refs[1 + TOP_K],
                               modp_ref[5])
        else:
            x = pltpu.einshape("btd->(tb)d", in_refs[0][...])
        return x, _modulated_norm(x, row_of(VEC_G1), shift1, scale1).astype(BF16)

    @pl.when(i == 0)
    def _():
        for ref in (conv_c, p1_c, p2_c, p4_c, p8_c, h_c, run_c, logit_c):
            ref[...] = jnp.zeros(ref.shape, ref.dtype)
        x_c[0], hb_c[0] = prologue(first)
        xl_c[...] = jnp.dot(hb_c[0], win_ref[:, 0:d], preferred_element_type=F32)

    def route(padded, block, live):
        cols = pl.ds(pl.multiple_of(block * rows, rows), rows)
        _route_rows(padded, brc_ref, run_c, live, cols, idx_ref, pos_ref, wt_ref)

    def in_proj(lo, hi):
        return jnp.dot(hb_c[cur], win_ref[:, lo:hi], preferred_element_type=F32)

    x_lru = xl_c[...]
    cw = convw_ref[...]
    xc = (row_of(VEC_CONV_B) + cw[3:4] * x_lru
          + cw[2:3] * _delay(x_lru, 1, conv_c)
          + cw[1:2] * _delay(x_lru, 2, conv_c)
          + cw[0:1] * _delay(x_lru, 3, conv_c))
    _save_tail(x_lru, conv_c)
    xcb = xc.astype(BF16)
    pre = [jnp.dot(xcb[:, k * HEAD_DIM:(k + 1) * HEAD_DIM], wrg_ref[k],
                   preferred_element_type=F32) for k in range(N_HEADS)]
    pre_a = jnp.concatenate([p[:, :HEAD_DIM] for p in pre], axis=-1)
    pre_x = jnp.concatenate([p[:, HEAD_DIM:] for p in pre], axis=-1)
    r = _sigmoid_of_twice(pre_a + row_of(VEC_BRG_A))
    gi = _sigmoid_of_twice(pre_x + row_of(VEC_BRG_X))
    rate = (-LRU_C * LOG2_E) * jax.nn.softplus(-row_of(VEC_LAM))
    a = jnp.exp2(r * rate)
    y = 1.0 - a * a
    mult = jnp.where(y > 0.0, y * lax.rsqrt(y), 0.0)
    u = mult * (gi * xc)
    h = h_c[...]
    hs = []
    for t in range(steps):
        sl = slice(t * BATCH, (t + 1) * BATCH)
        h = a[sl] * h + u[sl]
        hs.append(h)
    h_c[...] = h
    y_lru = jnp.concatenate(hs, axis=0) * jax.nn.gelu(in_proj(d, 2 * d))
    p_a = jnp.dot(y_lru.astype(BF16), wupa_ref[...], preferred_element_type=F32)
    merged = _sigmoid_of_twice(in_proj(2 * d + POOL_WIDTH, 3 * d + POOL_WIDTH)) * p_a

    route(logit_c[...], jnp.maximum(i - 1, 0), i > 0)
    x_c[1 - cur], hb_c[1 - cur] = prologue(nxt)

    x_pool = in_proj(2 * d, 2 * d + POOL_WIDTH)
    g = POOL_GROUP_DIM
    s2 = x_pool + _delay(x_pool, 1, p1_c)
    s4 = s2[:, g:] + _delay(s2[:, g:], 2, p2_c)
    s8 = s4[:, g:] + _delay(s4[:, g:], 4, p4_c)
    s16 = s8[:, g:] + _delay(s8[:, g:], 8, p8_c)
    _save_tail(x_pool, p1_c)
    _save_tail(s2[:, g:], p2_c)
    _save_tail(s4[:, g:], p4_c)
    _save_tail(s8[:, g:], p8_c)
    wins = (s2[:, :g], s4[:, :g], s8[:, :g], s16)
    row = lax.broadcasted_iota(I32, (rows, 1), 0)
    t1 = (i * steps + row // BATCH + 1).astype(F32)
    pooled = []
    for k, w in enumerate(POOL_WINDOWS):
        cnt = jnp.minimum(t1, float(w))
        pk = wins[k] / cnt - x_pool[:, k * g:(k + 1) * g]
        pooled.append(jnp.dot(pk.astype(BF16), wpool_ref[k], preferred_element_type=F32))
    y_pool = jnp.concatenate(pooled, axis=-1) * row_of(VEC_PSCALE)[:, :POOL_WIDTH]
    p_b = jnp.dot(y_pool.astype(BF16), wupb_ref[...], preferred_element_type=F32)
    gates_b = _sigmoid_of_twice(in_proj(3 * d + POOL_WIDTH, 4 * d + POOL_WIDTH))
    merged = merged + gates_b * p_b

    mix = jnp.dot(merged.astype(BF16), wout_ref[...], preferred_element_type=F32)
    x_mid = x_c[cur] + _per_batch(mix, gate1, jnp.multiply)
    o_ref[...] = x_mid
    xl_c[...] = jnp.dot(hb_c[1 - cur], win_ref[:, 0:d], preferred_element_type=F32)

    hr = _modulated_norm(x_mid, row_of(VEC_G2), mod_ref[3], mod_ref[4])
    hp_ref[...] = _pack_rows(hr)
    padded = jnp.dot(hr.astype(BF16), wrt_ref[...], preferred_element_type=F32)
    logit_c[...] = padded

    @pl.when(i == last)
    def _():
        route(logit_c[...], i, True)

    cnt_ref[...] = run_c[...].astype(I32)


def _pending_specs(rows, n_tok, block_of):
    nb = n_tok // rows
    y_spec = lambda k: pl.BlockSpec((rows, PACKED), lambda i: (k * nb + block_of(i), 0))
    return ([y_spec(k) for k in range(TOP_K)]
            + [pl.BlockSpec((rows, TOP_K), lambda i: (block_of(i), 0))])


def _mixer_layer(x_in, pending, mod_l, vec, w_in, conv_w, w_rg, w_pool, w_up_a, w_up_b, w_out,
                 w_route, b_router):
    rows = MIX_STEPS * BATCH
    weights = [mod_l, vec, w_in, conv_w, w_rg, w_pool, w_up_a, w_up_b, w_out, w_route,
               b_router.reshape(N_EXPERTS, 1)]
    if pending is None:
        n_tok, d = x_in.shape[0] * x_in.shape[1], x_in.shape[2]
        inputs = [x_in]
        specs = lambda block_of: [
            pl.BlockSpec((BATCH, MIX_STEPS, d), lambda i: (0, block_of(i), 0))]
    else:
        n_tok, d = x_in.shape
        y4, wts_prev, mod_prev = pending
        inputs = [x_in, y4, y4, y4, y4, wts_prev]
        weights.insert(0, mod_prev)
        specs = lambda block_of: ([pl.BlockSpec((rows, d), lambda i: (block_of(i), 0))]
                                  + _pending_specs(rows, n_tok, block_of))
    nb = n_tok // rows
    in_specs = (specs(lambda i: 0) + specs(lambda i: jnp.minimum(i + 1, nb - 1))
                + [_const_spec(a.shape) for a in weights])
    g = POOL_GROUP_DIM
    scratch = [
        pltpu.VMEM(((CONV_WIDTH - 1) * BATCH, d), F32),
        pltpu.VMEM((1 * BATCH, 4 * g), F32), pltpu.VMEM((2 * BATCH, 3 * g), F32),
        pltpu.VMEM((4 * BATCH, 2 * g), F32), pltpu.VMEM((8 * BATCH, g), F32),
        pltpu.VMEM((BATCH, d), F32),
        pltpu.VMEM((N_EXPERTS, 1), F32),
        pltpu.VMEM((rows, ROUTER_COLS), F32),
        pltpu.VMEM((2, rows, d), F32), pltpu.VMEM((2, rows, d), BF16),
        pltpu.VMEM((rows, d), F32),
    ]
    routed = lambda dtype: jax.ShapeDtypeStruct((TOP_K, n_tok), dtype)
    whole = lambda shape: pl.BlockSpec(shape, lambda i: (0,) * len(shape))
    routed_spec = whole((TOP_K, n_tok))
    return pl.pallas_call(
        functools.partial(_mixer_kernel, pending is not None),
        out_shape=(
            jax.ShapeDtypeStruct((n_tok, d), F32),
            jax.ShapeDtypeStruct((n_tok, PACKED), I32),
            routed(I32), routed(I32), routed(F32),
            jax.ShapeDtypeStruct((N_EXPERTS, 1), I32),
        ),
        grid=(nb,),
        in_specs=in_specs,
        out_specs=(
            pl.BlockSpec((rows, d), lambda i: (i, 0)),
            pl.BlockSpec((rows, PACKED), lambda i: (i, 0)),
            routed_spec, routed_spec, routed_spec,
            whole((N_EXPERTS, 1)),
        ),
        scratch_shapes=scratch,
        compiler_params=pltpu.CompilerParams(
            dimension_semantics=("arbitrary",), vmem_limit_bytes=VMEM_LIMIT),
        name="mixer",
    )(*(inputs + inputs + weights))


def _padded_router_weight(w_router):
    pad = jnp.zeros((w_router.shape[0], ROUTER_COLS - N_EXPERTS), BF16)
    return jnp.concatenate([w_router.astype(BF16), pad], axis=1)


def _sc_mesh():
    return plsc.VectorSubcoreMesh(core_axis_name="c", subcore_axis_name="s")


def _sc_worker_id():
    return lax.axis_index("s") * SC_CORES + lax.axis_index("c")


def _sc_dispatch(h_packed, dest_sc, n_slots):
    n_tok, width = h_packed.shape
    per_w = n_tok // SC_WORKERS
    n_chunks = per_w // SC_CHUNK

    def body(h_hbm, dest_hbm, xs_hbm, idx_v, rows_v, sem_in, sem_out):
        base = _sc_worker_id() * per_w
        pltpu.sync_copy(dest_hbm.at[_sc_worker_id()], idx_v)

        def load(j, buf):
            return pltpu.make_async_copy(
                h_hbm.at[pl.ds(base + j * SC_CHUNK, SC_CHUNK)], rows_v.at[buf], sem_in)

        load(0, 0).start()

        @pl.loop(0, n_chunks, step=2)
        def _(j0):
            for buf in range(2):
                j = j0 + buf
                load(j, buf).wait()

                @pl.when(j + 1 < n_chunks)
                def _():
                    load(j + 1, 1 - buf).start()

                copies = [
                    pltpu.make_async_copy(
                        rows_v.at[buf], xs_hbm.at[idx_v.at[j * TOP_K + k]], sem_out)
                    for k in range(TOP_K)]
                for cp in copies:
                    cp.start()
                for cp in copies:
                    cp.wait()

    return pl.kernel(
        body,
        out_type=jax.ShapeDtypeStruct((n_slots, width), I32),
        mesh=_sc_mesh(),
        scratch_types=[
            pltpu.VMEM((n_chunks * TOP_K, SC_CHUNK), I32),
            pltpu.VMEM((2, SC_CHUNK, width), I32),
            pltpu.SemaphoreType.DMA,
            pltpu.SemaphoreType.DMA,
        ],
        name="sc_dispatch",
    )(h_packed, dest_sc)


def _sc_gather(ys_packed, dest_sc):
    width = ys_packed.shape[1]
    n_chunks = dest_sc.shape[1]
    per_w = n_chunks * SC_CHUNK
    n_rows = SC_WORKERS * per_w

    def body(ys_hbm, dest_hbm, out_hbm, idx_v, rows_v, sem_g, sem_w):
        base = _sc_worker_id() * per_w
        pltpu.sync_copy(dest_hbm.at[_sc_worker_id()], idx_v)

        def gather(j, buf):
            return pltpu.make_async_copy(ys_hbm.at[idx_v.at[j]], rows_v.at[buf], sem_g)

        def writeout(j, buf):
            return pltpu.make_async_copy(
                rows_v.at[buf], out_hbm.at[pl.ds(base + j * SC_CHUNK, SC_CHUNK)], sem_w)

        gather(0, 0).start()

        @pl.loop(0, n_chunks, step=2)
        def _(j0):
            for buf in range(2):
                j = j0 + buf
                gather(j, buf).wait()

                @pl.when(j >= 1)
                def _():
                    writeout(j - 1, 1 - buf).wait()

                @pl.when(j + 1 < n_chunks)
                def _():
                    gather(j + 1, 1 - buf).start()

                writeout(j, buf).start()

        writeout(n_chunks - 1, 1).wait()

    return pl.kernel(
        body,
        out_type=jax.ShapeDtypeStruct((n_rows, width), I32),
        mesh=_sc_mesh(),
        scratch_types=[
            pltpu.VMEM((n_chunks, SC_CHUNK), I32),
            pltpu.VMEM((2, SC_CHUNK, width), I32),
            pltpu.SemaphoreType.DMA,
            pltpu.SemaphoreType.DMA,
        ],
        name="sc_gather",
    )(ys_packed, dest_sc)


def _expert_kernel(layer, be_ref, nv_ref, first_ref, next_ref, nu_ref, xs_ref, win_hbm, bin_ref,
                   wout_hbm, bout_ref, ys_ref, win_f32, wout_f32, win_bf, wout_bf, sem):
    i = pl.program_id(0)
    valid = nv_ref[i]

    def fetch(e):
        return (pltpu.make_async_copy(win_hbm.at[layer, e], win_f32, sem.at[0]),
                pltpu.make_async_copy(wout_hbm.at[layer, e], wout_f32, sem.at[1]))

    @pl.when(first_ref[i] == 1)
    def _():
        @pl.when(i == 0)
        def _():
            for cp in fetch(be_ref[i]):
                cp.start()

        for cp in fetch(be_ref[i]):
            cp.wait()

        def cast_in(r, c):
            rows = pl.ds(pl.multiple_of(r * CAST_ROWS, CAST_ROWS), CAST_ROWS)
            win_bf[rows, :] = win_f32[rows, :].astype(BF16)
            return c

        def cast_out(r, c):
            rows = pl.ds(pl.multiple_of(r * CAST_ROWS, CAST_ROWS), CAST_ROWS)
            wout_bf[rows, :] = wout_f32[rows, :].astype(BF16)
            return c

        lax.fori_loop(0, D_MODEL // CAST_ROWS, cast_in, 0)
        lax.fori_loop(0, D_FF // CAST_ROWS, cast_out, 0)

        @pl.when(next_ref[i] >= 0)
        def _():
            for cp in fetch(next_ref[i]):
                cp.start()

    def load_rows(start, size):
        rows = pl.ds(start, size)
        row = lax.broadcasted_iota(I32, (size, 1), 0)
        lo, hi = _unpack_rows(jnp.where(row + start < valid, xs_ref[rows, :], 0))
        return lo.astype(BF16), hi.astype(BF16)

    def up_proj(lo, hi):
        return (jnp.dot(lo, win_bf[:PACKED, :], preferred_element_type=F32)
                + jnp.dot(hi, win_bf[PACKED:, :], preferred_element_type=F32) + bin_ref[0])

    def activation(gu):
        gate = jnp.minimum(gu[:, :D_FF], SWIGLU_LIMIT)
        up = jnp.clip(gu[:, D_FF:], -SWIGLU_LIMIT, SWIGLU_LIMIT)
        gated = gate * _sigmoid_of_twice(0.5 * (SWIGLU_ALPHA * gate))
        return (gated * (up + 1.0)).astype(BF16)

    def down_proj(act):
        return jnp.dot(act, wout_bf[...], preferred_element_type=F32) + bout_ref[0]

    def store_rows(start, size, out):
        rows = pl.ds(start, size)
        ys_ref[rows, :] = _pack_rows(out)

    def pipelined(n_pass):
        size = EXPERT_PASS
        starts = range(0, n_pass * size, size)
        gus, outs = {}, {}
        for n, start in enumerate(starts):
            gus[start] = up_proj(*load_rows(start, size))
            if n > 0:
                prev = starts[n - 1]
                outs[prev] = down_proj(activation(gus.pop(prev)))
            if n > 1:
                store_rows(starts[n - 2], size, outs.pop(starts[n - 2]))
        outs[starts[-1]] = down_proj(activation(gus.pop(starts[-1])))
        for start in list(outs):
            store_rows(start, size, outs.pop(start))

    passes = (valid + EXPERT_PASS - 1) // EXPERT_PASS
    for n_pass in range(1, EXPERT_ROWS // EXPERT_PASS + 1):
        pl.when(passes == n_pass)(functools.partial(pipelined, n_pass))

    def clear(j, c):
        rows = pl.ds(pl.multiple_of(j * EXPERT_PASS, EXPERT_PASS), EXPERT_PASS)
        ys_ref[rows, :] = jnp.zeros((EXPERT_PASS, ys_ref.shape[1]), ys_ref.dtype)
        return c

    lax.fori_loop(passes, EXPERT_ROWS // EXPERT_PASS, clear, 0)


def _experts(layer, block_e, n_valid, is_first, next_e, n_used, xs, w_e_in, b_e_in, w_e_out,
             b_e_out):
    n_slots, width = xs.shape
    d = D_MODEL
    tm = EXPERT_ROWS
    bias_index = lambda i, be, nv, fi, nx, nu: (layer, be[i], 0, 0)
    in_index = lambda i, be, nv, fi, nx, nu: (jnp.minimum(i, nu[0] - 1), 0)
    out_index = lambda i, be, nv, fi, nx, nu: (jnp.minimum(i, nu[0]), 0)
    grid_spec = pltpu.PrefetchScalarGridSpec(
        num_scalar_prefetch=5,
        grid=(n_slots // tm,),
        in_specs=[
            pl.BlockSpec((tm, width), in_index),
            pl.BlockSpec(memory_space=pl.ANY),
            pl.BlockSpec((None, 1, 1, 2 * D_FF), bias_index),
            pl.BlockSpec(memory_space=pl.ANY),
            pl.BlockSpec((None, 1, 1, d), bias_index),
        ],
        out_specs=pl.BlockSpec((tm, width), out_index),
        scratch_shapes=[
            pltpu.VMEM((d, 2 * D_FF), F32), pltpu.VMEM((D_FF, d), F32),
            pltpu.VMEM((d, 2 * D_FF), BF16), pltpu.VMEM((D_FF, d), BF16),
            pltpu.SemaphoreType.DMA((2,)),
        ],
    )
    return pl.pallas_call(
        functools.partial(_expert_kernel, layer),
        out_shape=jax.ShapeDtypeStruct((n_slots, width), I32),
        grid_spec=grid_spec,
        compiler_params=pltpu.CompilerParams(
            dimension_semantics=("arbitrary",), vmem_limit_bytes=VMEM_LIMIT),
        name="experts",
    )(block_e, n_valid, is_first, next_e, n_used, xs, w_e_in,
      b_e_in.reshape(-1, N_EXPERTS, 1, 2 * D_FF), w_e_out, b_e_out.reshape(-1, N_EXPERTS, 1, d))


def _final_kernel(x_ref, y0_ref, y1_ref, y2_ref, y3_ref, wt_ref, mod_ref, g_ref, o_ref):
    x = _combined_rows(x_ref, (y0_ref, y1_ref, y2_ref, y3_ref), wt_ref, mod_ref[5])
    ms = jnp.mean(x * x, axis=-1, keepdims=True)
    out = x * lax.rsqrt(ms + NORM_EPS) * g_ref[...]
    o_ref[...] = pltpu.einshape("(tb)d->btd", out, b=BATCH)


def _final_norm(x2d, pending, g):
    n_tok, d = x2d.shape
    tr = NORM_ROWS
    y4, wts, mod_l = pending
    return pl.pallas_call(
        _final_kernel,
        out_shape=jax.ShapeDtypeStruct((BATCH, n_tok // BATCH, d), F32),
        grid=(n_tok // tr,),
        in_specs=([pl.BlockSpec((tr, d), lambda i: (i, 0))]
                  + _pending_specs(tr, n_tok, lambda i: i)
                  + [_const_spec(mod_l.shape), _const_spec((1, d))]),
        out_specs=pl.BlockSpec((BATCH, tr // BATCH, d), lambda i: (0, i, 0)),
        compiler_params=pltpu.CompilerParams(
            dimension_semantics=("arbitrary",), vmem_limit_bytes=VMEM_LIMIT),
        name="final_norm",
    )(x2d, y4, y4, y4, y4, wts, mod_l, g.reshape(1, d))


def _moe_layer(layer, h_packed, idx, pos, counts, w_e_in, b_e_in, w_e_out, b_e_out):
    n_tok = h_packed.shape[0]
    tm = EXPERT_ROWS
    padded = ((counts + tm - 1) // tm) * tm
    pad_end = jnp.cumsum(padded)
    start_pad = pad_end - padded
    experts = jnp.arange(N_EXPERTS, dtype=I32)
    dest = jnp.sum(jnp.where(idx[:, :, None] == experts, start_pad, 0), axis=-1) + pos
    n_blocks = (n_tok * TOP_K) // tm + N_EXPERTS
    n_slots = n_blocks * tm
    block_start = jnp.arange(n_blocks, dtype=I32) * tm
    block_e = jnp.minimum(
        jnp.sum((block_start[:, None] >= pad_end[None, :]).astype(I32), axis=-1),
        N_EXPERTS - 1)
    hot_e = block_e[:, None] == experts[None, :]
    rows_left = jnp.sum(jnp.where(hot_e, (start_pad + counts)[None, :], 0), axis=-1) - block_start
    n_valid = jnp.where(block_start < pad_end[-1], jnp.clip(rows_left, 0, tm), 0).astype(I32)
    is_first = jnp.sum(jnp.where(hot_e, start_pad[None, :], 0), axis=-1) == block_start
    is_first = (is_first & (n_valid > 0)).astype(I32)
    later = (experts[None, :] > experts[:, None]) & (counts[None, :] > 0)
    next_of = jnp.min(jnp.where(later, experts[None, :], N_EXPERTS), axis=-1)
    next_of = jnp.where(next_of == N_EXPERTS, -1, next_of)
    next_e = jnp.sum(jnp.where(hot_e, next_of[None, :], 0), axis=-1).astype(I32)

    per_w = n_tok // SC_WORKERS
    n_chunks = per_w // SC_CHUNK
    dest_scatter = (dest.reshape(TOP_K, SC_WORKERS, n_chunks, SC_CHUNK)
                    .transpose(1, 2, 0, 3)
                    .reshape(SC_WORKERS, n_chunks * TOP_K, SC_CHUNK))
    xs = _sc_dispatch(h_packed, dest_scatter, n_slots)
    n_used = (pad_end[-1] // tm).astype(I32).reshape(1)
    ys = _experts(layer, block_e, n_valid, is_first, next_e, n_used, xs,
                  w_e_in, b_e_in, w_e_out, b_e_out)
    dest_gather = dest.reshape(SC_WORKERS, n_chunks * TOP_K, SC_CHUNK)
    return _sc_gather(ys, dest_gather)


def kernel(x, c, norm1_g, norm2_g, w_ada, b_ada, w_in, conv_w, conv_b, w_rg_a, b_rg_a,
           w_rg_x, b_rg_x, lru_lambda, w_pool, pool_scale, w_up_a, w_up_b, w_out,
           w_router, b_router, w_e_in, b_e_in, w_e_out, b_e_out, final_g):
    depth = w_in.shape[0]
    bsz, seq, d = x.shape
    assert bsz == BATCH and d == D_MODEL
    mod = _ada_mod(c, w_ada, b_ada).reshape(depth, bsz, 6, d).transpose(0, 2, 1, 3)
    w_rg = (0.5 * jnp.concatenate([w_rg_a, w_rg_x], axis=-1)).astype(BF16)
    gate_cols = jnp.arange(w_in.shape[2]) >= 2 * d + POOL_WIDTH
    w_in_scaled = jnp.where(gate_cols, 0.5, 1.0) * w_in
    pending = None
    xcur = x
    for l in range(depth):
        pad = jnp.zeros((d - pool_scale.shape[1],), F32)
        vec = jnp.stack([norm1_g[l], conv_b[l], 0.5 * b_rg_a[l], 0.5 * b_rg_x[l],
                         lru_lambda[l], norm2_g[l], jnp.concatenate([pool_scale[l], pad]),
                         jnp.zeros((d,), F32)])
        xcur, h_packed, idx, pos, wts, counts = _mixer_layer(
            xcur, pending, mod[l], vec, w_in_scaled[l].astype(BF16), conv_w[l], w_rg[l],
            w_pool[l].astype(BF16), w_up_a[l].astype(BF16), w_up_b[l].astype(BF16),
            w_out[l].astype(BF16), _padded_router_weight(w_router[l]), b_router[l])
        y4 = _moe_layer(l, h_packed, idx, pos, counts[:, 0], w_e_in, b_e_in, w_e_out, b_e_out)
        pending = (y4, wts.T, mod[l])
    return _final_norm(xcur, pending, final_g)
```
